```python
import math
import jax, jax.numpy as jnp
from jax import lax
import numpy as np

D_MODEL = 1024
BATCH = 8
SEQ = 2048
DEPTH = 1
DEC_BATCH = 128
DEC_SEQ = 4
PAST_LEN = 16384
PAGE_SIZE = 128

HGRN_DK = 128
HGRN_DV = 128
HGRN_HEADS = D_MODEL // HGRN_DK
HGRN_WIDTH = HGRN_HEADS * HGRN_DK
HGRN_VWIDTH = HGRN_HEADS * HGRN_DV
HGRN_CHUNK = 64
RWKV_N = 64
RWKV_HEADS = D_MODEL // RWKV_N
RWKV_WIDTH = RWKV_HEADS * RWKV_N
W_LORA = max(32, int(round(1.8 * D_MODEL ** 0.5 / 32)) * 32)
A_LORA = max(32, int(round(1.8 * D_MODEL ** 0.5 / 32)) * 32)
G_LORA = max(32, int(round(0.6 * D_MODEL ** 0.8 / 32)) * 32)
RWKV_SIZES = (RWKV_WIDTH, RWKV_WIDTH, RWKV_WIDTH, W_LORA, A_LORA, G_LORA)
RWKV_COLS = 3 * RWKV_WIDTH + W_LORA + A_LORA + G_LORA
IN_SIZES = (HGRN_WIDTH, HGRN_WIDTH, HGRN_VWIDTH, HGRN_VWIDTH, RWKV_COLS, D_MODEL, D_MODEL)
IN_COLS = 2 * HGRN_WIDTH + 2 * HGRN_VWIDTH + RWKV_COLS + 2 * D_MODEL
D_FF = 4 * D_MODEL
RMS_EPS = 1e-6
LNX_EPS = 64e-5

kernel_name = 'hybrid_hgrn2_rwkv7_adaln_step'


def _split(a, sizes):
    idx = np.cumsum(np.array(sizes))[:-1].tolist()
    return jnp.split(a, idx, axis=-1)


def rms_norm(x, w):
    xf = x.astype(jnp.float32)
    y = xf * lax.rsqrt(jnp.mean(xf * xf, axis=-1, keepdims=True) + RMS_EPS)
    return (y * w.astype(jnp.float32)).astype(x.dtype)


def hgrn2_chunked(q, log_f, k, v, s0):
    B, T, H, K = q.shape
    V = v.shape[-1]
    C = min(HGRN_CHUNK, T)
    pad = (-T) % C
    if pad:
        pw = ((0, 0), (0, pad), (0, 0), (0, 0))
        q, log_f, k, v = (jnp.pad(a, pw) for a in (q, log_f, k, v))
    n = (T + pad) // C

    def to_chunks(a):
        return a.reshape(B, n, C, H, a.shape[-1]).transpose(1, 0, 3, 2, 4)

    causal = jnp.tril(jnp.ones((C, C), dtype=bool))[:, :, None]

    def step(S, inp):
        qc, lfc, kc, vc = inp
        b = jnp.cumsum(lfc, axis=2)
        diff = b[:, :, :, None, :] - b[:, :, None, :, :]
        decay = jnp.exp(jnp.where(causal, diff, -jnp.inf))
        attn = jnp.einsum('bhtk,bhsk,bhtsk->bhts', qc, kc, decay)
        o_intra = jnp.einsum('bhts,bhsv->bhtv', attn, vc)
        o_inter = jnp.einsum('bhtk,bhkv->bhtv', qc * jnp.exp(b), S)
        b_last = b[:, :, -1:, :]
        S_new = jnp.exp(b_last[:, :, 0, :])[..., None] * S + jnp.einsum(
            'bhsk,bhsv->bhkv', kc * jnp.exp(b_last - b), vc)
        return S_new, o_intra + o_inter

    S, o = lax.scan(step, s0, (to_chunks(q), to_chunks(log_f), to_chunks(k), to_chunks(v)))
    o = o.transpose(1, 0, 3, 2, 4).reshape(B, n * C, H, V)[:, :T]
    return o, S


def rwkv7_scan(r, log_w, k, v, kk, a, s0):
    def step(S, inp):
        r_t, lw_t, k_t, v_t, kk_t, a_t = inp
        sk = jnp.einsum('bhij,bhj->bhi', S, kk_t)
        S = (S * jnp.exp(lw_t)[:, :, None, :]
             - sk[..., None] * (kk_t * a_t)[:, :, None, :]
             + v_t[..., None] * k_t[:, :, None, :])
        o = jnp.einsum('bhij,bhj->bhi', S, r_t)
        return S, o

    xs = tuple(jnp.moveaxis(t, 1, 0) for t in (r, log_w, k, v, kk, a))
    S, o = lax.scan(step, s0, xs)
    return jnp.moveaxis(o, 0, 1), S


def _layer(x, c, s_hgrn, s_rwkv, s_shift, lb, p):
    (norm1_w, norm2_w, ada_w, ada_b, w_in, hgrn_norm_w, mu, w0, w2, a0, a2, g2,
     k_k, k_a, r_k, lnx_w, lnx_b, w_out, w_up, w_down) = p
    f32 = jnp.float32
    B, T, _ = x.shape
    dt = x.dtype
    mod = (jax.nn.silu(c) @ ada_w + ada_b)[:, None, :]
    sh1, sc1, gt1, sh2, sc2, gt2 = jnp.split(mod, 6, axis=-1)

    h = rms_norm(x, norm1_w) * (1 + sc1) + sh1
    P = h @ w_in
    q, f_pre, i_val, og, rc, gate_a, gate_b = _split(P, IN_SIZES)

    f = lb + (1.0 - lb) * jax.nn.sigmoid(f_pre.astype(f32))
    log_f = jnp.log(f)
    k_in = 1.0 - f
    heads_a = lambda t, n: t.astype(f32).reshape(B, T, HGRN_HEADS, n)
    o_a, s_hgrn_new = hgrn2_chunked(heads_a(q, HGRN_DK), log_f.reshape(B, T, HGRN_HEADS, HGRN_DK),
                                    k_in.reshape(B, T, HGRN_HEADS, HGRN_DK), heads_a(i_val, HGRN_DV),
                                    s_hgrn.astype(f32))
    o_a = (o_a * lax.rsqrt(jnp.mean(o_a * o_a, axis=-1, keepdims=True) + RMS_EPS)
           * hgrn_norm_w.astype(f32) * jax.nn.silu(heads_a(og, HGRN_DV)))
    o_a = o_a.reshape(B, T, HGRN_VWIDTH).astype(dt)

    prev = jnp.concatenate([s_shift[:, None, :].astype(rc.dtype), rc[:, :-1]], axis=1)
    xs = rc + mu * (prev - rc)
    r, kb, v, xw, xa, xg = _split(xs, RWKV_SIZES)
    w = (w0 + jnp.tanh(xw) @ w2).astype(f32)
    log_w = -jnp.exp(-jax.nn.softplus(-w) - 0.5)
    a = jax.nn.sigmoid((a0 + xa @ a2).astype(f32))
    g = (jax.nn.sigmoid(xg) @ g2).astype(f32)
    heads_b = lambda t: t.astype(f32).reshape(B, T, RWKV_HEADS, RWKV_N)
    kk = heads_b(kb * k_k)
    kk = kk / jnp.maximum(jnp.sqrt(jnp.sum(kk * kk, axis=-1, keepdims=True)), 1e-12)
    kmod = kb.astype(f32) * (1.0 + (a - 1.0) * k_a.astype(f32))
    r_h, k_h, v_h, a_h = heads_b(r), heads_b(kmod), heads_b(v), heads_b(a)
    o_b, s_rwkv_new = rwkv7_scan(r_h, heads_b(log_w), k_h, v_h, kk, a_h, s_rwkv.astype(f32))
    mean = jnp.mean(o_b, axis=-1, keepdims=True)
    var = jnp.mean(jnp.square(o_b - mean), axis=-1, keepdims=True)
    o_b = ((o_b - mean) * lax.rsqrt(var + LNX_EPS)).reshape(B, T, RWKV_WIDTH)
    o_b = o_b * lnx_w.astype(f32) + lnx_b.astype(f32)
    bonus = jnp.sum(r_h * k_h * r_k.astype(f32), axis=-1, keepdims=True) * v_h
    o_b = ((o_b + bonus.reshape(B, T, RWKV_WIDTH)) * g).astype(dt)

    m = jax.nn.sigmoid(gate_a) * o_a + jax.nn.sigmoid(gate_b) * o_b
    x = x + gt1 * (m @ w_out)

    h2 = rms_norm(x, norm2_w) * (1 + sc2) + sh2
    x = x + gt2 * (jnp.square(jax.nn.relu(h2 @ w_up)) @ w_down)
    return (x, s_hgrn_new.astype(s_hgrn.dtype), s_rwkv_new.astype(s_rwkv.dtype),
            rc[:, -1].astype(s_shift.dtype))


def _run(x, c, s_h, s_r, s_s, lb_all, layer_params, final_norm_w):
    hs, rs, ss = [], [], []
    for l in range(DEPTH):
        x, h_new, r_new, s_new = _layer(x, c, s_h[l], s_r[l], s_s[l], lb_all[l],
                                        tuple(p[l] for p in layer_params))
        hs.append(h_new)
        rs.append(r_new)
        ss.append(s_new)
    return rms_norm(x, final_norm_w), jnp.stack(hs), jnp.stack(rs), jnp.stack(ss)


def setup_inputs(seed: int = 0) -> dict:
    key = jax.random.key(seed)
    ks = iter(list(jax.random.split(key, 48)))
    L, D = DEPTH, D_MODEL

    def nrm(shape, scale):
        return scale * jax.random.normal(next(ks), shape, jnp.float32)

    def gain(shape):
        return 1.0 + nrm(shape, 0.02)

    return {
        'x_prompt': nrm((BATCH, SEQ, D), 1.0),
        'x_sample': nrm((DEC_BATCH, DEC_SEQ, D), 1.0),
        'state_hgrn': nrm((L, DEC_BATCH, HGRN_HEADS, HGRN_DK, HGRN_DV), 0.5),
        'state_rwkv': nrm((L, DEC_BATCH, RWKV_HEADS, RWKV_N, RWKV_N), 0.5),
        'state_shift': nrm((L, DEC_BATCH, RWKV_COLS), 1.0),
        'c_prompt': nrm((BATCH, D), 1.0),
        'c_sample': nrm((DEC_BATCH, D), 1.0),
        'norm1_w': gain((L, D)),
        'norm2_w': gain((L, D)),
        'ada_w': nrm((L, D, 6 * D), 0.3 * D ** -0.5),
        'ada_b': nrm((L, 6 * D), 0.02),
        'w_in': nrm((L, D, IN_COLS), D ** -0.5),
        'lb_logits': nrm((L + 1, HGRN_WIDTH), 0.5),
        'hgrn_norm_w': gain((L, HGRN_DV)),
        'rwkv_mu': jax.random.uniform(next(ks), (L, RWKV_COLS), jnp.float32, 0.1, 0.9),
        'rwkv_w0': -2.5 + nrm((L, RWKV_WIDTH), 1.0),
        'rwkv_w2': nrm((L, W_LORA, RWKV_WIDTH), 0.1 * W_LORA ** -0.5),
        'rwkv_a0': nrm((L, RWKV_WIDTH), 0.5),
        'rwkv_a2': nrm((L, A_LORA, RWKV_WIDTH), 0.5 * A_LORA ** -0.5),
        'rwkv_g2': nrm((L, G_LORA, RWKV_WIDTH), G_LORA ** -0.5),
        'rwkv_k_k': 0.85 + nrm((L, RWKV_WIDTH), 0.05),
        'rwkv_k_a': 1.0 + nrm((L, RWKV_WIDTH), 0.05),
        'rwkv_r_k': nrm((L, RWKV_HEADS, RWKV_N), 0.1),
        'rwkv_lnx_w': gain((L, RWKV_WIDTH)),
        'rwkv_lnx_b': nrm((L, RWKV_WIDTH), 0.02),
        'w_out': nrm((L, D, D), D ** -0.5),
        'w_up': nrm((L, D, D_FF), D ** -0.5),
        'w_down': nrm((L, D_FF, D), D_FF ** -0.5),
        'final_norm_w': gain((D,)),
    }


def reference(x_prompt, x_sample, state_hgrn, state_rwkv, state_shift, c_prompt, c_sample,
              norm1_w, norm2_w, ada_w, ada_b, w_in, lb_logits, hgrn_norm_w, rwkv_mu, rwkv_w0,
              rwkv_w2, rwkv_a0, rwkv_a2, rwkv_g2, rwkv_k_k, rwkv_k_a, rwkv_r_k, rwkv_lnx_w,
              rwkv_lnx_b, w_out, w_up, w_down, final_norm_w):
    layer_params = (norm1_w, norm2_w, ada_w, ada_b, w_in, hgrn_norm_w, rwkv_mu, rwkv_w0, rwkv_w2,
                    rwkv_a0, rwkv_a2, rwkv_g2, rwkv_k_k, rwkv_k_a, rwkv_r_k, rwkv_lnx_w,
                    rwkv_lnx_b, w_out, w_up, w_down)
    lb_all = jnp.cumsum(jax.nn.softmax(lb_logits.astype(jnp.float32), axis=0), axis=0)
    zh = jnp.zeros((DEPTH, BATCH, HGRN_HEADS, HGRN_DK, HGRN_DV), state_hgrn.dtype)
    zr = jnp.zeros((DEPTH, BATCH, RWKV_HEADS, RWKV_N, RWKV_N), state_rwkv.dtype)
    zs = jnp.zeros((DEPTH, BATCH, RWKV_COLS), state_shift.dtype)
    y_prompt, hgrn_p, rwkv_p, shift_p = _run(x_prompt, c_prompt, zh, zr, zs, lb_all,
                                             layer_params, final_norm_w)
    y_sample, hgrn_s, rwkv_s, shift_s = _run(x_sample, c_sample, state_hgrn, state_rwkv,
                                             state_shift, lb_all, layer_params, final_norm_w)
    return (y_prompt, y_sample, hgrn_p, rwkv_p, shift_p, hgrn_s, rwkv_s, shift_s)
```

```python
import functools
import math

import jax
import jax.numpy as jnp
from jax import lax
from jax.experimental import pallas as pl
from jax.experimental.pallas import tpu as pltpu

F32 = jnp.float32
BF16 = jnp.bfloat16

LANES = 128
SUBLANES = 8
VMEM_LIMIT = 48 * 1024 * 1024

HGRN_DK = 128
RWKV_N = 64
W_LORA = 64
A_LORA = 64
G_LORA = 160
G_LORA_PAD = 256
RC_MAIN = 3 * 1024
RC_COLS = RC_MAIN + W_LORA + A_LORA + G_LORA
RC_PAD = RC_MAIN + LANES + G_LORA_PAD
RMS_EPS = 1e-6
LNX_EPS = 64e-5
CHUNK = 64
SUB = 16
SEQ_PAD = 8


def _cparams(sem):
    return pltpu.CompilerParams(dimension_semantics=sem, vmem_limit_bytes=VMEM_LIMIT)


_NN = (((1,), (0,)), ((), ()))
_NT = (((1,), (1,)), ((), ()))
_TN = (((0,), (0,)), ((), ()))


def _dot(a, b, dims=_NN):
    return lax.dot_general(a.astype(BF16), b.astype(BF16), dims, preferred_element_type=F32)


def _split2(x):
    hi = x.astype(BF16)
    lo = (x - hi.astype(F32)).astype(BF16)
    return hi, lo


def _split3(x):
    hi = x.astype(BF16)
    r1 = x - hi.astype(F32)
    mid = r1.astype(BF16)
    lo = (r1 - mid.astype(F32)).astype(BF16)
    return hi, mid, lo


def _dot_exact_lhs(a, x, dims=_NN):
    a = a.astype(BF16)
    hi, mid, lo = _split3(x)
    d = lambda p: lax.dot_general(a, p, dims, preferred_element_type=F32)
    return d(hi) + d(mid) + d(lo)


def _dot_exact_rhs(x, b, dims=_NN):
    b = b.astype(BF16)
    hi, mid, lo = _split3(x)
    d = lambda p: lax.dot_general(p, b, dims, preferred_element_type=F32)
    return d(hi) + d(mid) + d(lo)


def _dot_hp(a, b, dims=_NN):
    ah, al = _split2(a)
    bh, bl = _split2(b)
    d = lambda p, q: lax.dot_general(p, q, dims, preferred_element_type=F32)
    return d(ah, bh) + d(ah, bl) + d(al, bh)


def _sigmoid(x):
    return 1.0 / (1.0 + jnp.exp(-x))


def _iota2(shape, dim):
    return lax.broadcasted_iota(jnp.int32, shape, dim)


def _neumann_inv(L, eye, nsq):
    X = L
    P = eye - L
    for _ in range(nsq):
        X = _dot_hp(X, X)
        P = P + _dot_hp(P, X)
    return P


def _adaln_kernel(c_ref, w_ref, b_ref, o_ref):
    c = c_ref[...]
    s = c * _sigmoid(c)
    o_ref[...] = _dot(s, w_ref[...]) + b_ref[...]


def _adaln(c_all, ada_w, ada_b, tn=1536):
    R, D = c_all.shape
    N = ada_w.shape[1]
    return pl.pallas_call(
        _adaln_kernel,
        grid=(N // tn,),
        in_specs=[pl.BlockSpec((R, D), lambda j: (0, 0)),
                  pl.BlockSpec((D, tn), lambda j: (0, j)),
                  pl.BlockSpec((1, tn), lambda j: (0, j))],
        out_specs=pl.BlockSpec((R, tn), lambda j: (0, j)),
        out_shape=jax.ShapeDtypeStruct((R, N), F32),
        compiler_params=_cparams(("parallel",)),
        name="adaln",
    )(c_all, ada_w, ada_b.reshape(1, N))


class _Mod:
    def __init__(self, arr, per_seq, seq_rows, D):
        self.arr, self.per_seq, self.seq_rows, self.D = arr, per_seq, seq_rows, D

    def spec(self, tm, comp):
        D = self.D
        if self.per_seq:
            tiles = self.seq_rows // tm
            return pl.BlockSpec((None, 1, D), lambda i, *_: (i // tiles, 0, comp))
        return pl.BlockSpec((tm, D), lambda i, *_: (i, comp))


def _rms(x, w):
    return x * lax.rsqrt(jnp.mean(x * x, axis=-1, keepdims=True) + RMS_EPS) * w


def _norm_mod_kernel(x_ref, nw_ref, sh_ref, sc_ref, o_ref):
    h = _rms(x_ref[...], nw_ref[...]) * (1.0 + sc_ref[...]) + sh_ref[...]
    o_ref[...] = h.astype(o_ref.dtype)


def _norm_mod(x, nw, mod, sh_comp, sc_comp, tm):
    R, D = x.shape
    return pl.pallas_call(
        _norm_mod_kernel,
        grid=(R // tm,),
        in_specs=[pl.BlockSpec((tm, D), lambda i: (i, 0)),
                  pl.BlockSpec((1, D), lambda i: (0, 0)),
                  mod.spec(tm, sh_comp), mod.spec(tm, sc_comp)],
        out_specs=pl.BlockSpec((tm, D), lambda i: (i, 0)),
        out_shape=jax.ShapeDtypeStruct((R, D), BF16),
        compiler_params=_cparams(("parallel",)),
        name="norm_mod",
    )(x, nw.reshape(1, D), mod.arr, mod.arr)


def _mm_kernel(x_ref, w_ref, o_ref):
    o_ref[...] = jnp.dot(x_ref[...], w_ref[...], preferred_element_type=F32)


def _matmul(x, w, tm, tn, name):
    R, K = x.shape
    N = w.shape[1]
    return pl.pallas_call(
        _mm_kernel,
        grid=(R // tm, N // tn),
        in_specs=[pl.BlockSpec((tm, K), lambda i, j: (i, 0)),
                  pl.BlockSpec((K, tn), lambda i, j: (0, j))],
        out_specs=pl.BlockSpec((tm, tn), lambda i, j: (i, j)),
        out_shape=jax.ShapeDtypeStruct((R, N), F32),
        compiler_params=_cparams(("parallel", "arbitrary")),
        name=name,
    )(x, w)


def _hgrn_gates(fp, lb_logits):
    m = jnp.max(lb_logits, axis=0, keepdims=True)
    e = jnp.exp(lb_logits - m)
    lb = e[0:1, :] / jnp.sum(e, axis=0, keepdims=True)
    f = lb + (1.0 - lb) * _sigmoid(fp)
    return jnp.log(f), 1.0 - f


def _hgrn_diag(q, kin, b, c):
    R = q.shape[0]
    nb = R // c
    q3 = q.reshape(nb, c, LANES)
    k3 = kin.reshape(nb, c, LANES)
    b3 = b.reshape(nb, c, LANES)
    tpos = lax.broadcasted_iota(jnp.int32, (nb, c, LANES), 1)
    lane_mod = lax.broadcasted_iota(jnp.int32, (nb, c, R), 2) % c
    acc = jnp.zeros((nb, c, R), F32)
    for s in range(c):
        d = jnp.where(tpos >= s, b3 - b3[:, s:s + 1, :], -1e30)
        col = jnp.sum(q3 * k3[:, s:s + 1, :] * jnp.exp(d), axis=-1, keepdims=True)
        acc = jnp.where(lane_mod == s, col, acc)
    same_blk = (_iota2((R, R), 0) // c) == (_iota2((R, R), 1) // c)
    return jnp.where(same_blk, acc.reshape(R, R), 0.0)


def _hgrn_out(o, og, nw):
    o = o * lax.rsqrt(jnp.mean(o * o, axis=-1, keepdims=True) + RMS_EPS) * nw
    return o * (og * _sigmoid(og))


def _hgrn_prompt_kernel(q_ref, f_ref, i_ref, g_ref, lb_ref, nw_ref, o_ref, s_ref, st_ref, *, nchunks):
    tt = pl.program_id(2)

    @pl.when(tt == 0)
    def _():
        st_ref[...] = jnp.zeros_like(st_ref)

    C, c = CHUNK, SUB
    tri = (_iota2((C, C), 0) >= _iota2((C, C), 1)).astype(F32)
    lbl = lb_ref[...]
    nw = nw_ref[...]

    def chunk(ci, carry):
        r0 = pl.multiple_of(ci * C, C)
        q = q_ref[pl.ds(r0, C), :]
        v = i_ref[pl.ds(r0, C), :]
        lf, kin = _hgrn_gates(f_ref[pl.ds(r0, C), :], lbl)
        b = _dot_exact_lhs(tri, lf)
        b_last = b[C - 1:C, :]
        ST = st_ref[...]
        o = _dot(q * jnp.exp(b), ST, _NT)
        o = o + _dot(_hgrn_diag(q, kin, b, c), v)
        for J in range(C // c - 1):
            lo, hi = J * c, (J + 1) * c
            ref = b[hi - 1:hi, :]
            kJ = kin[lo:hi, :] * jnp.exp(ref - b[lo:hi, :])
            qR = q[hi:, :] * jnp.exp(b[hi:, :] - ref)
            contrib = _dot(_dot(qR, kJ, _NT), v[lo:hi, :])
            o = o + jnp.concatenate([jnp.zeros((hi, LANES), F32), contrib], axis=0)
        st_ref[...] = ST * jnp.exp(b_last) + _dot(v, kin * jnp.exp(b_last - b), _TN)
        o_ref[pl.ds(r0, C), :] = _hgrn_out(o, g_ref[pl.ds(r0, C), :], nw)
        return carry

    lax.fori_loop(0, nchunks, chunk, 0)

    @pl.when(tt == pl.num_programs(2) - 1)
    def _():
        s_ref[...] = st_ref[...].T


def _hgrn_prompt(P_h, lb_logits, nw, B, T, tt_rows=512):
    R, W = P_h.shape
    H = W // (4 * HGRN_DK)
    tt_rows = min(tt_rows, T)
    nt = T // tt_rows
    col = lambda off: pl.BlockSpec((tt_rows, HGRN_DK), lambda b, h, t: (b * nt + t, off * H + h))
    o, s = pl.pallas_call(
        functools.partial(_hgrn_prompt_kernel, nchunks=tt_rows // CHUNK),
        grid=(B, H, nt),
        in_specs=[col(0), col(1), col(2), col(3),
                  pl.BlockSpec((lb_logits.shape[0], HGRN_DK), lambda b, h, t: (0, h)),
                  pl.BlockSpec((1, HGRN_DK), lambda b, h, t: (0, 0))],
        out_specs=[pl.BlockSpec((tt_rows, HGRN_DK), lambda b, h, t: (b * nt + t, h)),
                   pl.BlockSpec((None, None, HGRN_DK, HGRN_DK), lambda b, h, t: (b, h, 0, 0))],
        out_shape=[jax.ShapeDtypeStruct((R, H * HGRN_DK), F32),
                   jax.ShapeDtypeStruct((B, H, HGRN_DK, HGRN_DK), F32)],
        scratch_shapes=[pltpu.VMEM((HGRN_DK, HGRN_DK), F32)],
        compiler_params=_cparams(("parallel", "parallel", "arbitrary")),
        name="hgrn_prompt",
    )(P_h, P_h, P_h, P_h, lb_logits, nw.reshape(1, HGRN_DK))
    return o, s


def _hgrn_sample_kernel(q_ref, f_ref, i_ref, g_ref, lb_ref, nw_ref, s0_ref, o_ref, s_ref, *, nseq, tvalid):
    P = SEQ_PAD
    R = nseq * P
    valid = (_iota2((R, 1), 0) % P) < tvalid
    q = q_ref[...]
    v = i_ref[...]
    lf, kin = _hgrn_gates(f_ref[...], lb_ref[...])
    lf = jnp.where(valid, lf, 0.0)
    kin = jnp.where(valid, kin, 0.0)
    rr, cc = _iota2((R, R), 0), _iota2((R, R), 1)
    seg_incl = ((rr // P == cc // P) & (rr >= cc)).astype(F32)
    seg_all = (rr // P == cc // P).astype(F32)
    b = _dot_exact_lhs(seg_incl, lf)
    b_tot = _dot_exact_lhs(seg_all, lf)
    o = _dot(_hgrn_diag(q, kin, b, P), v)
    qg = q * jnp.exp(b)
    kh = kin * jnp.exp(b_tot - b)
    eb = jnp.exp(b_tot)
    pieces = []
    for n in range(nseq):
        sl = slice(n * P, (n + 1) * P)
        ST = s0_ref[n].T
        pieces.append(_dot(qg[sl], ST, _NT))
        STn = ST * eb[n * P:n * P + 1, :] + _dot(v[sl], kh[sl], _TN)
        s_ref[n] = STn.T
    o = o + jnp.concatenate(pieces, axis=0)
    o_ref[...] = _hgrn_out(o, g_ref[...], nw_ref[...])


def _hgrn_sample(P_h, lb_logits, nw, s0, tvalid, nseq=8):
    R, W = P_h.shape
    H = W // (4 * HGRN_DK)
    B = R // SEQ_PAD
    rows = nseq * SEQ_PAD
    col = lambda off: pl.BlockSpec((rows, HGRN_DK), lambda h, g: (g, off * H + h))
    sspec = pl.BlockSpec((nseq, None, HGRN_DK, HGRN_DK), lambda h, g: (g, h, 0, 0))
    o, s = pl.pallas_call(
        functools.partial(_hgrn_sample_kernel, nseq=nseq, tvalid=tvalid),
        grid=(H, B // nseq),
        in_specs=[col(0), col(1), col(2), col(3),
                  pl.BlockSpec((lb_logits.shape[0], HGRN_DK), lambda h, g: (0, h)),
                  pl.BlockSpec((1, HGRN_DK), lambda h, g: (0, 0)),
                  sspec],
        out_specs=[pl.BlockSpec((rows, HGRN_DK), lambda h, g: (g, h)), sspec],
        out_shape=[jax.ShapeDtypeStruct((R, H * HGRN_DK), F32),
                   jax.ShapeDtypeStruct(s0.shape, F32)],
        compiler_params=_cparams(("parallel", "parallel")),
        name="hgrn_sample",
    )(P_h, P_h, P_h, P_h, lb_logits, nw.reshape(1, HGRN_DK), s0)
    return o, s


def _rwkv_prep_body(rc, prev, mu_ref, w0_ref, a0_ref, kk_ref, ka_ref, wwa_ref, g2_ref, outs):
    r_o, lw_o, k_o, v_o, kap_o, bet_o, g_o = outs
    D = RC_MAIN // 3
    xs = rc + mu_ref[...] * (prev - rc)
    r = xs[:, 0:D]
    kb = xs[:, D:2 * D]
    v = xs[:, 2 * D:3 * D]
    xwa = xs[:, RC_MAIN:RC_MAIN + LANES]
    xg = xs[:, RC_MAIN + LANES:]
    lane = _iota2((1, LANES), 1)
    wa = _dot(jnp.where(lane < W_LORA, jnp.tanh(xwa), xwa), wwa_ref[...])
    w = w0_ref[...] + wa[:, :D]
    a = _sigmoid(a0_ref[...] + wa[:, D:])
    lw = -math.exp(-0.5) * _sigmoid(w)
    g = _dot(_sigmoid(xg), g2_ref[...])
    kk = kb * kk_ref[...]
    seg = (_iota2((LANES, LANES), 0) // RWKV_N == _iota2((LANES, LANES), 1) // RWKV_N).astype(F32)
    k_a = ka_ref[...]
    for p in range(D // LANES):
        sl = slice(p * LANES, (p + 1) * LANES)
        kkp = kk[:, sl]
        nrm = jnp.maximum(jnp.sqrt(_dot_exact_rhs(kkp * kkp, seg)), 1e-12)
        kap = kkp / nrm
        ap = a[:, sl]
        kap_o[:, sl] = kap
        bet_o[:, sl] = ap * kap
        k_o[:, sl] = kb[:, sl] * (1.0 + (ap - 1.0) * k_a[:, sl])
    r_o[...] = r
    lw_o[...] = lw
    v_o[...] = v
    g_o[...] = g


def _rwkv_prep_prompt_kernel(p_ref, prev8_ref, shift_ref, mu_ref, w0_ref, a0_ref, kk_ref, ka_ref,
                             wwa_ref, g2_ref, *outs, tiles_per_seq):
    i = pl.program_id(0)
    rc = p_ref[...]
    tm = rc.shape[0]
    rolled = pltpu.roll(rc, 1, axis=0)
    start = (i % tiles_per_seq) == 0
    first = jnp.where(start, shift_ref[...], prev8_ref[SUBLANES - 1:SUBLANES, :])
    prev = jnp.where(_iota2((tm, 1), 0) == 0, first, rolled)
    _rwkv_prep_body(rc, prev, mu_ref, w0_ref, a0_ref, kk_ref, ka_ref, wwa_ref, g2_ref, outs)


def _rwkv_prep_sample_kernel(p_ref, shift_ref, mu_ref, w0_ref, a0_ref, kk_ref, ka_ref,
                             wwa_ref, g2_ref, *outs, seq_rows):
    rc = p_ref[...]
    tm = rc.shape[0]
    rolled = pltpu.roll(rc, 1, axis=0)
    prev = jnp.where((_iota2((tm, 1), 0) % seq_rows) == 0, shift_ref[...], rolled)
    _rwkv_prep_body(rc, prev, mu_ref, w0_ref, a0_ref, kk_ref, ka_ref, wwa_ref, g2_ref, outs)


def _rwkv_prep(P_r, shift, params, per_seq, seq_rows, tm=256):
    R = P_r.shape[0]
    D = RC_MAIN // 3
    tm = min(tm, seq_rows) if per_seq else min(tm, R)
    mu, w0, a0, k_k, k_a, wwa, g2p = params
    full = lambda a: pl.BlockSpec(a.shape, lambda i: (0,) * a.ndim)
    wspecs = [full(mu), full(w0), full(a0), full(k_k), full(k_a), full(wwa), full(g2p)]
    out_spec = pl.BlockSpec((tm, D), lambda i: (i, 0))
    out_shape = jax.ShapeDtypeStruct((R, D), F32)
    if per_seq:
        tiles = seq_rows // tm
        sub = tm // SUBLANES
        kern = functools.partial(_rwkv_prep_prompt_kernel, tiles_per_seq=tiles)
        in_specs = [pl.BlockSpec((tm, RC_PAD), lambda i: (i, 0)),
                    pl.BlockSpec((SUBLANES, RC_PAD), lambda i: (jnp.maximum(i * sub - 1, 0), 0)),
                    pl.BlockSpec((None, 1, RC_PAD), lambda i: (i // tiles, 0, 0))] + wspecs
        args = (P_r, P_r, shift)
    else:
        kern = functools.partial(_rwkv_prep_sample_kernel, seq_rows=seq_rows)
        in_specs = [pl.BlockSpec((tm, RC_PAD), lambda i: (i, 0)),
                    pl.BlockSpec((tm, RC_PAD), lambda i: (i, 0))] + wspecs
        args = (P_r, shift)
    return pl.pallas_call(
        kern,
        grid=(R // tm,),
        in_specs=in_specs,
        out_specs=[out_spec] * 7,
        out_shape=[out_shape] * 7,
        compiler_params=_cparams(("parallel",)),
        name="rwkv_prep",
    )(*args, mu, w0, a0, k_k, k_a, wwa, g2p)


def _pair_masks():
    lane = _iota2((1, LANES), 1)
    return (lane < RWKV_N).astype(F32), (lane >= RWKV_N).astype(F32)


def _stack(x, mA, mB):
    return jnp.concatenate([x * mA, x * mB], axis=0)


def _rwkv_scores(cs, lw, cs_end, r, k, kap, bet, mA, mB):
    e_neg = jnp.exp(-cs)
    e_rem = jnp.exp(cs_end - cs)
    rg = _stack(r * jnp.exp(cs), mA, mB)
    kg = _stack(kap * jnp.exp(cs - lw), mA, mB)
    kd = _stack(k * e_neg, mA, mB)
    bd = _stack(bet * e_neg, mA, mB)
    kh = _stack(k * e_rem, mA, mB)
    bh = _stack(bet * e_rem, mA, mB)
    return rg, kg, kd, bd, kh, bh


def _rwkv_epilogue(o, r, k, v, g, lnw, lnb, rk):
    avg = jnp.where(_iota2((LANES, LANES), 0) // RWKV_N == _iota2((LANES, LANES), 1) // RWKV_N,
                    1.0 / RWKV_N, 0.0).astype(F32)
    mean = _dot_exact_rhs(o, avg)
    d = o - mean
    var = _dot_exact_rhs(d * d, avg)
    on = d * lax.rsqrt(var + LNX_EPS) * lnw + lnb
    bonus = _dot_exact_rhs(r * k * rk, avg) * float(RWKV_N) * v
    return (on + bonus) * g


def _rwkv_prompt_kernel(r_ref, lw_ref, k_ref, v_ref, kap_ref, bet_ref, g_ref, lnw_ref, lnb_ref, rk_ref,
                        o_ref, s_ref, st_ref, *, nchunks):
    tt = pl.program_id(2)

    @pl.when(tt == 0)
    def _():
        st_ref[...] = jnp.zeros_like(st_ref)

    C, c = CHUNK, SUB
    C2 = 2 * C
    mA, mB = _pair_masks()
    rr, cc = _iota2((C2, C2), 0), _iota2((C2, C2), 1)
    eye = (rr == cc).astype(F32)
    strict = rr > cc
    incl = rr >= cc
    blk = (rr // c) == (cc // c)
    tri = (_iota2((C, C), 0) >= _iota2((C, C), 1)).astype(F32)
    lnw, lnb, rk = lnw_ref[...], lnb_ref[...], rk_ref[...]

    def chunk(ci, carry):
        r0 = pl.multiple_of(ci * C, C)
        ld = lambda ref: ref[pl.ds(r0, C), :]
        r, lw, k, v, kap, bet = ld(r_ref), ld(lw_ref), ld(k_ref), ld(v_ref), ld(kap_ref), ld(bet_ref)
        cs = _dot_exact_lhs(tri, lw)
        cs_end = cs[C - 1:C, :]
        rg, kg, kd, bd, kh, bh = _rwkv_scores(cs, lw, cs_end, r, k, kap, bet, mA, mB)
        vs = _stack(v, mA, mB)
        Ak = jnp.where(strict, _dot(kg, kd, _NT), 0.0)
        Ab = jnp.where(strict, _dot(kg, bd, _NT), 0.0)
        Bk = jnp.where(incl, _dot(rg, kd, _NT), 0.0)
        Bb = jnp.where(incl, _dot(rg, bd, _NT), 0.0)
        LD = jnp.where(blk, Ab, 0.0)
        Dinv = _neumann_inv(LD, eye, 3)
        Q = _dot_hp(Dinv, Ab - LD)
        Q2 = _dot_hp(Q, Q)
        Tm = _dot_hp((eye - Q) + (Q2 - _dot_hp(Q, Q2)), Dinv)
        S = st_ref[...]
        rhs = _dot(kg, S, _NT) + _dot(Ak, vs)
        U = _dot_hp(Tm, rhs)
        O = _dot(rg, S, _NT) + _dot(Bk, vs) - _dot(Bb, U)
        st_ref[...] = S * jnp.exp(cs_end) + _dot(vs, kh, _TN) - _dot(U, bh, _TN)
        o = O[:C, :] + O[C:, :]
        o_ref[pl.ds(r0, C), :] = _rwkv_epilogue(o, r, k, v, ld(g_ref), lnw, lnb, rk)
        return carry

    lax.fori_loop(0, nchunks, chunk, 0)

    @pl.when(tt == pl.num_programs(2) - 1)
    def _():
        s_ref[...] = st_ref[...]


def _rwkv_prompt(arrs, g, lnw, lnb, rk, B, T, tt_rows=512):
    R, D = arrs[0].shape
    NP = D // LANES
    tt_rows = min(tt_rows, T)
    nt = T // tt_rows
    col = pl.BlockSpec((tt_rows, LANES), lambda b, p, t: (b * nt + t, p))
    par = pl.BlockSpec((1, LANES), lambda b, p, t: (0, p))
    o, s = pl.pallas_call(
        functools.partial(_rwkv_prompt_kernel, nchunks=tt_rows // CHUNK),
        grid=(B, NP, nt),
        in_specs=[col] * 7 + [par] * 3,
        out_specs=[col, pl.BlockSpec((None, None, LANES, LANES), lambda b, p, t: (b, p, 0, 0))],
        out_shape=[jax.ShapeDtypeStruct((R, D), F32),
                   jax.ShapeDtypeStruct((B, NP, LANES, LANES), F32)],
        scratch_shapes=[pltpu.VMEM((LANES, LANES), F32)],
        compiler_params=_cparams(("parallel", "parallel", "arbitrary")),
        name="rwkv_prompt",
    )(*arrs, g, lnw.reshape(1, D), lnb.reshape(1, D), rk.reshape(1, D))
    return o, s


def _rwkv_sample_kernel(r_ref, lw_ref, k_ref, v_ref, kap_ref, bet_ref, g_ref, lnw_ref, lnb_ref, rk_ref,
                        s0_ref, o_ref, s_ref, *, nseq):
    P = SEQ_PAD
    C = nseq * P
    C2 = 2 * C
    mA, mB = _pair_masks()
    bdm = (_iota2((LANES, LANES), 0) // RWKV_N == _iota2((LANES, LANES), 1) // RWKV_N).astype(F32)
    rr, cc = _iota2((C2, C2), 0), _iota2((C2, C2), 1)
    eye = (rr == cc).astype(F32)
    same = (rr // P) == (cc // P)
    strict = same & (rr > cc)
    incl = same & (rr >= cc)
    r1, c1 = _iota2((C, C), 0), _iota2((C, C), 1)
    seg_incl = ((r1 // P == c1 // P) & (r1 >= c1)).astype(F32)
    seg_all = (r1 // P == c1 // P).astype(F32)

    r, lw, k, v, kap, bet = r_ref[...], lw_ref[...], k_ref[...], v_ref[...], kap_ref[...], bet_ref[...]
    cs = _dot_exact_lhs(seg_incl, lw)
    cs_end = _dot_exact_lhs(seg_all, lw)
    rg, kg, kd, bd, kh, bh = _rwkv_scores(cs, lw, cs_end, r, k, kap, bet, mA, mB)
    vs = _stack(v, mA, mB)
    Ak = jnp.where(strict, _dot(kg, kd, _NT), 0.0)
    Ab = jnp.where(strict, _dot(kg, bd, _NT), 0.0)
    Bk = jnp.where(incl, _dot(rg, kd, _NT), 0.0)
    Bb = jnp.where(incl, _dot(rg, bd, _NT), 0.0)
    Tm = _neumann_inv(Ab, eye, 2)
    kgp = kg[:C, :] + kg[C:, :]
    rgp = rg[:C, :] + rg[C:, :]
    ks, rs = [], []
    for n in range(nseq):
        sl = slice(n * P, (n + 1) * P)
        S = s0_ref[n]
        ks.append(_dot(kgp[sl], S, _NT))
        rs.append(_dot(rgp[sl], S, _NT))
    KS = _stack(jnp.concatenate(ks, axis=0), mA, mB)
    RS = _stack(jnp.concatenate(rs, axis=0), mA, mB)
    rhs = KS + _dot(Ak, vs)
    U = _dot_hp(Tm, rhs)
    O = RS + _dot(Bk, vs) - _dot(Bb, U)
    Up = U[:C, :] + U[C:, :]
    khp = kh[:C, :] + kh[C:, :]
    bhp = bh[:C, :] + bh[C:, :]
    ecs = jnp.exp(cs_end)
    for n in range(nseq):
        sl = slice(n * P, (n + 1) * P)
        upd = _dot(v[sl], khp[sl], _TN) - _dot(Up[sl], bhp[sl], _TN)
        s_ref[n] = s0_ref[n] * ecs[n * P:n * P + 1, :] + upd * bdm
    o = O[:C, :] + O[C:, :]
    o_ref[...] = _rwkv_epilogue(o, r, k, v, g_ref[...], lnw_ref[...], lnb_ref[...], rk_ref[...])


def _rwkv_sample(arrs, g, lnw, lnb, rk, s0_bd, nseq=8):
    R, D = arrs[0].shape
    NP = D // LANES
    B = R // SEQ_PAD
    rows = nseq * SEQ_PAD
    col = pl.BlockSpec((rows, LANES), lambda p, gi: (gi, p))
    par = pl.BlockSpec((1, LANES), lambda p, gi: (0, p))
    sspec = pl.BlockSpec((nseq, None, LANES, LANES), lambda p, gi: (gi, p, 0, 0))
    o, s = pl.pallas_call(
        functools.partial(_rwkv_sample_kernel, nseq=nseq),
        grid=(NP, B // nseq),
        in_specs=[col] * 7 + [par] * 3 + [sspec],
        out_specs=[col, sspec],
        out_shape=[jax.ShapeDtypeStruct((R, D), F32), jax.ShapeDtypeStruct(s0_bd.shape, F32)],
        compiler_params=_cparams(("parallel", "parallel")),
        name="rwkv_sample",
    )(*arrs, g, lnw.reshape(1, D), lnb.reshape(1, D), rk.reshape(1, D), s0_bd)
    return o, s


def _merge_kernel(x_ref, ga_ref, gb_ref, oa_ref, ob_ref, w_ref, gt_ref, o_ref):
    m = _sigmoid(ga_ref[...]) * oa_ref[...] + _sigmoid(gb_ref[...]) * ob_ref[...]
    o_ref[...] = x_ref[...] + gt_ref[...] * _dot(m, w_ref[...])


def _merge_out(x, P_g, o_a, o_b, w_out, mod, tm):
    R, D = x.shape
    row = lambda j: pl.BlockSpec((tm, D), lambda i: (i, j))
    return pl.pallas_call(
        _merge_kernel,
        grid=(R // tm,),
        in_specs=[row(0), row(0), row(1), row(0), row(0),
                  pl.BlockSpec((D, D), lambda i: (0, 0)), mod.spec(tm, 2)],
        out_specs=row(0),
        out_shape=jax.ShapeDtypeStruct((R, D), F32),
        compiler_params=_cparams(("parallel",)),
        name="merge_out",
    )(x, P_g, P_g, o_a, o_b, w_out, mod.arr)


def _mlp_kernel(x_ref, nw_ref, sh_ref, sc_ref, gt_ref, wu_ref, wd_ref, fw_ref, o_ref, h_ref, acc_ref):
    kf = pl.program_id(1)

    @pl.when(kf == 0)
    def _():
        h = _rms(x_ref[...], nw_ref[...]) * (1.0 + sc_ref[...]) + sh_ref[...]
        h_ref[...] = h.astype(BF16)
        acc_ref[...] = jnp.zeros_like(acc_ref)

    u = jnp.dot(h_ref[...], wu_ref[...], preferred_element_type=F32)
    u = jnp.square(jnp.maximum(u, 0.0))
    acc_ref[...] += _dot(u, wd_ref[...])

    @pl.when(kf == pl.num_programs(1) - 1)
    def _():
        y = x_ref[...] + gt_ref[...] * acc_ref[...]
        o_ref[...] = _rms(y, fw_ref[...])


def _mlp(x, nw, w_up, w_down, fw, mod, tm, tf=1024):
    R, D = x.shape
    DF = w_up.shape[1]
    return pl.pallas_call(
        _mlp_kernel,
        grid=(R // tm, DF // tf),
        in_specs=[pl.BlockSpec((tm, D), lambda i, kf: (i, 0)),
                  pl.BlockSpec((1, D), lambda i, kf: (0, 0)),
                  mod.spec(tm, 3), mod.spec(tm, 4), mod.spec(tm, 5),
                  pl.BlockSpec((D, tf), lambda i, kf: (0, kf)),
                  pl.BlockSpec((tf, D), lambda i, kf: (kf, 0)),
                  pl.BlockSpec((1, D), lambda i, kf: (0, 0))],
        out_specs=pl.BlockSpec((tm, D), lambda i, kf: (i, 0)),
        out_shape=jax.ShapeDtypeStruct((R, D), F32),
        scratch_shapes=[pltpu.VMEM((tm, D), BF16), pltpu.VMEM((tm, D), F32)],
        compiler_params=_cparams(("parallel", "arbitrary")),
        name="mlp",
    )(x, nw.reshape(1, D), mod.arr, mod.arr, mod.arr, w_up, w_down, fw.reshape(1, D))


def _pad_rows(a, T, P):
    W = a.shape[1]
    return jnp.pad(a.reshape(-1, T, W), ((0, 0), (0, P - T), (0, 0))).reshape(-1, W)


def _unpad_rows(a, T, P):
    W = a.shape[1]
    return a.reshape(-1, P, W)[:, :T].reshape(-1, W)


def _bd_from_heads(s):
    B, H, N, _ = s.shape
    s = s.reshape(B, H // 2, 2, N, N)
    z = jnp.zeros_like(s[:, :, 0])
    top = jnp.concatenate([s[:, :, 0], z], axis=-1)
    bot = jnp.concatenate([z, s[:, :, 1]], axis=-1)
    return jnp.concatenate([top, bot], axis=-2)


def _heads_from_bd(s):
    B, NP, _, _ = s.shape
    N = RWKV_N
    return jnp.stack([s[:, :, :N, :N], s[:, :, N:, N:]], axis=2).reshape(B, 2 * NP, N, N)


def _run(x3, mod_rows, s_hgrn, s_rwkv, s_shift, W, prompt):
    B, T, D = x3.shape
    R = B * T
    x = x3.reshape(R, D)
    if prompt:
        mod = _Mod(mod_rows.reshape(B, 1, -1), True, T, D)
        tm = min(1024, T)
    else:
        mod = _Mod(jnp.repeat(mod_rows, T, axis=0), False, T, D)
        tm = min(512, R)

    h = _norm_mod(x, W["norm1_w"], mod, 0, 1, min(tm, 512))
    P_r = _matmul(h, W["w_r"], tm, RC_PAD // 3, "proj_r")
    P_h = _matmul(h, W["w_h"], tm, 2048, "proj_h")
    P_g = _matmul(h, W["w_g"], tm, 2048, "proj_g")
    shift_out = P_r.reshape(B, T, RC_PAD)[:, T - 1, :RC_COLS]

    rw = (W["mu"], W["w0"], W["a0"], W["k_k"], W["k_a"], W["wwa"], W["g2p"])
    if prompt:
        o_a, hg = _hgrn_prompt(P_h, W["lb_logits"], W["hgrn_norm_w"], B, T)
        shift_in = jnp.zeros((B, 1, RC_PAD), F32)
        arrs = _rwkv_prep(P_r, shift_in, rw, True, T)
        o_b, rs_bd = _rwkv_prompt(arrs[:6], arrs[6], W["lnx_w"], W["lnx_b"], W["r_k"], B, T)
    else:
        o_a, hg = _hgrn_sample(_pad_rows(P_h, T, SEQ_PAD), W["lb_logits"], W["hgrn_norm_w"], s_hgrn, T)
        o_a = _unpad_rows(o_a, T, SEQ_PAD)
        shift_rows = jnp.repeat(jnp.pad(s_shift, ((0, 0), (0, RC_PAD - RC_COLS))), T, axis=0)
        arrs = _rwkv_prep(P_r, shift_rows, rw, False, T)
        arrs = [_pad_rows(a, T, SEQ_PAD) for a in arrs]
        o_b, rs_bd = _rwkv_sample(arrs[:6], arrs[6], W["lnx_w"], W["lnx_b"], W["r_k"],
                                  _bd_from_heads(s_rwkv))
        o_b = _unpad_rows(o_b, T, SEQ_PAD)

    x1 = _merge_out(x, P_g, o_a, o_b, W["w_out"], mod, min(tm, 512))
    y = _mlp(x1, W["norm2_w"], W["w_up"], W["w_down"], W["final_norm_w"], mod, min(tm, 512))
    return y.reshape(B, T, D), hg, _heads_from_bd(rs_bd), shift_out


def kernel(x_prompt, x_sample, state_hgrn, state_rwkv, state_shift, c_prompt, c_sample, norm1_w, norm2_w, ada_w, ada_b, w_in, lb_logits, hgrn_norm_w, rwkv_mu, rwkv_w0, rwkv_w2, rwkv_a0, rwkv_a2, rwkv_g2, rwkv_k_k, rwkv_k_a, rwkv_r_k, rwkv_lnx_w, rwkv_lnx_b, w_out, w_up, w_down, final_norm_w):
    depth = norm1_w.shape[0]
    assert depth == 1, "single-layer step"
    D = x_prompt.shape[-1]
    HW = 4 * D
    w_in0 = w_in[0]
    row = lambda a: a.reshape(1, -1)
    wwa = jnp.zeros((LANES, 2 * D), F32)
    wwa = wwa.at[:W_LORA, :D].set(rwkv_w2[0]).at[W_LORA:W_LORA + A_LORA, D:].set(rwkv_a2[0])
    W = {
        "norm1_w": norm1_w[0], "norm2_w": norm2_w[0], "final_norm_w": final_norm_w,
        "w_h": w_in0[:, :HW].astype(BF16),
        "w_r": jnp.pad(w_in0[:, HW:HW + RC_COLS], ((0, 0), (0, RC_PAD - RC_COLS))).astype(BF16),
        "w_g": w_in0[:, HW + RC_COLS:].astype(BF16),
        "lb_logits": lb_logits, "hgrn_norm_w": hgrn_norm_w[0],
        "mu": jnp.pad(row(rwkv_mu[0]), ((0, 0), (0, RC_PAD - RC_COLS))),
        "w0": row(rwkv_w0[0]), "a0": row(rwkv_a0[0]), "k_k": row(rwkv_k_k[0]), "k_a": row(rwkv_k_a[0]),
        "wwa": wwa.astype(BF16),
        "g2p": jnp.pad(rwkv_g2[0], ((0, G_LORA_PAD - G_LORA), (0, 0))).astype(BF16),
        "lnx_w": rwkv_lnx_w[0], "lnx_b": rwkv_lnx_b[0], "r_k": rwkv_r_k[0].reshape(-1),
        "w_out": w_out[0].astype(BF16), "w_up": w_up[0].astype(BF16), "w_down": w_down[0].astype(BF16),
    }
    nb = x_prompt.shape[0]
    mod = _adaln(jnp.concatenate([c_prompt, c_sample], axis=0), ada_w[0], ada_b[0])
    y_p, hg_p, rw_p, sh_p = _run(x_prompt, mod[:nb], None, None, None, W, True)
    y_s, hg_s, rw_s, sh_s = _run(x_sample, mod[nb:], state_hgrn[0], state_rwkv[0], state_shift[0], W, False)
    ex = lambda a: a[None]
    return (y_p, y_s, ex(hg_p), ex(rw_p), ex(sh_p), ex(hg_s), ex(rw_s), ex(sh_s))
```

```python
import functools
import math

import jax
import jax.numpy as jnp
from jax import lax
from jax.experimental import pallas as pl
from jax.experimental.pallas import tpu as pltpu

F32 = jnp.float32
BF16 = jnp.bfloat16

LANES = 128
SUBLANES = 8
VMEM_LIMIT = 48 * 1024 * 1024

HGRN_DK = 128
RWKV_N = 64
W_LORA = 64
A_LORA = 64
G_LORA = 160
G_LORA_PAD = 256
RC_MAIN = 3 * 1024
RC_COLS = RC_MAIN + W_LORA + A_LORA + G_LORA
RC_PAD = RC_MAIN + LANES + G_LORA_PAD
RMS_EPS = 1e-6
LNX_EPS = 64e-5
CHUNK = 64
SUB = 16
SEQ_PAD = 8


def _cparams(sem):
    return pltpu.CompilerParams(dimension_semantics=sem, vmem_limit_bytes=VMEM_LIMIT)


_NN = (((1,), (0,)), ((), ()))
_NT = (((1,), (1,)), ((), ()))
_TN = (((0,), (0,)), ((), ()))


def _dot(a, b, dims=_NN):
    return lax.dot_general(a.astype(BF16), b.astype(BF16), dims, preferred_element_type=F32)


def _split2(x):
    hi = x.astype(BF16)
    lo = (x - hi.astype(F32)).astype(BF16)
    return hi, lo


def _split3(x):
    hi = x.astype(BF16)
    r1 = x - hi.astype(F32)
    mid = r1.astype(BF16)
    lo = (r1 - mid.astype(F32)).astype(BF16)
    return hi, mid, lo


def _dot_exact_lhs(a, x, dims=_NN):
    a = a.astype(BF16)
    hi, mid, lo = _split3(x)
    d = lambda p: lax.dot_general(a, p, dims, preferred_element_type=F32)
    return d(hi) + d(mid) + d(lo)


def _dot_exact_rhs(x, b, dims=_NN):
    b = b.astype(BF16)
    hi, mid, lo = _split3(x)
    d = lambda p: lax.dot_general(p, b, dims, preferred_element_type=F32)
    return d(hi) + d(mid) + d(lo)


def _dot_hp(a, b, dims=_NN):
    ah, al = _split2(a)
    bh, bl = _split2(b)
    d = lambda p, q: lax.dot_general(p, q, dims, preferred_element_type=F32)
    return d(ah, bh) + d(ah, bl) + d(al, bh)


def _sigmoid(x):
    return 1.0 / (1.0 + jnp.exp(-x))


def _iota2(shape, dim):
    return lax.broadcasted_iota(jnp.int32, shape, dim)


def _neumann_inv(L, eye, nsq):
    X = L
    P = eye - L
    for _ in range(nsq):
        X = _dot_hp(X, X)
        P = P + _dot_hp(P, X)
    return P


def _adaln_kernel(c_ref, w_ref, b_ref, o_ref):
    c = c_ref[...]
    s = c * _sigmoid(c)
    o_ref[...] = _dot(s, w_ref[...]) + b_ref[...]


def _adaln(c_all, ada_w, ada_b, tn=1536):
    R, D = c_all.shape
    N = ada_w.shape[1]
    return pl.pallas_call(
        _adaln_kernel,
        grid=(N // tn,),
        in_specs=[pl.BlockSpec((R, D), lambda j: (0, 0)),
                  pl.BlockSpec((D, tn), lambda j: (0, j)),
                  pl.BlockSpec((1, tn), lambda j: (0, j))],
        out_specs=pl.BlockSpec((R, tn), lambda j: (0, j)),
        out_shape=jax.ShapeDtypeStruct((R, N), F32),
        compiler_params=_cparams(("parallel",)),
        name="adaln",
    )(c_all, ada_w, ada_b.reshape(1, N))


class _Mod:
    def __init__(self, arr, per_seq, seq_rows, D):
        self.arr, self.per_seq, self.seq_rows, self.D = arr, per_seq, seq_rows, D

    def spec(self, tm, comp):
        D = self.D
        if self.per_seq:
            tiles = self.seq_rows // tm
            return pl.BlockSpec((None, 1, D), lambda i, *_: (i // tiles, 0, comp))
        return pl.BlockSpec((tm, D), lambda i, *_: (i, comp))


def _rms(x, w):
    return x * lax.rsqrt(jnp.mean(x * x, axis=-1, keepdims=True) + RMS_EPS) * w


def _norm_mod_kernel(x_ref, nw_ref, sh_ref, sc_ref, o_ref):
    h = _rms(x_ref[...], nw_ref[...]) * (1.0 + sc_ref[...]) + sh_ref[...]
    o_ref[...] = h.astype(o_ref.dtype)


def _norm_mod(x, nw, mod, sh_comp, sc_comp, tm):
    R, D = x.shape
    return pl.pallas_call(
        _norm_mod_kernel,
        grid=(R // tm,),
        in_specs=[pl.BlockSpec((tm, D), lambda i: (i, 0)),
                  pl.BlockSpec((1, D), lambda i: (0, 0)),
                  mod.spec(tm, sh_comp), mod.spec(tm, sc_comp)],
        out_specs=pl.BlockSpec((tm, D), lambda i: (i, 0)),
        out_shape=jax.ShapeDtypeStruct((R, D), BF16),
        compiler_params=_cparams(("parallel",)),
        name="norm_mod",
    )(x, nw.reshape(1, D), mod.arr, mod.arr)


def _mm_kernel(x_ref, w_ref, o_ref):
    o_ref[...] = jnp.dot(x_ref[...], w_ref[...], preferred_element_type=F32)


def _matmul(x, w, tm, tn, name):
    R, K = x.shape
    N = w.shape[1]
    return pl.pallas_call(
        _mm_kernel,
        grid=(R // tm, N // tn),
        in_specs=[pl.BlockSpec((tm, K), lambda i, j: (i, 0)),
                  pl.BlockSpec((K, tn), lambda i, j: (0, j))],
        out_specs=pl.BlockSpec((tm, tn), lambda i, j: (i, j)),
        out_shape=jax.ShapeDtypeStruct((R, N), F32),
        compiler_params=_cparams(("parallel", "arbitrary")),
        name=name,
    )(x, w)


def _hgrn_gates(fp, lb_logits):
    m = jnp.max(lb_logits, axis=0, keepdims=True)
    e = jnp.exp(lb_logits - m)
    lb = e[0:1, :] / jnp.sum(e, axis=0, keepdims=True)
    f = lb + (1.0 - lb) * _sigmoid(fp)
    return jnp.log(f), 1.0 - f


def _hgrn_diag(q, kin, b, c):
    R = q.shape[0]
    nb = R // c
    q3 = q.reshape(nb, c, LANES)
    k3 = kin.reshape(nb, c, LANES)
    b3 = b.reshape(nb, c, LANES)
    tpos = lax.broadcasted_iota(jnp.int32, (nb, c, LANES), 1)
    lane_mod = lax.broadcasted_iota(jnp.int32, (nb, c, R), 2) % c
    acc = jnp.zeros((nb, c, R), F32)
    for s in range(c):
        d = jnp.where(tpos >= s, b3 - b3[:, s:s + 1, :], -1e30)
        col = jnp.sum(q3 * k3[:, s:s + 1, :] * jnp.exp(d), axis=-1, keepdims=True)
        acc = jnp.where(lane_mod == s, col, acc)
    same_blk = (_iota2((R, R), 0) // c) == (_iota2((R, R), 1) // c)
    return jnp.where(same_blk, acc.reshape(R, R), 0.0)


def _hgrn_out(o, og, nw):
    o = o * lax.rsqrt(jnp.mean(o * o, axis=-1, keepdims=True) + RMS_EPS) * nw
    return o * (og * _sigmoid(og))


def _hgrn_prompt_kernel(q_ref, f_ref, i_ref, g_ref, lb_ref, nw_ref, o_ref, s_ref, st_ref, *, nchunks, hp):
    tt = pl.program_id(2)

    @pl.when(tt == 0)
    def _():
        st_ref[...] = jnp.zeros_like(st_ref)

    C, c = CHUNK, SUB
    tri = (_iota2((C, C), 0) >= _iota2((C, C), 1)).astype(F32)
    nw = nw_ref[...]

    def chunk(ci, carry):
        r0 = pl.multiple_of(ci * C, C)
        H = range(hp)
        cols = [slice(j * LANES, (j + 1) * LANES) for j in H]
        q = [q_ref[pl.ds(r0, C), cols[j]] for j in H]
        v = [i_ref[pl.ds(r0, C), cols[j]] for j in H]
        gates = [_hgrn_gates(f_ref[pl.ds(r0, C), cols[j]], lb_ref[:, cols[j]]) for j in H]
        lf = [gt[0] for gt in gates]
        kin = [gt[1] for gt in gates]
        b = [_dot_exact_lhs(tri, lf[j]) for j in H]
        b_last = [b[j][C - 1:C, :] for j in H]
        ST = [st_ref[j] for j in H]
        o = [_dot(q[j] * jnp.exp(b[j]), ST[j], _NT) for j in H]
        for j in H:
            st_ref[j] = ST[j] * jnp.exp(b_last[j]) + _dot(v[j], kin[j] * jnp.exp(b_last[j] - b[j]), _TN)
        for J in range(C // c - 1):
            lo, hi = J * c, (J + 1) * c
            sc = []
            for j in H:
                ref = b[j][hi - 1:hi, :]
                kJ = kin[j][lo:hi, :] * jnp.exp(ref - b[j][lo:hi, :])
                qR = q[j][hi:, :] * jnp.exp(b[j][hi:, :] - ref)
                sc.append(_dot(qR, kJ, _NT))
            for j in H:
                contrib = _dot(sc[j], v[j][lo:hi, :])
                o[j] = o[j] + jnp.concatenate([jnp.zeros((hi, LANES), F32), contrib], axis=0)
        dg = [_hgrn_diag(q[j], kin[j], b[j], c) for j in H]
        o = [o[j] + _dot(dg[j], v[j]) for j in H]
        for j in H:
            o_ref[pl.ds(r0, C), cols[j]] = _hgrn_out(o[j], g_ref[pl.ds(r0, C), cols[j]], nw)
        return carry

    lax.fori_loop(0, nchunks, chunk, 0)

    @pl.when(tt == pl.num_programs(2) - 1)
    def _():
        for j in range(hp):
            s_ref[j] = st_ref[j].T


def _hgrn_prompt(P_h, lb_logits, nw, B, T, tt_rows=512, hp=8):
    R, W = P_h.shape
    H = W // (4 * HGRN_DK)
    HG = H // hp
    wd = hp * HGRN_DK
    tt_rows = min(tt_rows, T)
    nt = T // tt_rows
    col = lambda off: pl.BlockSpec((tt_rows, wd), lambda b, h, t: (b * nt + t, off * HG + h))
    o, s = pl.pallas_call(
        functools.partial(_hgrn_prompt_kernel, nchunks=tt_rows // CHUNK, hp=hp),
        grid=(B, HG, nt),
        in_specs=[col(0), col(1), col(2), col(3),
                  pl.BlockSpec((lb_logits.shape[0], wd), lambda b, h, t: (0, h)),
                  pl.BlockSpec((1, HGRN_DK), lambda b, h, t: (0, 0))],
        out_specs=[pl.BlockSpec((tt_rows, wd), lambda b, h, t: (b * nt + t, h)),
                   pl.BlockSpec((None, hp, HGRN_DK, HGRN_DK), lambda b, h, t: (b, h, 0, 0))],
        out_shape=[jax.ShapeDtypeStruct((R, H * HGRN_DK), F32),
                   jax.ShapeDtypeStruct((B, H, HGRN_DK, HGRN_DK), F32)],
        scratch_shapes=[pltpu.VMEM((hp, HGRN_DK, HGRN_DK), F32)],
        compiler_params=_cparams(("parallel", "parallel", "arbitrary")),
        name="hgrn_prompt",
    )(P_h, P_h, P_h, P_h, lb_logits, nw.reshape(1, HGRN_DK))
    return o, s


def _hgrn_sample_kernel(q_ref, f_ref, i_ref, g_ref, lb_ref, nw_ref, s0_ref, o_ref, s_ref, *, nseq, tvalid):
    P = SEQ_PAD
    R = nseq * P
    valid = (_iota2((R, 1), 0) % P) < tvalid
    q = q_ref[...]
    v = i_ref[...]
    lf, kin = _hgrn_gates(f_ref[...], lb_ref[...])
    lf = jnp.where(valid, lf, 0.0)
    kin = jnp.where(valid, kin, 0.0)
    rr, cc = _iota2((R, R), 0), _iota2((R, R), 1)
    seg_incl = ((rr // P == cc // P) & (rr >= cc)).astype(F32)
    seg_all = (rr // P == cc // P).astype(F32)
    b = _dot_exact_lhs(seg_incl, lf)
    b_tot = _dot_exact_lhs(seg_all, lf)
    o = _dot(_hgrn_diag(q, kin, b, P), v)
    qg = q * jnp.exp(b)
    kh = kin * jnp.exp(b_tot - b)
    eb = jnp.exp(b_tot)
    pieces = []
    for n in range(nseq):
        sl = slice(n * P, (n + 1) * P)
        ST = s0_ref[n].T
        pieces.append(_dot(qg[sl], ST, _NT))
        STn = ST * eb[n * P:n * P + 1, :] + _dot(v[sl], kh[sl], _TN)
        s_ref[n] = STn.T
    o = o + jnp.concatenate(pieces, axis=0)
    o_ref[...] = _hgrn_out(o, g_ref[...], nw_ref[...])


def _hgrn_sample(P_h, lb_logits, nw, s0, tvalid, nseq=8):
    R, W = P_h.shape
    H = W // (4 * HGRN_DK)
    B = R // SEQ_PAD
    rows = nseq * SEQ_PAD
    col = lambda off: pl.BlockSpec((rows, HGRN_DK), lambda h, g: (g, off * H + h))
    sspec = pl.BlockSpec((nseq, None, HGRN_DK, HGRN_DK), lambda h, g: (g, h, 0, 0))
    o, s = pl.pallas_call(
        functools.partial(_hgrn_sample_kernel, nseq=nseq, tvalid=tvalid),
        grid=(H, B // nseq),
        in_specs=[col(0), col(1), col(2), col(3),
                  pl.BlockSpec((lb_logits.shape[0], HGRN_DK), lambda h, g: (0, h)),
                  pl.BlockSpec((1, HGRN_DK), lambda h, g: (0, 0)),
                  sspec],
        out_specs=[pl.BlockSpec((rows, HGRN_DK), lambda h, g: (g, h)), sspec],
        out_shape=[jax.ShapeDtypeStruct((R, H * HGRN_DK), F32),
                   jax.ShapeDtypeStruct(s0.shape, F32)],
        compiler_params=_cparams(("parallel", "parallel")),
        name="hgrn_sample",
    )(P_h, P_h, P_h, P_h, lb_logits, nw.reshape(1, HGRN_DK), s0)
    return o, s


def _rwkv_prep_body(rc, prev, mu_ref, w0_ref, a0_ref, kk_ref, ka_ref, wwa_ref, g2_ref, outs):
    r_o, lw_o, k_o, v_o, kap_o, bet_o, g_o = outs
    D = RC_MAIN // 3
    xs = rc + mu_ref[...] * (prev - rc)
    r = xs[:, 0:D]
    kb = xs[:, D:2 * D]
    v = xs[:, 2 * D:3 * D]
    xwa = xs[:, RC_MAIN:RC_MAIN + LANES]
    xg = xs[:, RC_MAIN + LANES:]
    lane = _iota2((1, LANES), 1)
    wa = _dot(jnp.where(lane < W_LORA, jnp.tanh(xwa), xwa), wwa_ref[...])
    w = w0_ref[...] + wa[:, :D]
    a = _sigmoid(a0_ref[...] + wa[:, D:])
    lw = -math.exp(-0.5) * _sigmoid(w)
    g = _dot(_sigmoid(xg), g2_ref[...])
    kk = kb * kk_ref[...]
    seg = (_iota2((LANES, LANES), 0) // RWKV_N == _iota2((LANES, LANES), 1) // RWKV_N).astype(F32)
    k_a = ka_ref[...]
    for p in range(D // LANES):
        sl = slice(p * LANES, (p + 1) * LANES)
        kkp = kk[:, sl]
        nrm = jnp.maximum(jnp.sqrt(_dot_exact_rhs(kkp * kkp, seg)), 1e-12)
        kap = kkp / nrm
        ap = a[:, sl]
        kap_o[:, sl] = kap
        bet_o[:, sl] = ap * kap
        k_o[:, sl] = kb[:, sl] * (1.0 + (ap - 1.0) * k_a[:, sl])
    r_o[...] = r
    lw_o[...] = lw
    v_o[...] = v
    g_o[...] = g


def _rwkv_prep_prompt_kernel(p_ref, prev8_ref, shift_ref, mu_ref, w0_ref, a0_ref, kk_ref, ka_ref,
                             wwa_ref, g2_ref, *outs, tiles_per_seq):
    i = pl.program_id(0)
    rc = p_ref[...]
    tm = rc.shape[0]
    rolled = pltpu.roll(rc, 1, axis=0)
    start = (i % tiles_per_seq) == 0
    first = jnp.where(start, shift_ref[...], prev8_ref[SUBLANES - 1:SUBLANES, :])
    prev = jnp.where(_iota2((tm, 1), 0) == 0, first, rolled)
    _rwkv_prep_body(rc, prev, mu_ref, w0_ref, a0_ref, kk_ref, ka_ref, wwa_ref, g2_ref, outs)


def _rwkv_prep_sample_kernel(p_ref, shift_ref, mu_ref, w0_ref, a0_ref, kk_ref, ka_ref,
                             wwa_ref, g2_ref, *outs, seq_rows):
    rc = p_ref[...]
    tm = rc.shape[0]
    rolled = pltpu.roll(rc, 1, axis=0)
    prev = jnp.where((_iota2((tm, 1), 0) % seq_rows) == 0, shift_ref[...], rolled)
    _rwkv_prep_body(rc, prev, mu_ref, w0_ref, a0_ref, kk_ref, ka_ref, wwa_ref, g2_ref, outs)


def _rwkv_prep(P_r, shift, params, per_seq, seq_rows, tm=256):
    R = P_r.shape[0]
    D = RC_MAIN // 3
    tm = min(tm, seq_rows) if per_seq else min(tm, R)
    mu, w0, a0, k_k, k_a, wwa, g2p = params
    full = lambda a: pl.BlockSpec(a.shape, lambda i: (0,) * a.ndim)
    wspecs = [full(mu), full(w0), full(a0), full(k_k), full(k_a), full(wwa), full(g2p)]
    out_spec = pl.BlockSpec((tm, D), lambda i: (i, 0))
    out_shape = jax.ShapeDtypeStruct((R, D), F32)
    if per_seq:
        tiles = seq_rows // tm
        sub = tm // SUBLANES
        kern = functools.partial(_rwkv_prep_prompt_kernel, tiles_per_seq=tiles)
        in_specs = [pl.BlockSpec((tm, RC_PAD), lambda i: (i, 0)),
                    pl.BlockSpec((SUBLANES, RC_PAD), lambda i: (jnp.maximum(i * sub - 1, 0), 0)),
                    pl.BlockSpec((None, 1, RC_PAD), lambda i: (i // tiles, 0, 0))] + wspecs
        args = (P_r, P_r, shift)
    else:
        kern = functools.partial(_rwkv_prep_sample_kernel, seq_rows=seq_rows)
        in_specs = [pl.BlockSpec((tm, RC_PAD), lambda i: (i, 0)),
                    pl.BlockSpec((tm, RC_PAD), lambda i: (i, 0))] + wspecs
        args = (P_r, shift)
    return pl.pallas_call(
        kern,
        grid=(R // tm,),
        in_specs=in_specs,
        out_specs=[out_spec] * 7,
        out_shape=[out_shape] * 7,
        compiler_params=_cparams(("parallel",)),
        name="rwkv_prep",
    )(*args, mu, w0, a0, k_k, k_a, wwa, g2p)


def _pair_masks():
    lane = _iota2((1, LANES), 1)
    return (lane < RWKV_N).astype(F32), (lane >= RWKV_N).astype(F32)


def _stack(x, mA, mB):
    return jnp.concatenate([x * mA, x * mB], axis=0)


def _rwkv_scores(cs, lw, cs_end, r, k, kap, bet, mA, mB):
    e_neg = jnp.exp(-cs)
    e_rem = jnp.exp(cs_end - cs)
    rg = _stack(r * jnp.exp(cs), mA, mB)
    kg = _stack(kap * jnp.exp(cs - lw), mA, mB)
    kd = _stack(k * e_neg, mA, mB)
    bd = _stack(bet * e_neg, mA, mB)
    kh = _stack(k * e_rem, mA, mB)
    bh = _stack(bet * e_rem, mA, mB)
    return rg, kg, kd, bd, kh, bh


def _rwkv_epilogue(o, r, k, v, g, lnw, lnb, rk):
    avg = jnp.where(_iota2((LANES, LANES), 0) // RWKV_N == _iota2((LANES, LANES), 1) // RWKV_N,
                    1.0 / RWKV_N, 0.0).astype(F32)
    mean = _dot_exact_rhs(o, avg)
    d = o - mean
    var = _dot_exact_rhs(d * d, avg)
    on = d * lax.rsqrt(var + LNX_EPS) * lnw + lnb
    bonus = _dot_exact_rhs(r * k * rk, avg) * float(RWKV_N) * v
    return (on + bonus) * g


def _rwkv_prompt_kernel(r_ref, lw_ref, k_ref, v_ref, kap_ref, bet_ref, g_ref, lnw_ref, lnb_ref, rk_ref,
                        o_ref, s_ref, st_ref, *, nchunks, hp):
    tt = pl.program_id(2)

    @pl.when(tt == 0)
    def _():
        st_ref[...] = jnp.zeros_like(st_ref)

    C, c = CHUNK, SUB
    C2 = 2 * C
    mA, mB = _pair_masks()
    rr, cc = _iota2((C2, C2), 0), _iota2((C2, C2), 1)
    eye = (rr == cc).astype(F32)
    strict = rr > cc
    incl = rr >= cc
    blk = (rr // c) == (cc // c)
    tri = (_iota2((C, C), 0) >= _iota2((C, C), 1)).astype(F32)

    def chunk(ci, carry):
        r0 = pl.multiple_of(ci * C, C)
        J = range(hp)
        cols = [slice(j * LANES, (j + 1) * LANES) for j in J]
        ld = lambda ref: [ref[pl.ds(r0, C), cols[j]] for j in J]
        r, lw, k, v, kap, bet = ld(r_ref), ld(lw_ref), ld(k_ref), ld(v_ref), ld(kap_ref), ld(bet_ref)
        cs = [_dot_exact_lhs(tri, lw[j]) for j in J]
        cs_end = [cs[j][C - 1:C, :] for j in J]
        sc = [_rwkv_scores(cs[j], lw[j], cs_end[j], r[j], k[j], kap[j], bet[j], mA, mB) for j in J]
        rg, kg, kd, bd, kh, bh = ([s[i] for s in sc] for i in range(6))
        vs = [_stack(v[j], mA, mB) for j in J]
        Ab = [jnp.where(strict, _dot(kg[j], bd[j], _NT), 0.0) for j in J]
        Ak = [jnp.where(strict, _dot(kg[j], kd[j], _NT), 0.0) for j in J]
        Bk = [jnp.where(incl, _dot(rg[j], kd[j], _NT), 0.0) for j in J]
        Bb = [jnp.where(incl, _dot(rg[j], bd[j], _NT), 0.0) for j in J]
        LD = [jnp.where(blk, Ab[j], 0.0) for j in J]
        X = LD
        Dinv = [eye - LD[j] for j in J]
        for _ in range(3):
            X = [_dot_hp(X[j], X[j]) for j in J]
            Dinv = [Dinv[j] + _dot_hp(Dinv[j], X[j]) for j in J]
        Q = [_dot_hp(Dinv[j], Ab[j] - LD[j]) for j in J]
        Q2 = [_dot_hp(Q[j], Q[j]) for j in J]
        Q3 = [_dot_hp(Q[j], Q2[j]) for j in J]
        Tm = [_dot_hp((eye - Q[j]) + (Q2[j] - Q3[j]), Dinv[j]) for j in J]
        S = [st_ref[j] for j in J]
        rhs = [_dot(kg[j], S[j], _NT) + _dot(Ak[j], vs[j]) for j in J]
        U = [_dot_hp(Tm[j], rhs[j]) for j in J]
        for j in J:
            st_ref[j] = S[j] * jnp.exp(cs_end[j]) + _dot(vs[j], kh[j], _TN) - _dot(U[j], bh[j], _TN)
        O = [_dot(rg[j], S[j], _NT) + _dot(Bk[j], vs[j]) - _dot(Bb[j], U[j]) for j in J]
        for j in J:
            o = O[j][:C, :] + O[j][C:, :]
            o_ref[pl.ds(r0, C), cols[j]] = _rwkv_epilogue(
                o, r[j], k[j], v[j], g_ref[pl.ds(r0, C), cols[j]],
                lnw_ref[:, cols[j]], lnb_ref[:, cols[j]], rk_ref[:, cols[j]])
        return carry

    lax.fori_loop(0, nchunks, chunk, 0)

    @pl.when(tt == pl.num_programs(2) - 1)
    def _():
        s_ref[...] = st_ref[...]


def _rwkv_prompt(arrs, g, lnw, lnb, rk, B, T, tt_rows=512, hp=8):
    R, D = arrs[0].shape
    NP = D // LANES
    wd = hp * LANES
    tt_rows = min(tt_rows, T)
    nt = T // tt_rows
    col = pl.BlockSpec((tt_rows, wd), lambda b, p, t: (b * nt + t, p))
    par = pl.BlockSpec((1, wd), lambda b, p, t: (0, p))
    o, s = pl.pallas_call(
        functools.partial(_rwkv_prompt_kernel, nchunks=tt_rows // CHUNK, hp=hp),
        grid=(B, NP // hp, nt),
        in_specs=[col] * 7 + [par] * 3,
        out_specs=[col, pl.BlockSpec((None, hp, LANES, LANES), lambda b, p, t: (b, p, 0, 0))],
        out_shape=[jax.ShapeDtypeStruct((R, D), F32),
                   jax.ShapeDtypeStruct((B, NP, LANES, LANES), F32)],
        scratch_shapes=[pltpu.VMEM((hp, LANES, LANES), F32)],
        compiler_params=_cparams(("parallel", "parallel", "arbitrary")),
        name="rwkv_prompt",
    )(*arrs, g, lnw.reshape(1, D), lnb.reshape(1, D), rk.reshape(1, D))
    return o, s


def _rwkv_sample_kernel(r_ref, lw_ref, k_ref, v_ref, kap_ref, bet_ref, g_ref, lnw_ref, lnb_ref, rk_ref,
                        s0_ref, o_ref, s_ref, *, nseq):
    P = SEQ_PAD
    C = nseq * P
    C2 = 2 * C
    mA, mB = _pair_masks()
    bdm = (_iota2((LANES, LANES), 0) // RWKV_N == _iota2((LANES, LANES), 1) // RWKV_N).astype(F32)
    rr, cc = _iota2((C2, C2), 0), _iota2((C2, C2), 1)
    eye = (rr == cc).astype(F32)
    same = (rr // P) == (cc // P)
    strict = same & (rr > cc)
    incl = same & (rr >= cc)
    r1, c1 = _iota2((C, C), 0), _iota2((C, C), 1)
    seg_incl = ((r1 // P == c1 // P) & (r1 >= c1)).astype(F32)
    seg_all = (r1 // P == c1 // P).astype(F32)

    r, lw, k, v, kap, bet = r_ref[...], lw_ref[...], k_ref[...], v_ref[...], kap_ref[...], bet_ref[...]
    cs = _dot_exact_lhs(seg_incl, lw)
    cs_end = _dot_exact_lhs(seg_all, lw)
    rg, kg, kd, bd, kh, bh = _rwkv_scores(cs, lw, cs_end, r, k, kap, bet, mA, mB)
    vs = _stack(v, mA, mB)
    Ak = jnp.where(strict, _dot(kg, kd, _NT), 0.0)
    Ab = jnp.where(strict, _dot(kg, bd, _NT), 0.0)
    Bk = jnp.where(incl, _dot(rg, kd, _NT), 0.0)
    Bb = jnp.where(incl, _dot(rg, bd, _NT), 0.0)
    Tm = _neumann_inv(Ab, eye, 2)
    kgp = kg[:C, :] + kg[C:, :]
    rgp = rg[:C, :] + rg[C:, :]
    ks, rs = [], []
    for n in range(nseq):
        sl = slice(n * P, (n + 1) * P)
        S = s0_ref[n]
        ks.append(_dot(kgp[sl], S, _NT))
        rs.append(_dot(rgp[sl], S, _NT))
    KS = _stack(jnp.concatenate(ks, axis=0), mA, mB)
    RS = _stack(jnp.concatenate(rs, axis=0), mA, mB)
    rhs = KS + _dot(Ak, vs)
    U = _dot_hp(Tm, rhs)
    O = RS + _dot(Bk, vs) - _dot(Bb, U)
    Up = U[:C, :] + U[C:, :]
    khp = kh[:C, :] + kh[C:, :]
    bhp = bh[:C, :] + bh[C:, :]
    ecs = jnp.exp(cs_end)
    for n in range(nseq):
        sl = slice(n * P, (n + 1) * P)
        upd = _dot(v[sl], khp[sl], _TN) - _dot(Up[sl], bhp[sl], _TN)
        s_ref[n] = s0_ref[n] * ecs[n * P:n * P + 1, :] + upd * bdm
    o = O[:C, :] + O[C:, :]
    o_ref[...] = _rwkv_epilogue(o, r, k, v, g_ref[...], lnw_ref[...], lnb_ref[...], rk_ref[...])


def _rwkv_sample(arrs, g, lnw, lnb, rk, s0_bd, nseq=8):
    R, D = arrs[0].shape
    NP = D // LANES
    B = R // SEQ_PAD
    rows = nseq * SEQ_PAD
    col = pl.BlockSpec((rows, LANES), lambda p, gi: (gi, p))
    par = pl.BlockSpec((1, LANES), lambda p, gi: (0, p))
    sspec = pl.BlockSpec((nseq, None, LANES, LANES), lambda p, gi: (gi, p, 0, 0))
    o, s = pl.pallas_call(
        functools.partial(_rwkv_sample_kernel, nseq=nseq),
        grid=(NP, B // nseq),
        in_specs=[col] * 7 + [par] * 3 + [sspec],
        out_specs=[col, sspec],
        out_shape=[jax.ShapeDtypeStruct((R, D), F32), jax.ShapeDtypeStruct(s0_bd.shape, F32)],
        compiler_params=_cparams(("parallel", "parallel")),
        name="rwkv_sample",
    )(*arrs, g, lnw.reshape(1, D), lnb.reshape(1, D), rk.reshape(1, D), s0_bd)
    return o, s


def _merge_kernel(x_ref, ga_ref, gb_ref, oa_ref, ob_ref, w_ref, gt_ref, o_ref):
    m = _sigmoid(ga_ref[...]) * oa_ref[...] + _sigmoid(gb_ref[...]) * ob_ref[...]
    o_ref[...] = x_ref[...] + gt_ref[...] * _dot(m, w_ref[...])


def _merge_out(x, P_g, o_a, o_b, w_out, mod, tm):
    R, D = x.shape
    row = lambda j: pl.BlockSpec((tm, D), lambda i: (i, j))
    return pl.pallas_call(
        _merge_kernel,
        grid=(R // tm,),
        in_specs=[row(0), row(0), row(1), row(0), row(0),
                  pl.BlockSpec((D, D), lambda i: (0, 0)), mod.spec(tm, 2)],
        out_specs=row(0),
        out_shape=jax.ShapeDtypeStruct((R, D), F32),
        compiler_params=_cparams(("parallel",)),
        name="merge_out",
    )(x, P_g, P_g, o_a, o_b, w_out, mod.arr)


def _mlp_kernel(x_ref, nw_ref, sh_ref, sc_ref, gt_ref, wu_ref, wd_ref, fw_ref, o_ref, h_ref, acc_ref):
    kf = pl.program_id(1)

    @pl.when(kf == 0)
    def _():
        h = _rms(x_ref[...], nw_ref[...]) * (1.0 + sc_ref[...]) + sh_ref[...]
        h_ref[...] = h.astype(BF16)
        acc_ref[...] = jnp.zeros_like(acc_ref)

    u = jnp.dot(h_ref[...], wu_ref[...], preferred_element_type=F32)
    u = jnp.square(jnp.maximum(u, 0.0))
    acc_ref[...] += _dot(u, wd_ref[...])

    @pl.when(kf == pl.num_programs(1) - 1)
    def _():
        y = x_ref[...] + gt_ref[...] * acc_ref[...]
        o_ref[...] = _rms(y, fw_ref[...])


def _mlp(x, nw, w_up, w_down, fw, mod, tm, tf=1024):
    R, D = x.shape
    DF = w_up.shape[1]
    return pl.pallas_call(
        _mlp_kernel,
        grid=(R // tm, DF // tf),
        in_specs=[pl.BlockSpec((tm, D), lambda i, kf: (i, 0)),
                  pl.BlockSpec((1, D), lambda i, kf: (0, 0)),
                  mod.spec(tm, 3), mod.spec(tm, 4), mod.spec(tm, 5),
                  pl.BlockSpec((D, tf), lambda i, kf: (0, kf)),
                  pl.BlockSpec((tf, D), lambda i, kf: (kf, 0)),
                  pl.BlockSpec((1, D), lambda i, kf: (0, 0))],
        out_specs=pl.BlockSpec((tm, D), lambda i, kf: (i, 0)),
        out_shape=jax.ShapeDtypeStruct((R, D), F32),
        scratch_shapes=[pltpu.VMEM((tm, D), BF16), pltpu.VMEM((tm, D), F32)],
        compiler_params=_cparams(("parallel", "arbitrary")),
        name="mlp",
    )(x, nw.reshape(1, D), mod.arr, mod.arr, mod.arr, w_up, w_down, fw.reshape(1, D))


def _pad_rows(a, T, P):
    W = a.shape[1]
    return jnp.pad(a.reshape(-1, T, W), ((0, 0), (0, P - T), (0, 0))).reshape(-1, W)


def _unpad_rows(a, T, P):
    W = a.shape[1]
    return a.reshape(-1, P, W)[:, :T].reshape(-1, W)


def _bd_from_heads(s):
    B, H, N, _ = s.shape
    s = s.reshape(B, H // 2, 2, N, N)
    z = jnp.zeros_like(s[:, :, 0])
    top = jnp.concatenate([s[:, :, 0], z], axis=-1)
    bot = jnp.concatenate([z, s[:, :, 1]], axis=-1)
    return jnp.concatenate([top, bot], axis=-2)


def _heads_from_bd(s):
    B, NP, _, _ = s.shape
    N = RWKV_N
    return jnp.stack([s[:, :, :N, :N], s[:, :, N:, N:]], axis=2).reshape(B, 2 * NP, N, N)


def _run(x3, mod_rows, s_hgrn, s_rwkv, s_shift, W, prompt):
    B, T, D = x3.shape
    R = B * T
    x = x3.reshape(R, D)
    if prompt:
        mod = _Mod(mod_rows.reshape(B, 1, -1), True, T, D)
        tm = min(1024, T)
    else:
        mod = _Mod(jnp.repeat(mod_rows, T, axis=0), False, T, D)
        tm = min(512, R)

    h = _norm_mod(x, W["norm1_w"], mod, 0, 1, min(tm, 512))
    P_r = _matmul(h, W["w_r"], tm, RC_PAD // 3, "proj_r")
    P_h = _matmul(h, W["w_h"], tm, 2048, "proj_h")
    P_g = _matmul(h, W["w_g"], tm, 2048, "proj_g")
    shift_out = P_r.reshape(B, T, RC_PAD)[:, T - 1, :RC_COLS]

    rw = (W["mu"], W["w0"], W["a0"], W["k_k"], W["k_a"], W["wwa"], W["g2p"])
    if prompt:
        o_a, hg = _hgrn_prompt(P_h, W["lb_logits"], W["hgrn_norm_w"], B, T)
        shift_in = jnp.zeros((B, 1, RC_PAD), F32)
        arrs = _rwkv_prep(P_r, shift_in, rw, True, T)
        o_b, rs_bd = _rwkv_prompt(arrs[:6], arrs[6], W["lnx_w"], W["lnx_b"], W["r_k"], B, T)
    else:
        o_a, hg = _hgrn_sample(_pad_rows(P_h, T, SEQ_PAD), W["lb_logits"], W["hgrn_norm_w"], s_hgrn, T)
        o_a = _unpad_rows(o_a, T, SEQ_PAD)
        shift_rows = jnp.repeat(jnp.pad(s_shift, ((0, 0), (0, RC_PAD - RC_COLS))), T, axis=0)
        arrs = _rwkv_prep(P_r, shift_rows, rw, False, T)
        arrs = [_pad_rows(a, T, SEQ_PAD) for a in arrs]
        o_b, rs_bd = _rwkv_sample(arrs[:6], arrs[6], W["lnx_w"], W["lnx_b"], W["r_k"],
                                  _bd_from_heads(s_rwkv))
        o_b = _unpad_rows(o_b, T, SEQ_PAD)

    x1 = _merge_out(x, P_g, o_a, o_b, W["w_out"], mod, min(tm, 512))
    y = _mlp(x1, W["norm2_w"], W["w_up"], W["w_down"], W["final_norm_w"], mod, min(tm, 512))
    return y.reshape(B, T, D), hg, _heads_from_bd(rs_bd), shift_out


def kernel(x_prompt, x_sample, state_hgrn, state_rwkv, state_shift, c_prompt, c_sample, norm1_w, norm2_w, ada_w, ada_b, w_in, lb_logits, hgrn_norm_w, rwkv_mu, rwkv_w0, rwkv_w2, rwkv_a0, rwkv_a2, rwkv_g2, rwkv_k_k, rwkv_k_a, rwkv_r_k, rwkv_lnx_w, rwkv_lnx_b, w_out, w_up, w_down, final_norm_w):
    depth = norm1_w.shape[0]
    assert depth == 1, "single-layer step"
    D = x_prompt.shape[-1]
    HW = 4 * D
    w_in0 = w_in[0]
    row = lambda a: a.reshape(1, -1)
    wwa = jnp.zeros((LANES, 2 * D), F32)
    wwa = wwa.at[:W_LORA, :D].set(rwkv_w2[0]).at[W_LORA:W_LORA + A_LORA, D:].set(rwkv_a2[0])
    W = {
        "norm1_w": norm1_w[0], "norm2_w": norm2_w[0], "final_norm_w": final_norm_w,
        "w_h": w_in0[:, :HW].astype(BF16),
        "w_r": jnp.pad(w_in0[:, HW:HW + RC_COLS], ((0, 0), (0, RC_PAD - RC_COLS))).astype(BF16),
        "w_g": w_in0[:, HW + RC_COLS:].astype(BF16),
        "lb_logits": lb_logits, "hgrn_norm_w": hgrn_norm_w[0],
        "mu": jnp.pad(row(rwkv_mu[0]), ((0, 0), (0, RC_PAD - RC_COLS))),
        "w0": row(rwkv_w0[0]), "a0": row(rwkv_a0[0]), "k_k": row(rwkv_k_k[0]), "k_a": row(rwkv_k_a[0]),
        "wwa": wwa.astype(BF16),
        "g2p": jnp.pad(rwkv_g2[0], ((0, G_LORA_PAD - G_LORA), (0, 0))).astype(BF16),
        "lnx_w": rwkv_lnx_w[0], "lnx_b": rwkv_lnx_b[0], "r_k": rwkv_r_k[0].reshape(-1),
        "w_out": w_out[0].astype(BF16), "w_up": w_up[0].astype(BF16), "w_down": w_down[0].astype(BF16),
    }
    nb = x_prompt.shape[0]
    mod = _adaln(jnp.concatenate([c_prompt, c_sample], axis=0), ada_w[0], ada_b[0])
    y_p, hg_p, rw_p, sh_p = _run(x_prompt, mod[:nb], None, None, None, W, True)
    y_s, hg_s, rw_s, sh_s = _run(x_sample, mod[nb:], state_hgrn[0], state_rwkv[0], state_shift[0], W, False)
    ex = lambda a: a[None]
    return (y_p, y_s, ex(hg_p), ex(rw_p), ex(sh_p), ex(hg_s), ex(rw_s), ex(sh_s))
```

```python
import functools
import math

import jax
import jax.numpy as jnp
from jax import lax
from jax.experimental import pallas as pl
from jax.experimental.pallas import tpu as pltpu

F32 = jnp.float32
BF16 = jnp.bfloat16

LANES = 128
SUBLANES = 8
VMEM_LIMIT = 48 * 1024 * 1024

HGRN_DK = 128
RWKV_N = 64
W_LORA = 64
A_LORA = 64
G_LORA = 160
G_LORA_PAD = 256
RC_MAIN = 3 * 1024
RC_COLS = RC_MAIN + W_LORA + A_LORA + G_LORA
RC_PAD = RC_MAIN + LANES + G_LORA_PAD
RMS_EPS = 1e-6
LNX_EPS = 64e-5
CHUNK = 64
SUB = 8
RWKV_SUB = 8
SEQ_PAD = 8


def _cparams(sem):
    return pltpu.CompilerParams(dimension_semantics=sem, vmem_limit_bytes=VMEM_LIMIT)


_NN = (((1,), (0,)), ((), ()))
_NT = (((1,), (1,)), ((), ()))
_TN = (((0,), (0,)), ((), ()))


def _dot(a, b, dims=_NN):
    return lax.dot_general(a.astype(BF16), b.astype(BF16), dims, preferred_element_type=F32)


def _split2(x):
    hi = x.astype(BF16)
    lo = (x - hi.astype(F32)).astype(BF16)
    return hi, lo


def _split3(x):
    hi = x.astype(BF16)
    r1 = x - hi.astype(F32)
    mid = r1.astype(BF16)
    lo = (r1 - mid.astype(F32)).astype(BF16)
    return hi, mid, lo


def _dot_exact_lhs(a, x, dims=_NN):
    a = a.astype(BF16)
    hi, mid, lo = _split3(x)
    d = lambda p: lax.dot_general(a, p, dims, preferred_element_type=F32)
    return d(hi) + d(mid) + d(lo)


def _dot_exact_rhs(x, b, dims=_NN):
    b = b.astype(BF16)
    hi, mid, lo = _split3(x)
    d = lambda p: lax.dot_general(p, b, dims, preferred_element_type=F32)
    return d(hi) + d(mid) + d(lo)


def _dot_hp(a, b, dims=_NN):
    ah, al = _split2(a)
    bh, bl = _split2(b)
    d = lambda p, q: lax.dot_general(p, q, dims, preferred_element_type=F32)
    return d(ah, bh) + d(ah, bl) + d(al, bh)


def _sigmoid(x):
    return 1.0 / (1.0 + jnp.exp(-x))


def _iota2(shape, dim):
    return lax.broadcasted_iota(jnp.int32, shape, dim)


def _neumann_inv(L, eye, nsq):
    X = L
    P = eye - L
    for _ in range(nsq):
        X = _dot_hp(X, X)
        P = P + _dot_hp(P, X)
    return P


def _adaln_kernel(c_ref, w_ref, b_ref, o_ref):
    c = c_ref[...]
    s = c * _sigmoid(c)
    o_ref[...] = _dot(s, w_ref[...]) + b_ref[...]


def _adaln(c_all, ada_w, ada_b, tn=1536):
    R, D = c_all.shape
    N = ada_w.shape[1]
    return pl.pallas_call(
        _adaln_kernel,
        grid=(N // tn,),
        in_specs=[pl.BlockSpec((R, D), lambda j: (0, 0)),
                  pl.BlockSpec((D, tn), lambda j: (0, j)),
                  pl.BlockSpec((1, tn), lambda j: (0, j))],
        out_specs=pl.BlockSpec((R, tn), lambda j: (0, j)),
        out_shape=jax.ShapeDtypeStruct((R, N), F32),
        compiler_params=_cparams(("parallel",)),
        name="adaln",
    )(c_all, ada_w, ada_b.reshape(1, N))


class _Mod:
    def __init__(self, arr, per_seq, seq_rows, D):
        self.arr, self.per_seq, self.seq_rows, self.D = arr, per_seq, seq_rows, D

    def spec(self, tm, comp):
        D = self.D
        if self.per_seq:
            tiles = self.seq_rows // tm
            return pl.BlockSpec((None, 1, D), lambda i, *_: (i // tiles, 0, comp))
        return pl.BlockSpec((tm, D), lambda i, *_: (i, comp))


def _rms(x, w):
    return x * lax.rsqrt(jnp.mean(x * x, axis=-1, keepdims=True) + RMS_EPS) * w


def _norm_mod_kernel(x_ref, nw_ref, sh_ref, sc_ref, o_ref):
    h = _rms(x_ref[...], nw_ref[...]) * (1.0 + sc_ref[...]) + sh_ref[...]
    o_ref[...] = h.astype(o_ref.dtype)


def _norm_mod(x, nw, mod, sh_comp, sc_comp, tm):
    R, D = x.shape
    return pl.pallas_call(
        _norm_mod_kernel,
        grid=(R // tm,),
        in_specs=[pl.BlockSpec((tm, D), lambda i: (i, 0)),
                  pl.BlockSpec((1, D), lambda i: (0, 0)),
                  mod.spec(tm, sh_comp), mod.spec(tm, sc_comp)],
        out_specs=pl.BlockSpec((tm, D), lambda i: (i, 0)),
        out_shape=jax.ShapeDtypeStruct((R, D), BF16),
        compiler_params=_cparams(("parallel",)),
        name="norm_mod",
    )(x, nw.reshape(1, D), mod.arr, mod.arr)


def _mm_kernel(x_ref, w_ref, o_ref):
    o_ref[...] = jnp.dot(x_ref[...], w_ref[...], preferred_element_type=F32)


def _matmul(x, w, tm, tn, name):
    R, K = x.shape
    N = w.shape[1]
    return pl.pallas_call(
        _mm_kernel,
        grid=(R // tm, N // tn),
        in_specs=[pl.BlockSpec((tm, K), lambda i, j: (i, 0)),
                  pl.BlockSpec((K, tn), lambda i, j: (0, j))],
        out_specs=pl.BlockSpec((tm, tn), lambda i, j: (i, j)),
        out_shape=jax.ShapeDtypeStruct((R, N), F32),
        compiler_params=_cparams(("parallel", "arbitrary")),
        name=name,
    )(x, w)


def _hgrn_gates(fp, lb_logits):
    m = jnp.max(lb_logits, axis=0, keepdims=True)
    e = jnp.exp(lb_logits - m)
    lb = e[0:1, :] / jnp.sum(e, axis=0, keepdims=True)
    f = lb + (1.0 - lb) * _sigmoid(fp)
    return jnp.log(f), 1.0 - f


def _hgrn_diag(q, kin, b, c):
    R = q.shape[0]
    nb = R // c
    q3 = q.reshape(nb, c, LANES)
    k3 = kin.reshape(nb, c, LANES)
    b3 = (b * math.log2(math.e)).reshape(nb, c, LANES)
    lane_mod = lax.broadcasted_iota(jnp.int32, (nb, c, R), 2) % c
    acc = jnp.zeros((nb, c, R), F32)
    for s in range(c):
        d = jnp.minimum(b3 - b3[:, s:s + 1, :], 0.0)
        col = jnp.sum(q3 * k3[:, s:s + 1, :] * jnp.exp2(d), axis=-1, keepdims=True)
        acc = jnp.where(lane_mod == s, col, acc)
    rr, cc = _iota2((R, R), 0), _iota2((R, R), 1)
    keep = ((rr // c) == (cc // c)) & (rr >= cc)
    return jnp.where(keep, acc.reshape(R, R), 0.0)


def _hgrn_cross(q, kin, b, m):
    R = q.shape[0]
    h = m // 2
    nb = R // m
    ref = jnp.broadcast_to(b.reshape(nb, m, LANES)[:, h - 1:h, :], (nb, m, LANES)).reshape(R, LANES)
    upper = (_iota2((R, 1), 0) % m) >= h
    qs = q * jnp.exp(jnp.where(upper, b - ref, -1e30))
    ks = kin * jnp.exp(jnp.where(upper, -1e30, ref - b))
    sc = _dot(qs, ks, _NT)
    if nb == 1:
        return sc
    same_blk = (_iota2((R, R), 0) // m) == (_iota2((R, R), 1) // m)
    return jnp.where(same_blk, sc, 0.0)


def _hgrn_out(o, og, nw):
    o = o * lax.rsqrt(jnp.mean(o * o, axis=-1, keepdims=True) + RMS_EPS) * nw
    return o * (og * _sigmoid(og))


def _hgrn_prompt_kernel(q_ref, f_ref, i_ref, g_ref, lb_ref, nw_ref, o_ref, s_ref, st_ref, *, nchunks, hp):
    tt = pl.program_id(2)

    @pl.when(tt == 0)
    def _():
        st_ref[...] = jnp.zeros_like(st_ref)

    C, c = CHUNK, SUB
    tri = (_iota2((C, C), 0) >= _iota2((C, C), 1)).astype(F32)
    nw = nw_ref[...]

    def chunk(ci, carry):
        r0 = pl.multiple_of(ci * C, C)
        H = range(hp)
        cols = [slice(j * LANES, (j + 1) * LANES) for j in H]
        q = [q_ref[pl.ds(r0, C), cols[j]] for j in H]
        v = [i_ref[pl.ds(r0, C), cols[j]] for j in H]
        gates = [_hgrn_gates(f_ref[pl.ds(r0, C), cols[j]], lb_ref[:, cols[j]]) for j in H]
        lf = [gt[0] for gt in gates]
        kin = [gt[1] for gt in gates]
        b = [_dot_exact_lhs(tri, lf[j]) for j in H]
        b_last = [b[j][C - 1:C, :] for j in H]
        ST = [st_ref[j] for j in H]
        o = [_dot(q[j] * jnp.exp(b[j]), ST[j], _NT) for j in H]
        for j in H:
            st_ref[j] = ST[j] * jnp.exp(b_last[j]) + _dot(v[j], kin[j] * jnp.exp(b_last[j] - b[j]), _TN)
        A = [_hgrn_diag(q[j], kin[j], b[j], c) for j in H]
        m = 2 * c
        while m <= C:
            A = [A[j] + _hgrn_cross(q[j], kin[j], b[j], m) for j in H]
            m *= 2
        o = [o[j] + _dot(A[j], v[j]) for j in H]
        for j in H:
            o_ref[pl.ds(r0, C), cols[j]] = _hgrn_out(o[j], g_ref[pl.ds(r0, C), cols[j]], nw)
        return carry

    lax.fori_loop(0, nchunks, chunk, 0)

    @pl.when(tt == pl.num_programs(2) - 1)
    def _():
        for j in range(hp):
            s_ref[j] = st_ref[j].T


def _hgrn_prompt(P_h, lb_logits, nw, B, T, tt_rows=512, hp=8):
    R, W = P_h.shape
    H = W // (4 * HGRN_DK)
    HG = H // hp
    wd = hp * HGRN_DK
    tt_rows = min(tt_rows, T)
    nt = T // tt_rows
    col = lambda off: pl.BlockSpec((tt_rows, wd), lambda b, h, t: (b * nt + t, off * HG + h))
    o, s = pl.pallas_call(
        functools.partial(_hgrn_prompt_kernel, nchunks=tt_rows // CHUNK, hp=hp),
        grid=(B, HG, nt),
        in_specs=[col(0), col(1), col(2), col(3),
                  pl.BlockSpec((lb_logits.shape[0], wd), lambda b, h, t: (0, h)),
                  pl.BlockSpec((1, HGRN_DK), lambda b, h, t: (0, 0))],
        out_specs=[pl.BlockSpec((tt_rows, wd), lambda b, h, t: (b * nt + t, h)),
                   pl.BlockSpec((None, hp, HGRN_DK, HGRN_DK), lambda b, h, t: (b, h, 0, 0))],
        out_shape=[jax.ShapeDtypeStruct((R, H * HGRN_DK), F32),
                   jax.ShapeDtypeStruct((B, H, HGRN_DK, HGRN_DK), F32)],
        scratch_shapes=[pltpu.VMEM((hp, HGRN_DK, HGRN_DK), F32)],
        compiler_params=_cparams(("parallel", "parallel", "arbitrary")),
        name="hgrn_prompt",
    )(P_h, P_h, P_h, P_h, lb_logits, nw.reshape(1, HGRN_DK))
    return o, s


def _hgrn_sample_kernel(q_ref, f_ref, i_ref, g_ref, lb_ref, nw_ref, s0_ref, o_ref, s_ref, *, nseq, tvalid):
    P = SEQ_PAD
    R = nseq * P
    valid = (_iota2((R, 1), 0) % P) < tvalid
    q = q_ref[...]
    v = i_ref[...]
    lf, kin = _hgrn_gates(f_ref[...], lb_ref[...])
    lf = jnp.where(valid, lf, 0.0)
    kin = jnp.where(valid, kin, 0.0)
    rr, cc = _iota2((R, R), 0), _iota2((R, R), 1)
    seg_incl = ((rr // P == cc // P) & (rr >= cc)).astype(F32)
    seg_all = (rr // P == cc // P).astype(F32)
    b = _dot_exact_lhs(seg_incl, lf)
    b_tot = _dot_exact_lhs(seg_all, lf)
    o = _dot(_hgrn_diag(q, kin, b, P), v)
    qg = q * jnp.exp(b)
    kh = kin * jnp.exp(b_tot - b)
    eb = jnp.exp(b_tot)
    pieces = []
    for n in range(nseq):
        sl = slice(n * P, (n + 1) * P)
        ST = s0_ref[n].T
        pieces.append(_dot(qg[sl], ST, _NT))
        STn = ST * eb[n * P:n * P + 1, :] + _dot(v[sl], kh[sl], _TN)
        s_ref[n] = STn.T
    o = o + jnp.concatenate(pieces, axis=0)
    o_ref[...] = _hgrn_out(o, g_ref[...], nw_ref[...])


def _hgrn_sample(P_h, lb_logits, nw, s0, tvalid, nseq=8):
    R, W = P_h.shape
    H = W // (4 * HGRN_DK)
    B = R // SEQ_PAD
    rows = nseq * SEQ_PAD
    col = lambda off: pl.BlockSpec((rows, HGRN_DK), lambda h, g: (g, off * H + h))
    sspec = pl.BlockSpec((nseq, None, HGRN_DK, HGRN_DK), lambda h, g: (g, h, 0, 0))
    o, s = pl.pallas_call(
        functools.partial(_hgrn_sample_kernel, nseq=nseq, tvalid=tvalid),
        grid=(H, B // nseq),
        in_specs=[col(0), col(1), col(2), col(3),
                  pl.BlockSpec((lb_logits.shape[0], HGRN_DK), lambda h, g: (0, h)),
                  pl.BlockSpec((1, HGRN_DK), lambda h, g: (0, 0)),
                  sspec],
        out_specs=[pl.BlockSpec((rows, HGRN_DK), lambda h, g: (g, h)), sspec],
        out_shape=[jax.ShapeDtypeStruct((R, H * HGRN_DK), F32),
                   jax.ShapeDtypeStruct(s0.shape, F32)],
        compiler_params=_cparams(("parallel", "parallel")),
        name="hgrn_sample",
    )(P_h, P_h, P_h, P_h, lb_logits, nw.reshape(1, HGRN_DK), s0)
    return o, s


def _rwkv_prep_body(rc, prev, mu_ref, w0_ref, a0_ref, kk_ref, ka_ref, wwa_ref, g2_ref, outs):
    r_o, lw_o, k_o, v_o, kap_o, bet_o, g_o = outs
    D = RC_MAIN // 3
    xs = rc + mu_ref[...] * (prev - rc)
    r = xs[:, 0:D]
    kb = xs[:, D:2 * D]
    v = xs[:, 2 * D:3 * D]
    xwa = xs[:, RC_MAIN:RC_MAIN + LANES]
    xg = xs[:, RC_MAIN + LANES:]
    lane = _iota2((1, LANES), 1)
    wa = _dot(jnp.where(lane < W_LORA, jnp.tanh(xwa), xwa), wwa_ref[...])
    w = w0_ref[...] + wa[:, :D]
    a = _sigmoid(a0_ref[...] + wa[:, D:])
    lw = -math.exp(-0.5) * _sigmoid(w)
    g = _dot(_sigmoid(xg), g2_ref[...])
    kk = kb * kk_ref[...]
    seg = (_iota2((LANES, LANES), 0) // RWKV_N == _iota2((LANES, LANES), 1) // RWKV_N).astype(F32)
    k_a = ka_ref[...]
    for p in range(D // LANES):
        sl = slice(p * LANES, (p + 1) * LANES)
        kkp = kk[:, sl]
        nrm = jnp.maximum(jnp.sqrt(_dot_exact_rhs(kkp * kkp, seg)), 1e-12)
        kap = kkp / nrm
        ap = a[:, sl]
        kap_o[:, sl] = kap
        bet_o[:, sl] = ap * kap
        k_o[:, sl] = kb[:, sl] * (1.0 + (ap - 1.0) * k_a[:, sl])
    r_o[...] = r
    lw_o[...] = lw
    v_o[...] = v
    g_o[...] = g


def _rwkv_prep_prompt_kernel(p_ref, prev8_ref, shift_ref, mu_ref, w0_ref, a0_ref, kk_ref, ka_ref,
                             wwa_ref, g2_ref, *outs, tiles_per_seq):
    i = pl.program_id(0)
    rc = p_ref[...]
    tm = rc.shape[0]
    rolled = pltpu.roll(rc, 1, axis=0)
    start = (i % tiles_per_seq) == 0
    first = jnp.where(start, shift_ref[...], prev8_ref[SUBLANES - 1:SUBLANES, :])
    prev = jnp.where(_iota2((tm, 1), 0) == 0, first, rolled)
    _rwkv_prep_body(rc, prev, mu_ref, w0_ref, a0_ref, kk_ref, ka_ref, wwa_ref, g2_ref, outs)


def _rwkv_prep_sample_kernel(p_ref, shift_ref, mu_ref, w0_ref, a0_ref, kk_ref, ka_ref,
                             wwa_ref, g2_ref, *outs, seq_rows):
    rc = p_ref[...]
    tm = rc.shape[0]
    rolled = pltpu.roll(rc, 1, axis=0)
    prev = jnp.where((_iota2((tm, 1), 0) % seq_rows) == 0, shift_ref[...], rolled)
    _rwkv_prep_body(rc, prev, mu_ref, w0_ref, a0_ref, kk_ref, ka_ref, wwa_ref, g2_ref, outs)


def _rwkv_prep(P_r, shift, params, per_seq, seq_rows, tm=256):
    R = P_r.shape[0]
    D = RC_MAIN // 3
    tm = min(tm, seq_rows) if per_seq else min(tm, R)
    mu, w0, a0, k_k, k_a, wwa, g2p = params
    full = lambda a: pl.BlockSpec(a.shape, lambda i: (0,) * a.ndim)
    wspecs = [full(mu), full(w0), full(a0), full(k_k), full(k_a), full(wwa), full(g2p)]
    out_spec = pl.BlockSpec((tm, D), lambda i: (i, 0))
    out_shape = jax.ShapeDtypeStruct((R, D), F32)
    if per_seq:
        tiles = seq_rows // tm
        sub = tm // SUBLANES
        kern = functools.partial(_rwkv_prep_prompt_kernel, tiles_per_seq=tiles)
        in_specs = [pl.BlockSpec((tm, RC_PAD), lambda i: (i, 0)),
                    pl.BlockSpec((SUBLANES, RC_PAD), lambda i: (jnp.maximum(i * sub - 1, 0), 0)),
                    pl.BlockSpec((None, 1, RC_PAD), lambda i: (i // tiles, 0, 0))] + wspecs
        args = (P_r, P_r, shift)
    else:
        kern = functools.partial(_rwkv_prep_sample_kernel, seq_rows=seq_rows)
        in_specs = [pl.BlockSpec((tm, RC_PAD), lambda i: (i, 0)),
                    pl.BlockSpec((tm, RC_PAD), lambda i: (i, 0))] + wspecs
        args = (P_r, shift)
    return pl.pallas_call(
        kern,
        grid=(R // tm,),
        in_specs=in_specs,
        out_specs=[out_spec] * 7,
        out_shape=[out_shape] * 7,
        compiler_params=_cparams(("parallel",)),
        name="rwkv_prep",
    )(*args, mu, w0, a0, k_k, k_a, wwa, g2p)


def _pair_masks():
    lane = _iota2((1, LANES), 1)
    return (lane < RWKV_N).astype(F32), (lane >= RWKV_N).astype(F32)


def _stack(x, mA, mB):
    return jnp.concatenate([x * mA, x * mB], axis=0)


def _rwkv_scores(cs, lw, cs_end, r, k, kap, bet, mA, mB):
    e_neg = jnp.exp(-cs)
    e_rem = jnp.exp(cs_end - cs)
    rg = _stack(r * jnp.exp(cs), mA, mB)
    kg = _stack(kap * jnp.exp(cs - lw), mA, mB)
    kd = _stack(k * e_neg, mA, mB)
    bd = _stack(bet * e_neg, mA, mB)
    kh = _stack(k * e_rem, mA, mB)
    bh = _stack(bet * e_rem, mA, mB)
    return rg, kg, kd, bd, kh, bh


def _rwkv_epilogue(o, r, k, v, g, lnw, lnb, rk):
    avg = jnp.where(_iota2((LANES, LANES), 0) // RWKV_N == _iota2((LANES, LANES), 1) // RWKV_N,
                    1.0 / RWKV_N, 0.0).astype(F32)
    mean = _dot_exact_rhs(o, avg)
    d = o - mean
    var = _dot_exact_rhs(d * d, avg)
    on = d * lax.rsqrt(var + LNX_EPS) * lnw + lnb
    bonus = _dot_exact_rhs(r * k * rk, avg) * float(RWKV_N) * v
    return (on + bonus) * g


def _rwkv_prompt_kernel(r_ref, lw_ref, k_ref, v_ref, kap_ref, bet_ref, g_ref, lnw_ref, lnb_ref, rk_ref,
                        o_ref, s_ref, st_ref, *, nchunks, hp):
    tt = pl.program_id(2)

    @pl.when(tt == 0)
    def _():
        st_ref[...] = jnp.zeros_like(st_ref)

    C, c = CHUNK, RWKV_SUB
    C2 = 2 * C
    mA, mB = _pair_masks()
    rr, cc = _iota2((C2, C2), 0), _iota2((C2, C2), 1)
    eye = (rr == cc).astype(F32)
    strict = rr > cc
    incl = rr >= cc
    blk = (rr // c) == (cc // c)
    tri = (_iota2((C, C), 0) >= _iota2((C, C), 1)).astype(F32)
    unroll = 2 if nchunks % 2 == 0 else 1

    def front(chains):
        J = range(len(chains))
        ld = lambda ref: [ref[pl.ds(r0, C), p * LANES:(p + 1) * LANES] for r0, p in chains]
        r, lw, k, v, kap, bet = ld(r_ref), ld(lw_ref), ld(k_ref), ld(v_ref), ld(kap_ref), ld(bet_ref)
        cs = [_dot_exact_lhs(tri, lw[j]) for j in J]
        cs_end = [cs[j][C - 1:C, :] for j in J]
        sc = [_rwkv_scores(cs[j], lw[j], cs_end[j], r[j], k[j], kap[j], bet[j], mA, mB) for j in J]
        rg, kg, kd, bd, kh, bh = ([s[i] for s in sc] for i in range(6))
        vs = [_stack(v[j], mA, mB) for j in J]
        kbd = [jnp.concatenate([kd[j], bd[j]], axis=0) for j in J]
        sck = [_dot(kg[j], kbd[j], _NT) for j in J]
        scr = [_dot(rg[j], kbd[j], _NT) for j in J]
        Ak = [jnp.where(strict, sck[j][:, :C2], 0.0) for j in J]
        Ab = [jnp.where(strict, sck[j][:, C2:], 0.0) for j in J]
        Bkb = [jnp.concatenate([jnp.where(incl, scr[j][:, :C2], 0.0),
                                jnp.where(incl, -scr[j][:, C2:], 0.0)], axis=1) for j in J]
        LD = [jnp.where(blk, Ab[j], 0.0) for j in J]
        X = LD
        Dinv = [eye - LD[j] for j in J]
        for _ in range(int(math.log2(c)) - 1):
            X = [_dot(X[j], X[j]) for j in J]
            Dinv = [Dinv[j] + _dot(Dinv[j], X[j]) for j in J]
        Q = [_dot(Dinv[j], Ab[j] - LD[j]) for j in J]
        T0 = [Dinv[j] - _dot(Q[j], Dinv[j]) for j in J]
        for _ in range(int(math.log2(C // c)) - 1):
            Q = [_dot(Q[j], Q[j]) for j in J]
            T0 = [T0[j] + _dot(Q[j], T0[j]) for j in J]
        E = [(eye - T0[j]) - _dot_hp(Ab[j], T0[j]) for j in J]
        Tm = [T0[j] + _dot(T0[j], E[j]) for j in J]
        khb = [jnp.concatenate([kh[j], -bh[j]], axis=0) for j in J]
        AkV = [_dot(Ak[j], vs[j]) for j in J]
        dec = [jnp.exp(cs_end[j]) for j in J]
        avg = jnp.where(_iota2((LANES, LANES), 0) // RWKV_N == _iota2((LANES, LANES), 1) // RWKV_N,
                        1.0 / RWKV_N, 0.0).astype(F32)
        bonus = [_dot_exact_rhs(r[j] * k[j] * rk_ref[:, p * LANES:(p + 1) * LANES], avg)
                 * float(RWKV_N) * v[j] for j, (_, p) in zip(J, chains)]
        return [dict(kg=kg[j], rg=rg[j], vs=vs[j], Bkb=Bkb[j], Tm=Tm[j], khb=khb[j], AkV=AkV[j],
                     dec=dec[j], bonus=bonus[j]) for j in J]

    def back(chains, fr):
        J = range(len(chains))
        avg = jnp.where(_iota2((LANES, LANES), 0) // RWKV_N == _iota2((LANES, LANES), 1) // RWKV_N,
                        1.0 / RWKV_N, 0.0).astype(F32)
        S = [st_ref[p] for _, p in chains]
        rhs = [_dot(fr[j]["kg"], S[j], _NT) + fr[j]["AkV"] for j in J]
        U = [_dot(fr[j]["Tm"], rhs[j]) for j in J]
        vu = [jnp.concatenate([fr[j]["vs"], U[j]], axis=0) for j in J]
        for j, (_, p) in zip(J, chains):
            st_ref[p] = S[j] * fr[j]["dec"] + _dot(vu[j], fr[j]["khb"], _TN)
        O = [_dot(fr[j]["rg"], S[j], _NT) + _dot(fr[j]["Bkb"], vu[j]) for j in J]
        o = [O[j][:C, :] + O[j][C:, :] for j in J]
        mean = [_dot_exact_rhs(o[j], avg) for j in J]
        d = [o[j] - mean[j] for j in J]
        var = [_dot_exact_rhs(d[j] * d[j], avg) for j in J]
        for j, (r0, p) in zip(J, chains):
            cols = slice(p * LANES, (p + 1) * LANES)
            on = d[j] * lax.rsqrt(var[j] + LNX_EPS) * lnw_ref[:, cols] + lnb_ref[:, cols]
            o_ref[pl.ds(r0, C), cols] = (on + fr[j]["bonus"]) * g_ref[pl.ds(r0, C), cols]

    def body(ci, carry):
        base = ci * (unroll * C)
        offs = [pl.multiple_of(base + u * C, C) for u in range(unroll)]
        chains = [[(offs[u], p) for p in range(hp)] for u in range(unroll)]
        fr = front([ch for cu in chains for ch in cu])
        for u in range(unroll):
            back(chains[u], fr[u * hp:(u + 1) * hp])
        return carry

    lax.fori_loop(0, nchunks // unroll, body, 0)

    @pl.when(tt == pl.num_programs(2) - 1)
    def _():
        s_ref[...] = st_ref[...]


def _rwkv_prompt(arrs, g, lnw, lnb, rk, B, T, tt_rows=512, hp=8):
    R, D = arrs[0].shape
    NP = D // LANES
    wd = hp * LANES
    tt_rows = min(tt_rows, T)
    nt = T // tt_rows
    col = pl.BlockSpec((tt_rows, wd), lambda b, p, t: (b * nt + t, p))
    par = pl.BlockSpec((1, wd), lambda b, p, t: (0, p))
    o, s = pl.pallas_call(
        functools.partial(_rwkv_prompt_kernel, nchunks=tt_rows // CHUNK, hp=hp),
        grid=(B, NP // hp, nt),
        in_specs=[col] * 7 + [par] * 3,
        out_specs=[col, pl.BlockSpec((None, hp, LANES, LANES), lambda b, p, t: (b, p, 0, 0))],
        out_shape=[jax.ShapeDtypeStruct((R, D), F32),
                   jax.ShapeDtypeStruct((B, NP, LANES, LANES), F32)],
        scratch_shapes=[pltpu.VMEM((hp, LANES, LANES), F32)],
        compiler_params=_cparams(("parallel", "parallel", "arbitrary")),
        name="rwkv_prompt",
    )(*arrs, g, lnw.reshape(1, D), lnb.reshape(1, D), rk.reshape(1, D))
    return o, s


def _rwkv_sample_kernel(r_ref, lw_ref, k_ref, v_ref, kap_ref, bet_ref, g_ref, lnw_ref, lnb_ref, rk_ref,
                        s0_ref, o_ref, s_ref, *, nseq):
    P = SEQ_PAD
    C = nseq * P
    C2 = 2 * C
    mA, mB = _pair_masks()
    bdm = (_iota2((LANES, LANES), 0) // RWKV_N == _iota2((LANES, LANES), 1) // RWKV_N).astype(F32)
    rr, cc = _iota2((C2, C2), 0), _iota2((C2, C2), 1)
    eye = (rr == cc).astype(F32)
    same = (rr // P) == (cc // P)
    strict = same & (rr > cc)
    incl = same & (rr >= cc)
    r1, c1 = _iota2((C, C), 0), _iota2((C, C), 1)
    seg_incl = ((r1 // P == c1 // P) & (r1 >= c1)).astype(F32)
    seg_all = (r1 // P == c1 // P).astype(F32)

    r, lw, k, v, kap, bet = r_ref[...], lw_ref[...], k_ref[...], v_ref[...], kap_ref[...], bet_ref[...]
    cs = _dot_exact_lhs(seg_incl, lw)
    cs_end = _dot_exact_lhs(seg_all, lw)
    rg, kg, kd, bd, kh, bh = _rwkv_scores(cs, lw, cs_end, r, k, kap, bet, mA, mB)
    vs = _stack(v, mA, mB)
    Ak = jnp.where(strict, _dot(kg, kd, _NT), 0.0)
    Ab = jnp.where(strict, _dot(kg, bd, _NT), 0.0)
    Bk = jnp.where(incl, _dot(rg, kd, _NT), 0.0)
    Bb = jnp.where(incl, _dot(rg, bd, _NT), 0.0)
    Tm = _neumann_inv(Ab, eye, 2)
    kgp = kg[:C, :] + kg[C:, :]
    rgp = rg[:C, :] + rg[C:, :]
    ks, rs = [], []
    for n in range(nseq):
        sl = slice(n * P, (n + 1) * P)
        S = s0_ref[n]
        ks.append(_dot(kgp[sl], S, _NT))
        rs.append(_dot(rgp[sl], S, _NT))
    KS = _stack(jnp.concatenate(ks, axis=0), mA, mB)
    RS = _stack(jnp.concatenate(rs, axis=0), mA, mB)
    rhs = KS + _dot(Ak, vs)
    U = _dot_hp(Tm, rhs)
    O = RS + _dot(Bk, vs) - _dot(Bb, U)
    Up = U[:C, :] + U[C:, :]
    khp = kh[:C, :] + kh[C:, :]
    bhp = bh[:C, :] + bh[C:, :]
    ecs = jnp.exp(cs_end)
    for n in range(nseq):
        sl = slice(n * P, (n + 1) * P)
        upd = _dot(v[sl], khp[sl], _TN) - _dot(Up[sl], bhp[sl], _TN)
        s_ref[n] = s0_ref[n] * ecs[n * P:n * P + 1, :] + upd * bdm
    o = O[:C, :] + O[C:, :]
    o_ref[...] = _rwkv_epilogue(o, r, k, v, g_ref[...], lnw_ref[...], lnb_ref[...], rk_ref[...])


def _rwkv_sample(arrs, g, lnw, lnb, rk, s0_bd, nseq=8):
    R, D = arrs[0].shape
    NP = D // LANES
    B = R // SEQ_PAD
    rows = nseq * SEQ_PAD
    col = pl.BlockSpec((rows, LANES), lambda p, gi: (gi, p))
    par = pl.BlockSpec((1, LANES), lambda p, gi: (0, p))
    sspec = pl.BlockSpec((nseq, None, LANES, LANES), lambda p, gi: (gi, p, 0, 0))
    o, s = pl.pallas_call(
        functools.partial(_rwkv_sample_kernel, nseq=nseq),
        grid=(NP, B // nseq),
        in_specs=[col] * 7 + [par] * 3 + [sspec],
        out_specs=[col, sspec],
        out_shape=[jax.ShapeDtypeStruct((R, D), F32), jax.ShapeDtypeStruct(s0_bd.shape, F32)],
        compiler_params=_cparams(("parallel", "parallel")),
        name="rwkv_sample",
    )(*arrs, g, lnw.reshape(1, D), lnb.reshape(1, D), rk.reshape(1, D), s0_bd)
    return o, s


def _merge_kernel(x_ref, ga_ref, gb_ref, oa_ref, ob_ref, w_ref, gt_ref, o_ref):
    m = _sigmoid(ga_ref[...]) * oa_ref[...] + _sigmoid(gb_ref[...]) * ob_ref[...]
    o_ref[...] = x_ref[...] + gt_ref[...] * _dot(m, w_ref[...])


def _merge_out(x, P_g, o_a, o_b, w_out, mod, tm):
    R, D = x.shape
    row = lambda j: pl.BlockSpec((tm, D), lambda i: (i, j))
    return pl.pallas_call(
        _merge_kernel,
        grid=(R // tm,),
        in_specs=[row(0), row(0), row(1), row(0), row(0),
                  pl.BlockSpec((D, D), lambda i: (0, 0)), mod.spec(tm, 2)],
        out_specs=row(0),
        out_shape=jax.ShapeDtypeStruct((R, D), F32),
        compiler_params=_cparams(("parallel",)),
        name="merge_out",
    )(x, P_g, P_g, o_a, o_b, w_out, mod.arr)


def _mlp_kernel(x_ref, nw_ref, sh_ref, sc_ref, gt_ref, wu_ref, wd_ref, fw_ref, o_ref, h_ref, acc_ref):
    kf = pl.program_id(1)

    @pl.when(kf == 0)
    def _():
        h = _rms(x_ref[...], nw_ref[...]) * (1.0 + sc_ref[...]) + sh_ref[...]
        h_ref[...] = h.astype(BF16)
        acc_ref[...] = jnp.zeros_like(acc_ref)

    u = jnp.dot(h_ref[...], wu_ref[...], preferred_element_type=F32)
    u = jnp.square(jnp.maximum(u, 0.0))
    acc_ref[...] += _dot(u, wd_ref[...])

    @pl.when(kf == pl.num_programs(1) - 1)
    def _():
        y = x_ref[...] + gt_ref[...] * acc_ref[...]
        o_ref[...] = _rms(y, fw_ref[...])


def _mlp(x, nw, w_up, w_down, fw, mod, tm, tf=1024):
    R, D = x.shape
    DF = w_up.shape[1]
    return pl.pallas_call(
        _mlp_kernel,
        grid=(R // tm, DF // tf),
        in_specs=[pl.BlockSpec((tm, D), lambda i, kf: (i, 0)),
                  pl.BlockSpec((1, D), lambda i, kf: (0, 0)),
                  mod.spec(tm, 3), mod.spec(tm, 4), mod.spec(tm, 5),
                  pl.BlockSpec((D, tf), lambda i, kf: (0, kf)),
                  pl.BlockSpec((tf, D), lambda i, kf: (kf, 0)),
                  pl.BlockSpec((1, D), lambda i, kf: (0, 0))],
        out_specs=pl.BlockSpec((tm, D), lambda i, kf: (i, 0)),
        out_shape=jax.ShapeDtypeStruct((R, D), F32),
        scratch_shapes=[pltpu.VMEM((tm, D), BF16), pltpu.VMEM((tm, D), F32)],
        compiler_params=_cparams(("parallel", "arbitrary")),
        name="mlp",
    )(x, nw.reshape(1, D), mod.arr, mod.arr, mod.arr, w_up, w_down, fw.reshape(1, D))


def _pad_rows(a, T, P):
    W = a.shape[1]
    return jnp.pad(a.reshape(-1, T, W), ((0, 0), (0, P - T), (0, 0))).reshape(-1, W)


def _unpad_rows(a, T, P):
    W = a.shape[1]
    return a.reshape(-1, P, W)[:, :T].reshape(-1, W)


def _bd_from_heads(s):
    B, H, N, _ = s.shape
    s = s.reshape(B, H // 2, 2, N, N)
    z = jnp.zeros_like(s[:, :, 0])
    top = jnp.concatenate([s[:, :, 0], z], axis=-1)
    bot = jnp.concatenate([z, s[:, :, 1]], axis=-1)
    return jnp.concatenate([top, bot], axis=-2)


def _heads_from_bd(s):
    B, NP, _, _ = s.shape
    N = RWKV_N
    return jnp.stack([s[:, :, :N, :N], s[:, :, N:, N:]], axis=2).reshape(B, 2 * NP, N, N)


def _run(x3, mod_rows, s_hgrn, s_rwkv, s_shift, W, prompt):
    B, T, D = x3.shape
    R = B * T
    x = x3.reshape(R, D)
    if prompt:
        mod = _Mod(mod_rows.reshape(B, 1, -1), True, T, D)
        tm = min(1024, T)
    else:
        mod = _Mod(jnp.repeat(mod_rows, T, axis=0), False, T, D)
        tm = min(512, R)

    h = _norm_mod(x, W["norm1_w"], mod, 0, 1, min(tm, 512))
    P_r = _matmul(h, W["w_r"], tm, RC_PAD // 3, "proj_r")
    P_h = _matmul(h, W["w_h"], tm, 2048, "proj_h")
    P_g = _matmul(h, W["w_g"], tm, 2048, "proj_g")
    shift_out = P_r.reshape(B, T, RC_PAD)[:, T - 1, :RC_COLS]

    rw = (W["mu"], W["w0"], W["a0"], W["k_k"], W["k_a"], W["wwa"], W["g2p"])
    if prompt:
        o_a, hg = _hgrn_prompt(P_h, W["lb_logits"], W["hgrn_norm_w"], B, T)
        shift_in = jnp.zeros((B, 1, RC_PAD), F32)
        arrs = _rwkv_prep(P_r, shift_in, rw, True, T)
        o_b, rs_bd = _rwkv_prompt(arrs[:6], arrs[6], W["lnx_w"], W["lnx_b"], W["r_k"], B, T)
    else:
        o_a, hg = _hgrn_sample(_pad_rows(P_h, T, SEQ_PAD), W["lb_logits"], W["hgrn_norm_w"], s_hgrn, T)
        o_a = _unpad_rows(o_a, T, SEQ_PAD)
        shift_rows = jnp.repeat(jnp.pad(s_shift, ((0, 0), (0, RC_PAD - RC_COLS))), T, axis=0)
        arrs = _rwkv_prep(P_r, shift_rows, rw, False, T)
        arrs = [_pad_rows(a, T, SEQ_PAD) for a in arrs]
        o_b, rs_bd = _rwkv_sample(arrs[:6], arrs[6], W["lnx_w"], W["lnx_b"], W["r_k"],
                                  _bd_from_heads(s_rwkv))
        o_b = _unpad_rows(o_b, T, SEQ_PAD)

    x1 = _merge_out(x, P_g, o_a, o_b, W["w_out"], mod, min(tm, 512))
    y = _mlp(x1, W["norm2_w"], W["w_up"], W["w_down"], W["final_norm_w"], mod, min(tm, 512))
    return y.reshape(B, T, D), hg, _heads_from_bd(rs_bd), shift_out


def kernel(x_prompt, x_sample, state_hgrn, state_rwkv, state_shift, c_prompt, c_sample, norm1_w, norm2_w, ada_w, ada_b, w_in, lb_logits, hgrn_norm_w, rwkv_mu, rwkv_w0, rwkv_w2, rwkv_a0, rwkv_a2, rwkv_g2, rwkv_k_k, rwkv_k_a, rwkv_r_k, rwkv_lnx_w, rwkv_lnx_b, w_out, w_up, w_down, final_norm_w):
    depth = norm1_w.shape[0]
    assert depth == 1, "single-layer step"
    D = x_prompt.shape[-1]
    HW = 4 * D
    w_in0 = w_in[0]
    row = lambda a: a.reshape(1, -1)
    wwa = jnp.zeros((LANES, 2 * D), F32)
    wwa = wwa.at[:W_LORA, :D].set(rwkv_w2[0]).at[W_LORA:W_LORA + A_LORA, D:].set(rwkv_a2[0])
    W = {
        "norm1_w": norm1_w[0], "norm2_w": norm2_w[0], "final_norm_w": final_norm_w,
        "w_h": w_in0[:, :HW].astype(BF16),
        "w_r": jnp.pad(w_in0[:, HW:HW + RC_COLS], ((0, 0), (0, RC_PAD - RC_COLS))).astype(BF16),
        "w_g": w_in0[:, HW + RC_COLS:].astype(BF16),
        "lb_logits": lb_logits, "hgrn_norm_w": hgrn_norm_w[0],
        "mu": jnp.pad(row(rwkv_mu[0]), ((0, 0), (0, RC_PAD - RC_COLS))),
        "w0": row(rwkv_w0[0]), "a0": row(rwkv_a0[0]), "k_k": row(rwkv_k_k[0]), "k_a": row(rwkv_k_a[0]),
        "wwa": wwa.astype(BF16),
        "g2p": jnp.pad(rwkv_g2[0], ((0, G_LORA_PAD - G_LORA), (0, 0))).astype(BF16),
        "lnx_w": rwkv_lnx_w[0], "lnx_b": rwkv_lnx_b[0], "r_k": rwkv_r_k[0].reshape(-1),
        "w_out": w_out[0].astype(BF16), "w_up": w_up[0].astype(BF16), "w_down": w_down[0].astype(BF16),
    }
    nb = x_prompt.shape[0]
    mod = _adaln(jnp.concatenate([c_prompt, c_sample], axis=0), ada_w[0], ada_b[0])
    y_p, hg_p, rw_p, sh_p = _run(x_prompt, mod[:nb], None, None, None, W, True)
    y_s, hg_s, rw_s, sh_s = _run(x_sample, mod[nb:], state_hgrn[0], state_rwkv[0], state_shift[0], W, False)
    ex = lambda a: a[None]
    return (y_p, y_s, ex(hg_p), ex(rw_p), ex(sh_p), ex(hg_s), ex(rw_s), ex(sh_s))
```

```python
import functools
import math

import jax
import jax.numpy as jnp
from jax import lax
from jax.experimental import pallas as pl
from jax.experimental.pallas import tpu as pltpu

F32 = jnp.float32
BF16 = jnp.bfloat16

LANES = 128
SUBLANES = 8
VMEM_LIMIT = 48 * 1024 * 1024

HGRN_DK = 128
RWKV_N = 64
W_LORA = 64
A_LORA = 64
G_LORA = 160
G_LORA_PAD = 256
RC_MAIN = 3 * 1024
RC_COLS = RC_MAIN + W_LORA + A_LORA + G_LORA
RC_PAD = RC_MAIN + LANES + G_LORA_PAD
RMS_EPS = 1e-6
LNX_EPS = 64e-5
CHUNK = 64
SUB = 8
RWKV_SUB = 8


def _cparams(sem):
    return pltpu.CompilerParams(dimension_semantics=sem, vmem_limit_bytes=VMEM_LIMIT)


_NN = (((1,), (0,)), ((), ()))
_NT = (((1,), (1,)), ((), ()))
_TN = (((0,), (0,)), ((), ()))


def _dot(a, b, dims=_NN):
    return lax.dot_general(a.astype(BF16), b.astype(BF16), dims, preferred_element_type=F32)


def _split2(x):
    hi = x.astype(BF16)
    lo = (x - hi.astype(F32)).astype(BF16)
    return hi, lo


def _split3(x):
    hi = x.astype(BF16)
    r1 = x - hi.astype(F32)
    mid = r1.astype(BF16)
    lo = (r1 - mid.astype(F32)).astype(BF16)
    return hi, mid, lo


def _dot_exact_lhs(a, x, dims=_NN):
    a = a.astype(BF16)
    hi, mid, lo = _split3(x)
    d = lambda p: lax.dot_general(a, p, dims, preferred_element_type=F32)
    return d(hi) + d(mid) + d(lo)


def _dot_exact_rhs(x, b, dims=_NN):
    b = b.astype(BF16)
    hi, mid, lo = _split3(x)
    d = lambda p: lax.dot_general(p, b, dims, preferred_element_type=F32)
    return d(hi) + d(mid) + d(lo)


def _dot_hp(a, b, dims=_NN):
    ah, al = _split2(a)
    bh, bl = _split2(b)
    d = lambda p, q: lax.dot_general(p, q, dims, preferred_element_type=F32)
    return d(ah, bh) + d(ah, bl) + d(al, bh)


def _sigmoid(x):
    return 1.0 / (1.0 + jnp.exp(-x))


def _iota2(shape, dim):
    return lax.broadcasted_iota(jnp.int32, shape, dim)


def _neumann_inv(L, eye, nsq):
    X = L
    P = eye - L
    for _ in range(nsq):
        X = _dot_hp(X, X)
        P = P + _dot_hp(P, X)
    return P


def _adaln_kernel(c_ref, w_ref, b_ref, o_ref):
    c = c_ref[...]
    s = c * _sigmoid(c)
    o_ref[...] = _dot(s, w_ref[...]) + b_ref[...]


def _adaln(c_all, ada_w, ada_b, tn=1536):
    R, D = c_all.shape
    N = ada_w.shape[1]
    return pl.pallas_call(
        _adaln_kernel,
        grid=(N // tn,),
        in_specs=[pl.BlockSpec((R, D), lambda j: (0, 0)),
                  pl.BlockSpec((D, tn), lambda j: (0, j)),
                  pl.BlockSpec((1, tn), lambda j: (0, j))],
        out_specs=pl.BlockSpec((R, tn), lambda j: (0, j)),
        out_shape=jax.ShapeDtypeStruct((R, N), F32),
        compiler_params=_cparams(("parallel",)),
        name="adaln",
    )(c_all, ada_w, ada_b.reshape(1, N))


class _Mod:
    def __init__(self, arr, per_seq, seq_rows, D):
        self.arr, self.per_seq, self.seq_rows, self.D = arr, per_seq, seq_rows, D

    def spec(self, tm, comp):
        D = self.D
        if self.per_seq:
            tiles = self.seq_rows // tm
            return pl.BlockSpec((None, 1, D), lambda i, *_: (i // tiles, 0, comp))
        return pl.BlockSpec((tm, D), lambda i, *_: (i, comp))


def _rms(x, w):
    return x * lax.rsqrt(jnp.mean(x * x, axis=-1, keepdims=True) + RMS_EPS) * w


def _norm_mod_kernel(x_ref, nw_ref, sh_ref, sc_ref, o_ref):
    h = _rms(x_ref[...], nw_ref[...]) * (1.0 + sc_ref[...]) + sh_ref[...]
    o_ref[...] = h.astype(o_ref.dtype)


def _norm_mod(x, nw, mod, sh_comp, sc_comp, tm):
    R, D = x.shape
    return pl.pallas_call(
        _norm_mod_kernel,
        grid=(R // tm,),
        in_specs=[pl.BlockSpec((tm, D), lambda i: (i, 0)),
                  pl.BlockSpec((1, D), lambda i: (0, 0)),
                  mod.spec(tm, sh_comp), mod.spec(tm, sc_comp)],
        out_specs=pl.BlockSpec((tm, D), lambda i: (i, 0)),
        out_shape=jax.ShapeDtypeStruct((R, D), BF16),
        compiler_params=_cparams(("parallel",)),
        name="norm_mod",
    )(x, nw.reshape(1, D), mod.arr, mod.arr)


def _mm_kernel(x_ref, w_ref, o_ref):
    o_ref[...] = jnp.dot(x_ref[...], w_ref[...], preferred_element_type=F32)


def _matmul(x, w, tm, tn, name):
    R, K = x.shape
    N = w.shape[1]
    return pl.pallas_call(
        _mm_kernel,
        grid=(R // tm, N // tn),
        in_specs=[pl.BlockSpec((tm, K), lambda i, j: (i, 0)),
                  pl.BlockSpec((K, tn), lambda i, j: (0, j))],
        out_specs=pl.BlockSpec((tm, tn), lambda i, j: (i, j)),
        out_shape=jax.ShapeDtypeStruct((R, N), F32),
        compiler_params=_cparams(("parallel", "arbitrary")),
        name=name,
    )(x, w)


def _hgrn_gates(fp, lb_logits):
    m = jnp.max(lb_logits, axis=0, keepdims=True)
    e = jnp.exp(lb_logits - m)
    lb = e[0:1, :] / jnp.sum(e, axis=0, keepdims=True)
    f = lb + (1.0 - lb) * _sigmoid(fp)
    return jnp.log(f), 1.0 - f


def _hgrn_diag(q, kin, b, c):
    R = q.shape[0]
    nb = R // c
    q3 = q.reshape(nb, c, LANES)
    k3 = kin.reshape(nb, c, LANES)
    b3 = (b * math.log2(math.e)).reshape(nb, c, LANES)
    lane_mod = lax.broadcasted_iota(jnp.int32, (nb, c, R), 2) % c
    acc = jnp.zeros((nb, c, R), F32)
    for s in range(c):
        d = jnp.minimum(b3 - b3[:, s:s + 1, :], 0.0)
        col = jnp.sum(q3 * k3[:, s:s + 1, :] * jnp.exp2(d), axis=-1, keepdims=True)
        acc = jnp.where(lane_mod == s, col, acc)
    rr, cc = _iota2((R, R), 0), _iota2((R, R), 1)
    keep = ((rr // c) == (cc // c)) & (rr >= cc)
    return jnp.where(keep, acc.reshape(R, R), 0.0)


def _hgrn_cross(q, kin, b, m):
    R = q.shape[0]
    h = m // 2
    nb = R // m
    ref = jnp.broadcast_to(b.reshape(nb, m, LANES)[:, h - 1:h, :], (nb, m, LANES)).reshape(R, LANES)
    upper = (_iota2((R, 1), 0) % m) >= h
    qs = q * jnp.exp(jnp.where(upper, b - ref, -1e30))
    ks = kin * jnp.exp(jnp.where(upper, -1e30, ref - b))
    sc = _dot(qs, ks, _NT)
    if nb == 1:
        return sc
    same_blk = (_iota2((R, R), 0) // m) == (_iota2((R, R), 1) // m)
    return jnp.where(same_blk, sc, 0.0)


def _hgrn_out(o, og, nw):
    o = o * lax.rsqrt(jnp.mean(o * o, axis=-1, keepdims=True) + RMS_EPS) * nw
    return o * (og * _sigmoid(og))


def _hgrn_prompt_kernel(q_ref, f_ref, i_ref, g_ref, lb_ref, nw_ref, o_ref, s_ref, st_ref, *, nchunks, hp):
    tt = pl.program_id(2)

    @pl.when(tt == 0)
    def _():
        st_ref[...] = jnp.zeros_like(st_ref)

    C, c = CHUNK, SUB
    tri = (_iota2((C, C), 0) >= _iota2((C, C), 1)).astype(F32)
    nw = nw_ref[...]

    def chunk(ci, carry):
        r0 = pl.multiple_of(ci * C, C)
        H = range(hp)
        cols = [slice(j * LANES, (j + 1) * LANES) for j in H]
        q = [q_ref[pl.ds(r0, C), cols[j]] for j in H]
        v = [i_ref[pl.ds(r0, C), cols[j]] for j in H]
        gates = [_hgrn_gates(f_ref[pl.ds(r0, C), cols[j]], lb_ref[:, cols[j]]) for j in H]
        lf = [gt[0] for gt in gates]
        kin = [gt[1] for gt in gates]
        b = [_dot_exact_lhs(tri, lf[j]) for j in H]
        b_last = [b[j][C - 1:C, :] for j in H]
        ST = [st_ref[j] for j in H]
        o = [_dot(q[j] * jnp.exp(b[j]), ST[j], _NT) for j in H]
        for j in H:
            st_ref[j] = ST[j] * jnp.exp(b_last[j]) + _dot(v[j], kin[j] * jnp.exp(b_last[j] - b[j]), _TN)
        A = [_hgrn_diag(q[j], kin[j], b[j], c) for j in H]
        m = 2 * c
        while m <= C:
            A = [A[j] + _hgrn_cross(q[j], kin[j], b[j], m) for j in H]
            m *= 2
        o = [o[j] + _dot(A[j], v[j]) for j in H]
        for j in H:
            o_ref[pl.ds(r0, C), cols[j]] = _hgrn_out(o[j], g_ref[pl.ds(r0, C), cols[j]], nw)
        return carry

    lax.fori_loop(0, nchunks, chunk, 0)

    @pl.when(tt == pl.num_programs(2) - 1)
    def _():
        for j in range(hp):
            s_ref[j] = st_ref[j].T


def _hgrn_prompt(P_h, lb_logits, nw, B, T, tt_rows=512, hp=8):
    R, W = P_h.shape
    H = W // (4 * HGRN_DK)
    HG = H // hp
    wd = hp * HGRN_DK
    tt_rows = min(tt_rows, T)
    nt = T // tt_rows
    col = lambda off: pl.BlockSpec((tt_rows, wd), lambda b, h, t: (b * nt + t, off * HG + h))
    o, s = pl.pallas_call(
        functools.partial(_hgrn_prompt_kernel, nchunks=tt_rows // CHUNK, hp=hp),
        grid=(B, HG, nt),
        in_specs=[col(0), col(1), col(2), col(3),
                  pl.BlockSpec((lb_logits.shape[0], wd), lambda b, h, t: (0, h)),
                  pl.BlockSpec((1, HGRN_DK), lambda b, h, t: (0, 0))],
        out_specs=[pl.BlockSpec((tt_rows, wd), lambda b, h, t: (b * nt + t, h)),
                   pl.BlockSpec((None, hp, HGRN_DK, HGRN_DK), lambda b, h, t: (b, h, 0, 0))],
        out_shape=[jax.ShapeDtypeStruct((R, H * HGRN_DK), F32),
                   jax.ShapeDtypeStruct((B, H, HGRN_DK, HGRN_DK), F32)],
        scratch_shapes=[pltpu.VMEM((hp, HGRN_DK, HGRN_DK), F32)],
        compiler_params=_cparams(("parallel", "parallel", "arbitrary")),
        name="hgrn_prompt",
    )(P_h, P_h, P_h, P_h, lb_logits, nw.reshape(1, HGRN_DK))
    return o, s


def _hgrn_sample_kernel(q_ref, f_ref, i_ref, g_ref, lb_ref, nw_ref, s0_ref, o_ref, s_ref, *, nseq, T, hp):
    R = nseq * T
    rr, cc = _iota2((R, R), 0), _iota2((R, R), 1)
    same = (rr // T) == (cc // T)
    seg_incl = (same & (rr >= cc)).astype(F32)
    seg_all = same.astype(F32)
    nw = nw_ref[...]
    H = range(hp)
    cols = [slice(j * LANES, (j + 1) * LANES) for j in H]
    q = [q_ref[:, cols[j]] for j in H]
    v = [i_ref[:, cols[j]] for j in H]
    gates = [_hgrn_gates(f_ref[:, cols[j]], lb_ref[:, cols[j]]) for j in H]
    lf = [gt[0] for gt in gates]
    kin = [gt[1] for gt in gates]
    b = [_dot_exact_lhs(seg_incl, lf[j]) for j in H]
    b_tot = [_dot_exact_lhs(seg_all, lf[j]) for j in H]
    A = [_hgrn_diag(q[j], kin[j], b[j], T) for j in H]
    o = [_dot(A[j], v[j]) for j in H]
    qg = [q[j] * jnp.exp(b[j]) for j in H]
    kh = [kin[j] * jnp.exp(b_tot[j] - b[j]) for j in H]
    ecol = [jnp.transpose(jnp.exp(b_tot[j]).reshape(nseq, T, LANES)[:, 0, :]) for j in H]
    inter = [[] for _ in H]
    for n in range(nseq):
        sl = slice(n * T, (n + 1) * T)
        for j in H:
            S = s0_ref[n, j]
            inter[j].append(_dot(qg[j][sl], S))
            s_ref[n, j] = S * ecol[j][:, n:n + 1] + _dot(kh[j][sl], v[j][sl], _TN)
    for j in H:
        oj = o[j] + jnp.concatenate(inter[j], axis=0)
        o_ref[:, cols[j]] = _hgrn_out(oj, g_ref[:, cols[j]], nw)


def _hgrn_sample(P_h, lb_logits, nw, s0, T, nseq=16, hp=4):
    R, W = P_h.shape
    H = W // (4 * HGRN_DK)
    HG = H // hp
    wd = hp * HGRN_DK
    B = R // T
    rows = nseq * T
    col = lambda off: pl.BlockSpec((rows, wd), lambda h, g: (g, off * HG + h))
    sspec = pl.BlockSpec((nseq, hp, HGRN_DK, HGRN_DK), lambda h, g: (g, h, 0, 0))
    o, s = pl.pallas_call(
        functools.partial(_hgrn_sample_kernel, nseq=nseq, T=T, hp=hp),
        grid=(HG, B // nseq),
        in_specs=[col(0), col(1), col(2), col(3),
                  pl.BlockSpec((lb_logits.shape[0], wd), lambda h, g: (0, h)),
                  pl.BlockSpec((1, HGRN_DK), lambda h, g: (0, 0)),
                  sspec],
        out_specs=[pl.BlockSpec((rows, wd), lambda h, g: (g, h)), sspec],
        out_shape=[jax.ShapeDtypeStruct((R, H * HGRN_DK), F32),
                   jax.ShapeDtypeStruct(s0.shape, F32)],
        compiler_params=_cparams(("parallel", "parallel")),
        name="hgrn_sample",
    )(P_h, P_h, P_h, P_h, lb_logits, nw.reshape(1, HGRN_DK), s0)
    return o, s


def _rwkv_prep_body(rc, prev, mu_ref, w0_ref, a0_ref, kk_ref, ka_ref, wwa_ref, g2_ref, outs):
    r_o, lw_o, k_o, v_o, kap_o, bet_o, g_o = outs
    D = RC_MAIN // 3
    xs = rc + mu_ref[...] * (prev - rc)
    r = xs[:, 0:D]
    kb = xs[:, D:2 * D]
    v = xs[:, 2 * D:3 * D]
    xwa = xs[:, RC_MAIN:RC_MAIN + LANES]
    xg = xs[:, RC_MAIN + LANES:]
    lane = _iota2((1, LANES), 1)
    wa = _dot(jnp.where(lane < W_LORA, jnp.tanh(xwa), xwa), wwa_ref[...])
    w = w0_ref[...] + wa[:, :D]
    a = _sigmoid(a0_ref[...] + wa[:, D:])
    lw = -math.exp(-0.5) * _sigmoid(w)
    g = _dot(_sigmoid(xg), g2_ref[...])
    kk = kb * kk_ref[...]
    seg = (_iota2((LANES, LANES), 0) // RWKV_N == _iota2((LANES, LANES), 1) // RWKV_N).astype(F32)
    k_a = ka_ref[...]
    for p in range(D // LANES):
        sl = slice(p * LANES, (p + 1) * LANES)
        kkp = kk[:, sl]
        nrm = jnp.maximum(jnp.sqrt(_dot_exact_rhs(kkp * kkp, seg)), 1e-12)
        kap = kkp / nrm
        ap = a[:, sl]
        kap_o[:, sl] = kap
        bet_o[:, sl] = ap * kap
        k_o[:, sl] = kb[:, sl] * (1.0 + (ap - 1.0) * k_a[:, sl])
    r_o[...] = r
    lw_o[...] = lw
    v_o[...] = v
    g_o[...] = g


def _rwkv_prep_prompt_kernel(p_ref, prev8_ref, shift_ref, mu_ref, w0_ref, a0_ref, kk_ref, ka_ref,
                             wwa_ref, g2_ref, *outs, tiles_per_seq):
    i = pl.program_id(0)
    rc = p_ref[...]
    tm = rc.shape[0]
    rolled = pltpu.roll(rc, 1, axis=0)
    start = (i % tiles_per_seq) == 0
    first = jnp.where(start, shift_ref[...], prev8_ref[SUBLANES - 1:SUBLANES, :])
    prev = jnp.where(_iota2((tm, 1), 0) == 0, first, rolled)
    _rwkv_prep_body(rc, prev, mu_ref, w0_ref, a0_ref, kk_ref, ka_ref, wwa_ref, g2_ref, outs)


def _rwkv_prep_sample_kernel(p_ref, shift_ref, mu_ref, w0_ref, a0_ref, kk_ref, ka_ref,
                             wwa_ref, g2_ref, *outs, seq_rows):
    rc = p_ref[...]
    tm = rc.shape[0]
    rolled = pltpu.roll(rc, 1, axis=0)
    prev = jnp.where((_iota2((tm, 1), 0) % seq_rows) == 0, shift_ref[...], rolled)
    _rwkv_prep_body(rc, prev, mu_ref, w0_ref, a0_ref, kk_ref, ka_ref, wwa_ref, g2_ref, outs)


def _rwkv_prep(P_r, shift, params, per_seq, seq_rows, tm=256):
    R = P_r.shape[0]
    D = RC_MAIN // 3
    tm = min(tm, seq_rows) if per_seq else min(tm, R)
    mu, w0, a0, k_k, k_a, wwa, g2p = params
    full = lambda a: pl.BlockSpec(a.shape, lambda i: (0,) * a.ndim)
    wspecs = [full(mu), full(w0), full(a0), full(k_k), full(k_a), full(wwa), full(g2p)]
    out_spec = pl.BlockSpec((tm, D), lambda i: (i, 0))
    out_shape = jax.ShapeDtypeStruct((R, D), F32)
    if per_seq:
        tiles = seq_rows // tm
        sub = tm // SUBLANES
        kern = functools.partial(_rwkv_prep_prompt_kernel, tiles_per_seq=tiles)
        in_specs = [pl.BlockSpec((tm, RC_PAD), lambda i: (i, 0)),
                    pl.BlockSpec((SUBLANES, RC_PAD), lambda i: (jnp.maximum(i * sub - 1, 0), 0)),
                    pl.BlockSpec((None, 1, RC_PAD), lambda i: (i // tiles, 0, 0))] + wspecs
        args = (P_r, P_r, shift)
    else:
        kern = functools.partial(_rwkv_prep_sample_kernel, seq_rows=seq_rows)
        in_specs = [pl.BlockSpec((tm, RC_PAD), lambda i: (i, 0)),
                    pl.BlockSpec((tm, RC_PAD), lambda i: (i, 0))] + wspecs
        args = (P_r, shift)
    return pl.pallas_call(
        kern,
        grid=(R // tm,),
        in_specs=in_specs,
        out_specs=[out_spec] * 7,
        out_shape=[out_shape] * 7,
        compiler_params=_cparams(("parallel",)),
        name="rwkv_prep",
    )(*args, mu, w0, a0, k_k, k_a, wwa, g2p)


def _pair_masks():
    lane = _iota2((1, LANES), 1)
    return (lane < RWKV_N).astype(F32), (lane >= RWKV_N).astype(F32)


def _stack(x, mA, mB):
    return jnp.concatenate([x * mA, x * mB], axis=0)


def _rwkv_scores(cs, lw, cs_end, r, k, kap, bet, mA, mB):
    e_neg = jnp.exp(-cs)
    e_rem = jnp.exp(cs_end - cs)
    rg = _stack(r * jnp.exp(cs), mA, mB)
    kg = _stack(kap * jnp.exp(cs - lw), mA, mB)
    kd = _stack(k * e_neg, mA, mB)
    bd = _stack(bet * e_neg, mA, mB)
    kh = _stack(k * e_rem, mA, mB)
    bh = _stack(bet * e_rem, mA, mB)
    return rg, kg, kd, bd, kh, bh


def _rwkv_epilogue(o, r, k, v, g, lnw, lnb, rk):
    avg = jnp.where(_iota2((LANES, LANES), 0) // RWKV_N == _iota2((LANES, LANES), 1) // RWKV_N,
                    1.0 / RWKV_N, 0.0).astype(F32)
    mean = _dot_exact_rhs(o, avg)
    d = o - mean
    var = _dot_exact_rhs(d * d, avg)
    on = d * lax.rsqrt(var + LNX_EPS) * lnw + lnb
    bonus = _dot_exact_rhs(r * k * rk, avg) * float(RWKV_N) * v
    return (on + bonus) * g


def _rwkv_prompt_kernel(r_ref, lw_ref, k_ref, v_ref, kap_ref, bet_ref, g_ref, lnw_ref, lnb_ref, rk_ref,
                        o_ref, s_ref, st_ref, *, nchunks, hp):
    tt = pl.program_id(2)

    @pl.when(tt == 0)
    def _():
        st_ref[...] = jnp.zeros_like(st_ref)

    C, c = CHUNK, RWKV_SUB
    C2 = 2 * C
    mA, mB = _pair_masks()
    rr, cc = _iota2((C2, C2), 0), _iota2((C2, C2), 1)
    eye = (rr == cc).astype(F32)
    strict = rr > cc
    incl = rr >= cc
    blk = (rr // c) == (cc // c)
    tri = (_iota2((C, C), 0) >= _iota2((C, C), 1)).astype(F32)
    unroll = 2 if nchunks % 2 == 0 else 1

    def front(chains):
        J = range(len(chains))
        ld = lambda ref: [ref[pl.ds(r0, C), p * LANES:(p + 1) * LANES] for r0, p in chains]
        r, lw, k, v, kap, bet = ld(r_ref), ld(lw_ref), ld(k_ref), ld(v_ref), ld(kap_ref), ld(bet_ref)
        cs = [_dot_exact_lhs(tri, lw[j]) for j in J]
        cs_end = [cs[j][C - 1:C, :] for j in J]
        sc = [_rwkv_scores(cs[j], lw[j], cs_end[j], r[j], k[j], kap[j], bet[j], mA, mB) for j in J]
        rg, kg, kd, bd, kh, bh = ([s[i] for s in sc] for i in range(6))
        vs = [_stack(v[j], mA, mB) for j in J]
        kbd = [jnp.concatenate([kd[j], bd[j]], axis=0) for j in J]
        sck = [_dot(kg[j], kbd[j], _NT) for j in J]
        scr = [_dot(rg[j], kbd[j], _NT) for j in J]
        Ak = [jnp.where(strict, sck[j][:, :C2], 0.0) for j in J]
        Ab = [jnp.where(strict, sck[j][:, C2:], 0.0) for j in J]
        Bkb = [jnp.concatenate([jnp.where(incl, scr[j][:, :C2], 0.0),
                                jnp.where(incl, -scr[j][:, C2:], 0.0)], axis=1) for j in J]
        LD = [jnp.where(blk, Ab[j], 0.0) for j in J]
        X = LD
        Dinv = [eye - LD[j] for j in J]
        for _ in range(int(math.log2(c)) - 1):
            X = [_dot(X[j], X[j]) for j in J]
            Dinv = [Dinv[j] + _dot(Dinv[j], X[j]) for j in J]
        Q = [_dot(Dinv[j], Ab[j] - LD[j]) for j in J]
        T0 = [Dinv[j] - _dot(Q[j], Dinv[j]) for j in J]
        for _ in range(int(math.log2(C // c)) - 1):
            Q = [_dot(Q[j], Q[j]) for j in J]
            T0 = [T0[j] + _dot(Q[j], T0[j]) for j in J]
        E = [(eye - T0[j]) - _dot_hp(Ab[j], T0[j]) for j in J]
        Tm = [T0[j] + _dot(T0[j], E[j]) for j in J]
        khb = [jnp.concatenate([kh[j], -bh[j]], axis=0) for j in J]
        AkV = [_dot(Ak[j], vs[j]) for j in J]
        dec = [jnp.exp(cs_end[j]) for j in J]
        avg = jnp.where(_iota2((LANES, LANES), 0) // RWKV_N == _iota2((LANES, LANES), 1) // RWKV_N,
                        1.0 / RWKV_N, 0.0).astype(F32)
        bonus = [_dot_exact_rhs(r[j] * k[j] * rk_ref[:, p * LANES:(p + 1) * LANES], avg)
                 * float(RWKV_N) * v[j] for j, (_, p) in zip(J, chains)]
        return [dict(kg=kg[j], rg=rg[j], vs=vs[j], Bkb=Bkb[j], Tm=Tm[j], khb=khb[j], AkV=AkV[j],
                     dec=dec[j], bonus=bonus[j]) for j in J]

    def back(chains, fr):
        J = range(len(chains))
        avg = jnp.where(_iota2((LANES, LANES), 0) // RWKV_N == _iota2((LANES, LANES), 1) // RWKV_N,
                        1.0 / RWKV_N, 0.0).astype(F32)
        S = [st_ref[p] for _, p in chains]
        rhs = [_dot(fr[j]["kg"], S[j], _NT) + fr[j]["AkV"] for j in J]
        U = [_dot(fr[j]["Tm"], rhs[j]) for j in J]
        vu = [jnp.concatenate([fr[j]["vs"], U[j]], axis=0) for j in J]
        for j, (_, p) in zip(J, chains):
            st_ref[p] = S[j] * fr[j]["dec"] + _dot(vu[j], fr[j]["khb"], _TN)
        O = [_dot(fr[j]["rg"], S[j], _NT) + _dot(fr[j]["Bkb"], vu[j]) for j in J]
        o = [O[j][:C, :] + O[j][C:, :] for j in J]
        mean = [_dot_exact_rhs(o[j], avg) for j in J]
        d = [o[j] - mean[j] for j in J]
        var = [_dot_exact_rhs(d[j] * d[j], avg) for j in J]
        for j, (r0, p) in zip(J, chains):
            cols = slice(p * LANES, (p + 1) * LANES)
            on = d[j] * lax.rsqrt(var[j] + LNX_EPS) * lnw_ref[:, cols] + lnb_ref[:, cols]
            o_ref[pl.ds(r0, C), cols] = (on + fr[j]["bonus"]) * g_ref[pl.ds(r0, C), cols]

    def body(ci, carry):
        base = ci * (unroll * C)
        offs = [pl.multiple_of(base + u * C, C) for u in range(unroll)]
        chains = [[(offs[u], p) for p in range(hp)] for u in range(unroll)]
        fr = front([ch for cu in chains for ch in cu])
        for u in range(unroll):
            back(chains[u], fr[u * hp:(u + 1) * hp])
        return carry

    lax.fori_loop(0, nchunks // unroll, body, 0)

    @pl.when(tt == pl.num_programs(2) - 1)
    def _():
        for p in range(hp):
            S = st_ref[p]
            s_ref[2 * p] = S[:RWKV_N, :RWKV_N]
            s_ref[2 * p + 1] = S[RWKV_N:, RWKV_N:]


def _rwkv_prompt(arrs, g, lnw, lnb, rk, B, T, tt_rows=512, hp=8):
    R, D = arrs[0].shape
    NP = D // LANES
    wd = hp * LANES
    tt_rows = min(tt_rows, T)
    nt = T // tt_rows
    col = pl.BlockSpec((tt_rows, wd), lambda b, p, t: (b * nt + t, p))
    par = pl.BlockSpec((1, wd), lambda b, p, t: (0, p))
    o, s = pl.pallas_call(
        functools.partial(_rwkv_prompt_kernel, nchunks=tt_rows // CHUNK, hp=hp),
        grid=(B, NP // hp, nt),
        in_specs=[col] * 7 + [par] * 3,
        out_specs=[col, pl.BlockSpec((None, 2 * hp, RWKV_N, RWKV_N), lambda b, p, t: (b, p, 0, 0))],
        out_shape=[jax.ShapeDtypeStruct((R, D), F32),
                   jax.ShapeDtypeStruct((B, 2 * NP, RWKV_N, RWKV_N), F32)],
        scratch_shapes=[pltpu.VMEM((hp, LANES, LANES), F32)],
        compiler_params=_cparams(("parallel", "parallel", "arbitrary")),
        name="rwkv_prompt",
    )(*arrs, g, lnw.reshape(1, D), lnb.reshape(1, D), rk.reshape(1, D))
    return o, s


def _rwkv_sample_kernel(r_ref, lw_ref, k_ref, v_ref, kap_ref, bet_ref, g_ref, lnw_ref, lnb_ref, rk_ref,
                        s0_ref, o_ref, s_ref, sbd_ref, *, nseq, T, hp):
    C = nseq * T
    C2 = 2 * C
    N = RWKV_N
    mA, mB = _pair_masks()
    bdm = (_iota2((LANES, LANES), 0) // N == _iota2((LANES, LANES), 1) // N).astype(F32)
    rr, cc = _iota2((C2, C2), 0), _iota2((C2, C2), 1)
    eye = (rr == cc).astype(F32)
    same = (rr // T) == (cc // T)
    strict = same & (rr > cc)
    incl = same & (rr >= cc)
    r1, c1 = _iota2((C, C), 0), _iota2((C, C), 1)
    seg_incl = ((r1 // T == c1 // T) & (r1 >= c1)).astype(F32)
    seg_all = (r1 // T == c1 // T).astype(F32)
    zero = jnp.zeros((N, N), F32)

    J = range(hp)
    cols = [slice(j * LANES, (j + 1) * LANES) for j in J]
    ld = lambda ref: [ref[:, cols[j]] for j in J]
    r, lw, k, v, kap, bet = ld(r_ref), ld(lw_ref), ld(k_ref), ld(v_ref), ld(kap_ref), ld(bet_ref)
    cs = [_dot_exact_lhs(seg_incl, lw[j]) for j in J]
    cs_end = [_dot_exact_lhs(seg_all, lw[j]) for j in J]
    sc = [_rwkv_scores(cs[j], lw[j], cs_end[j], r[j], k[j], kap[j], bet[j], mA, mB) for j in J]
    rg, kg, kd, bd, kh, bh = ([s[i] for s in sc] for i in range(6))
    vs = [_stack(v[j], mA, mB) for j in J]
    kbd = [jnp.concatenate([kd[j], bd[j]], axis=0) for j in J]
    sck = [_dot(kg[j], kbd[j], _NT) for j in J]
    scr = [_dot(rg[j], kbd[j], _NT) for j in J]
    Ak = [jnp.where(strict, sck[j][:, :C2], 0.0) for j in J]
    Ab = [jnp.where(strict, sck[j][:, C2:], 0.0) for j in J]
    Bkb = [jnp.concatenate([jnp.where(incl, scr[j][:, :C2], 0.0),
                            jnp.where(incl, -scr[j][:, C2:], 0.0)], axis=1) for j in J]
    Tm = [_neumann_inv(Ab[j], eye, int(math.log2(T)) - 1) for j in J]
    kgp = [kg[j][:C, :] + kg[j][C:, :] for j in J]
    rgp = [rg[j][:C, :] + rg[j][C:, :] for j in J]
    ks = [[] for _ in J]
    rs = [[] for _ in J]
    for n in range(nseq):
        sl = slice(n * T, (n + 1) * T)
        for j in J:
            Sbd = jnp.concatenate(
                [jnp.concatenate([s0_ref[n, 2 * j], zero], axis=1),
                 jnp.concatenate([zero, s0_ref[n, 2 * j + 1]], axis=1)], axis=0)
            sbd_ref[n, j] = Sbd
            res = _dot(jnp.concatenate([kgp[j][sl], rgp[j][sl]], axis=0), Sbd, _NT)
            ks[j].append(res[:T])
            rs[j].append(res[T:])
    KS = [_stack(jnp.concatenate(ks[j], axis=0), mA, mB) for j in J]
    RS = [_stack(jnp.concatenate(rs[j], axis=0), mA, mB) for j in J]
    rhs = [KS[j] + _dot(Ak[j], vs[j]) for j in J]
    U = [_dot(Tm[j], rhs[j]) for j in J]
    vu = [jnp.concatenate([vs[j], U[j]], axis=0) for j in J]
    O = [RS[j] + _dot(Bkb[j], vu[j]) for j in J]
    Up = [U[j][:C, :] + U[j][C:, :] for j in J]
    khp = [kh[j][:C, :] + kh[j][C:, :] for j in J]
    nbhp = [-(bh[j][:C, :] + bh[j][C:, :]) for j in J]
    dec = [jnp.exp(cs_end[j]) for j in J]
    for n in range(nseq):
        sl = slice(n * T, (n + 1) * T)
        for j in J:
            upd = _dot(jnp.concatenate([v[j][sl], Up[j][sl]], axis=0),
                       jnp.concatenate([khp[j][sl], nbhp[j][sl]], axis=0), _TN)
            Sn = sbd_ref[n, j] * dec[j][n * T:n * T + 1, :] + upd * bdm
            s_ref[n, 2 * j] = Sn[:N, :N]
            s_ref[n, 2 * j + 1] = Sn[N:, N:]
    for j in J:
        o = O[j][:C, :] + O[j][C:, :]
        o_ref[:, cols[j]] = _rwkv_epilogue(o, r[j], k[j], v[j], g_ref[:, cols[j]],
                                           lnw_ref[:, cols[j]], lnb_ref[:, cols[j]], rk_ref[:, cols[j]])


def _rwkv_sample(arrs, g, lnw, lnb, rk, s0, T, nseq=16, hp=4):
    R, D = arrs[0].shape
    NP = D // LANES
    B = R // T
    rows = nseq * T
    wd = hp * LANES
    col = pl.BlockSpec((rows, wd), lambda p, gi: (gi, p))
    par = pl.BlockSpec((1, wd), lambda p, gi: (0, p))
    sspec = pl.BlockSpec((nseq, 2 * hp, RWKV_N, RWKV_N), lambda p, gi: (gi, p, 0, 0))
    o, s = pl.pallas_call(
        functools.partial(_rwkv_sample_kernel, nseq=nseq, T=T, hp=hp),
        grid=(NP // hp, B // nseq),
        in_specs=[col] * 7 + [par] * 3 + [sspec],
        out_specs=[col, sspec],
        out_shape=[jax.ShapeDtypeStruct((R, D), F32), jax.ShapeDtypeStruct(s0.shape, F32)],
        scratch_shapes=[pltpu.VMEM((nseq, hp, LANES, LANES), F32)],
        compiler_params=_cparams(("parallel", "parallel")),
        name="rwkv_sample",
    )(*arrs, g, lnw.reshape(1, D), lnb.reshape(1, D), rk.reshape(1, D), s0)
    return o, s


def _merge_kernel(x_ref, ga_ref, gb_ref, oa_ref, ob_ref, w_ref, gt_ref, o_ref):
    m = _sigmoid(ga_ref[...]) * oa_ref[...] + _sigmoid(gb_ref[...]) * ob_ref[...]
    o_ref[...] = x_ref[...] + gt_ref[...] * _dot(m, w_ref[...])


def _merge_out(x, P_g, o_a, o_b, w_out, mod, tm):
    R, D = x.shape
    row = lambda j: pl.BlockSpec((tm, D), lambda i: (i, j))
    return pl.pallas_call(
        _merge_kernel,
        grid=(R // tm,),
        in_specs=[row(0), row(0), row(1), row(0), row(0),
                  pl.BlockSpec((D, D), lambda i: (0, 0)), mod.spec(tm, 2)],
        out_specs=row(0),
        out_shape=jax.ShapeDtypeStruct((R, D), F32),
        compiler_params=_cparams(("parallel",)),
        name="merge_out",
    )(x, P_g, P_g, o_a, o_b, w_out, mod.arr)


def _mlp_kernel(x_ref, nw_ref, sh_ref, sc_ref, gt_ref, wu_ref, wd_ref, fw_ref, o_ref, h_ref, acc_ref):
    kf = pl.program_id(1)

    @pl.when(kf == 0)
    def _():
        h = _rms(x_ref[...], nw_ref[...]) * (1.0 + sc_ref[...]) + sh_ref[...]
        h_ref[...] = h.astype(BF16)
        acc_ref[...] = jnp.zeros_like(acc_ref)

    u = jnp.dot(h_ref[...], wu_ref[...], preferred_element_type=F32)
    u = jnp.square(jnp.maximum(u, 0.0))
    acc_ref[...] += _dot(u, wd_ref[...])

    @pl.when(kf == pl.num_programs(1) - 1)
    def _():
        y = x_ref[...] + gt_ref[...] * acc_ref[...]
        o_ref[...] = _rms(y, fw_ref[...])


def _mlp(x, nw, w_up, w_down, fw, mod, tm, tf=1024):
    R, D = x.shape
    DF = w_up.shape[1]
    return pl.pallas_call(
        _mlp_kernel,
        grid=(R // tm, DF // tf),
        in_specs=[pl.BlockSpec((tm, D), lambda i, kf: (i, 0)),
                  pl.BlockSpec((1, D), lambda i, kf: (0, 0)),
                  mod.spec(tm, 3), mod.spec(tm, 4), mod.spec(tm, 5),
                  pl.BlockSpec((D, tf), lambda i, kf: (0, kf)),
                  pl.BlockSpec((tf, D), lambda i, kf: (kf, 0)),
                  pl.BlockSpec((1, D), lambda i, kf: (0, 0))],
        out_specs=pl.BlockSpec((tm, D), lambda i, kf: (i, 0)),
        out_shape=jax.ShapeDtypeStruct((R, D), F32),
        scratch_shapes=[pltpu.VMEM((tm, D), BF16), pltpu.VMEM((tm, D), F32)],
        compiler_params=_cparams(("parallel", "arbitrary")),
        name="mlp",
    )(x, nw.reshape(1, D), mod.arr, mod.arr, mod.arr, w_up, w_down, fw.reshape(1, D))


def _run(x3, mod_rows, s_hgrn, s_rwkv, s_shift, W, prompt):
    B, T, D = x3.shape
    R = B * T
    x = x3.reshape(R, D)
    if prompt:
        mod = _Mod(mod_rows.reshape(B, 1, -1), True, T, D)
        tm = min(1024, T)
    else:
        mod = _Mod(jnp.repeat(mod_rows, T, axis=0), False, T, D)
        tm = min(512, R)

    h = _norm_mod(x, W["norm1_w"], mod, 0, 1, min(tm, 512))
    P_r = _matmul(h, W["w_r"], tm, RC_PAD // 3, "proj_r")
    P_h = _matmul(h, W["w_h"], tm, 2048, "proj_h")
    P_g = _matmul(h, W["w_g"], tm, 2048, "proj_g")
    shift_out = P_r.reshape(B, T, RC_PAD)[:, T - 1, :RC_COLS]

    rw = (W["mu"], W["w0"], W["a0"], W["k_k"], W["k_a"], W["wwa"], W["g2p"])
    if prompt:
        o_a, hg = _hgrn_prompt(P_h, W["lb_logits"], W["hgrn_norm_w"], B, T)
        shift_in = jnp.zeros((B, 1, RC_PAD), F32)
        arrs = _rwkv_prep(P_r, shift_in, rw, True, T)
        o_b, rs = _rwkv_prompt(arrs[:6], arrs[6], W["lnx_w"], W["lnx_b"], W["r_k"], B, T)
    else:
        o_a, hg = _hgrn_sample(P_h, W["lb_logits"], W["hgrn_norm_w"], s_hgrn, T)
        shift_rows = jnp.repeat(jnp.pad(s_shift, ((0, 0), (0, RC_PAD - RC_COLS))), T, axis=0)
        arrs = _rwkv_prep(P_r, shift_rows, rw, False, T)
        o_b, rs = _rwkv_sample(arrs[:6], arrs[6], W["lnx_w"], W["lnx_b"], W["r_k"], s_rwkv, T)

    x1 = _merge_out(x, P_g, o_a, o_b, W["w_out"], mod, min(tm, 512))
    y = _mlp(x1, W["norm2_w"], W["w_up"], W["w_down"], W["final_norm_w"], mod, min(tm, 512))
    return y.reshape(B, T, D), hg, rs, shift_out


def kernel(x_prompt, x_sample, state_hgrn, state_rwkv, state_shift, c_prompt, c_sample, norm1_w, norm2_w, ada_w, ada_b, w_in, lb_logits, hgrn_norm_w, rwkv_mu, rwkv_w0, rwkv_w2, rwkv_a0, rwkv_a2, rwkv_g2, rwkv_k_k, rwkv_k_a, rwkv_r_k, rwkv_lnx_w, rwkv_lnx_b, w_out, w_up, w_down, final_norm_w):
    depth = norm1_w.shape[0]
    assert depth == 1, "single-layer step"
    D = x_prompt.shape[-1]
    HW = 4 * D
    w_in0 = w_in[0]
    row = lambda a: a.reshape(1, -1)
    wwa = jnp.zeros((LANES, 2 * D), F32)
    wwa = wwa.at[:W_LORA, :D].set(rwkv_w2[0]).at[W_LORA:W_LORA + A_LORA, D:].set(rwkv_a2[0])
    W = {
        "norm1_w": norm1_w[0], "norm2_w": norm2_w[0], "final_norm_w": final_norm_w,
        "w_h": w_in0[:, :HW].astype(BF16),
        "w_r": jnp.pad(w_in0[:, HW:HW + RC_COLS], ((0, 0), (0, RC_PAD - RC_COLS))).astype(BF16),
        "w_g": w_in0[:, HW + RC_COLS:].astype(BF16),
        "lb_logits": lb_logits, "hgrn_norm_w": hgrn_norm_w[0],
        "mu": jnp.pad(row(rwkv_mu[0]), ((0, 0), (0, RC_PAD - RC_COLS))),
        "w0": row(rwkv_w0[0]), "a0": row(rwkv_a0[0]), "k_k": row(rwkv_k_k[0]), "k_a": row(rwkv_k_a[0]),
        "wwa": wwa.astype(BF16),
        "g2p": jnp.pad(rwkv_g2[0], ((0, G_LORA_PAD - G_LORA), (0, 0))).astype(BF16),
        "lnx_w": rwkv_lnx_w[0], "lnx_b": rwkv_lnx_b[0], "r_k": rwkv_r_k[0].reshape(-1),
        "w_out": w_out[0].astype(BF16), "w_up": w_up[0].astype(BF16), "w_down": w_down[0].astype(BF16),
    }
    nb = x_prompt.shape[0]
    mod = _adaln(jnp.concatenate([c_prompt, c_sample], axis=0), ada_w[0], ada_b[0])
    y_p, hg_p, rw_p, sh_p = _run(x_prompt, mod[:nb], None, None, None, W, True)
    y_s, hg_s, rw_s, sh_s = _run(x_sample, mod[nb:], state_hgrn[0], state_rwkv[0], state_shift[0], W, False)
    ex = lambda a: a[None]
    return (y_p, y_s, ex(hg_p), ex(rw_p), ex(sh_p), ex(hg_s), ex(rw_s), ex(sh_s))
```

```python
import functools
import math

import jax
import jax.numpy as jnp
from jax import lax
from jax.experimental import pallas as pl
from jax.experimental.pallas import tpu as pltpu

F32 = jnp.float32
BF16 = jnp.bfloat16

LANES = 128
SUBLANES = 8
VMEM_LIMIT = 48 * 1024 * 1024

HGRN_DK = 128
RWKV_N = 64
W_LORA = 64
A_LORA = 64
G_LORA = 160
G_LORA_PAD = 256
RC_MAIN = 3 * 1024
RC_COLS = RC_MAIN + W_LORA + A_LORA + G_LORA
RC_PAD = RC_MAIN + LANES + G_LORA_PAD
RMS_EPS = 1e-6
LNX_EPS = 64e-5
CHUNK = 64
SUB = 8
RWKV_SUB = 8


def _cparams(sem, vmem=VMEM_LIMIT):
    return pltpu.CompilerParams(dimension_semantics=sem, vmem_limit_bytes=vmem)


_NN = (((1,), (0,)), ((), ()))
_NT = (((1,), (1,)), ((), ()))
_TN = (((0,), (0,)), ((), ()))


def _dot(a, b, dims=_NN):
    return lax.dot_general(a.astype(BF16), b.astype(BF16), dims, preferred_element_type=F32)


def _split2(x):
    hi = x.astype(BF16)
    lo = (x - hi.astype(F32)).astype(BF16)
    return hi, lo


def _split3(x):
    hi = x.astype(BF16)
    r1 = x - hi.astype(F32)
    mid = r1.astype(BF16)
    lo = (r1 - mid.astype(F32)).astype(BF16)
    return hi, mid, lo


def _dot_exact_lhs(a, x, dims=_NN):
    a = a.astype(BF16)
    hi, mid, lo = _split3(x)
    d = lambda p: lax.dot_general(a, p, dims, preferred_element_type=F32)
    return d(hi) + d(mid) + d(lo)


def _dot_exact_rhs(x, b, dims=_NN, parts=3):
    b = b.astype(BF16)
    d = lambda p: lax.dot_general(p, b, dims, preferred_element_type=F32)
    if parts == 2:
        hi, lo = _split2(x)
        return d(hi) + d(lo)
    hi, mid, lo = _split3(x)
    return d(hi) + d(mid) + d(lo)


def _dot_hp(a, b, dims=_NN):
    ah, al = _split2(a)
    bh, bl = _split2(b)
    d = lambda p, q: lax.dot_general(p, q, dims, preferred_element_type=F32)
    return d(ah, bh) + d(ah, bl) + d(al, bh)


def _sigmoid(x):
    return 1.0 / (1.0 + jnp.exp(-x))


def _iota2(shape, dim):
    return lax.broadcasted_iota(jnp.int32, shape, dim)


def _neumann_inv(L, eye, nsq):
    X = L
    P = eye - L
    for _ in range(nsq):
        X = _dot_hp(X, X)
        P = P + _dot_hp(P, X)
    return P


def _adaln_kernel(c_ref, w_ref, b_ref, o_ref):
    c = c_ref[...]
    s = c * _sigmoid(c)
    o_ref[...] = _dot(s, w_ref[...]) + b_ref[...]


def _adaln(c_all, ada_w, ada_b, tn=1536):
    R, D = c_all.shape
    N = ada_w.shape[1]
    return pl.pallas_call(
        _adaln_kernel,
        grid=(N // tn,),
        in_specs=[pl.BlockSpec((R, D), lambda j: (0, 0)),
                  pl.BlockSpec((D, tn), lambda j: (0, j)),
                  pl.BlockSpec((1, tn), lambda j: (0, j))],
        out_specs=pl.BlockSpec((R, tn), lambda j: (0, j)),
        out_shape=jax.ShapeDtypeStruct((R, N), F32),
        compiler_params=_cparams(("parallel",)),
        name="adaln",
    )(c_all, ada_w, ada_b.reshape(1, N))


class _Mod:
    def __init__(self, arr, per_seq, seq_rows, D):
        self.arr, self.per_seq, self.seq_rows, self.D = arr, per_seq, seq_rows, D

    def spec(self, tm, comp):
        D = self.D
        if self.per_seq:
            tiles = self.seq_rows // tm
            return pl.BlockSpec((None, 1, D), lambda i, *_: (i // tiles, 0, comp))
        return pl.BlockSpec((tm, D), lambda i, *_: (i, comp))


def _rms(x, w):
    return x * lax.rsqrt(jnp.mean(x * x, axis=-1, keepdims=True) + RMS_EPS) * w


def _mm_kernel(x_ref, wt_ref, o_ref):
    o_ref[...] = lax.dot_general(x_ref[...], wt_ref[...], _NT, preferred_element_type=F32)


def _matmul(x, wt, tm, tn, name):
    R, K = x.shape
    N = wt.shape[0]
    return pl.pallas_call(
        _mm_kernel,
        grid=(R // tm, N // tn),
        in_specs=[pl.BlockSpec((tm, K), lambda i, j: (i, 0)),
                  pl.BlockSpec((tn, K), lambda i, j: (j, 0))],
        out_specs=pl.BlockSpec((tm, tn), lambda i, j: (i, j)),
        out_shape=jax.ShapeDtypeStruct((R, N), F32),
        compiler_params=_cparams(("parallel", "arbitrary")),
        name=name,
    )(x, wt)


def _hgrn_gates(fp, lb_logits):
    m = jnp.max(lb_logits, axis=0, keepdims=True)
    e = jnp.exp(lb_logits - m)
    lb = e[0:1, :] / jnp.sum(e, axis=0, keepdims=True)
    f = lb + (1.0 - lb) * _sigmoid(fp)
    return jnp.log(f), 1.0 - f


def _hgrn_diag(q, kin, b, c):
    R = q.shape[0]
    nb = R // c
    q3 = q.reshape(nb, c, LANES)
    k3 = kin.reshape(nb, c, LANES)
    b3 = (b * math.log2(math.e)).reshape(nb, c, LANES)
    lane_mod = lax.broadcasted_iota(jnp.int32, (nb, c, R), 2) % c
    acc = jnp.zeros((nb, c, R), F32)
    for s in range(c):
        d = jnp.minimum(b3 - b3[:, s:s + 1, :], 0.0)
        col = jnp.sum(q3 * k3[:, s:s + 1, :] * jnp.exp2(d), axis=-1, keepdims=True)
        acc = jnp.where(lane_mod == s, col, acc)
    rr, cc = _iota2((R, R), 0), _iota2((R, R), 1)
    keep = ((rr // c) == (cc // c)) & (rr >= cc)
    return jnp.where(keep, acc.reshape(R, R), 0.0)


def _hgrn_cross(q, kin, b, m):
    R = q.shape[0]
    h = m // 2
    nb = R // m
    ref = jnp.broadcast_to(b.reshape(nb, m, LANES)[:, h - 1:h, :], (nb, m, LANES)).reshape(R, LANES)
    upper = (_iota2((R, 1), 0) % m) >= h
    qs = q * jnp.exp(jnp.where(upper, b - ref, -1e30))
    ks = kin * jnp.exp(jnp.where(upper, -1e30, ref - b))
    sc = _dot(qs, ks, _NT)
    if nb == 1:
        return sc
    same_blk = (_iota2((R, R), 0) // m) == (_iota2((R, R), 1) // m)
    return jnp.where(same_blk, sc, 0.0)


def _hgrn_out(o, og, nw):
    o = o * lax.rsqrt(jnp.mean(o * o, axis=-1, keepdims=True) + RMS_EPS) * nw
    return o * (og * _sigmoid(og))


def _hgrn_prompt_kernel(q_ref, f_ref, i_ref, g_ref, lb_ref, nw_ref, o_ref, s_ref, st_ref, *, nchunks, hp):
    tt = pl.program_id(2)

    @pl.when(tt == 0)
    def _():
        st_ref[...] = jnp.zeros_like(st_ref)

    C, c = CHUNK, SUB
    tri = (_iota2((C, C), 0) >= _iota2((C, C), 1)).astype(F32)
    nw = nw_ref[...]

    def chunk(ci, carry):
        r0 = pl.multiple_of(ci * C, C)
        H = range(hp)
        cols = [slice(j * LANES, (j + 1) * LANES) for j in H]
        q = [q_ref[pl.ds(r0, C), cols[j]] for j in H]
        v = [i_ref[pl.ds(r0, C), cols[j]] for j in H]
        gates = [_hgrn_gates(f_ref[pl.ds(r0, C), cols[j]], lb_ref[:, cols[j]]) for j in H]
        lf = [gt[0] for gt in gates]
        kin = [gt[1] for gt in gates]
        b = [_dot_exact_lhs(tri, lf[j]) for j in H]
        b_last = [b[j][C - 1:C, :] for j in H]
        ST = [st_ref[j] for j in H]
        o = [_dot(q[j] * jnp.exp(b[j]), ST[j], _NT) for j in H]
        for j in H:
            st_ref[j] = ST[j] * jnp.exp(b_last[j]) + _dot(v[j], kin[j] * jnp.exp(b_last[j] - b[j]), _TN)
        A = [_hgrn_diag(q[j], kin[j], b[j], c) for j in H]
        m = 2 * c
        while m <= C:
            A = [A[j] + _hgrn_cross(q[j], kin[j], b[j], m) for j in H]
            m *= 2
        o = [o[j] + _dot(A[j], v[j]) for j in H]
        for j in H:
            o_ref[pl.ds(r0, C), cols[j]] = _hgrn_out(o[j], g_ref[pl.ds(r0, C), cols[j]], nw)
        return carry

    lax.fori_loop(0, nchunks, chunk, 0)

    @pl.when(tt == pl.num_programs(2) - 1)
    def _():
        for j in range(hp):
            s_ref[j] = st_ref[j].T


def _hgrn_prompt(P_h, lb_logits, nw, B, T, tt_rows=512, hp=8):
    R, W = P_h.shape
    H = W // (4 * HGRN_DK)
    HG = H // hp
    wd = hp * HGRN_DK
    tt_rows = min(tt_rows, T)
    nt = T // tt_rows
    col = lambda off: pl.BlockSpec((tt_rows, wd), lambda b, h, t: (b * nt + t, off * HG + h))
    o, s = pl.pallas_call(
        functools.partial(_hgrn_prompt_kernel, nchunks=tt_rows // CHUNK, hp=hp),
        grid=(B, HG, nt),
        in_specs=[col(0), col(1), col(2), col(3),
                  pl.BlockSpec((lb_logits.shape[0], wd), lambda b, h, t: (0, h)),
                  pl.BlockSpec((1, HGRN_DK), lambda b, h, t: (0, 0))],
        out_specs=[pl.BlockSpec((tt_rows, wd), lambda b, h, t: (b * nt + t, h)),
                   pl.BlockSpec((None, hp, HGRN_DK, HGRN_DK), lambda b, h, t: (b, h, 0, 0))],
        out_shape=[jax.ShapeDtypeStruct((R, H * HGRN_DK), F32),
                   jax.ShapeDtypeStruct((B, H, HGRN_DK, HGRN_DK), F32)],
        scratch_shapes=[pltpu.VMEM((hp, HGRN_DK, HGRN_DK), F32)],
        compiler_params=_cparams(("parallel", "parallel", "arbitrary")),
        name="hgrn_prompt",
    )(P_h, P_h, P_h, P_h, lb_logits, nw.reshape(1, HGRN_DK))
    return o, s


def _hgrn_sample_kernel(q_ref, f_ref, i_ref, g_ref, lb_ref, nw_ref, s0_ref, o_ref, s_ref, *, nseq, T, hp):
    R = nseq * T
    rr, cc = _iota2((R, R), 0), _iota2((R, R), 1)
    same = (rr // T) == (cc // T)
    seg_incl = (same & (rr >= cc)).astype(F32)
    seg_all = same.astype(F32)
    nw = nw_ref[...]
    H = range(hp)
    cols = [slice(j * LANES, (j + 1) * LANES) for j in H]
    q = [q_ref[:, cols[j]] for j in H]
    v = [i_ref[:, cols[j]] for j in H]
    gates = [_hgrn_gates(f_ref[:, cols[j]], lb_ref[:, cols[j]]) for j in H]
    lf = [gt[0] for gt in gates]
    kin = [gt[1] for gt in gates]
    b = [_dot_exact_lhs(seg_incl, lf[j]) for j in H]
    b_tot = [_dot_exact_lhs(seg_all, lf[j]) for j in H]
    A = [_hgrn_diag(q[j], kin[j], b[j], T) for j in H]
    o = [_dot(A[j], v[j]) for j in H]
    qg = [q[j] * jnp.exp(b[j]) for j in H]
    kh = [kin[j] * jnp.exp(b_tot[j] - b[j]) for j in H]
    ecol = [jnp.transpose(jnp.exp(b_tot[j]).reshape(nseq, T, LANES)[:, 0, :]) for j in H]
    inter = [[] for _ in H]
    for n in range(nseq):
        sl = slice(n * T, (n + 1) * T)
        for j in H:
            S = s0_ref[n, j]
            inter[j].append(_dot(qg[j][sl], S))
            s_ref[n, j] = S * ecol[j][:, n:n + 1] + _dot(kh[j][sl], v[j][sl], _TN)
    for j in H:
        oj = o[j] + jnp.concatenate(inter[j], axis=0)
        o_ref[:, cols[j]] = _hgrn_out(oj, g_ref[:, cols[j]], nw)


def _hgrn_sample(P_h, lb_logits, nw, s0, T, nseq=16, hp=4):
    R, W = P_h.shape
    H = W // (4 * HGRN_DK)
    HG = H // hp
    wd = hp * HGRN_DK
    B = R // T
    rows = nseq * T
    col = lambda off: pl.BlockSpec((rows, wd), lambda h, g: (g, off * HG + h))
    sspec = pl.BlockSpec((nseq, hp, HGRN_DK, HGRN_DK), lambda h, g: (g, h, 0, 0))
    o, s = pl.pallas_call(
        functools.partial(_hgrn_sample_kernel, nseq=nseq, T=T, hp=hp),
        grid=(HG, B // nseq),
        in_specs=[col(0), col(1), col(2), col(3),
                  pl.BlockSpec((lb_logits.shape[0], wd), lambda h, g: (0, h)),
                  pl.BlockSpec((1, HGRN_DK), lambda h, g: (0, 0)),
                  sspec],
        out_specs=[pl.BlockSpec((rows, wd), lambda h, g: (g, h)), sspec],
        out_shape=[jax.ShapeDtypeStruct((R, H * HGRN_DK), F32),
                   jax.ShapeDtypeStruct(s0.shape, F32)],
        compiler_params=_cparams(("parallel", "parallel")),
        name="hgrn_sample",
    )(P_h, P_h, P_h, P_h, lb_logits, nw.reshape(1, HGRN_DK), s0)
    return o, s


def _rwkv_prep_body(rc, prev, mu_ref, w0_ref, a0_ref, kk_ref, ka_ref, wwa_ref, g2_ref, outs, rows):
    r_o, lw_o, k_o, v_o, kap_o, bet_o, g_o = outs
    D = RC_MAIN // 3
    xs = rc + mu_ref[...] * (prev - rc)
    r = xs[:, 0:D]
    kb = xs[:, D:2 * D]
    v = xs[:, 2 * D:3 * D]
    xwa = xs[:, RC_MAIN:RC_MAIN + LANES]
    xg = xs[:, RC_MAIN + LANES:]
    lane = _iota2((1, LANES), 1)
    wa = _dot(jnp.where(lane < W_LORA, jnp.tanh(xwa), xwa), wwa_ref[...])
    w = w0_ref[...] + wa[:, :D]
    a = _sigmoid(a0_ref[...] + wa[:, D:])
    lw = -math.exp(-0.5) * _sigmoid(w)
    g = _dot(_sigmoid(xg), g2_ref[...])
    kk = kb * kk_ref[...]
    seg = (_iota2((LANES, LANES), 0) // RWKV_N == _iota2((LANES, LANES), 1) // RWKV_N).astype(F32)
    k_a = ka_ref[...]
    for p in range(D // LANES):
        sl = slice(p * LANES, (p + 1) * LANES)
        kkp = kk[:, sl]
        kap = kkp * lax.rsqrt(jnp.maximum(_dot_exact_rhs(kkp * kkp, seg, parts=2), 1e-24))
        ap = a[:, sl]
        kap_o[rows, sl] = kap
        bet_o[rows, sl] = ap * kap
        k_o[rows, sl] = kb[:, sl] * (1.0 + (ap - 1.0) * k_a[:, sl])
    r_o[rows, :] = r
    lw_o[rows, :] = lw
    v_o[rows, :] = v
    g_o[rows, :] = g


def _in_rwkv_kernel(x_ref, nw_ref, sh_ref, sc_ref, wr_ref, shift_ref, mu_ref, w0_ref, a0_ref, kk_ref,
                    ka_ref, wwa_ref, g2_ref, h_ref, r_o, lw_o, k_o, v_o, kap_o, bet_o, g_o, last_o,
                    carry_ref, rc_ref, *, seq_rows, sub):
    i = pl.program_id(0)
    tm = x_ref.shape[0]
    per_seq = seq_rows >= tm

    @pl.when(i == 0)
    def _():
        carry_ref[...] = jnp.zeros_like(carry_ref)

    h = (_rms(x_ref[...], nw_ref[...]) * (1.0 + sc_ref[...]) + sh_ref[...]).astype(BF16)
    h_ref[...] = h
    outs = (r_o, lw_o, k_o, v_o, kap_o, bet_o, g_o)
    rid = _iota2((sub, 1), 0)
    rc_ref[...] = lax.dot_general(h, wr_ref[...], _NT, preferred_element_type=F32)
    prev_last = None
    for s in range(tm // sub):
        rows = slice(s * sub, (s + 1) * sub)
        rc = rc_ref[rows, :]
        rolled = pltpu.roll(rc, 1, axis=0)
        if per_seq:
            if s == 0:
                start = (i % (seq_rows // tm)) == 0
                first = jnp.where(start, shift_ref[...], carry_ref[SUBLANES - 1:SUBLANES, :])
            else:
                first = prev_last
            prev = jnp.where(rid == 0, first, rolled)
        else:
            prev = jnp.where(rid % seq_rows == 0, shift_ref[rows, :], rolled)
            last_o[rows, :] = rc
        prev_last = rc[sub - 1:sub, :]
        if s == tm // sub - 1 and per_seq:
            carry_ref[...] = rc[sub - SUBLANES:, :]
            last_o[...] = prev_last
        _rwkv_prep_body(rc, prev, mu_ref, w0_ref, a0_ref, kk_ref, ka_ref, wwa_ref, g2_ref, outs, rows)


def _in_rwkv(x, nw, mod, w_r, shift, params, per_seq, seq_rows, tm=256, sub=128):
    R, D = x.shape
    tm = min(tm, seq_rows) if per_seq else min(tm, R)
    sub = min(sub, tm)
    mu, w0, a0, k_k, k_a, wwa, g2p = params
    const = lambda a: pl.BlockSpec(a.shape, lambda i: (0,) * a.ndim, pipeline_mode=pl.Buffered(1))
    wspecs = [const(a) for a in (mu, w0, a0, k_k, k_a, wwa, g2p)]
    row = lambda w: pl.BlockSpec((tm, w), lambda i: (i, 0))
    if per_seq:
        tiles = seq_rows // tm
        shift_spec = pl.BlockSpec((None, 1, RC_PAD), lambda i: (i // tiles, 0, 0))
        last_spec = pl.BlockSpec((None, 1, RC_PAD), lambda i: (i // tiles, 0, 0))
        last_shape = jax.ShapeDtypeStruct((R // seq_rows, 1, RC_PAD), F32)
    else:
        shift_spec = row(RC_PAD)
        last_spec = row(RC_PAD)
        last_shape = jax.ShapeDtypeStruct((R, RC_PAD), F32)
    outs = pl.pallas_call(
        functools.partial(_in_rwkv_kernel, seq_rows=seq_rows, sub=sub),
        grid=(R // tm,),
        in_specs=[row(D), const(nw), mod.spec(tm, 0), mod.spec(tm, 1), const(w_r), shift_spec] + wspecs,
        out_specs=[row(D)] + [row(D)] * 7 + [last_spec],
        out_shape=[jax.ShapeDtypeStruct((R, D), BF16)] + [jax.ShapeDtypeStruct((R, D), F32)] * 7
                  + [last_shape],
        scratch_shapes=[pltpu.VMEM((SUBLANES, RC_PAD), F32), pltpu.VMEM((tm, RC_PAD), F32)],
        compiler_params=_cparams(("arbitrary",)),
        name="in_rwkv",
    )(x, nw, mod.arr, mod.arr, w_r, shift, mu, w0, a0, k_k, k_a, wwa, g2p)
    return outs[0], outs[1:8], outs[8]


def _pair_masks():
    lane = _iota2((1, LANES), 1)
    return (lane < RWKV_N).astype(F32), (lane >= RWKV_N).astype(F32)


def _stack(x, mA, mB):
    return jnp.concatenate([x * mA, x * mB], axis=0)


def _rwkv_scores(cs, lw, cs_end, r, k, kap, bet, mA, mB):
    e_neg = jnp.exp(-cs)
    e_rem = jnp.exp(cs_end - cs)
    rg = _stack(r * jnp.exp(cs), mA, mB)
    kg = _stack(kap * jnp.exp(cs - lw), mA, mB)
    kd = _stack(k * e_neg, mA, mB)
    bd = _stack(bet * e_neg, mA, mB)
    kh = _stack(k * e_rem, mA, mB)
    bh = _stack(bet * e_rem, mA, mB)
    return rg, kg, kd, bd, kh, bh


def _rwkv_epilogue(o, r, k, v, g, lnw, lnb, rk):
    avg = jnp.where(_iota2((LANES, LANES), 0) // RWKV_N == _iota2((LANES, LANES), 1) // RWKV_N,
                    1.0 / RWKV_N, 0.0).astype(F32)
    mean = _dot_exact_rhs(o, avg)
    d = o - mean
    var = _dot_exact_rhs(d * d, avg)
    on = d * lax.rsqrt(var + LNX_EPS) * lnw + lnb
    bonus = _dot_exact_rhs(r * k * rk, avg) * float(RWKV_N) * v
    return (on + bonus) * g


def _rwkv_prompt_kernel(r_ref, lw_ref, k_ref, v_ref, kap_ref, bet_ref, g_ref, lnw_ref, lnb_ref, rk_ref,
                        o_ref, s_ref, st_ref, *, nchunks, hp):
    tt = pl.program_id(2)

    @pl.when(tt == 0)
    def _():
        st_ref[...] = jnp.zeros_like(st_ref)

    C, c = CHUNK, RWKV_SUB
    C2 = 2 * C
    mA, mB = _pair_masks()
    rr, cc = _iota2((C2, C2), 0), _iota2((C2, C2), 1)
    eye = (rr == cc).astype(F32)
    strict = rr > cc
    incl = rr >= cc
    blk = (rr // c) == (cc // c)
    tri = (_iota2((C, C), 0) >= _iota2((C, C), 1)).astype(F32)
    unroll = 2 if nchunks % 2 == 0 else 1

    def front(chains):
        J = range(len(chains))
        ld = lambda ref: [ref[pl.ds(r0, C), p * LANES:(p + 1) * LANES] for r0, p in chains]
        r, lw, k, v, kap, bet = ld(r_ref), ld(lw_ref), ld(k_ref), ld(v_ref), ld(kap_ref), ld(bet_ref)
        cs = [_dot_exact_lhs(tri, lw[j]) for j in J]
        cs_end = [cs[j][C - 1:C, :] for j in J]
        sc = [_rwkv_scores(cs[j], lw[j], cs_end[j], r[j], k[j], kap[j], bet[j], mA, mB) for j in J]
        rg, kg, kd, bd, kh, bh = ([s[i] for s in sc] for i in range(6))
        vs = [_stack(v[j], mA, mB) for j in J]
        kbd = [jnp.concatenate([kd[j], bd[j]], axis=0) for j in J]
        sck = [_dot(kg[j], kbd[j], _NT) for j in J]
        scr = [_dot(rg[j], kbd[j], _NT) for j in J]
        Ak = [jnp.where(strict, sck[j][:, :C2], 0.0) for j in J]
        Ab = [jnp.where(strict, sck[j][:, C2:], 0.0) for j in J]
        Bkb = [jnp.concatenate([jnp.where(incl, scr[j][:, :C2], 0.0),
                                jnp.where(incl, -scr[j][:, C2:], 0.0)], axis=1) for j in J]
        LD = [jnp.where(blk, Ab[j], 0.0) for j in J]
        X = LD
        Dinv = [eye - LD[j] for j in J]
        for _ in range(int(math.log2(c)) - 1):
            X = [_dot(X[j], X[j]) for j in J]
            Dinv = [Dinv[j] + _dot(Dinv[j], X[j]) for j in J]
        Q = [_dot(Dinv[j], Ab[j] - LD[j]) for j in J]
        T0 = [Dinv[j] - _dot(Q[j], Dinv[j]) for j in J]
        for _ in range(int(math.log2(C // c)) - 1):
            Q = [_dot(Q[j], Q[j]) for j in J]
            T0 = [T0[j] + _dot(Q[j], T0[j]) for j in J]
        E = [(eye - T0[j]) - _dot_hp(Ab[j], T0[j]) for j in J]
        Tm = [T0[j] + _dot(T0[j], E[j]) for j in J]
        khb = [jnp.concatenate([kh[j], -bh[j]], axis=0) for j in J]
        AkV = [_dot(Ak[j], vs[j]) for j in J]
        dec = [jnp.exp(cs_end[j]) for j in J]
        avg = jnp.where(_iota2((LANES, LANES), 0) // RWKV_N == _iota2((LANES, LANES), 1) // RWKV_N,
                        1.0 / RWKV_N, 0.0).astype(F32)
        bonus = [_dot_exact_rhs(r[j] * k[j] * rk_ref[:, p * LANES:(p + 1) * LANES], avg)
                 * float(RWKV_N) * v[j] for j, (_, p) in zip(J, chains)]
        return [dict(kg=kg[j], rg=rg[j], vs=vs[j], Bkb=Bkb[j], Tm=Tm[j], khb=khb[j], AkV=AkV[j],
                     dec=dec[j], bonus=bonus[j]) for j in J]

    def back(chains, fr):
        J = range(len(chains))
        avg = jnp.where(_iota2((LANES, LANES), 0) // RWKV_N == _iota2((LANES, LANES), 1) // RWKV_N,
                        1.0 / RWKV_N, 0.0).astype(F32)
        S = [st_ref[p] for _, p in chains]
        rhs = [_dot(fr[j]["kg"], S[j], _NT) + fr[j]["AkV"] for j in J]
        U = [_dot(fr[j]["Tm"], rhs[j]) for j in J]
        vu = [jnp.concatenate([fr[j]["vs"], U[j]], axis=0) for j in J]
        for j, (_, p) in zip(J, chains):
            st_ref[p] = S[j] * fr[j]["dec"] + _dot(vu[j], fr[j]["khb"], _TN)
        O = [_dot(fr[j]["rg"], S[j], _NT) + _dot(fr[j]["Bkb"], vu[j]) for j in J]
        o = [O[j][:C, :] + O[j][C:, :] for j in J]
        mean = [_dot_exact_rhs(o[j], avg) for j in J]
        d = [o[j] - mean[j] for j in J]
        var = [_dot_exact_rhs(d[j] * d[j], avg) for j in J]
        for j, (r0, p) in zip(J, chains):
            cols = slice(p * LANES, (p + 1) * LANES)
            on = d[j] * lax.rsqrt(var[j] + LNX_EPS) * lnw_ref[:, cols] + lnb_ref[:, cols]
            o_ref[pl.ds(r0, C), cols] = (on + fr[j]["bonus"]) * g_ref[pl.ds(r0, C), cols]

    def body(ci, carry):
        base = ci * (unroll * C)
        offs = [pl.multiple_of(base + u * C, C) for u in range(unroll)]
        chains = [[(offs[u], p) for p in range(hp)] for u in range(unroll)]
        fr = front([ch for cu in chains for ch in cu])
        for u in range(unroll):
            back(chains[u], fr[u * hp:(u + 1) * hp])
        return carry

    lax.fori_loop(0, nchunks // unroll, body, 0)

    @pl.when(tt == pl.num_programs(2) - 1)
    def _():
        for p in range(hp):
            S = st_ref[p]
            s_ref[2 * p] = S[:RWKV_N, :RWKV_N]
            s_ref[2 * p + 1] = S[RWKV_N:, RWKV_N:]


def _rwkv_prompt(arrs, g, lnw, lnb, rk, B, T, tt_rows=512, hp=8):
    R, D = arrs[0].shape
    NP = D // LANES
    wd = hp * LANES
    tt_rows = min(tt_rows, T)
    nt = T // tt_rows
    col = pl.BlockSpec((tt_rows, wd), lambda b, p, t: (b * nt + t, p))
    par = pl.BlockSpec((1, wd), lambda b, p, t: (0, p))
    o, s = pl.pallas_call(
        functools.partial(_rwkv_prompt_kernel, nchunks=tt_rows // CHUNK, hp=hp),
        grid=(B, NP // hp, nt),
        in_specs=[col] * 7 + [par] * 3,
        out_specs=[col, pl.BlockSpec((None, 2 * hp, RWKV_N, RWKV_N), lambda b, p, t: (b, p, 0, 0))],
        out_shape=[jax.ShapeDtypeStruct((R, D), F32),
                   jax.ShapeDtypeStruct((B, 2 * NP, RWKV_N, RWKV_N), F32)],
        scratch_shapes=[pltpu.VMEM((hp, LANES, LANES), F32)],
        compiler_params=_cparams(("parallel", "parallel", "arbitrary")),
        name="rwkv_prompt",
    )(*arrs, g, lnw.reshape(1, D), lnb.reshape(1, D), rk.reshape(1, D))
    return o, s


def _rwkv_sample_kernel(r_ref, lw_ref, k_ref, v_ref, kap_ref, bet_ref, g_ref, lnw_ref, lnb_ref, rk_ref,
                        s0_ref, o_ref, s_ref, sbd_ref, *, nseq, T, hp):
    C = nseq * T
    C2 = 2 * C
    N = RWKV_N
    mA, mB = _pair_masks()
    bdm = (_iota2((LANES, LANES), 0) // N == _iota2((LANES, LANES), 1) // N).astype(F32)
    rr, cc = _iota2((C2, C2), 0), _iota2((C2, C2), 1)
    eye = (rr == cc).astype(F32)
    same = (rr // T) == (cc // T)
    strict = same & (rr > cc)
    incl = same & (rr >= cc)
    r1, c1 = _iota2((C, C), 0), _iota2((C, C), 1)
    seg_incl = ((r1 // T == c1 // T) & (r1 >= c1)).astype(F32)
    seg_all = (r1 // T == c1 // T).astype(F32)
    zero = jnp.zeros((N, N), F32)

    J = range(hp)
    cols = [slice(j * LANES, (j + 1) * LANES) for j in J]
    ld = lambda ref: [ref[:, cols[j]] for j in J]
    r, lw, k, v, kap, bet = ld(r_ref), ld(lw_ref), ld(k_ref), ld(v_ref), ld(kap_ref), ld(bet_ref)
    cs = [_dot_exact_lhs(seg_incl, lw[j]) for j in J]
    cs_end = [_dot_exact_lhs(seg_all, lw[j]) for j in J]
    sc = [_rwkv_scores(cs[j], lw[j], cs_end[j], r[j], k[j], kap[j], bet[j], mA, mB) for j in J]
    rg, kg, kd, bd, kh, bh = ([s[i] for s in sc] for i in range(6))
    vs = [_stack(v[j], mA, mB) for j in J]
    kbd = [jnp.concatenate([kd[j], bd[j]], axis=0) for j in J]
    sck = [_dot(kg[j], kbd[j], _NT) for j in J]
    scr = [_dot(rg[j], kbd[j], _NT) for j in J]
    Ak = [jnp.where(strict, sck[j][:, :C2], 0.0) for j in J]
    Ab = [jnp.where(strict, sck[j][:, C2:], 0.0) for j in J]
    Bkb = [jnp.concatenate([jnp.where(incl, scr[j][:, :C2], 0.0),
                            jnp.where(incl, -scr[j][:, C2:], 0.0)], axis=1) for j in J]
    Tm = [_neumann_inv(Ab[j], eye, int(math.log2(T)) - 1) for j in J]
    kgp = [kg[j][:C, :] + kg[j][C:, :] for j in J]
    rgp = [rg[j][:C, :] + rg[j][C:, :] for j in J]
    ks = [[] for _ in J]
    rs = [[] for _ in J]
    for n in range(nseq):
        sl = slice(n * T, (n + 1) * T)
        for j in J:
            Sbd = jnp.concatenate(
                [jnp.concatenate([s0_ref[n, 2 * j], zero], axis=1),
                 jnp.concatenate([zero, s0_ref[n, 2 * j + 1]], axis=1)], axis=0)
            sbd_ref[n, j] = Sbd
            res = _dot(jnp.concatenate([kgp[j][sl], rgp[j][sl]], axis=0), Sbd, _NT)
            ks[j].append(res[:T])
            rs[j].append(res[T:])
    KS = [_stack(jnp.concatenate(ks[j], axis=0), mA, mB) for j in J]
    RS = [_stack(jnp.concatenate(rs[j], axis=0), mA, mB) for j in J]
    rhs = [KS[j] + _dot(Ak[j], vs[j]) for j in J]
    U = [_dot(Tm[j], rhs[j]) for j in J]
    vu = [jnp.concatenate([vs[j], U[j]], axis=0) for j in J]
    O = [RS[j] + _dot(Bkb[j], vu[j]) for j in J]
    Up = [U[j][:C, :] + U[j][C:, :] for j in J]
    khp = [kh[j][:C, :] + kh[j][C:, :] for j in J]
    nbhp = [-(bh[j][:C, :] + bh[j][C:, :]) for j in J]
    dec = [jnp.exp(cs_end[j]) for j in J]
    for n in range(nseq):
        sl = slice(n * T, (n + 1) * T)
        for j in J:
            upd = _dot(jnp.concatenate([v[j][sl], Up[j][sl]], axis=0),
                       jnp.concatenate([khp[j][sl], nbhp[j][sl]], axis=0), _TN)
            Sn = sbd_ref[n, j] * dec[j][n * T:n * T + 1, :] + upd * bdm
            s_ref[n, 2 * j] = Sn[:N, :N]
            s_ref[n, 2 * j + 1] = Sn[N:, N:]
    for j in J:
        o = O[j][:C, :] + O[j][C:, :]
        o_ref[:, cols[j]] = _rwkv_epilogue(o, r[j], k[j], v[j], g_ref[:, cols[j]],
                                           lnw_ref[:, cols[j]], lnb_ref[:, cols[j]], rk_ref[:, cols[j]])


def _rwkv_sample(arrs, g, lnw, lnb, rk, s0, T, nseq=16, hp=4):
    R, D = arrs[0].shape
    NP = D // LANES
    B = R // T
    rows = nseq * T
    wd = hp * LANES
    col = pl.BlockSpec((rows, wd), lambda p, gi: (gi, p))
    par = pl.BlockSpec((1, wd), lambda p, gi: (0, p))
    sspec = pl.BlockSpec((nseq, 2 * hp, RWKV_N, RWKV_N), lambda p, gi: (gi, p, 0, 0))
    o, s = pl.pallas_call(
        functools.partial(_rwkv_sample_kernel, nseq=nseq, T=T, hp=hp),
        grid=(NP // hp, B // nseq),
        in_specs=[col] * 7 + [par] * 3 + [sspec],
        out_specs=[col, sspec],
        out_shape=[jax.ShapeDtypeStruct((R, D), F32), jax.ShapeDtypeStruct(s0.shape, F32)],
        scratch_shapes=[pltpu.VMEM((nseq, hp, LANES, LANES), F32)],
        compiler_params=_cparams(("parallel", "parallel")),
        name="rwkv_sample",
    )(*arrs, g, lnw.reshape(1, D), lnb.reshape(1, D), rk.reshape(1, D), s0)
    return o, s


def _merge_kernel(x_ref, h_ref, oa_ref, ob_ref, wg_ref, w_ref, gt_ref, o_ref):
    D = x_ref.shape[1]
    pg = lax.dot_general(h_ref[...], wg_ref[...], _NT, preferred_element_type=F32)
    m = _sigmoid(pg[:, :D]) * oa_ref[...] + _sigmoid(pg[:, D:]) * ob_ref[...]
    o_ref[...] = x_ref[...] + gt_ref[...] * _dot(m, w_ref[...])


def _merge_out(x, h, o_a, o_b, w_g, w_out, mod, tm):
    R, D = x.shape
    row = pl.BlockSpec((tm, D), lambda i: (i, 0))
    const = lambda a: pl.BlockSpec(a.shape, lambda i: (0,) * a.ndim, pipeline_mode=pl.Buffered(1))
    return pl.pallas_call(
        _merge_kernel,
        grid=(R // tm,),
        in_specs=[row, row, row, row, const(w_g), const(w_out), mod.spec(tm, 2)],
        out_specs=row,
        out_shape=jax.ShapeDtypeStruct((R, D), F32),
        compiler_params=_cparams(("parallel",)),
        name="merge_out",
    )(x, h, o_a, o_b, w_g, w_out, mod.arr)


def _mlp_kernel(x_ref, nw_ref, sh_ref, sc_ref, gt_ref, wu_ref, wd_ref, fw_ref, o_ref, h_ref, acc_ref):
    kf = pl.program_id(1)

    @pl.when(kf == 0)
    def _():
        h = _rms(x_ref[...], nw_ref[...]) * (1.0 + sc_ref[...]) + sh_ref[...]
        h_ref[...] = h.astype(BF16)
        acc_ref[...] = jnp.zeros_like(acc_ref)

    u = jnp.dot(h_ref[...], wu_ref[...], preferred_element_type=F32)
    u = jnp.square(jnp.maximum(u, 0.0))
    acc_ref[...] += _dot(u, wd_ref[...])

    @pl.when(kf == pl.num_programs(1) - 1)
    def _():
        y = x_ref[...] + gt_ref[...] * acc_ref[...]
        o_ref[...] = _rms(y, fw_ref[...])


def _mlp(x, nw, w_up, w_down, fw, mod, tm, tf=1024):
    R, D = x.shape
    DF = w_up.shape[1]
    return pl.pallas_call(
        _mlp_kernel,
        grid=(R // tm, DF // tf),
        in_specs=[pl.BlockSpec((tm, D), lambda i, kf: (i, 0)),
                  pl.BlockSpec((1, D), lambda i, kf: (0, 0)),
                  mod.spec(tm, 3), mod.spec(tm, 4), mod.spec(tm, 5),
                  pl.BlockSpec((D, tf), lambda i, kf: (0, kf)),
                  pl.BlockSpec((tf, D), lambda i, kf: (kf, 0)),
                  pl.BlockSpec((1, D), lambda i, kf: (0, 0))],
        out_specs=pl.BlockSpec((tm, D), lambda i, kf: (i, 0)),
        out_shape=jax.ShapeDtypeStruct((R, D), F32),
        scratch_shapes=[pltpu.VMEM((tm, D), BF16), pltpu.VMEM((tm, D), F32)],
        compiler_params=_cparams(("parallel", "arbitrary")),
        name="mlp",
    )(x, nw.reshape(1, D), mod.arr, mod.arr, mod.arr, w_up, w_down, fw.reshape(1, D))


def _run(x3, mod_rows, s_hgrn, s_rwkv, s_shift, W, prompt):
    B, T, D = x3.shape
    R = B * T
    x = x3.reshape(R, D)
    if prompt:
        mod = _Mod(mod_rows.reshape(B, 1, -1), True, T, D)
        tm = min(1024, T)
    else:
        mod = _Mod(jnp.repeat(mod_rows, T, axis=0), False, T, D)
        tm = min(512, R)

    rw = (W["mu"], W["w0"], W["a0"], W["k_k"], W["k_a"], W["wwa"], W["g2p"])
    nw1 = W["norm1_w"].reshape(1, D)
    if prompt:
        shift_in = jnp.zeros((B, 1, RC_PAD), F32)
        h, arrs, last = _in_rwkv(x, nw1, mod, W["w_r"], shift_in, rw, True, T)
        shift_out = last[:, 0, :RC_COLS]
        P_h = _matmul(h, W["w_h"], tm, 2048, "proj_h")
        o_a, hg = _hgrn_prompt(P_h, W["lb_logits"], W["hgrn_norm_w"], B, T)
        o_b, rs = _rwkv_prompt(arrs[:6], arrs[6], W["lnx_w"], W["lnx_b"], W["r_k"], B, T)
    else:
        shift_rows = jnp.repeat(jnp.pad(s_shift, ((0, 0), (0, RC_PAD - RC_COLS))), T, axis=0)
        h, arrs, last = _in_rwkv(x, nw1, mod, W["w_r"], shift_rows, rw, False, T)
        shift_out = last.reshape(B, T, RC_PAD)[:, T - 1, :RC_COLS]
        P_h = _matmul(h, W["w_h"], tm, 2048, "proj_h")
        o_a, hg = _hgrn_sample(P_h, W["lb_logits"], W["hgrn_norm_w"], s_hgrn, T)
        o_b, rs = _rwkv_sample(arrs[:6], arrs[6], W["lnx_w"], W["lnx_b"], W["r_k"], s_rwkv, T)

    x1 = _merge_out(x, h, o_a, o_b, W["w_g"], W["w_out"], mod, min(tm, 512))
    y = _mlp(x1, W["norm2_w"], W["w_up"], W["w_down"], W["final_norm_w"], mod, min(tm, 512))
    return y.reshape(B, T, D), hg, rs, shift_out


def kernel(x_prompt, x_sample, state_hgrn, state_rwkv, state_shift, c_prompt, c_sample, norm1_w, norm2_w, ada_w, ada_b, w_in, lb_logits, hgrn_norm_w, rwkv_mu, rwkv_w0, rwkv_w2, rwkv_a0, rwkv_a2, rwkv_g2, rwkv_k_k, rwkv_k_a, rwkv_r_k, rwkv_lnx_w, rwkv_lnx_b, w_out, w_up, w_down, final_norm_w):
    depth = norm1_w.shape[0]
    assert depth == 1, "single-layer step"
    D = x_prompt.shape[-1]
    HW = 4 * D
    w_in_t = w_in[0].T
    row = lambda a: a.reshape(1, -1)
    wwa = jnp.zeros((LANES, 2 * D), F32)
    wwa = wwa.at[:W_LORA, :D].set(rwkv_w2[0]).at[W_LORA:W_LORA + A_LORA, D:].set(rwkv_a2[0])
    W = {
        "norm1_w": norm1_w[0], "norm2_w": norm2_w[0], "final_norm_w": final_norm_w,
        "w_h": w_in_t[:HW].astype(BF16),
        "w_r": jnp.pad(w_in_t[HW:HW + RC_COLS], ((0, RC_PAD - RC_COLS), (0, 0))).astype(BF16),
        "w_g": w_in_t[HW + RC_COLS:].astype(BF16),
        "lb_logits": lb_logits, "hgrn_norm_w": hgrn_norm_w[0],
        "mu": jnp.pad(row(rwkv_mu[0]), ((0, 0), (0, RC_PAD - RC_COLS))),
        "w0": row(rwkv_w0[0]), "a0": row(rwkv_a0[0]), "k_k": row(rwkv_k_k[0]), "k_a": row(rwkv_k_a[0]),
        "wwa": wwa.astype(BF16),
        "g2p": jnp.pad(rwkv_g2[0], ((0, G_LORA_PAD - G_LORA), (0, 0))).astype(BF16),
        "lnx_w": rwkv_lnx_w[0], "lnx_b": rwkv_lnx_b[0], "r_k": rwkv_r_k[0].reshape(-1),
        "w_out": w_out[0].astype(BF16), "w_up": w_up[0].astype(BF16), "w_down": w_down[0].astype(BF16),
    }
    nb = x_prompt.shape[0]
    mod = _adaln(jnp.concatenate([c_prompt, c_sample], axis=0), ada_w[0], ada_b[0])
    y_p, hg_p, rw_p, sh_p = _run(x_prompt, mod[:nb], None, None, None, W, True)
    y_s, hg_s, rw_s, sh_s = _run(x_sample, mod[nb:], state_hgrn[0], state_rwkv[0], state_shift[0], W, False)
    ex = lambda a: a[None]
    return (y_p, y_s, ex(hg_p), ex(rw_p), ex(sh_p), ex(hg_s), ex(rw_s), ex(sh_s))
```

```python
import functools
import math

import jax
import jax.numpy as jnp
from jax import lax
from jax.experimental import pallas as pl
from jax.experimental.pallas import tpu as pltpu

F32 = jnp.float32
BF16 = jnp.bfloat16

LANES = 128
SUBLANES = 8
VMEM_LIMIT = 48 * 1024 * 1024

HGRN_DK = 128
RWKV_N = 64
W_LORA = 64
A_LORA = 64
G_LORA = 160
G_LORA_PAD = 256
RC_MAIN = 3 * 1024
RC_COLS = RC_MAIN + W_LORA + A_LORA + G_LORA
RC_PAD = RC_MAIN + LANES + G_LORA_PAD
RMS_EPS = 1e-6
LNX_EPS = 64e-5
CHUNK = 64
SUB = 8
RWKV_SUB = 8


def _cparams(sem, vmem=VMEM_LIMIT):
    return pltpu.CompilerParams(dimension_semantics=sem, vmem_limit_bytes=vmem)


_NN = (((1,), (0,)), ((), ()))
_NT = (((1,), (1,)), ((), ()))
_TN = (((0,), (0,)), ((), ()))


def _dot(a, b, dims=_NN):
    return lax.dot_general(a.astype(BF16), b.astype(BF16), dims, preferred_element_type=F32)


def _split2(x):
    hi = x.astype(BF16)
    lo = (x - hi.astype(F32)).astype(BF16)
    return hi, lo


def _split3(x):
    hi = x.astype(BF16)
    r1 = x - hi.astype(F32)
    mid = r1.astype(BF16)
    lo = (r1 - mid.astype(F32)).astype(BF16)
    return hi, mid, lo


def _dot_exact_lhs(a, x, dims=_NN):
    a = a.astype(BF16)
    hi, mid, lo = _split3(x)
    d = lambda p: lax.dot_general(a, p, dims, preferred_element_type=F32)
    return d(hi) + d(mid) + d(lo)


def _dot_exact_rhs(x, b, dims=_NN, parts=3):
    b = b.astype(BF16)
    d = lambda p: lax.dot_general(p, b, dims, preferred_element_type=F32)
    if parts == 2:
        hi, lo = _split2(x)
        return d(hi) + d(lo)
    hi, mid, lo = _split3(x)
    return d(hi) + d(mid) + d(lo)


def _dot_hp(a, b, dims=_NN):
    ah, al = _split2(a)
    bh, bl = _split2(b)
    d = lambda p, q: lax.dot_general(p, q, dims, preferred_element_type=F32)
    return d(ah, bh) + d(ah, bl) + d(al, bh)


def _sigmoid(x):
    return 1.0 / (1.0 + jnp.exp(-x))


def _iota2(shape, dim):
    return lax.broadcasted_iota(jnp.int32, shape, dim)


def _neumann_inv(L, eye, nsq):
    X = L
    P = eye - L
    for _ in range(nsq):
        X = _dot_hp(X, X)
        P = P + _dot_hp(P, X)
    return P


def _adaln_kernel(c_ref, w_ref, b_ref, o_ref):
    c = c_ref[...]
    s = c * _sigmoid(c)
    o_ref[...] = _dot(s, w_ref[...]) + b_ref[...]


def _adaln(c_all, ada_w, ada_b, tn=1536):
    R, D = c_all.shape
    N = ada_w.shape[1]
    return pl.pallas_call(
        _adaln_kernel,
        grid=(N // tn,),
        in_specs=[pl.BlockSpec((R, D), lambda j: (0, 0)),
                  pl.BlockSpec((D, tn), lambda j: (0, j)),
                  pl.BlockSpec((1, tn), lambda j: (0, j))],
        out_specs=pl.BlockSpec((R, tn), lambda j: (0, j)),
        out_shape=jax.ShapeDtypeStruct((R, N), F32),
        compiler_params=_cparams(("parallel",)),
        name="adaln",
    )(c_all, ada_w, ada_b.reshape(1, N))


class _Mod:
    def __init__(self, arr, per_seq, seq_rows, D):
        self.arr, self.per_seq, self.seq_rows, self.D = arr, per_seq, seq_rows, D
        self.rep = 1 if per_seq else seq_rows

    def spec(self, tm, comp):
        D = self.D
        if self.per_seq:
            tiles = self.seq_rows // tm
            return pl.BlockSpec((None, 1, D), lambda i, *_: (i // tiles, 0, comp))
        return pl.BlockSpec((tm // self.seq_rows, D), lambda i, *_: (i, comp))


def _rep_rows(v, rep):
    if rep == 1:
        return v
    n, w = v.shape
    return jnp.broadcast_to(v[:, None, :], (n, rep, w)).reshape(n * rep, w)


def _rms(x, w):
    return x * lax.rsqrt(jnp.mean(x * x, axis=-1, keepdims=True) + RMS_EPS) * w


def _mm_kernel(x_ref, wt_ref, o_ref):
    o_ref[...] = lax.dot_general(x_ref[...], wt_ref[...], _NT, preferred_element_type=F32)


def _matmul(x, wt, tm, tn, name):
    R, K = x.shape
    N = wt.shape[0]
    return pl.pallas_call(
        _mm_kernel,
        grid=(R // tm, N // tn),
        in_specs=[pl.BlockSpec((tm, K), lambda i, j: (i, 0)),
                  pl.BlockSpec((tn, K), lambda i, j: (j, 0))],
        out_specs=pl.BlockSpec((tm, tn), lambda i, j: (i, j)),
        out_shape=jax.ShapeDtypeStruct((R, N), F32),
        compiler_params=_cparams(("parallel", "arbitrary")),
        name=name,
    )(x, wt)


def _hgrn_gates(fp, lb_logits):
    m = jnp.max(lb_logits, axis=0, keepdims=True)
    e = jnp.exp(lb_logits - m)
    lb = e[0:1, :] / jnp.sum(e, axis=0, keepdims=True)
    f = lb + (1.0 - lb) * _sigmoid(fp)
    return jnp.log(f), 1.0 - f


def _hgrn_diag(q, kin, b, c):
    R = q.shape[0]
    nb = R // c
    q3 = q.reshape(nb, c, LANES)
    k3 = kin.reshape(nb, c, LANES)
    b3 = (b * math.log2(math.e)).reshape(nb, c, LANES)
    lane_mod = lax.broadcasted_iota(jnp.int32, (nb, c, R), 2) % c
    acc = jnp.zeros((nb, c, R), F32)
    for s in range(c):
        d = jnp.minimum(b3 - b3[:, s:s + 1, :], 0.0)
        col = jnp.sum(q3 * k3[:, s:s + 1, :] * jnp.exp2(d), axis=-1, keepdims=True)
        acc = jnp.where(lane_mod == s, col, acc)
    rr, cc = _iota2((R, R), 0), _iota2((R, R), 1)
    keep = ((rr // c) == (cc // c)) & (rr >= cc)
    return jnp.where(keep, acc.reshape(R, R), 0.0)


def _hgrn_cross(q, kin, b, m):
    R = q.shape[0]
    h = m // 2
    nb = R // m
    ref = jnp.broadcast_to(b.reshape(nb, m, LANES)[:, h - 1:h, :], (nb, m, LANES)).reshape(R, LANES)
    upper = (_iota2((R, 1), 0) % m) >= h
    qs = q * jnp.exp(jnp.where(upper, b - ref, -1e30))
    ks = kin * jnp.exp(jnp.where(upper, -1e30, ref - b))
    sc = _dot(qs, ks, _NT)
    if nb == 1:
        return sc
    same_blk = (_iota2((R, R), 0) // m) == (_iota2((R, R), 1) // m)
    return jnp.where(same_blk, sc, 0.0)


def _hgrn_out(o, og, nw):
    o = o * lax.rsqrt(jnp.mean(o * o, axis=-1, keepdims=True) + RMS_EPS) * nw
    return o * (og * _sigmoid(og))


def _hgrn_prompt_kernel(q_ref, f_ref, i_ref, g_ref, lb_ref, nw_ref, o_ref, s_ref, st_ref, *, nchunks, hp):
    tt = pl.program_id(2)

    @pl.when(tt == 0)
    def _():
        st_ref[...] = jnp.zeros_like(st_ref)

    C, c = CHUNK, SUB
    tri = (_iota2((C, C), 0) >= _iota2((C, C), 1)).astype(F32)
    nw = nw_ref[...]

    def chunk(ci, carry):
        r0 = pl.multiple_of(ci * C, C)
        H = range(hp)
        cols = [slice(j * LANES, (j + 1) * LANES) for j in H]
        q = [q_ref[pl.ds(r0, C), cols[j]] for j in H]
        v = [i_ref[pl.ds(r0, C), cols[j]] for j in H]
        gates = [_hgrn_gates(f_ref[pl.ds(r0, C), cols[j]], lb_ref[:, cols[j]]) for j in H]
        lf = [gt[0] for gt in gates]
        kin = [gt[1] for gt in gates]
        b = [_dot_exact_lhs(tri, lf[j]) for j in H]
        b_last = [b[j][C - 1:C, :] for j in H]
        ST = [st_ref[j] for j in H]
        o = [_dot(q[j] * jnp.exp(b[j]), ST[j], _NT) for j in H]
        for j in H:
            st_ref[j] = ST[j] * jnp.exp(b_last[j]) + _dot(v[j], kin[j] * jnp.exp(b_last[j] - b[j]), _TN)
        A = [_hgrn_diag(q[j], kin[j], b[j], c) for j in H]
        m = 2 * c
        while m <= C:
            A = [A[j] + _hgrn_cross(q[j], kin[j], b[j], m) for j in H]
            m *= 2
        o = [o[j] + _dot(A[j], v[j]) for j in H]
        for j in H:
            o_ref[pl.ds(r0, C), cols[j]] = _hgrn_out(o[j], g_ref[pl.ds(r0, C), cols[j]], nw)
        return carry

    lax.fori_loop(0, nchunks, chunk, 0)

    @pl.when(tt == pl.num_programs(2) - 1)
    def _():
        for j in range(hp):
            s_ref[j] = st_ref[j].T


def _hgrn_prompt(P_h, lb_logits, nw, B, T, tt_rows=512, hp=8):
    R, W = P_h.shape
    H = W // (4 * HGRN_DK)
    HG = H // hp
    wd = hp * HGRN_DK
    tt_rows = min(tt_rows, T)
    nt = T // tt_rows
    col = lambda off: pl.BlockSpec((tt_rows, wd), lambda b, h, t: (b * nt + t, off * HG + h))
    o, s = pl.pallas_call(
        functools.partial(_hgrn_prompt_kernel, nchunks=tt_rows // CHUNK, hp=hp),
        grid=(B, HG, nt),
        in_specs=[col(0), col(1), col(2), col(3),
                  pl.BlockSpec((lb_logits.shape[0], wd), lambda b, h, t: (0, h)),
                  pl.BlockSpec((1, HGRN_DK), lambda b, h, t: (0, 0))],
        out_specs=[pl.BlockSpec((tt_rows, wd), lambda b, h, t: (b * nt + t, h)),
                   pl.BlockSpec((None, hp, HGRN_DK, HGRN_DK), lambda b, h, t: (b, h, 0, 0))],
        out_shape=[jax.ShapeDtypeStruct((R, H * HGRN_DK), F32),
                   jax.ShapeDtypeStruct((B, H, HGRN_DK, HGRN_DK), F32)],
        scratch_shapes=[pltpu.VMEM((hp, HGRN_DK, HGRN_DK), F32)],
        compiler_params=_cparams(("parallel", "parallel", "arbitrary")),
        name="hgrn_prompt",
    )(P_h, P_h, P_h, P_h, lb_logits, nw.reshape(1, HGRN_DK))
    return o, s


def _hgrn_sample_kernel(q_ref, f_ref, i_ref, g_ref, lb_ref, nw_ref, s0_ref, o_ref, s_ref, *, nseq, T, hp):
    R = nseq * T
    rr, cc = _iota2((R, R), 0), _iota2((R, R), 1)
    same = (rr // T) == (cc // T)
    seg_incl = (same & (rr >= cc)).astype(F32)
    seg_all = same.astype(F32)
    nw = nw_ref[...]
    H = range(hp)
    cols = [slice(j * LANES, (j + 1) * LANES) for j in H]
    q = [q_ref[:, cols[j]] for j in H]
    v = [i_ref[:, cols[j]] for j in H]
    gates = [_hgrn_gates(f_ref[:, cols[j]], lb_ref[:, cols[j]]) for j in H]
    lf = [gt[0] for gt in gates]
    kin = [gt[1] for gt in gates]
    b = [_dot_exact_lhs(seg_incl, lf[j]) for j in H]
    b_tot = [_dot_exact_lhs(seg_all, lf[j]) for j in H]
    A = [_hgrn_diag(q[j], kin[j], b[j], T) for j in H]
    o = [_dot(A[j], v[j]) for j in H]
    qg = [q[j] * jnp.exp(b[j]) for j in H]
    kh = [kin[j] * jnp.exp(b_tot[j] - b[j]) for j in H]
    ecol = [jnp.transpose(jnp.exp(b_tot[j]).reshape(nseq, T, LANES)[:, 0, :]) for j in H]
    inter = [[] for _ in H]
    for n in range(nseq):
        sl = slice(n * T, (n + 1) * T)
        for j in H:
            S = s0_ref[n, j]
            inter[j].append(_dot(qg[j][sl], S))
            s_ref[n, j] = S * ecol[j][:, n:n + 1] + _dot(kh[j][sl], v[j][sl], _TN)
    for j in H:
        oj = o[j] + jnp.concatenate(inter[j], axis=0)
        o_ref[:, cols[j]] = _hgrn_out(oj, g_ref[:, cols[j]], nw)


def _hgrn_sample(P_h, lb_logits, nw, s0, T, nseq=16, hp=4):
    R, W = P_h.shape
    H = W // (4 * HGRN_DK)
    HG = H // hp
    wd = hp * HGRN_DK
    B = R // T
    rows = nseq * T
    col = lambda off: pl.BlockSpec((rows, wd), lambda h, g: (g, off * HG + h))
    sspec = pl.BlockSpec((nseq, hp, HGRN_DK, HGRN_DK), lambda h, g: (g, h, 0, 0))
    o, s = pl.pallas_call(
        functools.partial(_hgrn_sample_kernel, nseq=nseq, T=T, hp=hp),
        grid=(HG, B // nseq),
        in_specs=[col(0), col(1), col(2), col(3),
                  pl.BlockSpec((lb_logits.shape[0], wd), lambda h, g: (0, h)),
                  pl.BlockSpec((1, HGRN_DK), lambda h, g: (0, 0)),
                  sspec],
        out_specs=[pl.BlockSpec((rows, wd), lambda h, g: (g, h)), sspec],
        out_shape=[jax.ShapeDtypeStruct((R, H * HGRN_DK), F32),
                   jax.ShapeDtypeStruct(s0.shape, F32)],
        compiler_params=_cparams(("parallel", "parallel")),
        name="hgrn_sample",
    )(P_h, P_h, P_h, P_h, lb_logits, nw.reshape(1, HGRN_DK), s0)
    return o, s


def _rwkv_prep_body(rc, prev, mu_ref, w0_ref, a0_ref, kk_ref, ka_ref, wwa_ref, g2_ref, outs, rows):
    r_o, lw_o, k_o, v_o, kap_o, bet_o, g_o = outs
    D = RC_MAIN // 3
    xs = rc + mu_ref[...] * (prev - rc)
    r = xs[:, 0:D]
    kb = xs[:, D:2 * D]
    v = xs[:, 2 * D:3 * D]
    xwa = xs[:, RC_MAIN:RC_MAIN + LANES]
    xg = xs[:, RC_MAIN + LANES:]
    lane = _iota2((1, LANES), 1)
    wa = _dot(jnp.where(lane < W_LORA, jnp.tanh(xwa), xwa), wwa_ref[...])
    w = w0_ref[...] + wa[:, :D]
    a = _sigmoid(a0_ref[...] + wa[:, D:])
    lw = -math.exp(-0.5) * _sigmoid(w)
    g = _dot(_sigmoid(xg), g2_ref[...])
    kk = kb * kk_ref[...]
    seg = (_iota2((LANES, LANES), 0) // RWKV_N == _iota2((LANES, LANES), 1) // RWKV_N).astype(F32)
    k_a = ka_ref[...]
    for p in range(D // LANES):
        sl = slice(p * LANES, (p + 1) * LANES)
        kkp = kk[:, sl]
        kap = kkp * lax.rsqrt(jnp.maximum(_dot_exact_rhs(kkp * kkp, seg, parts=2), 1e-24))
        ap = a[:, sl]
        kap_o[rows, sl] = kap
        bet_o[rows, sl] = ap * kap
        k_o[rows, sl] = kb[:, sl] * (1.0 + (ap - 1.0) * k_a[:, sl])
    r_o[rows, :] = r
    lw_o[rows, :] = lw
    v_o[rows, :] = v
    g_o[rows, :] = g


def _in_rwkv_kernel(x_ref, nw_ref, sh_ref, sc_ref, wr_ref, shift_ref, mu_ref, w0_ref, a0_ref, kk_ref,
                    ka_ref, wwa_ref, g2_ref, h_ref, r_o, lw_o, k_o, v_o, kap_o, bet_o, g_o, last_o,
                    carry_ref, rc_ref, *, seq_rows, sub):
    i = pl.program_id(0)
    tm = x_ref.shape[0]
    per_seq = seq_rows >= tm

    @pl.when(i == 0)
    def _():
        carry_ref[...] = jnp.zeros_like(carry_ref)

    rep = 1 if per_seq else seq_rows
    h = (_rms(x_ref[...], nw_ref[...]) * (1.0 + _rep_rows(sc_ref[...], rep))
         + _rep_rows(sh_ref[...], rep)).astype(BF16)
    h_ref[...] = h
    outs = (r_o, lw_o, k_o, v_o, kap_o, bet_o, g_o)
    rid = _iota2((sub, 1), 0)
    rc_ref[...] = lax.dot_general(h, wr_ref[...], _NT, preferred_element_type=F32)
    prev_last = None
    for s in range(tm // sub):
        rows = slice(s * sub, (s + 1) * sub)
        rc = rc_ref[rows, :]
        rolled = pltpu.roll(rc, 1, axis=0)
        if per_seq:
            if s == 0:
                start = (i % (seq_rows // tm)) == 0
                first = jnp.where(start, shift_ref[...], carry_ref[SUBLANES - 1:SUBLANES, :])
            else:
                first = prev_last
            prev = jnp.where(rid == 0, first, rolled)
        else:
            seqs = slice(s * sub // seq_rows, (s + 1) * sub // seq_rows)
            prev = jnp.where(rid % seq_rows == 0, _rep_rows(shift_ref[seqs, :], seq_rows), rolled)
            last_o[seqs, :] = rc.reshape(sub // seq_rows, seq_rows, RC_PAD)[:, seq_rows - 1, :]
        prev_last = rc[sub - 1:sub, :]
        if s == tm // sub - 1 and per_seq:
            carry_ref[...] = rc[sub - SUBLANES:, :]
            last_o[...] = prev_last
        _rwkv_prep_body(rc, prev, mu_ref, w0_ref, a0_ref, kk_ref, ka_ref, wwa_ref, g2_ref, outs, rows)


def _in_rwkv(x, nw, mod, w_r, shift, params, per_seq, seq_rows, tm=256, sub=128):
    R, D = x.shape
    tm = min(tm, seq_rows) if per_seq else min(tm, R)
    sub = min(sub, tm)
    mu, w0, a0, k_k, k_a, wwa, g2p = params
    const = lambda a: pl.BlockSpec(a.shape, lambda i: (0,) * a.ndim, pipeline_mode=pl.Buffered(1))
    wspecs = [const(a) for a in (mu, w0, a0, k_k, k_a, wwa, g2p)]
    row = lambda w: pl.BlockSpec((tm, w), lambda i: (i, 0))
    if per_seq:
        tiles = seq_rows // tm
        shift_spec = pl.BlockSpec((None, 1, RC_PAD), lambda i: (i // tiles, 0, 0))
        last_spec = pl.BlockSpec((None, 1, RC_PAD), lambda i: (i // tiles, 0, 0))
        last_shape = jax.ShapeDtypeStruct((R // seq_rows, 1, RC_PAD), F32)
    else:
        shift_spec = pl.BlockSpec((tm // seq_rows, RC_PAD), lambda i: (i, 0))
        last_spec = pl.BlockSpec((tm // seq_rows, RC_PAD), lambda i: (i, 0))
        last_shape = jax.ShapeDtypeStruct((R // seq_rows, RC_PAD), F32)
    outs = pl.pallas_call(
        functools.partial(_in_rwkv_kernel, seq_rows=seq_rows, sub=sub),
        grid=(R // tm,),
        in_specs=[row(D), const(nw), mod.spec(tm, 0), mod.spec(tm, 1), const(w_r), shift_spec] + wspecs,
        out_specs=[row(D)] + [row(D)] * 7 + [last_spec],
        out_shape=[jax.ShapeDtypeStruct((R, D), BF16)] + [jax.ShapeDtypeStruct((R, D), F32)] * 7
                  + [last_shape],
        scratch_shapes=[pltpu.VMEM((SUBLANES, RC_PAD), F32), pltpu.VMEM((tm, RC_PAD), F32)],
        compiler_params=_cparams(("arbitrary",)),
        name="in_rwkv",
    )(x, nw, mod.arr, mod.arr, w_r, shift, mu, w0, a0, k_k, k_a, wwa, g2p)
    return outs[0], outs[1:8], outs[8]


def _pair_masks():
    lane = _iota2((1, LANES), 1)
    return (lane < RWKV_N).astype(F32), (lane >= RWKV_N).astype(F32)


def _stack(x, mA, mB):
    return jnp.concatenate([x * mA, x * mB], axis=0)


def _rwkv_scores(cs, lw, cs_end, r, k, kap, bet, mA, mB):
    e_neg = jnp.exp(-cs)
    e_rem = jnp.exp(cs_end - cs)
    rg = _stack(r * jnp.exp(cs), mA, mB)
    kg = _stack(kap * jnp.exp(cs - lw), mA, mB)
    kd = _stack(k * e_neg, mA, mB)
    bd = _stack(bet * e_neg, mA, mB)
    kh = _stack(k * e_rem, mA, mB)
    bh = _stack(bet * e_rem, mA, mB)
    return rg, kg, kd, bd, kh, bh


def _rwkv_epilogue(o, r, k, v, g, lnw, lnb, rk):
    avg = jnp.where(_iota2((LANES, LANES), 0) // RWKV_N == _iota2((LANES, LANES), 1) // RWKV_N,
                    1.0 / RWKV_N, 0.0).astype(F32)
    mean = _dot_exact_rhs(o, avg)
    d = o - mean
    var = _dot_exact_rhs(d * d, avg)
    on = d * lax.rsqrt(var + LNX_EPS) * lnw + lnb
    bonus = _dot_exact_rhs(r * k * rk, avg) * float(RWKV_N) * v
    return (on + bonus) * g


def _rwkv_prompt_kernel(r_ref, lw_ref, k_ref, v_ref, kap_ref, bet_ref, g_ref, lnw_ref, lnb_ref, rk_ref,
                        o_ref, s_ref, st_ref, *, nchunks, hp):
    tt = pl.program_id(2)

    @pl.when(tt == 0)
    def _():
        st_ref[...] = jnp.zeros_like(st_ref)

    C, c = CHUNK, RWKV_SUB
    C2 = 2 * C
    mA, mB = _pair_masks()
    rr, cc = _iota2((C2, C2), 0), _iota2((C2, C2), 1)
    eye = (rr == cc).astype(F32)
    strict = rr > cc
    incl = rr >= cc
    blk = (rr // c) == (cc // c)
    tri = (_iota2((C, C), 0) >= _iota2((C, C), 1)).astype(F32)
    unroll = 2 if nchunks % 2 == 0 else 1

    def front(chains):
        J = range(len(chains))
        ld = lambda ref: [ref[pl.ds(r0, C), p * LANES:(p + 1) * LANES] for r0, p in chains]
        r, lw, k, v, kap, bet = ld(r_ref), ld(lw_ref), ld(k_ref), ld(v_ref), ld(kap_ref), ld(bet_ref)
        cs = [_dot_exact_lhs(tri, lw[j]) for j in J]
        cs_end = [cs[j][C - 1:C, :] for j in J]
        sc = [_rwkv_scores(cs[j], lw[j], cs_end[j], r[j], k[j], kap[j], bet[j], mA, mB) for j in J]
        rg, kg, kd, bd, kh, bh = ([s[i] for s in sc] for i in range(6))
        vs = [_stack(v[j], mA, mB) for j in J]
        kbd = [jnp.concatenate([kd[j], bd[j]], axis=0) for j in J]
        sck = [_dot(kg[j], kbd[j], _NT) for j in J]
        scr = [_dot(rg[j], kbd[j], _NT) for j in J]
        Ak = [jnp.where(strict, sck[j][:, :C2], 0.0) for j in J]
        Ab = [jnp.where(strict, sck[j][:, C2:], 0.0) for j in J]
        Bkb = [jnp.concatenate([jnp.where(incl, scr[j][:, :C2], 0.0),
                                jnp.where(incl, -scr[j][:, C2:], 0.0)], axis=1) for j in J]
        LD = [jnp.where(blk, Ab[j], 0.0) for j in J]
        X = LD
        Dinv = [eye - LD[j] for j in J]
        for _ in range(int(math.log2(c)) - 1):
            X = [_dot(X[j], X[j]) for j in J]
            Dinv = [Dinv[j] + _dot(Dinv[j], X[j]) for j in J]
        Q = [_dot(Dinv[j], Ab[j] - LD[j]) for j in J]
        T0 = [Dinv[j] - _dot(Q[j], Dinv[j]) for j in J]
        for _ in range(int(math.log2(C // c)) - 1):
            Q = [_dot(Q[j], Q[j]) for j in J]
            T0 = [T0[j] + _dot(Q[j], T0[j]) for j in J]
        E = [(eye - T0[j]) - _dot_hp(Ab[j], T0[j]) for j in J]
        Tm = [T0[j] + _dot(T0[j], E[j]) for j in J]
        khb = [jnp.concatenate([kh[j], -bh[j]], axis=0) for j in J]
        AkV = [_dot(Ak[j], vs[j]) for j in J]
        dec = [jnp.exp(cs_end[j]) for j in J]
        avg = jnp.where(_iota2((LANES, LANES), 0) // RWKV_N == _iota2((LANES, LANES), 1) // RWKV_N,
                        1.0 / RWKV_N, 0.0).astype(F32)
        bonus = [_dot_exact_rhs(r[j] * k[j] * rk_ref[:, p * LANES:(p + 1) * LANES], avg, parts=2)
                 * float(RWKV_N) * v[j] for j, (_, p) in zip(J, chains)]
        return [dict(kg=kg[j], rg=rg[j], vs=vs[j], Bkb=Bkb[j], Tm=Tm[j], khb=khb[j], AkV=AkV[j],
                     dec=dec[j], bonus=bonus[j]) for j in J]

    def back(chains, fr):
        J = range(len(chains))
        avg = jnp.where(_iota2((LANES, LANES), 0) // RWKV_N == _iota2((LANES, LANES), 1) // RWKV_N,
                        1.0 / RWKV_N, 0.0).astype(F32)
        S = [st_ref[p] for _, p in chains]
        rhs = [_dot(fr[j]["kg"], S[j], _NT) + fr[j]["AkV"] for j in J]
        U = [_dot_hp(fr[j]["Tm"], rhs[j]) for j in J]
        vu = [jnp.concatenate([fr[j]["vs"], U[j]], axis=0) for j in J]
        for j, (_, p) in zip(J, chains):
            st_ref[p] = S[j] * fr[j]["dec"] + _dot(vu[j], fr[j]["khb"], _TN)
        O = [_dot(fr[j]["rg"], S[j], _NT) + _dot(fr[j]["Bkb"], vu[j]) for j in J]
        o = [O[j][:C, :] + O[j][C:, :] for j in J]
        mean = [_dot_exact_rhs(o[j], avg, parts=2) for j in J]
        d = [o[j] - mean[j] for j in J]
        var = [_dot_exact_rhs(d[j] * d[j], avg, parts=2) for j in J]
        for j, (r0, p) in zip(J, chains):
            cols = slice(p * LANES, (p + 1) * LANES)
            on = d[j] * lax.rsqrt(var[j] + LNX_EPS) * lnw_ref[:, cols] + lnb_ref[:, cols]
            o_ref[pl.ds(r0, C), cols] = (on + fr[j]["bonus"]) * g_ref[pl.ds(r0, C), cols]

    def body(ci, carry):
        base = ci * (unroll * C)
        offs = [pl.multiple_of(base + u * C, C) for u in range(unroll)]
        chains = [[(offs[u], p) for p in range(hp)] for u in range(unroll)]
        fr = front([ch for cu in chains for ch in cu])
        for u in range(unroll):
            back(chains[u], fr[u * hp:(u + 1) * hp])
        return carry

    lax.fori_loop(0, nchunks // unroll, body, 0)

    @pl.when(tt == pl.num_programs(2) - 1)
    def _():
        for p in range(hp):
            S = st_ref[p]
            s_ref[2 * p] = S[:RWKV_N, :RWKV_N]
            s_ref[2 * p + 1] = S[RWKV_N:, RWKV_N:]


def _rwkv_prompt(arrs, g, lnw, lnb, rk, B, T, tt_rows=512, hp=8):
    R, D = arrs[0].shape
    NP = D // LANES
    wd = hp * LANES
    tt_rows = min(tt_rows, T)
    nt = T // tt_rows
    col = pl.BlockSpec((tt_rows, wd), lambda b, p, t: (b * nt + t, p))
    par = pl.BlockSpec((1, wd), lambda b, p, t: (0, p))
    o, s = pl.pallas_call(
        functools.partial(_rwkv_prompt_kernel, nchunks=tt_rows // CHUNK, hp=hp),
        grid=(B, NP // hp, nt),
        in_specs=[col] * 7 + [par] * 3,
        out_specs=[col, pl.BlockSpec((None, 2 * hp, RWKV_N, RWKV_N), lambda b, p, t: (b, p, 0, 0))],
        out_shape=[jax.ShapeDtypeStruct((R, D), F32),
                   jax.ShapeDtypeStruct((B, 2 * NP, RWKV_N, RWKV_N), F32)],
        scratch_shapes=[pltpu.VMEM((hp, LANES, LANES), F32)],
        compiler_params=_cparams(("parallel", "parallel", "arbitrary")),
        name="rwkv_prompt",
    )(*arrs, g, lnw.reshape(1, D), lnb.reshape(1, D), rk.reshape(1, D))
    return o, s


def _rwkv_sample_kernel(r_ref, lw_ref, k_ref, v_ref, kap_ref, bet_ref, g_ref, lnw_ref, lnb_ref, rk_ref,
                        s0_ref, o_ref, s_ref, sbd_ref, *, nseq, T, hp):
    C = nseq * T
    C2 = 2 * C
    N = RWKV_N
    mA, mB = _pair_masks()
    bdm = (_iota2((LANES, LANES), 0) // N == _iota2((LANES, LANES), 1) // N).astype(F32)
    rr, cc = _iota2((C2, C2), 0), _iota2((C2, C2), 1)
    eye = (rr == cc).astype(F32)
    same = (rr // T) == (cc // T)
    strict = same & (rr > cc)
    incl = same & (rr >= cc)
    r1, c1 = _iota2((C, C), 0), _iota2((C, C), 1)
    seg_incl = ((r1 // T == c1 // T) & (r1 >= c1)).astype(F32)
    seg_all = (r1 // T == c1 // T).astype(F32)
    zero = jnp.zeros((N, N), F32)

    J = range(hp)
    cols = [slice(j * LANES, (j + 1) * LANES) for j in J]
    ld = lambda ref: [ref[:, cols[j]] for j in J]
    r, lw, k, v, kap, bet = ld(r_ref), ld(lw_ref), ld(k_ref), ld(v_ref), ld(kap_ref), ld(bet_ref)
    cs = [_dot_exact_lhs(seg_incl, lw[j]) for j in J]
    cs_end = [_dot_exact_lhs(seg_all, lw[j]) for j in J]
    sc = [_rwkv_scores(cs[j], lw[j], cs_end[j], r[j], k[j], kap[j], bet[j], mA, mB) for j in J]
    rg, kg, kd, bd, kh, bh = ([s[i] for s in sc] for i in range(6))
    vs = [_stack(v[j], mA, mB) for j in J]
    kbd = [jnp.concatenate([kd[j], bd[j]], axis=0) for j in J]
    sck = [_dot(kg[j], kbd[j], _NT) for j in J]
    scr = [_dot(rg[j], kbd[j], _NT) for j in J]
    Ak = [jnp.where(strict, sck[j][:, :C2], 0.0) for j in J]
    Ab = [jnp.where(strict, sck[j][:, C2:], 0.0) for j in J]
    Bkb = [jnp.concatenate([jnp.where(incl, scr[j][:, :C2], 0.0),
                            jnp.where(incl, -scr[j][:, C2:], 0.0)], axis=1) for j in J]
    Tm = [_neumann_inv(Ab[j], eye, int(math.log2(T)) - 1) for j in J]
    kgp = [kg[j][:C, :] + kg[j][C:, :] for j in J]
    rgp = [rg[j][:C, :] + rg[j][C:, :] for j in J]
    ks = [[] for _ in J]
    rs = [[] for _ in J]
    for n in range(nseq):
        sl = slice(n * T, (n + 1) * T)
        for j in J:
            Sbd = jnp.concatenate(
                [jnp.concatenate([s0_ref[n, 2 * j], zero], axis=1),
                 jnp.concatenate([zero, s0_ref[n, 2 * j + 1]], axis=1)], axis=0)
            sbd_ref[n, j] = Sbd
            res = _dot(jnp.concatenate([kgp[j][sl], rgp[j][sl]], axis=0), Sbd, _NT)
            ks[j].append(res[:T])
            rs[j].append(res[T:])
    KS = [_stack(jnp.concatenate(ks[j], axis=0), mA, mB) for j in J]
    RS = [_stack(jnp.concatenate(rs[j], axis=0), mA, mB) for j in J]
    rhs = [KS[j] + _dot(Ak[j], vs[j]) for j in J]
    U = [_dot_hp(Tm[j], rhs[j]) for j in J]
    vu = [jnp.concatenate([vs[j], U[j]], axis=0) for j in J]
    O = [RS[j] + _dot(Bkb[j], vu[j]) for j in J]
    Up = [U[j][:C, :] + U[j][C:, :] for j in J]
    khp = [kh[j][:C, :] + kh[j][C:, :] for j in J]
    nbhp = [-(bh[j][:C, :] + bh[j][C:, :]) for j in J]
    dec = [jnp.exp(cs_end[j]) for j in J]
    for n in range(nseq):
        sl = slice(n * T, (n + 1) * T)
        for j in J:
            upd = _dot(jnp.concatenate([v[j][sl], Up[j][sl]], axis=0),
                       jnp.concatenate([khp[j][sl], nbhp[j][sl]], axis=0), _TN)
            Sn = sbd_ref[n, j] * dec[j][n * T:n * T + 1, :] + upd * bdm
            s_ref[n, 2 * j] = Sn[:N, :N]
            s_ref[n, 2 * j + 1] = Sn[N:, N:]
    for j in J:
        o = O[j][:C, :] + O[j][C:, :]
        o_ref[:, cols[j]] = _rwkv_epilogue(o, r[j], k[j], v[j], g_ref[:, cols[j]],
                                           lnw_ref[:, cols[j]], lnb_ref[:, cols[j]], rk_ref[:, cols[j]])


def _rwkv_sample(arrs, g, lnw, lnb, rk, s0, T, nseq=16, hp=4):
    R, D = arrs[0].shape
    NP = D // LANES
    B = R // T
    rows = nseq * T
    wd = hp * LANES
    col = pl.BlockSpec((rows, wd), lambda p, gi: (gi, p))
    par = pl.BlockSpec((1, wd), lambda p, gi: (0, p))
    sspec = pl.BlockSpec((nseq, 2 * hp, RWKV_N, RWKV_N), lambda p, gi: (gi, p, 0, 0))
    o, s = pl.pallas_call(
        functools.partial(_rwkv_sample_kernel, nseq=nseq, T=T, hp=hp),
        grid=(NP // hp, B // nseq),
        in_specs=[col] * 7 + [par] * 3 + [sspec],
        out_specs=[col, sspec],
        out_shape=[jax.ShapeDtypeStruct((R, D), F32), jax.ShapeDtypeStruct(s0.shape, F32)],
        scratch_shapes=[pltpu.VMEM((nseq, hp, LANES, LANES), F32)],
        compiler_params=_cparams(("parallel", "parallel")),
        name="rwkv_sample",
    )(*arrs, g, lnw.reshape(1, D), lnb.reshape(1, D), rk.reshape(1, D), s0)
    return o, s


def _merge_kernel(x_ref, h_ref, oa_ref, ob_ref, wg_ref, w_ref, gt_ref, o_ref, *, rep):
    D = x_ref.shape[1]
    pg = lax.dot_general(h_ref[...], wg_ref[...], _NT, preferred_element_type=F32)
    m = _sigmoid(pg[:, :D]) * oa_ref[...] + _sigmoid(pg[:, D:]) * ob_ref[...]
    o_ref[...] = x_ref[...] + _rep_rows(gt_ref[...], rep) * _dot(m, w_ref[...])


def _merge_out(x, h, o_a, o_b, w_g, w_out, mod, tm):
    R, D = x.shape
    row = pl.BlockSpec((tm, D), lambda i: (i, 0))
    const = lambda a: pl.BlockSpec(a.shape, lambda i: (0,) * a.ndim, pipeline_mode=pl.Buffered(1))
    return pl.pallas_call(
        functools.partial(_merge_kernel, rep=mod.rep),
        grid=(R // tm,),
        in_specs=[row, row, row, row, const(w_g), const(w_out), mod.spec(tm, 2)],
        out_specs=row,
        out_shape=jax.ShapeDtypeStruct((R, D), F32),
        compiler_params=_cparams(("parallel",)),
        name="merge_out",
    )(x, h, o_a, o_b, w_g, w_out, mod.arr)


def _mlp_kernel(x_ref, nw_ref, sh_ref, sc_ref, gt_ref, wu_ref, wd_ref, fw_ref, o_ref, *, tf, rep):
    x = x_ref[...]
    h = (_rms(x, nw_ref[...]) * (1.0 + _rep_rows(sc_ref[...], rep)) + _rep_rows(sh_ref[...], rep)).astype(BF16)
    acc = None
    for c in range(wu_ref.shape[1] // tf):
        u = jnp.dot(h, wu_ref[:, c * tf:(c + 1) * tf], preferred_element_type=F32)
        u = jnp.square(jnp.maximum(u, 0.0))
        part = _dot(u, wd_ref[c * tf:(c + 1) * tf, :])
        acc = part if acc is None else acc + part
    o_ref[...] = _rms(x + _rep_rows(gt_ref[...], rep) * acc, fw_ref[...])


def _mlp(x, nw, w_up, w_down, fw, mod, tm, tf=1024):
    R, D = x.shape
    const = lambda a: pl.BlockSpec(a.shape, lambda i: (0,) * a.ndim, pipeline_mode=pl.Buffered(1))
    row = pl.BlockSpec((tm, D), lambda i: (i, 0))
    nw, fw = nw.reshape(1, D), fw.reshape(1, D)
    return pl.pallas_call(
        functools.partial(_mlp_kernel, tf=tf, rep=mod.rep),
        grid=(R // tm,),
        in_specs=[row, const(nw), mod.spec(tm, 3), mod.spec(tm, 4), mod.spec(tm, 5),
                  const(w_up), const(w_down), const(fw)],
        out_specs=row,
        out_shape=jax.ShapeDtypeStruct((R, D), F32),
        compiler_params=_cparams(("parallel",)),
        name="mlp",
    )(x, nw, mod.arr, mod.arr, mod.arr, w_up, w_down, fw)


def _run(x3, mod_rows, s_hgrn, s_rwkv, s_shift, W, prompt):
    B, T, D = x3.shape
    R = B * T
    x = x3.reshape(R, D)
    if prompt:
        mod = _Mod(mod_rows.reshape(B, 1, -1), True, T, D)
        tm = min(1024, T)
    else:
        mod = _Mod(mod_rows, False, T, D)
        tm = min(512, R)

    rw = (W["mu"], W["w0"], W["a0"], W["k_k"], W["k_a"], W["wwa"], W["g2p"])
    nw1 = W["norm1_w"].reshape(1, D)
    if prompt:
        shift_in = jnp.zeros((B, 1, RC_PAD), F32)
        h, arrs, last = _in_rwkv(x, nw1, mod, W["w_r"], shift_in, rw, True, T)
        shift_out = last[:, 0, :RC_COLS]
        P_h = _matmul(h, W["w_h"], tm, 2048, "proj_h")
        o_a, hg = _hgrn_prompt(P_h, W["lb_logits"], W["hgrn_norm_w"], B, T)
        o_b, rs = _rwkv_prompt(arrs[:6], arrs[6], W["lnx_w"], W["lnx_b"], W["r_k"], B, T)
    else:
        shift_in = jnp.pad(s_shift, ((0, 0), (0, RC_PAD - RC_COLS)))
        h, arrs, last = _in_rwkv(x, nw1, mod, W["w_r"], shift_in, rw, False, T)
        shift_out = last[:, :RC_COLS]
        P_h = _matmul(h, W["w_h"], tm, 2048, "proj_h")
        o_a, hg = _hgrn_sample(P_h, W["lb_logits"], W["hgrn_norm_w"], s_hgrn, T)
        o_b, rs = _rwkv_sample(arrs[:6], arrs[6], W["lnx_w"], W["lnx_b"], W["r_k"], s_rwkv, T)

    x1 = _merge_out(x, h, o_a, o_b, W["w_g"], W["w_out"], mod, min(tm, 512))
    y = _mlp(x1, W["norm2_w"], W["w_up"], W["w_down"], W["final_norm_w"], mod, min(tm, 512))
    return y.reshape(B, T, D), hg, rs, shift_out


def kernel(x_prompt, x_sample, state_hgrn, state_rwkv, state_shift, c_prompt, c_sample, norm1_w, norm2_w, ada_w, ada_b, w_in, lb_logits, hgrn_norm_w, rwkv_mu, rwkv_w0, rwkv_w2, rwkv_a0, rwkv_a2, rwkv_g2, rwkv_k_k, rwkv_k_a, rwkv_r_k, rwkv_lnx_w, rwkv_lnx_b, w_out, w_up, w_down, final_norm_w):
    depth = norm1_w.shape[0]
    assert depth == 1, "single-layer step"
    D = x_prompt.shape[-1]
    HW = 4 * D
    w_in_t = w_in[0].T
    row = lambda a: a.reshape(1, -1)
    wwa = jnp.zeros((LANES, 2 * D), F32)
    wwa = wwa.at[:W_LORA, :D].set(rwkv_w2[0]).at[W_LORA:W_LORA + A_LORA, D:].set(rwkv_a2[0])
    W = {
        "norm1_w": norm1_w[0], "norm2_w": norm2_w[0], "final_norm_w": final_norm_w,
        "w_h": w_in_t[:HW].astype(BF16),
        "w_r": jnp.pad(w_in_t[HW:HW + RC_COLS], ((0, RC_PAD - RC_COLS), (0, 0))).astype(BF16),
        "w_g": w_in_t[HW + RC_COLS:].astype(BF16),
        "lb_logits": lb_logits, "hgrn_norm_w": hgrn_norm_w[0],
        "mu": jnp.pad(row(rwkv_mu[0]), ((0, 0), (0, RC_PAD - RC_COLS))),
        "w0": row(rwkv_w0[0]), "a0": row(rwkv_a0[0]), "k_k": row(rwkv_k_k[0]), "k_a": row(rwkv_k_a[0]),
        "wwa": wwa.astype(BF16),
        "g2p": jnp.pad(rwkv_g2[0], ((0, G_LORA_PAD - G_LORA), (0, 0))).astype(BF16),
        "lnx_w": rwkv_lnx_w[0], "lnx_b": rwkv_lnx_b[0], "r_k": rwkv_r_k[0].reshape(-1),
        "w_out": w_out[0].astype(BF16), "w_up": w_up[0].astype(BF16), "w_down": w_down[0].astype(BF16),
    }
    nb = x_prompt.shape[0]
    mod = _adaln(jnp.concatenate([c_prompt, c_sample], axis=0), ada_w[0], ada_b[0])
    y_p, hg_p, rw_p, sh_p = _run(x_prompt, mod[:nb], None, None, None, W, True)
    y_s, hg_s, rw_s, sh_s = _run(x_sample, mod[nb:], state_hgrn[0], state_rwkv[0], state_shift[0], W, False)
    ex = lambda a: a[None]
    return (y_p, y_s, ex(hg_p), ex(rw_p), ex(sh_p), ex(hg_s), ex(rw_s), ex(sh_s))
```

```python
import functools
import math

import jax
import jax.numpy as jnp
from jax import lax
from jax.experimental import pallas as pl
from jax.experimental.pallas import tpu as pltpu

F32 = jnp.float32
BF16 = jnp.bfloat16

LANES = 128
SUBLANES = 8
VMEM_LIMIT = 48 * 1024 * 1024

HGRN_DK = 128
RWKV_N = 64
W_LORA = 64
A_LORA = 64
G_LORA = 160
G_LORA_PAD = 256
RC_MAIN = 3 * 1024
RC_COLS = RC_MAIN + W_LORA + A_LORA + G_LORA
RC_PAD = RC_MAIN + LANES + G_LORA_PAD
RMS_EPS = 1e-6
LNX_EPS = 64e-5
CHUNK = 64
SUB = 8
RWKV_SUB = 8


def _cparams(sem, vmem=VMEM_LIMIT):
    return pltpu.CompilerParams(dimension_semantics=sem, vmem_limit_bytes=vmem)


_NN = (((1,), (0,)), ((), ()))
_NT = (((1,), (1,)), ((), ()))
_TN = (((0,), (0,)), ((), ()))


def _dot(a, b, dims=_NN):
    return lax.dot_general(a.astype(BF16), b.astype(BF16), dims, preferred_element_type=F32)


def _split2(x):
    hi = x.astype(BF16)
    lo = (x - hi.astype(F32)).astype(BF16)
    return hi, lo


def _split3(x):
    hi = x.astype(BF16)
    r1 = x - hi.astype(F32)
    mid = r1.astype(BF16)
    lo = (r1 - mid.astype(F32)).astype(BF16)
    return hi, mid, lo


def _dot_exact_lhs(a, x, dims=_NN):
    a = a.astype(BF16)
    hi, mid, lo = _split3(x)
    d = lambda p: lax.dot_general(a, p, dims, preferred_element_type=F32)
    return d(hi) + d(mid) + d(lo)


def _dot_exact_rhs(x, b, dims=_NN, parts=3):
    b = b.astype(BF16)
    d = lambda p: lax.dot_general(p, b, dims, preferred_element_type=F32)
    if parts == 2:
        hi, lo = _split2(x)
        return d(hi) + d(lo)
    hi, mid, lo = _split3(x)
    return d(hi) + d(mid) + d(lo)


def _dot_hp(a, b, dims=_NN):
    ah, al = _split2(a)
    bh, bl = _split2(b)
    d = lambda p, q: lax.dot_general(p, q, dims, preferred_element_type=F32)
    return d(ah, bh) + d(ah, bl) + d(al, bh)


def _sigmoid(x):
    return 1.0 / (1.0 + jnp.exp(-x))


def _iota2(shape, dim):
    return lax.broadcasted_iota(jnp.int32, shape, dim)


def _neumann_inv(L, eye, nsq):
    X = L
    P = eye - L
    for _ in range(nsq):
        X = _dot_hp(X, X)
        P = P + _dot_hp(P, X)
    return P


def _adaln_kernel(c_ref, w_ref, b_ref, o_ref):
    c = c_ref[...]
    s = c * _sigmoid(c)
    o_ref[...] = _dot(s, w_ref[...]) + b_ref[...]


def _adaln(c_all, ada_w, ada_b, tn=1536):
    R, D = c_all.shape
    N = ada_w.shape[1]
    return pl.pallas_call(
        _adaln_kernel,
        grid=(N // tn,),
        in_specs=[pl.BlockSpec((R, D), lambda j: (0, 0)),
                  pl.BlockSpec((D, tn), lambda j: (0, j)),
                  pl.BlockSpec((1, tn), lambda j: (0, j))],
        out_specs=pl.BlockSpec((R, tn), lambda j: (0, j)),
        out_shape=jax.ShapeDtypeStruct((R, N), F32),
        compiler_params=_cparams(("parallel",)),
        name="adaln",
    )(c_all, ada_w, ada_b.reshape(1, N))


class _Mod:
    def __init__(self, arr, per_seq, seq_rows, D):
        self.arr, self.per_seq, self.seq_rows, self.D = arr, per_seq, seq_rows, D
        self.rep = 1 if per_seq else seq_rows

    def spec(self, tm, comp):
        D = self.D
        if self.per_seq:
            tiles = self.seq_rows // tm
            return pl.BlockSpec((None, 1, D), lambda i, *_: (i // tiles, 0, comp))
        return pl.BlockSpec((tm // self.seq_rows, D), lambda i, *_: (i, comp))


def _rep_rows(v, rep):
    if rep == 1:
        return v
    n, w = v.shape
    return jnp.broadcast_to(v[:, None, :], (n, rep, w)).reshape(n * rep, w)


def _rms(x, w):
    return x * lax.rsqrt(jnp.mean(x * x, axis=-1, keepdims=True) + RMS_EPS) * w


def _mm_kernel(x_ref, wt_ref, o_ref):
    o_ref[...] = lax.dot_general(x_ref[...], wt_ref[...], _NT, preferred_element_type=F32)


def _matmul(x, wt, tm, tn, name):
    R, K = x.shape
    N = wt.shape[0]
    return pl.pallas_call(
        _mm_kernel,
        grid=(R // tm, N // tn),
        in_specs=[pl.BlockSpec((tm, K), lambda i, j: (i, 0)),
                  pl.BlockSpec((tn, K), lambda i, j: (j, 0))],
        out_specs=pl.BlockSpec((tm, tn), lambda i, j: (i, j)),
        out_shape=jax.ShapeDtypeStruct((R, N), F32),
        compiler_params=_cparams(("parallel", "arbitrary")),
        name=name,
    )(x, wt)


def _hgrn_gates(fp, lb_logits):
    m = jnp.max(lb_logits, axis=0, keepdims=True)
    e = jnp.exp(lb_logits - m)
    lb = e[0:1, :] / jnp.sum(e, axis=0, keepdims=True)
    f = lb + (1.0 - lb) * _sigmoid(fp)
    return jnp.log(f), 1.0 - f


def _hgrn_diag(q, kin, b, c):
    R = q.shape[0]
    nb = R // c
    q3 = q.reshape(nb, c, LANES)
    k3 = kin.reshape(nb, c, LANES)
    b3 = (b * math.log2(math.e)).reshape(nb, c, LANES)
    lane_mod = lax.broadcasted_iota(jnp.int32, (nb, c, R), 2) % c
    acc = jnp.zeros((nb, c, R), F32)
    for s in range(c):
        d = jnp.minimum(b3 - b3[:, s:s + 1, :], 0.0)
        col = jnp.sum(q3 * k3[:, s:s + 1, :] * jnp.exp2(d), axis=-1, keepdims=True)
        acc = jnp.where(lane_mod == s, col, acc)
    rr, cc = _iota2((R, R), 0), _iota2((R, R), 1)
    keep = ((rr // c) == (cc // c)) & (rr >= cc)
    return jnp.where(keep, acc.reshape(R, R), 0.0)


def _hgrn_cross(q, kin, b, m):
    R = q.shape[0]
    h = m // 2
    nb = R // m
    ref = jnp.broadcast_to(b.reshape(nb, m, LANES)[:, h - 1:h, :], (nb, m, LANES)).reshape(R, LANES)
    upper = (_iota2((R, 1), 0) % m) >= h
    qs = q * jnp.exp(jnp.where(upper, b - ref, -1e30))
    ks = kin * jnp.exp(jnp.where(upper, -1e30, ref - b))
    sc = _dot(qs, ks, _NT)
    if nb == 1:
        return sc
    same_blk = (_iota2((R, R), 0) // m) == (_iota2((R, R), 1) // m)
    return jnp.where(same_blk, sc, 0.0)


def _hgrn_out(o, og, nw):
    o = o * lax.rsqrt(jnp.mean(o * o, axis=-1, keepdims=True) + RMS_EPS) * nw
    return o * (og * _sigmoid(og))


def _hgrn_chunks(q_ref, f_ref, i_ref, g_ref, lb_ref, nw_ref, o_ref, st_ref, offs, hp):
    C, c = CHUNK, SUB
    nb = C // c
    tri = (_iota2((C, C), 0) >= _iota2((C, C), 1)).astype(F32)
    nw = nw_ref[...]
    H = range(hp)
    cols = [slice(j * LANES, (j + 1) * LANES) for j in H]
    lane_mod = lax.broadcasted_iota(jnp.int32, (nb, c, C), 2) % c
    rr, cc = _iota2((C, C), 0), _iota2((C, C), 1)
    keep = ((rr // c) == (cc // c)) & (rr >= cc)
    for r0 in offs:
        q = [q_ref[pl.ds(r0, C), cols[j]] for j in H]
        v = [i_ref[pl.ds(r0, C), cols[j]] for j in H]
        gates = [_hgrn_gates(f_ref[pl.ds(r0, C), cols[j]], lb_ref[:, cols[j]]) for j in H]
        lf = [gt[0] for gt in gates]
        kin = [gt[1] for gt in gates]
        yield
        b = [_dot_exact_lhs(tri, lf[j]) for j in H]
        b_last = [b[j][C - 1:C, :] for j in H]
        yield
        ST = [st_ref[j] for j in H]
        o = [_dot(q[j] * jnp.exp(b[j]), ST[j], _NT) for j in H]
        yield
        for j in H:
            st_ref[j] = ST[j] * jnp.exp(b_last[j]) + _dot(v[j], kin[j] * jnp.exp(b_last[j] - b[j]), _TN)
        yield
        q3 = [q[j].reshape(nb, c, LANES) for j in H]
        k3 = [kin[j].reshape(nb, c, LANES) for j in H]
        b3 = [(b[j] * math.log2(math.e)).reshape(nb, c, LANES) for j in H]
        acc = [jnp.zeros((nb, c, C), F32) for _ in H]
        for s in range(c):
            for j in H:
                d = jnp.minimum(b3[j] - b3[j][:, s:s + 1, :], 0.0)
                col = jnp.sum(q3[j] * k3[j][:, s:s + 1, :] * jnp.exp2(d), axis=-1, keepdims=True)
                acc[j] = jnp.where(lane_mod == s, col, acc[j])
            yield
        A = [jnp.where(keep, acc[j].reshape(C, C), 0.0) for j in H]
        m = 2 * c
        while m <= C:
            A = [A[j] + _hgrn_cross(q[j], kin[j], b[j], m) for j in H]
            m *= 2
            yield
        o = [o[j] + _dot(A[j], v[j]) for j in H]
        yield
        for j in H:
            o_ref[pl.ds(r0, C), cols[j]] = _hgrn_out(o[j], g_ref[pl.ds(r0, C), cols[j]], nw)
        yield


def _hgrn_sample_kernel(q_ref, f_ref, i_ref, g_ref, lb_ref, nw_ref, s0_ref, o_ref, s_ref, *, nseq, T, hp):
    R = nseq * T
    rr, cc = _iota2((R, R), 0), _iota2((R, R), 1)
    same = (rr // T) == (cc // T)
    seg_incl = (same & (rr >= cc)).astype(F32)
    seg_all = same.astype(F32)
    nw = nw_ref[...]
    H = range(hp)
    cols = [slice(j * LANES, (j + 1) * LANES) for j in H]
    q = [q_ref[:, cols[j]] for j in H]
    v = [i_ref[:, cols[j]] for j in H]
    gates = [_hgrn_gates(f_ref[:, cols[j]], lb_ref[:, cols[j]]) for j in H]
    lf = [gt[0] for gt in gates]
    kin = [gt[1] for gt in gates]
    b = [_dot_exact_lhs(seg_incl, lf[j]) for j in H]
    b_tot = [_dot_exact_lhs(seg_all, lf[j]) for j in H]
    A = [_hgrn_diag(q[j], kin[j], b[j], T) for j in H]
    o = [_dot(A[j], v[j]) for j in H]
    qg = [q[j] * jnp.exp(b[j]) for j in H]
    kh = [kin[j] * jnp.exp(b_tot[j] - b[j]) for j in H]
    ecol = [jnp.transpose(jnp.exp(b_tot[j]).reshape(nseq, T, LANES)[:, 0, :]) for j in H]
    inter = [[] for _ in H]
    for n in range(nseq):
        sl = slice(n * T, (n + 1) * T)
        for j in H:
            S = s0_ref[n, j]
            inter[j].append(_dot(qg[j][sl], S))
            s_ref[n, j] = S * ecol[j][:, n:n + 1] + _dot(kh[j][sl], v[j][sl], _TN)
    for j in H:
        oj = o[j] + jnp.concatenate(inter[j], axis=0)
        o_ref[:, cols[j]] = _hgrn_out(oj, g_ref[:, cols[j]], nw)


def _hgrn_sample(P_h, lb_logits, nw, s0, T, nseq=16, hp=4):
    R, W = P_h.shape
    H = W // (4 * HGRN_DK)
    HG = H // hp
    wd = hp * HGRN_DK
    B = R // T
    rows = nseq * T
    col = lambda off: pl.BlockSpec((rows, wd), lambda h, g: (g, off * HG + h))
    sspec = pl.BlockSpec((nseq, hp, HGRN_DK, HGRN_DK), lambda h, g: (g, h, 0, 0))
    o, s = pl.pallas_call(
        functools.partial(_hgrn_sample_kernel, nseq=nseq, T=T, hp=hp),
        grid=(HG, B // nseq),
        in_specs=[col(0), col(1), col(2), col(3),
                  pl.BlockSpec((lb_logits.shape[0], wd), lambda h, g: (0, h)),
                  pl.BlockSpec((1, HGRN_DK), lambda h, g: (0, 0)),
                  sspec],
        out_specs=[pl.BlockSpec((rows, wd), lambda h, g: (g, h)), sspec],
        out_shape=[jax.ShapeDtypeStruct((R, H * HGRN_DK), F32),
                   jax.ShapeDtypeStruct(s0.shape, F32)],
        compiler_params=_cparams(("parallel", "parallel")),
        name="hgrn_sample",
    )(P_h, P_h, P_h, P_h, lb_logits, nw.reshape(1, HGRN_DK), s0)
    return o, s


def _rwkv_prep_body(rc, prev, mu_ref, w0_ref, a0_ref, kk_ref, ka_ref, wwa_ref, g2_ref, outs, rows):
    r_o, lw_o, k_o, v_o, kap_o, bet_o, g_o = outs
    D = RC_MAIN // 3
    xs = rc + mu_ref[...] * (prev - rc)
    r = xs[:, 0:D]
    kb = xs[:, D:2 * D]
    v = xs[:, 2 * D:3 * D]
    xwa = xs[:, RC_MAIN:RC_MAIN + LANES]
    xg = xs[:, RC_MAIN + LANES:]
    lane = _iota2((1, LANES), 1)
    wa = _dot(jnp.where(lane < W_LORA, jnp.tanh(xwa), xwa), wwa_ref[...])
    w = w0_ref[...] + wa[:, :D]
    a = _sigmoid(a0_ref[...] + wa[:, D:])
    lw = -math.exp(-0.5) * _sigmoid(w)
    g = _dot(_sigmoid(xg), g2_ref[...])
    kk = kb * kk_ref[...]
    seg = (_iota2((LANES, LANES), 0) // RWKV_N == _iota2((LANES, LANES), 1) // RWKV_N).astype(F32)
    k_a = ka_ref[...]
    for p in range(D // LANES):
        sl = slice(p * LANES, (p + 1) * LANES)
        kkp = kk[:, sl]
        kap = kkp * lax.rsqrt(jnp.maximum(_dot_exact_rhs(kkp * kkp, seg, parts=2), 1e-24))
        ap = a[:, sl]
        kap_o[rows, sl] = kap
        bet_o[rows, sl] = ap * kap
        k_o[rows, sl] = kb[:, sl] * (1.0 + (ap - 1.0) * k_a[:, sl])
    r_o[rows, :] = r
    lw_o[rows, :] = lw
    v_o[rows, :] = v
    g_o[rows, :] = g


def _in_rwkv_kernel(x_ref, nw_ref, sh_ref, sc_ref, wr_ref, shift_ref, mu_ref, w0_ref, a0_ref, kk_ref,
                    ka_ref, wwa_ref, g2_ref, h_ref, r_o, lw_o, k_o, v_o, kap_o, bet_o, g_o, last_o,
                    carry_ref, rc_ref, *, seq_rows, sub):
    i = pl.program_id(0)
    tm = x_ref.shape[0]
    per_seq = seq_rows >= tm

    @pl.when(i == 0)
    def _():
        carry_ref[...] = jnp.zeros_like(carry_ref)

    rep = 1 if per_seq else seq_rows
    h = (_rms(x_ref[...], nw_ref[...]) * (1.0 + _rep_rows(sc_ref[...], rep))
         + _rep_rows(sh_ref[...], rep)).astype(BF16)
    h_ref[...] = h
    outs = (r_o, lw_o, k_o, v_o, kap_o, bet_o, g_o)
    rid = _iota2((sub, 1), 0)
    rc_ref[...] = lax.dot_general(h, wr_ref[...], _NT, preferred_element_type=F32)
    prev_last = None
    for s in range(tm // sub):
        rows = slice(s * sub, (s + 1) * sub)
        rc = rc_ref[rows, :]
        rolled = pltpu.roll(rc, 1, axis=0)
        if per_seq:
            if s == 0:
                start = (i % (seq_rows // tm)) == 0
                first = jnp.where(start, shift_ref[...], carry_ref[SUBLANES - 1:SUBLANES, :])
            else:
                first = prev_last
            prev = jnp.where(rid == 0, first, rolled)
        else:
            seqs = slice(s * sub // seq_rows, (s + 1) * sub // seq_rows)
            prev = jnp.where(rid % seq_rows == 0, _rep_rows(shift_ref[seqs, :], seq_rows), rolled)
            last_o[seqs, :] = rc.reshape(sub // seq_rows, seq_rows, RC_PAD)[:, seq_rows - 1, :]
        prev_last = rc[sub - 1:sub, :]
        if s == tm // sub - 1 and per_seq:
            carry_ref[...] = rc[sub - SUBLANES:, :]
            last_o[...] = prev_last
        _rwkv_prep_body(rc, prev, mu_ref, w0_ref, a0_ref, kk_ref, ka_ref, wwa_ref, g2_ref, outs, rows)


def _in_rwkv(x, nw, mod, w_r, shift, params, per_seq, seq_rows, tm=256, sub=128):
    R, D = x.shape
    tm = min(tm, seq_rows) if per_seq else min(tm, R)
    sub = min(sub, tm)
    mu, w0, a0, k_k, k_a, wwa, g2p = params
    const = lambda a: pl.BlockSpec(a.shape, lambda i: (0,) * a.ndim, pipeline_mode=pl.Buffered(1))
    wspecs = [const(a) for a in (mu, w0, a0, k_k, k_a, wwa, g2p)]
    row = lambda w: pl.BlockSpec((tm, w), lambda i: (i, 0))
    if per_seq:
        tiles = seq_rows // tm
        shift_spec = pl.BlockSpec((None, 1, RC_PAD), lambda i: (i // tiles, 0, 0))
        last_spec = pl.BlockSpec((None, 1, RC_PAD), lambda i: (i // tiles, 0, 0))
        last_shape = jax.ShapeDtypeStruct((R // seq_rows, 1, RC_PAD), F32)
    else:
        shift_spec = pl.BlockSpec((tm // seq_rows, RC_PAD), lambda i: (i, 0))
        last_spec = pl.BlockSpec((tm // seq_rows, RC_PAD), lambda i: (i, 0))
        last_shape = jax.ShapeDtypeStruct((R // seq_rows, RC_PAD), F32)
    outs = pl.pallas_call(
        functools.partial(_in_rwkv_kernel, seq_rows=seq_rows, sub=sub),
        grid=(R // tm,),
        in_specs=[row(D), const(nw), mod.spec(tm, 0), mod.spec(tm, 1), const(w_r), shift_spec] + wspecs,
        out_specs=[row(D)] + [row(D)] * 7 + [last_spec],
        out_shape=[jax.ShapeDtypeStruct((R, D), BF16)] + [jax.ShapeDtypeStruct((R, D), F32)] * 7
                  + [last_shape],
        scratch_shapes=[pltpu.VMEM((SUBLANES, RC_PAD), F32), pltpu.VMEM((tm, RC_PAD), F32)],
        compiler_params=_cparams(("arbitrary",)),
        name="in_rwkv",
    )(x, nw, mod.arr, mod.arr, w_r, shift, mu, w0, a0, k_k, k_a, wwa, g2p)
    return outs[0], outs[1:8], outs[8]


def _pair_masks():
    lane = _iota2((1, LANES), 1)
    return (lane < RWKV_N).astype(F32), (lane >= RWKV_N).astype(F32)


def _stack(x, mA, mB):
    return jnp.concatenate([x * mA, x * mB], axis=0)


def _rwkv_scores(cs, lw, cs_end, r, k, kap, bet, mA, mB):
    e_neg = jnp.exp(-cs)
    e_rem = jnp.exp(cs_end - cs)
    rg = _stack(r * jnp.exp(cs), mA, mB)
    kg = _stack(kap * jnp.exp(cs - lw), mA, mB)
    kd = _stack(k * e_neg, mA, mB)
    bd = _stack(bet * e_neg, mA, mB)
    kh = _stack(k * e_rem, mA, mB)
    bh = _stack(bet * e_rem, mA, mB)
    return rg, kg, kd, bd, kh, bh


def _rwkv_epilogue(o, r, k, v, g, lnw, lnb, rk):
    avg = jnp.where(_iota2((LANES, LANES), 0) // RWKV_N == _iota2((LANES, LANES), 1) // RWKV_N,
                    1.0 / RWKV_N, 0.0).astype(F32)
    mean = _dot_exact_rhs(o, avg)
    d = o - mean
    var = _dot_exact_rhs(d * d, avg)
    on = d * lax.rsqrt(var + LNX_EPS) * lnw + lnb
    bonus = _dot_exact_rhs(r * k * rk, avg) * float(RWKV_N) * v
    return (on + bonus) * g


def _rwkv_chunks(r_ref, lw_ref, k_ref, v_ref, kap_ref, bet_ref, g_ref, lnw_ref, lnb_ref, rk_ref,
                 o_ref, st_ref, offs, hp):
    C, c = CHUNK, RWKV_SUB
    C2 = 2 * C
    mA, mB = _pair_masks()
    rr, cc = _iota2((C2, C2), 0), _iota2((C2, C2), 1)
    eye = (rr == cc).astype(F32)
    strict = rr > cc
    incl = rr >= cc
    blk = (rr // c) == (cc // c)
    tri = (_iota2((C, C), 0) >= _iota2((C, C), 1)).astype(F32)
    avg = jnp.where(_iota2((LANES, LANES), 0) // RWKV_N == _iota2((LANES, LANES), 1) // RWKV_N,
                    1.0 / RWKV_N, 0.0).astype(F32)
    chains = [(r0, p) for r0 in offs for p in range(hp)]
    J = range(len(chains))
    pcols = [slice(p * LANES, (p + 1) * LANES) for _, p in chains]

    ld = lambda ref: [ref[pl.ds(r0, C), pcols[j]] for j, (r0, _) in zip(J, chains)]
    r, lw, k, v, kap, bet = ld(r_ref), ld(lw_ref), ld(k_ref), ld(v_ref), ld(kap_ref), ld(bet_ref)
    yield
    cs = [_dot_exact_lhs(tri, lw[j]) for j in J]
    cs_end = [cs[j][C - 1:C, :] for j in J]
    yield
    sc = [_rwkv_scores(cs[j], lw[j], cs_end[j], r[j], k[j], kap[j], bet[j], mA, mB) for j in J]
    rg, kg, kd, bd, kh, bh = ([s[i] for s in sc] for i in range(6))
    vs = [_stack(v[j], mA, mB) for j in J]
    kbd = [jnp.concatenate([kd[j], bd[j]], axis=0) for j in J]
    yield
    sck = [_dot(kg[j], kbd[j], _NT) for j in J]
    yield
    scr = [_dot(rg[j], kbd[j], _NT) for j in J]
    Ak = [jnp.where(strict, sck[j][:, :C2], 0.0) for j in J]
    Ab = [jnp.where(strict, sck[j][:, C2:], 0.0) for j in J]
    yield
    Bkb = [jnp.concatenate([jnp.where(incl, scr[j][:, :C2], 0.0),
                            jnp.where(incl, -scr[j][:, C2:], 0.0)], axis=1) for j in J]
    LD = [jnp.where(blk, Ab[j], 0.0) for j in J]
    X = LD
    Dinv = [eye - LD[j] for j in J]
    for _ in range(int(math.log2(c)) - 1):
        X = [_dot(X[j], X[j]) for j in J]
        yield
        Dinv = [Dinv[j] + _dot(Dinv[j], X[j]) for j in J]
        yield
    Q = [_dot(Dinv[j], Ab[j] - LD[j]) for j in J]
    yield
    T0 = [Dinv[j] - _dot(Q[j], Dinv[j]) for j in J]
    yield
    for _ in range(int(math.log2(C // c)) - 1):
        Q = [_dot(Q[j], Q[j]) for j in J]
        yield
        T0 = [T0[j] + _dot(Q[j], T0[j]) for j in J]
        yield
    E = [(eye - T0[j]) - _dot_hp(Ab[j], T0[j]) for j in J]
    yield
    Tm = [T0[j] + _dot(T0[j], E[j]) for j in J]
    yield
    khb = [jnp.concatenate([kh[j], -bh[j]], axis=0) for j in J]
    AkV = [_dot(Ak[j], vs[j]) for j in J]
    dec = [jnp.exp(cs_end[j]) for j in J]
    yield
    bonus = [_dot_exact_rhs(r[j] * k[j] * rk_ref[:, pcols[j]], avg, parts=2) * float(RWKV_N) * v[j]
             for j in J]
    yield

    for u in range(len(offs)):
        K = range(u * hp, (u + 1) * hp)
        S = {j: st_ref[chains[j][1]] for j in K}
        rhs = {j: _dot(kg[j], S[j], _NT) + AkV[j] for j in K}
        yield
        U = {j: _dot_hp(Tm[j], rhs[j]) for j in K}
        yield
        vu = {j: jnp.concatenate([vs[j], U[j]], axis=0) for j in K}
        for j in K:
            st_ref[chains[j][1]] = S[j] * dec[j] + _dot(vu[j], khb[j], _TN)
        yield
        O = {j: _dot(rg[j], S[j], _NT) + _dot(Bkb[j], vu[j]) for j in K}
        o = {j: O[j][:C, :] + O[j][C:, :] for j in K}
        yield
        mean = {j: _dot_exact_rhs(o[j], avg, parts=2) for j in K}
        d = {j: o[j] - mean[j] for j in K}
        yield
        var = {j: _dot_exact_rhs(d[j] * d[j], avg, parts=2) for j in K}
        yield
        for j in K:
            r0 = chains[j][0]
            on = d[j] * lax.rsqrt(var[j] + LNX_EPS) * lnw_ref[:, pcols[j]] + lnb_ref[:, pcols[j]]
            o_ref[pl.ds(r0, C), pcols[j]] = (on + bonus[j]) * g_ref[pl.ds(r0, C), pcols[j]]
        yield


def _interleave(gens):
    live = list(gens)
    while live:
        for gen in list(live):
            try:
                next(gen)
            except StopIteration:
                live.remove(gen)


def _scan_prompt_kernel(q_ref, f_ref, i_ref, og_ref, lb_ref, nw_ref,
                        r_ref, lw_ref, k_ref, v_ref, kap_ref, bet_ref, g_ref, lnw_ref, lnb_ref, rk_ref,
                        oa_ref, hs_ref, ob_ref, rs_ref, hst_ref, rst_ref, *, nchunks, hh, hp):
    tt = pl.program_id(1)

    @pl.when(tt == 0)
    def _():
        hst_ref[...] = jnp.zeros_like(hst_ref)
        rst_ref[...] = jnp.zeros_like(rst_ref)

    C = CHUNK
    unroll = 2 if nchunks % 2 == 0 else 1

    def body(ci, carry):
        base = ci * (unroll * C)
        offs = [pl.multiple_of(base + u * C, C) for u in range(unroll)]
        _interleave([
            _rwkv_chunks(r_ref, lw_ref, k_ref, v_ref, kap_ref, bet_ref, g_ref, lnw_ref, lnb_ref, rk_ref,
                         ob_ref, rst_ref, offs, hp),
            _hgrn_chunks(q_ref, f_ref, i_ref, og_ref, lb_ref, nw_ref, oa_ref, hst_ref, offs, hh),
        ])
        return carry

    lax.fori_loop(0, nchunks // unroll, body, 0)

    @pl.when(tt == pl.num_programs(1) - 1)
    def _():
        for j in range(hh):
            hs_ref[j] = hst_ref[j].T
        for p in range(hp):
            S = rst_ref[p]
            rs_ref[2 * p] = S[:RWKV_N, :RWKV_N]
            rs_ref[2 * p + 1] = S[RWKV_N:, RWKV_N:]


def _scan_prompt(P_h, lb_logits, nw, arrs, g, lnw, lnb, rk, B, T, tt_rows=256):
    R, D = arrs[0].shape
    hh = P_h.shape[1] // (4 * HGRN_DK)
    hp = D // LANES
    wh = hh * HGRN_DK
    tt_rows = min(tt_rows, T)
    nt = T // tt_rows
    hcol = lambda off: pl.BlockSpec((tt_rows, wh), lambda b, t: (b * nt + t, off))
    col = pl.BlockSpec((tt_rows, D), lambda b, t: (b * nt + t, 0))
    const = lambda a: pl.BlockSpec(a.shape, lambda b, t: (0,) * a.ndim)
    nw = nw.reshape(1, HGRN_DK)
    lnw, lnb, rk = lnw.reshape(1, D), lnb.reshape(1, D), rk.reshape(1, D)
    oa, hs, ob, rs = pl.pallas_call(
        functools.partial(_scan_prompt_kernel, nchunks=tt_rows // CHUNK, hh=hh, hp=hp),
        grid=(B, nt),
        in_specs=[hcol(0), hcol(1), hcol(2), hcol(3), const(lb_logits), const(nw)]
                 + [col] * 7 + [const(lnw), const(lnb), const(rk)],
        out_specs=[pl.BlockSpec((tt_rows, wh), lambda b, t: (b * nt + t, 0)),
                   pl.BlockSpec((None, hh, HGRN_DK, HGRN_DK), lambda b, t: (b, 0, 0, 0)),
                   col,
                   pl.BlockSpec((None, 2 * hp, RWKV_N, RWKV_N), lambda b, t: (b, 0, 0, 0))],
        out_shape=[jax.ShapeDtypeStruct((R, wh), F32),
                   jax.ShapeDtypeStruct((B, hh, HGRN_DK, HGRN_DK), F32),
                   jax.ShapeDtypeStruct((R, D), F32),
                   jax.ShapeDtypeStruct((B, 2 * hp, RWKV_N, RWKV_N), F32)],
        scratch_shapes=[pltpu.VMEM((hh, HGRN_DK, HGRN_DK), F32), pltpu.VMEM((hp, LANES, LANES), F32)],
        compiler_params=_cparams(("parallel", "arbitrary")),
        name="scan_prompt",
    )(P_h, P_h, P_h, P_h, lb_logits, nw, *arrs, g, lnw, lnb, rk)
    return oa, hs, ob, rs


def _rwkv_sample_kernel(r_ref, lw_ref, k_ref, v_ref, kap_ref, bet_ref, g_ref, lnw_ref, lnb_ref, rk_ref,
                        s0_ref, o_ref, s_ref, sbd_ref, *, nseq, T, hp):
    C = nseq * T
    C2 = 2 * C
    N = RWKV_N
    mA, mB = _pair_masks()
    bdm = (_iota2((LANES, LANES), 0) // N == _iota2((LANES, LANES), 1) // N).astype(F32)
    rr, cc = _iota2((C2, C2), 0), _iota2((C2, C2), 1)
    eye = (rr == cc).astype(F32)
    same = (rr // T) == (cc // T)
    strict = same & (rr > cc)
    incl = same & (rr >= cc)
    r1, c1 = _iota2((C, C), 0), _iota2((C, C), 1)
    seg_incl = ((r1 // T == c1 // T) & (r1 >= c1)).astype(F32)
    seg_all = (r1 // T == c1 // T).astype(F32)
    zero = jnp.zeros((N, N), F32)

    J = range(hp)
    cols = [slice(j * LANES, (j + 1) * LANES) for j in J]
    ld = lambda ref: [ref[:, cols[j]] for j in J]
    r, lw, k, v, kap, bet = ld(r_ref), ld(lw_ref), ld(k_ref), ld(v_ref), ld(kap_ref), ld(bet_ref)
    cs = [_dot_exact_lhs(seg_incl, lw[j]) for j in J]
    cs_end = [_dot_exact_lhs(seg_all, lw[j]) for j in J]
    sc = [_rwkv_scores(cs[j], lw[j], cs_end[j], r[j], k[j], kap[j], bet[j], mA, mB) for j in J]
    rg, kg, kd, bd, kh, bh = ([s[i] for s in sc] for i in range(6))
    vs = [_stack(v[j], mA, mB) for j in J]
    kbd = [jnp.concatenate([kd[j], bd[j]], axis=0) for j in J]
    sck = [_dot(kg[j], kbd[j], _NT) for j in J]
    scr = [_dot(rg[j], kbd[j], _NT) for j in J]
    Ak = [jnp.where(strict, sck[j][:, :C2], 0.0) for j in J]
    Ab = [jnp.where(strict, sck[j][:, C2:], 0.0) for j in J]
    Bkb = [jnp.concatenate([jnp.where(incl, scr[j][:, :C2], 0.0),
                            jnp.where(incl, -scr[j][:, C2:], 0.0)], axis=1) for j in J]
    Tm = [_neumann_inv(Ab[j], eye, int(math.log2(T)) - 1) for j in J]
    kgp = [kg[j][:C, :] + kg[j][C:, :] for j in J]
    rgp = [rg[j][:C, :] + rg[j][C:, :] for j in J]
    ks = [[] for _ in J]
    rs = [[] for _ in J]
    for n in range(nseq):
        sl = slice(n * T, (n + 1) * T)
        for j in J:
            Sbd = jnp.concatenate(
                [jnp.concatenate([s0_ref[n, 2 * j], zero], axis=1),
                 jnp.concatenate([zero, s0_ref[n, 2 * j + 1]], axis=1)], axis=0)
            sbd_ref[n, j] = Sbd
            res = _dot(jnp.concatenate([kgp[j][sl], rgp[j][sl]], axis=0), Sbd, _NT)
            ks[j].append(res[:T])
            rs[j].append(res[T:])
    KS = [_stack(jnp.concatenate(ks[j], axis=0), mA, mB) for j in J]
    RS = [_stack(jnp.concatenate(rs[j], axis=0), mA, mB) for j in J]
    rhs = [KS[j] + _dot(Ak[j], vs[j]) for j in J]
    U = [_dot_hp(Tm[j], rhs[j]) for j in J]
    vu = [jnp.concatenate([vs[j], U[j]], axis=0) for j in J]
    O = [RS[j] + _dot(Bkb[j], vu[j]) for j in J]
    Up = [U[j][:C, :] + U[j][C:, :] for j in J]
    khp = [kh[j][:C, :] + kh[j][C:, :] for j in J]
    nbhp = [-(bh[j][:C, :] + bh[j][C:, :]) for j in J]
    dec = [jnp.exp(cs_end[j]) for j in J]
    for n in range(nseq):
        sl = slice(n * T, (n + 1) * T)
        for j in J:
            upd = _dot(jnp.concatenate([v[j][sl], Up[j][sl]], axis=0),
                       jnp.concatenate([khp[j][sl], nbhp[j][sl]], axis=0), _TN)
            Sn = sbd_ref[n, j] * dec[j][n * T:n * T + 1, :] + upd * bdm
            s_ref[n, 2 * j] = Sn[:N, :N]
            s_ref[n, 2 * j + 1] = Sn[N:, N:]
    for j in J:
        o = O[j][:C, :] + O[j][C:, :]
        o_ref[:, cols[j]] = _rwkv_epilogue(o, r[j], k[j], v[j], g_ref[:, cols[j]],
                                           lnw_ref[:, cols[j]], lnb_ref[:, cols[j]], rk_ref[:, cols[j]])


def _rwkv_sample(arrs, g, lnw, lnb, rk, s0, T, nseq=16, hp=4):
    R, D = arrs[0].shape
    NP = D // LANES
    B = R // T
    rows = nseq * T
    wd = hp * LANES
    col = pl.BlockSpec((rows, wd), lambda p, gi: (gi, p))
    par = pl.BlockSpec((1, wd), lambda p, gi: (0, p))
    sspec = pl.BlockSpec((nseq, 2 * hp, RWKV_N, RWKV_N), lambda p, gi: (gi, p, 0, 0))
    o, s = pl.pallas_call(
        functools.partial(_rwkv_sample_kernel, nseq=nseq, T=T, hp=hp),
        grid=(NP // hp, B // nseq),
        in_specs=[col] * 7 + [par] * 3 + [sspec],
        out_specs=[col, sspec],
        out_shape=[jax.ShapeDtypeStruct((R, D), F32), jax.ShapeDtypeStruct(s0.shape, F32)],
        scratch_shapes=[pltpu.VMEM((nseq, hp, LANES, LANES), F32)],
        compiler_params=_cparams(("parallel", "parallel")),
        name="rwkv_sample",
    )(*arrs, g, lnw.reshape(1, D), lnb.reshape(1, D), rk.reshape(1, D), s0)
    return o, s


def _merge_kernel(x_ref, h_ref, oa_ref, ob_ref, wg_ref, w_ref, gt_ref, o_ref, *, rep):
    D = x_ref.shape[1]
    pg = lax.dot_general(h_ref[...], wg_ref[...], _NT, preferred_element_type=F32)
    m = _sigmoid(pg[:, :D]) * oa_ref[...] + _sigmoid(pg[:, D:]) * ob_ref[...]
    o_ref[...] = x_ref[...] + _rep_rows(gt_ref[...], rep) * _dot(m, w_ref[...])


def _merge_out(x, h, o_a, o_b, w_g, w_out, mod, tm):
    R, D = x.shape
    row = pl.BlockSpec((tm, D), lambda i: (i, 0))
    const = lambda a: pl.BlockSpec(a.shape, lambda i: (0,) * a.ndim, pipeline_mode=pl.Buffered(1))
    return pl.pallas_call(
        functools.partial(_merge_kernel, rep=mod.rep),
        grid=(R // tm,),
        in_specs=[row, row, row, row, const(w_g), const(w_out), mod.spec(tm, 2)],
        out_specs=row,
        out_shape=jax.ShapeDtypeStruct((R, D), F32),
        compiler_params=_cparams(("parallel",)),
        name="merge_out",
    )(x, h, o_a, o_b, w_g, w_out, mod.arr)


def _mlp_kernel(x_ref, nw_ref, sh_ref, sc_ref, gt_ref, wu_ref, wd_ref, fw_ref, o_ref, *, tf, rep):
    x = x_ref[...]
    h = (_rms(x, nw_ref[...]) * (1.0 + _rep_rows(sc_ref[...], rep)) + _rep_rows(sh_ref[...], rep)).astype(BF16)
    acc = None
    for c in range(wu_ref.shape[1] // tf):
        u = jnp.dot(h, wu_ref[:, c * tf:(c + 1) * tf], preferred_element_type=F32)
        u = jnp.square(jnp.maximum(u, 0.0))
        part = _dot(u, wd_ref[c * tf:(c + 1) * tf, :])
        acc = part if acc is None else acc + part
    o_ref[...] = _rms(x + _rep_rows(gt_ref[...], rep) * acc, fw_ref[...])


def _mlp(x, nw, w_up, w_down, fw, mod, tm, tf=1024):
    R, D = x.shape
    const = lambda a: pl.BlockSpec(a.shape, lambda i: (0,) * a.ndim, pipeline_mode=pl.Buffered(1))
    row = pl.BlockSpec((tm, D), lambda i: (i, 0))
    nw, fw = nw.reshape(1, D), fw.reshape(1, D)
    return pl.pallas_call(
        functools.partial(_mlp_kernel, tf=tf, rep=mod.rep),
        grid=(R // tm,),
        in_specs=[row, const(nw), mod.spec(tm, 3), mod.spec(tm, 4), mod.spec(tm, 5),
                  const(w_up), const(w_down), const(fw)],
        out_specs=row,
        out_shape=jax.ShapeDtypeStruct((R, D), F32),
        compiler_params=_cparams(("parallel",)),
        name="mlp",
    )(x, nw, mod.arr, mod.arr, mod.arr, w_up, w_down, fw)


def _run(x3, mod_rows, s_hgrn, s_rwkv, s_shift, W, prompt):
    B, T, D = x3.shape
    R = B * T
    x = x3.reshape(R, D)
    if prompt:
        mod = _Mod(mod_rows.reshape(B, 1, -1), True, T, D)
        tm = min(1024, T)
    else:
        mod = _Mod(mod_rows, False, T, D)
        tm = min(512, R)

    rw = (W["mu"], W["w0"], W["a0"], W["k_k"], W["k_a"], W["wwa"], W["g2p"])
    nw1 = W["norm1_w"].reshape(1, D)
    if prompt:
        shift_in = jnp.zeros((B, 1, RC_PAD), F32)
        h, arrs, last = _in_rwkv(x, nw1, mod, W["w_r"], shift_in, rw, True, T)
        shift_out = last[:, 0, :RC_COLS]
        P_h = _matmul(h, W["w_h"], tm, 2048, "proj_h")
        o_a, hg, o_b, rs = _scan_prompt(P_h, W["lb_logits"], W["hgrn_norm_w"], arrs[:6], arrs[6],
                                        W["lnx_w"], W["lnx_b"], W["r_k"], B, T)
    else:
        shift_in = jnp.pad(s_shift, ((0, 0), (0, RC_PAD - RC_COLS)))
        h, arrs, last = _in_rwkv(x, nw1, mod, W["w_r"], shift_in, rw, False, T)
        shift_out = last[:, :RC_COLS]
        P_h = _matmul(h, W["w_h"], tm, 2048, "proj_h")
        o_a, hg = _hgrn_sample(P_h, W["lb_logits"], W["hgrn_norm_w"], s_hgrn, T)
        o_b, rs = _rwkv_sample(arrs[:6], arrs[6], W["lnx_w"], W["lnx_b"], W["r_k"], s_rwkv, T)

    x1 = _merge_out(x, h, o_a, o_b, W["w_g"], W["w_out"], mod, min(tm, 512))
    y = _mlp(x1, W["norm2_w"], W["w_up"], W["w_down"], W["final_norm_w"], mod, min(tm, 512))
    return y.reshape(B, T, D), hg, rs, shift_out


def kernel(x_prompt, x_sample, state_hgrn, state_rwkv, state_shift, c_prompt, c_sample, norm1_w, norm2_w, ada_w, ada_b, w_in, lb_logits, hgrn_norm_w, rwkv_mu, rwkv_w0, rwkv_w2, rwkv_a0, rwkv_a2, rwkv_g2, rwkv_k_k, rwkv_k_a, rwkv_r_k, rwkv_lnx_w, rwkv_lnx_b, w_out, w_up, w_down, final_norm_w):
    depth = norm1_w.shape[0]
    assert depth == 1, "single-layer step"
    D = x_prompt.shape[-1]
    HW = 4 * D
    w_in_t = w_in[0].T
    row = lambda a: a.reshape(1, -1)
    wwa = jnp.zeros((LANES, 2 * D), F32)
    wwa = wwa.at[:W_LORA, :D].set(rwkv_w2[0]).at[W_LORA:W_LORA + A_LORA, D:].set(rwkv_a2[0])
    W = {
        "norm1_w": norm1_w[0], "norm2_w": norm2_w[0], "final_norm_w": final_norm_w,
        "w_h": w_in_t[:HW].astype(BF16),
        "w_r": jnp.pad(w_in_t[HW:HW + RC_COLS], ((0, RC_PAD - RC_COLS), (0, 0))).astype(BF16),
        "w_g": w_in_t[HW + RC_COLS:].astype(BF16),
        "lb_logits": lb_logits, "hgrn_norm_w": hgrn_norm_w[0],
        "mu": jnp.pad(row(rwkv_mu[0]), ((0, 0), (0, RC_PAD - RC_COLS))),
        "w0": row(rwkv_w0[0]), "a0": row(rwkv_a0[0]), "k_k": row(rwkv_k_k[0]), "k_a": row(rwkv_k_a[0]),
        "wwa": wwa.astype(BF16),
        "g2p": jnp.pad(rwkv_g2[0], ((0, G_LORA_PAD - G_LORA), (0, 0))).astype(BF16),
        "lnx_w": rwkv_lnx_w[0], "lnx_b": rwkv_lnx_b[0], "r_k": rwkv_r_k[0].reshape(-1),
        "w_out": w_out[0].astype(BF16), "w_up": w_up[0].astype(BF16), "w_down": w_down[0].astype(BF16),
    }
    nb = x_prompt.shape[0]
    mod = _adaln(jnp.concatenate([c_prompt, c_sample], axis=0), ada_w[0], ada_b[0])
    y_p, hg_p, rw_p, sh_p = _run(x_prompt, mod[:nb], None, None, None, W, True)
    y_s, hg_s, rw_s, sh_s = _run(x_sample, mod[nb:], state_hgrn[0], state_rwkv[0], state_shift[0], W, False)
    ex = lambda a: a[None]
    return (y_p, y_s, ex(hg_p), ex(rw_p), ex(sh_p), ex(hg_s), ex(rw_s), ex(sh_s))
```

```python
import functools
import math

import jax
import jax.numpy as jnp
from jax import lax
from jax.experimental import pallas as pl
from jax.experimental.pallas import tpu as pltpu

F32 = jnp.float32
BF16 = jnp.bfloat16

LANES = 128
SUBLANES = 8
VMEM_LIMIT = 48 * 1024 * 1024

HGRN_DK = 128
RWKV_N = 64
W_LORA = 64
A_LORA = 64
G_LORA = 160
G_LORA_PAD = 256
RC_MAIN = 3 * 1024
RC_COLS = RC_MAIN + W_LORA + A_LORA + G_LORA
RC_PAD = RC_MAIN + LANES + G_LORA_PAD
RMS_EPS = 1e-6
LNX_EPS = 64e-5
CHUNK = 64
SUB = 8
RWKV_SUB = 8


def _cparams(sem, vmem=VMEM_LIMIT):
    return pltpu.CompilerParams(dimension_semantics=sem, vmem_limit_bytes=vmem)


_NN = (((1,), (0,)), ((), ()))
_NT = (((1,), (1,)), ((), ()))
_TN = (((0,), (0,)), ((), ()))


def _dot(a, b, dims=_NN):
    return lax.dot_general(a.astype(BF16), b.astype(BF16), dims, preferred_element_type=F32)


def _split2(x):
    hi = x.astype(BF16)
    lo = (x - hi.astype(F32)).astype(BF16)
    return hi, lo


def _split3(x):
    hi = x.astype(BF16)
    r1 = x - hi.astype(F32)
    mid = r1.astype(BF16)
    lo = (r1 - mid.astype(F32)).astype(BF16)
    return hi, mid, lo


def _dot_exact_lhs(a, x, dims=_NN):
    a = a.astype(BF16)
    hi, mid, lo = _split3(x)
    d = lambda p: lax.dot_general(a, p, dims, preferred_element_type=F32)
    return d(hi) + d(mid) + d(lo)


def _dot_exact_rhs(x, b, dims=_NN, parts=3):
    b = b.astype(BF16)
    d = lambda p: lax.dot_general(p, b, dims, preferred_element_type=F32)
    if parts == 2:
        hi, lo = _split2(x)
        return d(hi) + d(lo)
    hi, mid, lo = _split3(x)
    return d(hi) + d(mid) + d(lo)


def _dot_hp(a, b, dims=_NN):
    ah, al = _split2(a)
    bh, bl = _split2(b)
    d = lambda p, q: lax.dot_general(p, q, dims, preferred_element_type=F32)
    return d(ah, bh) + d(ah, bl) + d(al, bh)


def _sigmoid(x):
    return 1.0 / (1.0 + jnp.exp(-x))


def _iota2(shape, dim):
    return lax.broadcasted_iota(jnp.int32, shape, dim)


def _neumann_inv(L, eye, nsq):
    X = L
    P = eye - L
    for _ in range(nsq):
        X = _dot_hp(X, X)
        P = P + _dot_hp(P, X)
    return P


def _adaln_kernel(c_ref, w_ref, b_ref, o_ref):
    c = c_ref[...]
    s = c * _sigmoid(c)
    o_ref[...] = _dot(s, w_ref[...]) + b_ref[...]


def _adaln(c_all, ada_w, ada_b, tn=1536):
    R, D = c_all.shape
    N = ada_w.shape[1]
    return pl.pallas_call(
        _adaln_kernel,
        grid=(N // tn,),
        in_specs=[pl.BlockSpec((R, D), lambda j: (0, 0)),
                  pl.BlockSpec((D, tn), lambda j: (0, j)),
                  pl.BlockSpec((1, tn), lambda j: (0, j))],
        out_specs=pl.BlockSpec((R, tn), lambda j: (0, j)),
        out_shape=jax.ShapeDtypeStruct((R, N), F32),
        compiler_params=_cparams(("parallel",)),
        name="adaln",
    )(c_all, ada_w, ada_b.reshape(1, N))


class _Mod:
    def __init__(self, arr, per_seq, seq_rows, D):
        self.arr, self.per_seq, self.seq_rows, self.D = arr, per_seq, seq_rows, D
        self.rep = 1 if per_seq else seq_rows

    def spec(self, tm, comp):
        D = self.D
        if self.per_seq:
            tiles = self.seq_rows // tm
            return pl.BlockSpec((None, 1, D), lambda i, *_: (i // tiles, 0, comp))
        return pl.BlockSpec((tm // self.seq_rows, D), lambda i, *_: (i, comp))


def _rep_rows(v, rep):
    if rep == 1:
        return v
    n, w = v.shape
    return jnp.broadcast_to(v[:, None, :], (n, rep, w)).reshape(n * rep, w)


def _rms(x, w):
    return x * lax.rsqrt(jnp.mean(x * x, axis=-1, keepdims=True) + RMS_EPS) * w


def _mm_kernel(x_ref, wt_ref, o_ref):
    o_ref[...] = lax.dot_general(x_ref[...], wt_ref[...], _NT, preferred_element_type=F32)


def _matmul(x, wt, tm, tn, name):
    R, K = x.shape
    N = wt.shape[0]
    return pl.pallas_call(
        _mm_kernel,
        grid=(R // tm, N // tn),
        in_specs=[pl.BlockSpec((tm, K), lambda i, j: (i, 0)),
                  pl.BlockSpec((tn, K), lambda i, j: (j, 0))],
        out_specs=pl.BlockSpec((tm, tn), lambda i, j: (i, j)),
        out_shape=jax.ShapeDtypeStruct((R, N), F32),
        compiler_params=_cparams(("parallel", "arbitrary")),
        name=name,
    )(x, wt)


def _hgrn_gates(fp, lb_logits):
    m = jnp.max(lb_logits, axis=0, keepdims=True)
    e = jnp.exp(lb_logits - m)
    lb = e[0:1, :] / jnp.sum(e, axis=0, keepdims=True)
    f = lb + (1.0 - lb) * _sigmoid(fp)
    return jnp.log(f), 1.0 - f


def _hgrn_diag(q, kin, b, c):
    R = q.shape[0]
    nb = R // c
    q3 = q.reshape(nb, c, LANES)
    k3 = kin.reshape(nb, c, LANES)
    b3 = (b * math.log2(math.e)).reshape(nb, c, LANES)
    lane_mod = lax.broadcasted_iota(jnp.int32, (nb, c, R), 2) % c
    acc = jnp.zeros((nb, c, R), F32)
    for s in range(c):
        d = jnp.minimum(b3 - b3[:, s:s + 1, :], 0.0)
        col = jnp.sum(q3 * k3[:, s:s + 1, :] * jnp.exp2(d), axis=-1, keepdims=True)
        acc = jnp.where(lane_mod == s, col, acc)
    rr, cc = _iota2((R, R), 0), _iota2((R, R), 1)
    keep = ((rr // c) == (cc // c)) & (rr >= cc)
    return jnp.where(keep, acc.reshape(R, R), 0.0)


def _hgrn_cross(q, kin, b, m):
    R = q.shape[0]
    h = m // 2
    nb = R // m
    ref = jnp.broadcast_to(b.reshape(nb, m, LANES)[:, h - 1:h, :], (nb, m, LANES)).reshape(R, LANES)
    upper = (_iota2((R, 1), 0) % m) >= h
    qs = q * jnp.exp(jnp.where(upper, b - ref, -1e30))
    ks = kin * jnp.exp(jnp.where(upper, -1e30, ref - b))
    sc = _dot(qs, ks, _NT)
    if nb == 1:
        return sc
    same_blk = (_iota2((R, R), 0) // m) == (_iota2((R, R), 1) // m)
    return jnp.where(same_blk, sc, 0.0)


def _hgrn_out(o, og, nw):
    o = o * lax.rsqrt(jnp.mean(o * o, axis=-1, keepdims=True) + RMS_EPS) * nw
    return o * (og * _sigmoid(og))


def _hgrn_chunks(q_ref, f_ref, i_ref, g_ref, lb_ref, nw_ref, o_ref, st_ref, offs, hp):
    C, c = CHUNK, SUB
    nb = C // c
    tri = (_iota2((C, C), 0) >= _iota2((C, C), 1)).astype(F32)
    nw = nw_ref[...]
    H = range(hp)
    cols = [slice(j * LANES, (j + 1) * LANES) for j in H]
    lane_mod = lax.broadcasted_iota(jnp.int32, (nb, c, C), 2) % c
    rr, cc = _iota2((C, C), 0), _iota2((C, C), 1)
    keep = ((rr // c) == (cc // c)) & (rr >= cc)
    for r0 in offs:
        q = [q_ref[pl.ds(r0, C), cols[j]] for j in H]
        v = [i_ref[pl.ds(r0, C), cols[j]] for j in H]
        gates = [_hgrn_gates(f_ref[pl.ds(r0, C), cols[j]], lb_ref[:, cols[j]]) for j in H]
        lf = [gt[0] for gt in gates]
        kin = [gt[1] for gt in gates]
        yield
        b = [_dot_exact_lhs(tri, lf[j]) for j in H]
        b_last = [b[j][C - 1:C, :] for j in H]
        yield
        ST = [st_ref[j] for j in H]
        o = [_dot(q[j] * jnp.exp(b[j]), ST[j], _NT) for j in H]
        yield
        for j in H:
            st_ref[j] = ST[j] * jnp.exp(b_last[j]) + _dot(v[j], kin[j] * jnp.exp(b_last[j] - b[j]), _TN)
        yield
        q3 = [q[j].reshape(nb, c, LANES) for j in H]
        k3 = [kin[j].reshape(nb, c, LANES) for j in H]
        b3 = [(b[j] * math.log2(math.e)).reshape(nb, c, LANES) for j in H]
        acc = [jnp.zeros((nb, c, C), F32) for _ in H]
        for s in range(c):
            for j in H:
                d = jnp.minimum(b3[j] - b3[j][:, s:s + 1, :], 0.0)
                col = jnp.sum(q3[j] * k3[j][:, s:s + 1, :] * jnp.exp2(d), axis=-1, keepdims=True)
                acc[j] = jnp.where(lane_mod == s, col, acc[j])
            yield
        A = [jnp.where(keep, acc[j].reshape(C, C), 0.0) for j in H]
        m = 2 * c
        while m <= C:
            A = [A[j] + _hgrn_cross(q[j], kin[j], b[j], m) for j in H]
            m *= 2
            yield
        o = [o[j] + _dot(A[j], v[j]) for j in H]
        yield
        for j in H:
            o_ref[pl.ds(r0, C), cols[j]] = _hgrn_out(o[j], g_ref[pl.ds(r0, C), cols[j]], nw)
        yield


def _hgrn_sample_kernel(q_ref, f_ref, i_ref, g_ref, lb_ref, nw_ref, s0_ref, o_ref, s_ref, *, nseq, T, hp):
    R = nseq * T
    rr, cc = _iota2((R, R), 0), _iota2((R, R), 1)
    same = (rr // T) == (cc // T)
    seg_incl = (same & (rr >= cc)).astype(F32)
    seg_all = same.astype(F32)
    nw = nw_ref[...]
    H = range(hp)
    cols = [slice(j * LANES, (j + 1) * LANES) for j in H]
    q = [q_ref[:, cols[j]] for j in H]
    v = [i_ref[:, cols[j]] for j in H]
    gates = [_hgrn_gates(f_ref[:, cols[j]], lb_ref[:, cols[j]]) for j in H]
    lf = [gt[0] for gt in gates]
    kin = [gt[1] for gt in gates]
    b = [_dot_exact_lhs(seg_incl, lf[j]) for j in H]
    b_tot = [_dot_exact_lhs(seg_all, lf[j]) for j in H]
    A = [_hgrn_diag(q[j], kin[j], b[j], T) for j in H]
    o = [_dot(A[j], v[j]) for j in H]
    qg = [q[j] * jnp.exp(b[j]) for j in H]
    kh = [kin[j] * jnp.exp(b_tot[j] - b[j]) for j in H]
    ecol = [jnp.transpose(jnp.exp(b_tot[j]).reshape(nseq, T, LANES)[:, 0, :]) for j in H]
    inter = [[] for _ in H]
    for n in range(nseq):
        sl = slice(n * T, (n + 1) * T)
        for j in H:
            S = s0_ref[n, j]
            inter[j].append(_dot(qg[j][sl], S))
            s_ref[n, j] = S * ecol[j][:, n:n + 1] + _dot(kh[j][sl], v[j][sl], _TN)
    for j in H:
        oj = o[j] + jnp.concatenate(inter[j], axis=0)
        o_ref[:, cols[j]] = _hgrn_out(oj, g_ref[:, cols[j]], nw)


def _hgrn_sample(P_h, lb_logits, nw, s0, T, nseq=16, hp=4):
    R, W = P_h.shape
    H = W // (4 * HGRN_DK)
    HG = H // hp
    wd = hp * HGRN_DK
    B = R // T
    rows = nseq * T
    col = lambda off: pl.BlockSpec((rows, wd), lambda h, g: (g, off * HG + h))
    sspec = pl.BlockSpec((nseq, hp, HGRN_DK, HGRN_DK), lambda h, g: (g, h, 0, 0))
    o, s = pl.pallas_call(
        functools.partial(_hgrn_sample_kernel, nseq=nseq, T=T, hp=hp),
        grid=(HG, B // nseq),
        in_specs=[col(0), col(1), col(2), col(3),
                  pl.BlockSpec((lb_logits.shape[0], wd), lambda h, g: (0, h)),
                  pl.BlockSpec((1, HGRN_DK), lambda h, g: (0, 0)),
                  sspec],
        out_specs=[pl.BlockSpec((rows, wd), lambda h, g: (g, h)), sspec],
        out_shape=[jax.ShapeDtypeStruct((R, H * HGRN_DK), F32),
                   jax.ShapeDtypeStruct(s0.shape, F32)],
        compiler_params=_cparams(("parallel", "parallel")),
        name="hgrn_sample",
    )(P_h, P_h, P_h, P_h, lb_logits, nw.reshape(1, HGRN_DK), s0)
    return o, s


def _rwkv_prep_body(rc, prev, mu_ref, w0_ref, a0_ref, kk_ref, ka_ref, wwa_ref, g2_ref, outs, rows):
    r_o, lw_o, k_o, v_o, kap_o, bet_o, g_o = outs
    D = RC_MAIN // 3
    xs = rc + mu_ref[...] * (prev - rc)
    r = xs[:, 0:D]
    kb = xs[:, D:2 * D]
    v = xs[:, 2 * D:3 * D]
    xwa = xs[:, RC_MAIN:RC_MAIN + LANES]
    xg = xs[:, RC_MAIN + LANES:]
    lane = _iota2((1, LANES), 1)
    wa = _dot(jnp.where(lane < W_LORA, jnp.tanh(xwa), xwa), wwa_ref[...])
    w = w0_ref[...] + wa[:, :D]
    a = _sigmoid(a0_ref[...] + wa[:, D:])
    lw = -math.exp(-0.5) * _sigmoid(w)
    g = _dot(_sigmoid(xg), g2_ref[...])
    kk = kb * kk_ref[...]
    seg = (_iota2((LANES, LANES), 0) // RWKV_N == _iota2((LANES, LANES), 1) // RWKV_N).astype(F32)
    k_a = ka_ref[...]
    for p in range(D // LANES):
        sl = slice(p * LANES, (p + 1) * LANES)
        kkp = kk[:, sl]
        kap = kkp * lax.rsqrt(jnp.maximum(_dot_exact_rhs(kkp * kkp, seg, parts=2), 1e-24))
        ap = a[:, sl]
        kap_o[rows, sl] = kap
        bet_o[rows, sl] = ap * kap
        k_o[rows, sl] = kb[:, sl] * (1.0 + (ap - 1.0) * k_a[:, sl])
    r_o[rows, :] = r
    lw_o[rows, :] = lw
    v_o[rows, :] = v
    g_o[rows, :] = g


def _in_rwkv_kernel(x_ref, nw_ref, sh_ref, sc_ref, wr_ref, shift_ref, mu_ref, w0_ref, a0_ref, kk_ref,
                    ka_ref, wwa_ref, g2_ref, h_ref, r_o, lw_o, k_o, v_o, kap_o, bet_o, g_o, last_o,
                    carry_ref, rc_ref, *, seq_rows, sub):
    i = pl.program_id(0)
    tm = x_ref.shape[0]
    per_seq = seq_rows >= tm

    @pl.when(i == 0)
    def _():
        carry_ref[...] = jnp.zeros_like(carry_ref)

    rep = 1 if per_seq else seq_rows
    h = (_rms(x_ref[...], nw_ref[...]) * (1.0 + _rep_rows(sc_ref[...], rep))
         + _rep_rows(sh_ref[...], rep)).astype(BF16)
    h_ref[...] = h
    outs = (r_o, lw_o, k_o, v_o, kap_o, bet_o, g_o)
    rid = _iota2((sub, 1), 0)
    rc_ref[...] = jnp.dot(h, wr_ref[...], preferred_element_type=F32)
    prev_last = None
    for s in range(tm // sub):
        rows = slice(s * sub, (s + 1) * sub)
        rc = rc_ref[rows, :]
        rolled = pltpu.roll(rc, 1, axis=0)
        if per_seq:
            if s == 0:
                start = (i % (seq_rows // tm)) == 0
                first = jnp.where(start, shift_ref[...], carry_ref[SUBLANES - 1:SUBLANES, :])
            else:
                first = prev_last
            prev = jnp.where(rid == 0, first, rolled)
        else:
            seqs = slice(s * sub // seq_rows, (s + 1) * sub // seq_rows)
            prev = jnp.where(rid % seq_rows == 0, _rep_rows(shift_ref[seqs, :], seq_rows), rolled)
            last_o[seqs, :] = rc.reshape(sub // seq_rows, seq_rows, RC_PAD)[:, seq_rows - 1, :]
        prev_last = rc[sub - 1:sub, :]
        if s == tm // sub - 1 and per_seq:
            carry_ref[...] = rc[sub - SUBLANES:, :]
            last_o[...] = prev_last
        _rwkv_prep_body(rc, prev, mu_ref, w0_ref, a0_ref, kk_ref, ka_ref, wwa_ref, g2_ref, outs, rows)


def _in_rwkv(x, nw, mod, w_r, shift, params, per_seq, seq_rows, tm=256, sub=128):
    R, D = x.shape
    tm = min(tm, seq_rows) if per_seq else min(tm, R)
    sub = min(sub, tm)
    mu, w0, a0, k_k, k_a, wwa, g2p = params
    const = lambda a: pl.BlockSpec(a.shape, lambda i: (0,) * a.ndim, pipeline_mode=pl.Buffered(1))
    wspecs = [const(a) for a in (mu, w0, a0, k_k, k_a, wwa, g2p)]
    row = lambda w: pl.BlockSpec((tm, w), lambda i: (i, 0))
    if per_seq:
        tiles = seq_rows // tm
        shift_spec = pl.BlockSpec((None, 1, RC_PAD), lambda i: (i // tiles, 0, 0))
        last_spec = pl.BlockSpec((None, 1, RC_PAD), lambda i: (i // tiles, 0, 0))
        last_shape = jax.ShapeDtypeStruct((R // seq_rows, 1, RC_PAD), F32)
    else:
        shift_spec = pl.BlockSpec((tm // seq_rows, RC_PAD), lambda i: (i, 0))
        last_spec = pl.BlockSpec((tm // seq_rows, RC_PAD), lambda i: (i, 0))
        last_shape = jax.ShapeDtypeStruct((R // seq_rows, RC_PAD), F32)
    outs = pl.pallas_call(
        functools.partial(_in_rwkv_kernel, seq_rows=seq_rows, sub=sub),
        grid=(R // tm,),
        in_specs=[row(D), const(nw), mod.spec(tm, 0), mod.spec(tm, 1), const(w_r), shift_spec] + wspecs,
        out_specs=[row(D)] + [row(D)] * 7 + [last_spec],
        out_shape=[jax.ShapeDtypeStruct((R, D), BF16)] + [jax.ShapeDtypeStruct((R, D), F32)] * 7
                  + [last_shape],
        scratch_shapes=[pltpu.VMEM((SUBLANES, RC_PAD), F32), pltpu.VMEM((tm, RC_PAD), F32)],
        compiler_params=_cparams(("arbitrary",)),
        name="in_rwkv",
    )(x, nw, mod.arr, mod.arr, w_r, shift, mu, w0, a0, k_k, k_a, wwa, g2p)
    return outs[0], outs[1:8], outs[8]


def _pair_masks():
    lane = _iota2((1, LANES), 1)
    return (lane < RWKV_N).astype(F32), (lane >= RWKV_N).astype(F32)


def _stack(x, mA, mB):
    return jnp.concatenate([x * mA, x * mB], axis=0)


def _rwkv_scores(cs, lw, cs_end, r, k, kap, bet, mA, mB):
    e_neg = jnp.exp(-cs)
    e_rem = jnp.exp(cs_end - cs)
    rg = _stack(r * jnp.exp(cs), mA, mB)
    kg = _stack(kap * jnp.exp(cs - lw), mA, mB)
    kd = _stack(k * e_neg, mA, mB)
    bd = _stack(bet * e_neg, mA, mB)
    kh = _stack(k * e_rem, mA, mB)
    bh = _stack(bet * e_rem, mA, mB)
    return rg, kg, kd, bd, kh, bh


def _rwkv_epilogue(o, r, k, v, g, lnw, lnb, rk):
    avg = jnp.where(_iota2((LANES, LANES), 0) // RWKV_N == _iota2((LANES, LANES), 1) // RWKV_N,
                    1.0 / RWKV_N, 0.0).astype(F32)
    mean = _dot_exact_rhs(o, avg)
    d = o - mean
    var = _dot_exact_rhs(d * d, avg)
    on = d * lax.rsqrt(var + LNX_EPS) * lnw + lnb
    bonus = _dot_exact_rhs(r * k * rk, avg) * float(RWKV_N) * v
    return (on + bonus) * g


def _rwkv_chunks(r_ref, lw_ref, k_ref, v_ref, kap_ref, bet_ref, g_ref, lnw_ref, lnb_ref, rk_ref,
                 o_ref, st_ref, offs, hp):
    C, c = CHUNK, RWKV_SUB
    C2 = 2 * C
    mA, mB = _pair_masks()
    rr, cc = _iota2((C2, C2), 0), _iota2((C2, C2), 1)
    eye = (rr == cc).astype(F32)
    strict = rr > cc
    incl = rr >= cc
    blk = (rr // c) == (cc // c)
    tri = (_iota2((C, C), 0) >= _iota2((C, C), 1)).astype(F32)
    avg = jnp.where(_iota2((LANES, LANES), 0) // RWKV_N == _iota2((LANES, LANES), 1) // RWKV_N,
                    1.0 / RWKV_N, 0.0).astype(F32)
    chains = [(r0, p) for r0 in offs for p in range(hp)]
    J = range(len(chains))
    pcols = [slice(p * LANES, (p + 1) * LANES) for _, p in chains]

    ld = lambda ref: [ref[pl.ds(r0, C), pcols[j]] for j, (r0, _) in zip(J, chains)]
    r, lw, k, v, kap, bet = ld(r_ref), ld(lw_ref), ld(k_ref), ld(v_ref), ld(kap_ref), ld(bet_ref)
    yield
    cs = [_dot_exact_lhs(tri, lw[j]) for j in J]
    cs_end = [cs[j][C - 1:C, :] for j in J]
    yield
    sc = [_rwkv_scores(cs[j], lw[j], cs_end[j], r[j], k[j], kap[j], bet[j], mA, mB) for j in J]
    rg, kg, kd, bd, kh, bh = ([s[i] for s in sc] for i in range(6))
    vs = [_stack(v[j], mA, mB) for j in J]
    kbd = [jnp.concatenate([kd[j], bd[j]], axis=0) for j in J]
    yield
    sck = [_dot(kg[j], kbd[j], _NT) for j in J]
    yield
    scr = [_dot(rg[j], kbd[j], _NT) for j in J]
    Ak = [jnp.where(strict, sck[j][:, :C2], 0.0) for j in J]
    Ab = [jnp.where(strict, sck[j][:, C2:], 0.0) for j in J]
    yield
    Bkb = [jnp.concatenate([jnp.where(incl, scr[j][:, :C2], 0.0),
                            jnp.where(incl, -scr[j][:, C2:], 0.0)], axis=1) for j in J]
    LD = [jnp.where(blk, Ab[j], 0.0) for j in J]
    X = LD
    Dinv = [eye - LD[j] for j in J]
    for _ in range(int(math.log2(c)) - 1):
        X = [_dot(X[j], X[j]) for j in J]
        yield
        Dinv = [Dinv[j] + _dot(Dinv[j], X[j]) for j in J]
        yield
    Q = [_dot(Dinv[j], Ab[j] - LD[j]) for j in J]
    yield
    T0 = [Dinv[j] - _dot(Q[j], Dinv[j]) for j in J]
    yield
    for _ in range(int(math.log2(C // c)) - 1):
        Q = [_dot(Q[j], Q[j]) for j in J]
        yield
        T0 = [T0[j] + _dot(Q[j], T0[j]) for j in J]
        yield
    E = [(eye - T0[j]) - _dot_hp(Ab[j], T0[j]) for j in J]
    yield
    Tm = [T0[j] + _dot(T0[j], E[j]) for j in J]
    yield
    khb = [jnp.concatenate([kh[j], -bh[j]], axis=0) for j in J]
    AkV = [_dot(Ak[j], vs[j]) for j in J]
    dec = [jnp.exp(cs_end[j]) for j in J]
    yield
    bonus = [_dot_exact_rhs(r[j] * k[j] * rk_ref[:, pcols[j]], avg, parts=2) * float(RWKV_N) * v[j]
             for j in J]
    yield

    for u in range(len(offs)):
        K = range(u * hp, (u + 1) * hp)
        S = {j: st_ref[chains[j][1]] for j in K}
        rhs = {j: _dot(kg[j], S[j], _NT) + AkV[j] for j in K}
        yield
        U = {j: _dot_hp(Tm[j], rhs[j]) for j in K}
        yield
        vu = {j: jnp.concatenate([vs[j], U[j]], axis=0) for j in K}
        for j in K:
            st_ref[chains[j][1]] = S[j] * dec[j] + _dot(vu[j], khb[j], _TN)
        yield
        O = {j: _dot(rg[j], S[j], _NT) + _dot(Bkb[j], vu[j]) for j in K}
        o = {j: O[j][:C, :] + O[j][C:, :] for j in K}
        yield
        mean = {j: _dot_exact_rhs(o[j], avg, parts=2) for j in K}
        d = {j: o[j] - mean[j] for j in K}
        yield
        var = {j: _dot_exact_rhs(d[j] * d[j], avg, parts=2) for j in K}
        yield
        for j in K:
            r0 = chains[j][0]
            on = d[j] * lax.rsqrt(var[j] + LNX_EPS) * lnw_ref[:, pcols[j]] + lnb_ref[:, pcols[j]]
            o_ref[pl.ds(r0, C), pcols[j]] = (on + bonus[j]) * g_ref[pl.ds(r0, C), pcols[j]]
        yield


def _interleave(gens):
    live = list(gens)
    while live:
        for gen in list(live):
            try:
                next(gen)
            except StopIteration:
                live.remove(gen)


def _scan_prompt_kernel(q_ref, f_ref, i_ref, og_ref, lb_ref, nw_ref,
                        r_ref, lw_ref, k_ref, v_ref, kap_ref, bet_ref, g_ref, lnw_ref, lnb_ref, rk_ref,
                        oa_ref, hs_ref, ob_ref, rs_ref, hst_ref, rst_ref, *, nchunks, hh, hp):
    tt = pl.program_id(1)

    @pl.when(tt == 0)
    def _():
        hst_ref[...] = jnp.zeros_like(hst_ref)
        rst_ref[...] = jnp.zeros_like(rst_ref)

    C = CHUNK
    unroll = 2 if nchunks % 2 == 0 else 1

    def body(ci, carry):
        base = ci * (unroll * C)
        offs = [pl.multiple_of(base + u * C, C) for u in range(unroll)]
        _interleave([
            _rwkv_chunks(r_ref, lw_ref, k_ref, v_ref, kap_ref, bet_ref, g_ref, lnw_ref, lnb_ref, rk_ref,
                         ob_ref, rst_ref, offs, hp),
            _hgrn_chunks(q_ref, f_ref, i_ref, og_ref, lb_ref, nw_ref, oa_ref, hst_ref, offs, hh),
        ])
        return carry

    lax.fori_loop(0, nchunks // unroll, body, 0)

    @pl.when(tt == pl.num_programs(1) - 1)
    def _():
        for j in range(hh):
            hs_ref[j] = hst_ref[j].T
        for p in range(hp):
            S = rst_ref[p]
            rs_ref[2 * p] = S[:RWKV_N, :RWKV_N]
            rs_ref[2 * p + 1] = S[RWKV_N:, RWKV_N:]


def _scan_prompt(P_h, lb_logits, nw, arrs, g, lnw, lnb, rk, B, T, tt_rows=256):
    R, D = arrs[0].shape
    hh = P_h.shape[1] // (4 * HGRN_DK)
    hp = D // LANES
    wh = hh * HGRN_DK
    tt_rows = min(tt_rows, T)
    nt = T // tt_rows
    hcol = lambda off: pl.BlockSpec((tt_rows, wh), lambda b, t: (b * nt + t, off))
    col = pl.BlockSpec((tt_rows, D), lambda b, t: (b * nt + t, 0))
    const = lambda a: pl.BlockSpec(a.shape, lambda b, t: (0,) * a.ndim)
    nw = nw.reshape(1, HGRN_DK)
    lnw, lnb, rk = lnw.reshape(1, D), lnb.reshape(1, D), rk.reshape(1, D)
    oa, hs, ob, rs = pl.pallas_call(
        functools.partial(_scan_prompt_kernel, nchunks=tt_rows // CHUNK, hh=hh, hp=hp),
        grid=(B, nt),
        in_specs=[hcol(0), hcol(1), hcol(2), hcol(3), const(lb_logits), const(nw)]
                 + [col] * 7 + [const(lnw), const(lnb), const(rk)],
        out_specs=[pl.BlockSpec((tt_rows, wh), lambda b, t: (b * nt + t, 0)),
                   pl.BlockSpec((None, hh, HGRN_DK, HGRN_DK), lambda b, t: (b, 0, 0, 0)),
                   col,
                   pl.BlockSpec((None, 2 * hp, RWKV_N, RWKV_N), lambda b, t: (b, 0, 0, 0))],
        out_shape=[jax.ShapeDtypeStruct((R, wh), F32),
                   jax.ShapeDtypeStruct((B, hh, HGRN_DK, HGRN_DK), F32),
                   jax.ShapeDtypeStruct((R, D), F32),
                   jax.ShapeDtypeStruct((B, 2 * hp, RWKV_N, RWKV_N), F32)],
        scratch_shapes=[pltpu.VMEM((hh, HGRN_DK, HGRN_DK), F32), pltpu.VMEM((hp, LANES, LANES), F32)],
        compiler_params=_cparams(("parallel", "arbitrary")),
        name="scan_prompt",
    )(P_h, P_h, P_h, P_h, lb_logits, nw, *arrs, g, lnw, lnb, rk)
    return oa, hs, ob, rs


def _rwkv_sample_kernel(r_ref, lw_ref, k_ref, v_ref, kap_ref, bet_ref, g_ref, lnw_ref, lnb_ref, rk_ref,
                        s0_ref, o_ref, s_ref, sbd_ref, *, nseq, T, hp):
    C = nseq * T
    C2 = 2 * C
    N = RWKV_N
    mA, mB = _pair_masks()
    bdm = (_iota2((LANES, LANES), 0) // N == _iota2((LANES, LANES), 1) // N).astype(F32)
    rr, cc = _iota2((C2, C2), 0), _iota2((C2, C2), 1)
    eye = (rr == cc).astype(F32)
    same = (rr // T) == (cc // T)
    strict = same & (rr > cc)
    incl = same & (rr >= cc)
    r1, c1 = _iota2((C, C), 0), _iota2((C, C), 1)
    seg_incl = ((r1 // T == c1 // T) & (r1 >= c1)).astype(F32)
    seg_all = (r1 // T == c1 // T).astype(F32)
    zero = jnp.zeros((N, N), F32)

    J = range(hp)
    cols = [slice(j * LANES, (j + 1) * LANES) for j in J]
    ld = lambda ref: [ref[:, cols[j]] for j in J]
    r, lw, k, v, kap, bet = ld(r_ref), ld(lw_ref), ld(k_ref), ld(v_ref), ld(kap_ref), ld(bet_ref)
    cs = [_dot_exact_lhs(seg_incl, lw[j]) for j in J]
    cs_end = [_dot_exact_lhs(seg_all, lw[j]) for j in J]
    sc = [_rwkv_scores(cs[j], lw[j], cs_end[j], r[j], k[j], kap[j], bet[j], mA, mB) for j in J]
    rg, kg, kd, bd, kh, bh = ([s[i] for s in sc] for i in range(6))
    vs = [_stack(v[j], mA, mB) for j in J]
    kbd = [jnp.concatenate([kd[j], bd[j]], axis=0) for j in J]
    sck = [_dot(kg[j], kbd[j], _NT) for j in J]
    scr = [_dot(rg[j], kbd[j], _NT) for j in J]
    Ak = [jnp.where(strict, sck[j][:, :C2], 0.0) for j in J]
    Ab = [jnp.where(strict, sck[j][:, C2:], 0.0) for j in J]
    Bkb = [jnp.concatenate([jnp.where(incl, scr[j][:, :C2], 0.0),
                            jnp.where(incl, -scr[j][:, C2:], 0.0)], axis=1) for j in J]
    Tm = [_neumann_inv(Ab[j], eye, int(math.log2(T)) - 1) for j in J]
    kgp = [kg[j][:C, :] + kg[j][C:, :] for j in J]
    rgp = [rg[j][:C, :] + rg[j][C:, :] for j in J]
    ks = [[] for _ in J]
    rs = [[] for _ in J]
    for n in range(nseq):
        sl = slice(n * T, (n + 1) * T)
        for j in J:
            Sbd = jnp.concatenate(
                [jnp.concatenate([s0_ref[n, 2 * j], zero], axis=1),
                 jnp.concatenate([zero, s0_ref[n, 2 * j + 1]], axis=1)], axis=0)
            sbd_ref[n, j] = Sbd
            res = _dot(jnp.concatenate([kgp[j][sl], rgp[j][sl]], axis=0), Sbd, _NT)
            ks[j].append(res[:T])
            rs[j].append(res[T:])
    KS = [_stack(jnp.concatenate(ks[j], axis=0), mA, mB) for j in J]
    RS = [_stack(jnp.concatenate(rs[j], axis=0), mA, mB) for j in J]
    rhs = [KS[j] + _dot(Ak[j], vs[j]) for j in J]
    U = [_dot_hp(Tm[j], rhs[j]) for j in J]
    vu = [jnp.concatenate([vs[j], U[j]], axis=0) for j in J]
    O = [RS[j] + _dot(Bkb[j], vu[j]) for j in J]
    Up = [U[j][:C, :] + U[j][C:, :] for j in J]
    khp = [kh[j][:C, :] + kh[j][C:, :] for j in J]
    nbhp = [-(bh[j][:C, :] + bh[j][C:, :]) for j in J]
    dec = [jnp.exp(cs_end[j]) for j in J]
    for n in range(nseq):
        sl = slice(n * T, (n + 1) * T)
        for j in J:
            upd = _dot(jnp.concatenate([v[j][sl], Up[j][sl]], axis=0),
                       jnp.concatenate([khp[j][sl], nbhp[j][sl]], axis=0), _TN)
            Sn = sbd_ref[n, j] * dec[j][n * T:n * T + 1, :] + upd * bdm
            s_ref[n, 2 * j] = Sn[:N, :N]
            s_ref[n, 2 * j + 1] = Sn[N:, N:]
    for j in J:
        o = O[j][:C, :] + O[j][C:, :]
        o_ref[:, cols[j]] = _rwkv_epilogue(o, r[j], k[j], v[j], g_ref[:, cols[j]],
                                           lnw_ref[:, cols[j]], lnb_ref[:, cols[j]], rk_ref[:, cols[j]])


def _rwkv_sample(arrs, g, lnw, lnb, rk, s0, T, nseq=16, hp=4):
    R, D = arrs[0].shape
    NP = D // LANES
    B = R // T
    rows = nseq * T
    wd = hp * LANES
    col = pl.BlockSpec((rows, wd), lambda p, gi: (gi, p))
    par = pl.BlockSpec((1, wd), lambda p, gi: (0, p))
    sspec = pl.BlockSpec((nseq, 2 * hp, RWKV_N, RWKV_N), lambda p, gi: (gi, p, 0, 0))
    o, s = pl.pallas_call(
        functools.partial(_rwkv_sample_kernel, nseq=nseq, T=T, hp=hp),
        grid=(NP // hp, B // nseq),
        in_specs=[col] * 7 + [par] * 3 + [sspec],
        out_specs=[col, sspec],
        out_shape=[jax.ShapeDtypeStruct((R, D), F32), jax.ShapeDtypeStruct(s0.shape, F32)],
        scratch_shapes=[pltpu.VMEM((nseq, hp, LANES, LANES), F32)],
        compiler_params=_cparams(("parallel", "parallel")),
        name="rwkv_sample",
    )(*arrs, g, lnw.reshape(1, D), lnb.reshape(1, D), rk.reshape(1, D), s0)
    return o, s


def _merge_kernel(x_ref, h_ref, oa_ref, ob_ref, wg_ref, w_ref, gt_ref, o_ref, *, rep):
    D = x_ref.shape[1]
    pg = lax.dot_general(h_ref[...], wg_ref[...], _NT, preferred_element_type=F32)
    m = _sigmoid(pg[:, :D]) * oa_ref[...] + _sigmoid(pg[:, D:]) * ob_ref[...]
    o_ref[...] = x_ref[...] + _rep_rows(gt_ref[...], rep) * _dot(m, w_ref[...])


def _merge_out(x, h, o_a, o_b, w_g, w_out, mod, tm):
    R, D = x.shape
    row = pl.BlockSpec((tm, D), lambda i: (i, 0))
    const = lambda a: pl.BlockSpec(a.shape, lambda i: (0,) * a.ndim, pipeline_mode=pl.Buffered(1))
    return pl.pallas_call(
        functools.partial(_merge_kernel, rep=mod.rep),
        grid=(R // tm,),
        in_specs=[row, row, row, row, const(w_g), const(w_out), mod.spec(tm, 2)],
        out_specs=row,
        out_shape=jax.ShapeDtypeStruct((R, D), F32),
        compiler_params=_cparams(("parallel",)),
        name="merge_out",
    )(x, h, o_a, o_b, w_g, w_out, mod.arr)


def _mlp_kernel(x_ref, nw_ref, sh_ref, sc_ref, gt_ref, wu_ref, wd_ref, fw_ref, o_ref, *, tf, rep):
    x = x_ref[...]
    h = (_rms(x, nw_ref[...]) * (1.0 + _rep_rows(sc_ref[...], rep)) + _rep_rows(sh_ref[...], rep)).astype(BF16)
    acc = None
    for c in range(wu_ref.shape[1] // tf):
        u = jnp.dot(h, wu_ref[:, c * tf:(c + 1) * tf], preferred_element_type=F32)
        u = jnp.square(jnp.maximum(u, 0.0))
        part = _dot(u, wd_ref[c * tf:(c + 1) * tf, :])
        acc = part if acc is None else acc + part
    o_ref[...] = _rms(x + _rep_rows(gt_ref[...], rep) * acc, fw_ref[...])


def _mlp(x, nw, w_up, w_down, fw, mod, tm, tf=1024):
    R, D = x.shape
    const = lambda a: pl.BlockSpec(a.shape, lambda i: (0,) * a.ndim, pipeline_mode=pl.Buffered(1))
    row = pl.BlockSpec((tm, D), lambda i: (i, 0))
    nw, fw = nw.reshape(1, D), fw.reshape(1, D)
    return pl.pallas_call(
        functools.partial(_mlp_kernel, tf=tf, rep=mod.rep),
        grid=(R // tm,),
        in_specs=[row, const(nw), mod.spec(tm, 3), mod.spec(tm, 4), mod.spec(tm, 5),
                  const(w_up), const(w_down), const(fw)],
        out_specs=row,
        out_shape=jax.ShapeDtypeStruct((R, D), F32),
        compiler_params=_cparams(("parallel",)),
        name="mlp",
    )(x, nw, mod.arr, mod.arr, mod.arr, w_up, w_down, fw)


def _run(x3, mod_rows, s_hgrn, s_rwkv, s_shift, W, prompt):
    B, T, D = x3.shape
    R = B * T
    x = x3.reshape(R, D)
    if prompt:
        mod = _Mod(mod_rows.reshape(B, 1, -1), True, T, D)
        tm = min(1024, T)
    else:
        mod = _Mod(mod_rows, False, T, D)
        tm = min(512, R)

    rw = (W["mu"], W["w0"], W["a0"], W["k_k"], W["k_a"], W["wwa"], W["g2p"])
    nw1 = W["norm1_w"].reshape(1, D)
    if prompt:
        shift_in = jnp.zeros((B, 1, RC_PAD), F32)
        h, arrs, last = _in_rwkv(x, nw1, mod, W["w_r"], shift_in, rw, True, T)
        shift_out = last[:, 0, :RC_COLS]
        P_h = _matmul(h, W["w_h"], tm, 2048, "proj_h")
        o_a, hg, o_b, rs = _scan_prompt(P_h, W["lb_logits"], W["hgrn_norm_w"], arrs[:6], arrs[6],
                                        W["lnx_w"], W["lnx_b"], W["r_k"], B, T)
    else:
        shift_in = jnp.pad(s_shift, ((0, 0), (0, RC_PAD - RC_COLS)))
        h, arrs, last = _in_rwkv(x, nw1, mod, W["w_r"], shift_in, rw, False, T)
        shift_out = last[:, :RC_COLS]
        P_h = _matmul(h, W["w_h"], tm, 2048, "proj_h")
        o_a, hg = _hgrn_sample(P_h, W["lb_logits"], W["hgrn_norm_w"], s_hgrn, T)
        o_b, rs = _rwkv_sample(arrs[:6], arrs[6], W["lnx_w"], W["lnx_b"], W["r_k"], s_rwkv, T)

    x1 = _merge_out(x, h, o_a, o_b, W["w_g"], W["w_out"], mod, min(tm, 512))
    y = _mlp(x1, W["norm2_w"], W["w_up"], W["w_down"], W["final_norm_w"], mod, min(tm, 512))
    return y.reshape(B, T, D), hg, rs, shift_out


def kernel(x_prompt, x_sample, state_hgrn, state_rwkv, state_shift, c_prompt, c_sample, norm1_w, norm2_w, ada_w, ada_b, w_in, lb_logits, hgrn_norm_w, rwkv_mu, rwkv_w0, rwkv_w2, rwkv_a0, rwkv_a2, rwkv_g2, rwkv_k_k, rwkv_k_a, rwkv_r_k, rwkv_lnx_w, rwkv_lnx_b, w_out, w_up, w_down, final_norm_w):
    depth = norm1_w.shape[0]
    assert depth == 1, "single-layer step"
    D = x_prompt.shape[-1]
    HW = 4 * D
    w_in_t = w_in[0].T
    row = lambda a: a.reshape(1, -1)
    wwa = jnp.zeros((LANES, 2 * D), F32)
    wwa = wwa.at[:W_LORA, :D].set(rwkv_w2[0]).at[W_LORA:W_LORA + A_LORA, D:].set(rwkv_a2[0])
    W = {
        "norm1_w": norm1_w[0], "norm2_w": norm2_w[0], "final_norm_w": final_norm_w,
        "w_h": w_in_t[:HW].astype(BF16),
        "w_r": jnp.pad(w_in[0][:, HW:HW + RC_COLS], ((0, 0), (0, RC_PAD - RC_COLS))).astype(BF16),
        "w_g": w_in_t[HW + RC_COLS:].astype(BF16),
        "lb_logits": lb_logits, "hgrn_norm_w": hgrn_norm_w[0],
        "mu": jnp.pad(row(rwkv_mu[0]), ((0, 0), (0, RC_PAD - RC_COLS))),
        "w0": row(rwkv_w0[0]), "a0": row(rwkv_a0[0]), "k_k": row(rwkv_k_k[0]), "k_a": row(rwkv_k_a[0]),
        "wwa": wwa.astype(BF16),
        "g2p": jnp.pad(rwkv_g2[0], ((0, G_LORA_PAD - G_LORA), (0, 0))).astype(BF16),
        "lnx_w": rwkv_lnx_w[0], "lnx_b": rwkv_lnx_b[0], "r_k": rwkv_r_k[0].reshape(-1),
        "w_out": w_out[0].astype(BF16), "w_up": w_up[0].astype(BF16), "w_down": w_down[0].astype(BF16),
    }
    nb = x_prompt.shape[0]
    mod = _adaln(jnp.concatenate([c_prompt, c_sample], axis=0), ada_w[0], ada_b[0])
    y_p, hg_p, rw_p, sh_p = _run(x_prompt, mod[:nb], None, None, None, W, True)
    y_s, hg_s, rw_s, sh_s = _run(x_sample, mod[nb:], state_hgrn[0], state_rwkv[0], state_shift[0], W, False)
    ex = lambda a: a[None]
    return (y_p, y_s, ex(hg_p), ex(rw_p), ex(sh_p), ex(hg_s), ex(rw_s), ex(sh_s))
```

```python
import functools
import math

import jax
import jax.numpy as jnp
from jax import lax
from jax.experimental import pallas as pl
from jax.experimental.pallas import tpu as pltpu

F32 = jnp.float32
BF16 = jnp.bfloat16

LANES = 128
SUBLANES = 8
VMEM_LIMIT = 48 * 1024 * 1024

HGRN_DK = 128
RWKV_N = 64
W_LORA = 64
A_LORA = 64
G_LORA = 160
G_LORA_PAD = 256
RC_MAIN = 3 * 1024
RC_COLS = RC_MAIN + W_LORA + A_LORA + G_LORA
RC_PAD = RC_MAIN + LANES + G_LORA_PAD
RMS_EPS = 1e-6
LNX_EPS = 64e-5
CHUNK = 64
SUB = 8
RWKV_SUB = 8


def _cparams(sem, vmem=VMEM_LIMIT):
    return pltpu.CompilerParams(dimension_semantics=sem, vmem_limit_bytes=vmem)


_NN = (((1,), (0,)), ((), ()))
_NT = (((1,), (1,)), ((), ()))
_TN = (((0,), (0,)), ((), ()))


def _dot(a, b, dims=_NN):
    return lax.dot_general(a.astype(BF16), b.astype(BF16), dims, preferred_element_type=F32)


def _split2(x):
    hi = x.astype(BF16)
    lo = (x - hi.astype(F32)).astype(BF16)
    return hi, lo


def _split3(x):
    hi = x.astype(BF16)
    r1 = x - hi.astype(F32)
    mid = r1.astype(BF16)
    lo = (r1 - mid.astype(F32)).astype(BF16)
    return hi, mid, lo


def _dot_exact_lhs(a, x, dims=_NN):
    a = a.astype(BF16)
    hi, mid, lo = _split3(x)
    d = lambda p: lax.dot_general(a, p, dims, preferred_element_type=F32)
    return d(hi) + d(mid) + d(lo)


def _dot_exact_rhs(x, b, dims=_NN, parts=3):
    b = b.astype(BF16)
    d = lambda p: lax.dot_general(p, b, dims, preferred_element_type=F32)
    if parts == 2:
        hi, lo = _split2(x)
        return d(hi) + d(lo)
    hi, mid, lo = _split3(x)
    return d(hi) + d(mid) + d(lo)


def _dot_hp(a, b, dims=_NN):
    ah, al = _split2(a)
    bh, bl = _split2(b)
    d = lambda p, q: lax.dot_general(p, q, dims, preferred_element_type=F32)
    return d(ah, bh) + d(ah, bl) + d(al, bh)


def _sigmoid(x):
    return 1.0 / (1.0 + jnp.exp(-x))


def _iota2(shape, dim):
    return lax.broadcasted_iota(jnp.int32, shape, dim)


def _neumann_inv(L, eye, nsq):
    X = L
    P = eye - L
    for _ in range(nsq):
        X = _dot_hp(X, X)
        P = P + _dot_hp(P, X)
    return P


def _adaln_kernel(c_ref, w_ref, b_ref, o_ref):
    c = c_ref[...]
    s = c * _sigmoid(c)
    o_ref[...] = _dot(s, w_ref[...]) + b_ref[...]


def _adaln(c_all, ada_w, ada_b, tn=1536):
    R, D = c_all.shape
    N = ada_w.shape[1]
    return pl.pallas_call(
        _adaln_kernel,
        grid=(N // tn,),
        in_specs=[pl.BlockSpec((R, D), lambda j: (0, 0)),
                  pl.BlockSpec((D, tn), lambda j: (0, j)),
                  pl.BlockSpec((1, tn), lambda j: (0, j))],
        out_specs=pl.BlockSpec((R, tn), lambda j: (0, j)),
        out_shape=jax.ShapeDtypeStruct((R, N), F32),
        compiler_params=_cparams(("parallel",)),
        name="adaln",
    )(c_all, ada_w, ada_b.reshape(1, N))


class _Mod:
    def __init__(self, arr, per_seq, seq_rows, D):
        self.arr, self.per_seq, self.seq_rows, self.D = arr, per_seq, seq_rows, D
        self.rep = 1 if per_seq else seq_rows

    def spec(self, tm, comp):
        D = self.D
        if self.per_seq:
            tiles = self.seq_rows // tm
            return pl.BlockSpec((None, 1, D), lambda i, *_: (i // tiles, 0, comp))
        return pl.BlockSpec((tm // self.seq_rows, D), lambda i, *_: (i, comp))


def _rep_rows(v, rep):
    if rep == 1:
        return v
    n, w = v.shape
    return jnp.broadcast_to(v[:, None, :], (n, rep, w)).reshape(n * rep, w)


def _rms(x, w):
    return x * lax.rsqrt(jnp.mean(x * x, axis=-1, keepdims=True) + RMS_EPS) * w


def _mm_kernel(x_ref, wt_ref, o_ref):
    o_ref[...] = lax.dot_general(x_ref[...], wt_ref[...], _NT, preferred_element_type=F32)


def _matmul(x, wt, n_out, tm, tn, name):
    R, K = x.shape
    N = n_out
    return pl.pallas_call(
        _mm_kernel,
        grid=(R // tm, N // tn),
        in_specs=[pl.BlockSpec((tm, K), lambda i, j: (i, 0)),
                  pl.BlockSpec((tn, K), lambda i, j: (j, 0))],
        out_specs=pl.BlockSpec((tm, tn), lambda i, j: (i, j)),
        out_shape=jax.ShapeDtypeStruct((R, N), F32),
        compiler_params=_cparams(("parallel", "arbitrary")),
        name=name,
    )(x, wt)


def _hgrn_gates(fp, lb_logits):
    m = jnp.max(lb_logits, axis=0, keepdims=True)
    e = jnp.exp(lb_logits - m)
    lb = e[0:1, :] / jnp.sum(e, axis=0, keepdims=True)
    f = lb + (1.0 - lb) * _sigmoid(fp)
    return jnp.log(f), 1.0 - f


def _hgrn_diag(q, kin, b, c):
    R = q.shape[0]
    nb = R // c
    q3 = q.reshape(nb, c, LANES)
    k3 = kin.reshape(nb, c, LANES)
    b3 = (b * math.log2(math.e)).reshape(nb, c, LANES)
    lane_mod = lax.broadcasted_iota(jnp.int32, (nb, c, R), 2) % c
    acc = jnp.zeros((nb, c, R), F32)
    for s in range(c):
        d = jnp.minimum(b3 - b3[:, s:s + 1, :], 0.0)
        col = jnp.sum(q3 * k3[:, s:s + 1, :] * jnp.exp2(d), axis=-1, keepdims=True)
        acc = jnp.where(lane_mod == s, col, acc)
    rr, cc = _iota2((R, R), 0), _iota2((R, R), 1)
    keep = ((rr // c) == (cc // c)) & (rr >= cc)
    return jnp.where(keep, acc.reshape(R, R), 0.0)


def _hgrn_cross(q, kin, b, m):
    R = q.shape[0]
    h = m // 2
    nb = R // m
    ref = jnp.broadcast_to(b.reshape(nb, m, LANES)[:, h - 1:h, :], (nb, m, LANES)).reshape(R, LANES)
    upper = (_iota2((R, 1), 0) % m) >= h
    qs = q * jnp.exp(jnp.where(upper, b - ref, -1e30))
    ks = kin * jnp.exp(jnp.where(upper, -1e30, ref - b))
    sc = _dot(qs, ks, _NT)
    if nb == 1:
        return sc
    same_blk = (_iota2((R, R), 0) // m) == (_iota2((R, R), 1) // m)
    return jnp.where(same_blk, sc, 0.0)


def _hgrn_out(o, og, nw):
    o = o * lax.rsqrt(jnp.mean(o * o, axis=-1, keepdims=True) + RMS_EPS) * nw
    return o * (og * _sigmoid(og))


def _hgrn_chunks(q_ref, f_ref, i_ref, g_ref, lb_ref, nw_ref, o_ref, st_ref, offs, hp):
    C, c = CHUNK, SUB
    nb = C // c
    tri = (_iota2((C, C), 0) >= _iota2((C, C), 1)).astype(F32)
    nw = nw_ref[...]
    H = range(hp)
    cols = [slice(j * LANES, (j + 1) * LANES) for j in H]
    lane_mod = lax.broadcasted_iota(jnp.int32, (nb, c, C), 2) % c
    rr, cc = _iota2((C, C), 0), _iota2((C, C), 1)
    keep = ((rr // c) == (cc // c)) & (rr >= cc)
    for r0 in offs:
        q = [q_ref[pl.ds(r0, C), cols[j]] for j in H]
        v = [i_ref[pl.ds(r0, C), cols[j]] for j in H]
        gates = [_hgrn_gates(f_ref[pl.ds(r0, C), cols[j]], lb_ref[:, cols[j]]) for j in H]
        lf = [gt[0] for gt in gates]
        kin = [gt[1] for gt in gates]
        yield
        b = [_dot_exact_lhs(tri, lf[j]) for j in H]
        b_last = [b[j][C - 1:C, :] for j in H]
        yield
        ST = [st_ref[j] for j in H]
        o = [_dot(q[j] * jnp.exp(b[j]), ST[j], _NT) for j in H]
        yield
        for j in H:
            st_ref[j] = ST[j] * jnp.exp(b_last[j]) + _dot(v[j], kin[j] * jnp.exp(b_last[j] - b[j]), _TN)
        yield
        q3 = [q[j].reshape(nb, c, LANES) for j in H]
        k3 = [kin[j].reshape(nb, c, LANES) for j in H]
        b3 = [(b[j] * math.log2(math.e)).reshape(nb, c, LANES) for j in H]
        acc = [jnp.zeros((nb, c, C), F32) for _ in H]
        for s in range(c):
            for j in H:
                d = jnp.minimum(b3[j] - b3[j][:, s:s + 1, :], 0.0)
                col = jnp.sum(q3[j] * k3[j][:, s:s + 1, :] * jnp.exp2(d), axis=-1, keepdims=True)
                acc[j] = jnp.where(lane_mod == s, col, acc[j])
            yield
        A = [jnp.where(keep, acc[j].reshape(C, C), 0.0) for j in H]
        m = 2 * c
        while m <= C:
            A = [A[j] + _hgrn_cross(q[j], kin[j], b[j], m) for j in H]
            m *= 2
            yield
        o = [o[j] + _dot(A[j], v[j]) for j in H]
        yield
        for j in H:
            o_ref[pl.ds(r0, C), cols[j]] = _hgrn_out(o[j], g_ref[pl.ds(r0, C), cols[j]], nw)
        yield


def _hgrn_sample_kernel(q_ref, f_ref, i_ref, g_ref, lb_ref, nw_ref, s0_ref, o_ref, s_ref, *, nseq, T, hp):
    R = nseq * T
    rr, cc = _iota2((R, R), 0), _iota2((R, R), 1)
    same = (rr // T) == (cc // T)
    seg_incl = (same & (rr >= cc)).astype(F32)
    seg_all = same.astype(F32)
    nw = nw_ref[...]
    H = range(hp)
    cols = [slice(j * LANES, (j + 1) * LANES) for j in H]
    q = [q_ref[:, cols[j]] for j in H]
    v = [i_ref[:, cols[j]] for j in H]
    gates = [_hgrn_gates(f_ref[:, cols[j]], lb_ref[:, cols[j]]) for j in H]
    lf = [gt[0] for gt in gates]
    kin = [gt[1] for gt in gates]
    b = [_dot_exact_lhs(seg_incl, lf[j]) for j in H]
    b_tot = [_dot_exact_lhs(seg_all, lf[j]) for j in H]
    A = [_hgrn_diag(q[j], kin[j], b[j], T) for j in H]
    o = [_dot(A[j], v[j]) for j in H]
    qg = [q[j] * jnp.exp(b[j]) for j in H]
    kh = [kin[j] * jnp.exp(b_tot[j] - b[j]) for j in H]
    ecol = [jnp.transpose(jnp.exp(b_tot[j]).reshape(nseq, T, LANES)[:, 0, :]) for j in H]
    inter = [[] for _ in H]
    for n in range(nseq):
        sl = slice(n * T, (n + 1) * T)
        for j in H:
            S = s0_ref[n, j]
            inter[j].append(_dot(qg[j][sl], S))
            s_ref[n, j] = S * ecol[j][:, n:n + 1] + _dot(kh[j][sl], v[j][sl], _TN)
    for j in H:
        oj = o[j] + jnp.concatenate(inter[j], axis=0)
        o_ref[:, cols[j]] = _hgrn_out(oj, g_ref[:, cols[j]], nw)


def _hgrn_sample(P_h, lb_logits, nw, s0, T, nseq=16, hp=4):
    R, W = P_h.shape
    H = W // (4 * HGRN_DK)
    HG = H // hp
    wd = hp * HGRN_DK
    B = R // T
    rows = nseq * T
    col = lambda off: pl.BlockSpec((rows, wd), lambda h, g: (g, off * HG + h))
    sspec = pl.BlockSpec((nseq, hp, HGRN_DK, HGRN_DK), lambda h, g: (g, h, 0, 0))
    o, s = pl.pallas_call(
        functools.partial(_hgrn_sample_kernel, nseq=nseq, T=T, hp=hp),
        grid=(HG, B // nseq),
        in_specs=[col(0), col(1), col(2), col(3),
                  pl.BlockSpec((lb_logits.shape[0], wd), lambda h, g: (0, h)),
                  pl.BlockSpec((1, HGRN_DK), lambda h, g: (0, 0)),
                  sspec],
        out_specs=[pl.BlockSpec((rows, wd), lambda h, g: (g, h)), sspec],
        out_shape=[jax.ShapeDtypeStruct((R, H * HGRN_DK), F32),
                   jax.ShapeDtypeStruct(s0.shape, F32)],
        compiler_params=_cparams(("parallel", "parallel")),
        name="hgrn_sample",
    )(P_h, P_h, P_h, P_h, lb_logits, nw.reshape(1, HGRN_DK), s0)
    return o, s


def _rwkv_prep_body(rc, prev, mu_ref, w0_ref, a0_ref, kk_ref, ka_ref, wwa_ref, g2_ref, outs, rows):
    r_o, lw_o, k_o, v_o, kap_o, bet_o, g_o = outs
    D = RC_MAIN // 3
    xs = rc + mu_ref[...] * (prev - rc)
    r = xs[:, 0:D]
    kb = xs[:, D:2 * D]
    v = xs[:, 2 * D:3 * D]
    xwa = xs[:, RC_MAIN:RC_MAIN + LANES]
    xg = xs[:, RC_MAIN + LANES:]
    lane = _iota2((1, LANES), 1)
    wa = _dot(jnp.where(lane < W_LORA, jnp.tanh(xwa), xwa), wwa_ref[...])
    w = w0_ref[...] + wa[:, :D]
    a = _sigmoid(a0_ref[...] + wa[:, D:])
    lw = -math.exp(-0.5) * _sigmoid(w)
    g = _dot(_sigmoid(xg), g2_ref[...])
    kk = kb * kk_ref[...]
    seg = (_iota2((LANES, LANES), 0) // RWKV_N == _iota2((LANES, LANES), 1) // RWKV_N).astype(F32)
    k_a = ka_ref[...]
    for p in range(D // LANES):
        sl = slice(p * LANES, (p + 1) * LANES)
        kkp = kk[:, sl]
        kap = kkp * lax.rsqrt(jnp.maximum(_dot_exact_rhs(kkp * kkp, seg, parts=2), 1e-24))
        ap = a[:, sl]
        kap_o[rows, sl] = kap
        bet_o[rows, sl] = ap * kap
        k_o[rows, sl] = kb[:, sl] * (1.0 + (ap - 1.0) * k_a[:, sl])
    r_o[rows, :] = r
    lw_o[rows, :] = lw
    v_o[rows, :] = v
    g_o[rows, :] = g


def _in_rwkv_kernel(x_ref, nw_ref, sh_ref, sc_ref, wr_ref, shift_ref, mu_ref, w0_ref, a0_ref, kk_ref,
                    ka_ref, wwa_ref, g2_ref, h_ref, r_o, lw_o, k_o, v_o, kap_o, bet_o, g_o, last_o,
                    carry_ref, rc_ref, *, seq_rows, sub):
    i = pl.program_id(0)
    tm = x_ref.shape[0]
    per_seq = seq_rows >= tm

    @pl.when(i == 0)
    def _():
        carry_ref[...] = jnp.zeros_like(carry_ref)

    rep = 1 if per_seq else seq_rows
    h = (_rms(x_ref[...], nw_ref[...]) * (1.0 + _rep_rows(sc_ref[...], rep))
         + _rep_rows(sh_ref[...], rep)).astype(BF16)
    h_ref[...] = h
    outs = (r_o, lw_o, k_o, v_o, kap_o, bet_o, g_o)
    rid = _iota2((sub, 1), 0)
    rc_ref[...] = jnp.dot(h, wr_ref[...], preferred_element_type=F32)
    prev_last = None
    for s in range(tm // sub):
        rows = slice(s * sub, (s + 1) * sub)
        rc = rc_ref[rows, :]
        rolled = pltpu.roll(rc, 1, axis=0)
        if per_seq:
            if s == 0:
                start = (i % (seq_rows // tm)) == 0
                first = jnp.where(start, shift_ref[...], carry_ref[SUBLANES - 1:SUBLANES, :])
            else:
                first = prev_last
            prev = jnp.where(rid == 0, first, rolled)
        else:
            seqs = slice(s * sub // seq_rows, (s + 1) * sub // seq_rows)
            prev = jnp.where(rid % seq_rows == 0, _rep_rows(shift_ref[seqs, :], seq_rows), rolled)
            last_o[seqs, :] = rc.reshape(sub // seq_rows, seq_rows, RC_PAD)[:, seq_rows - 1, :]
        prev_last = rc[sub - 1:sub, :]
        if s == tm // sub - 1 and per_seq:
            carry_ref[...] = rc[sub - SUBLANES:, :]
            last_o[...] = prev_last
        _rwkv_prep_body(rc, prev, mu_ref, w0_ref, a0_ref, kk_ref, ka_ref, wwa_ref, g2_ref, outs, rows)


def _in_rwkv(x, nw, mod, w_r, shift, params, per_seq, seq_rows, tm=256, sub=128):
    R, D = x.shape
    tm = min(tm, seq_rows) if per_seq else min(tm, R)
    sub = min(sub, tm)
    mu, w0, a0, k_k, k_a, wwa, g2p = params
    const = lambda a: pl.BlockSpec(a.shape, lambda i: (0,) * a.ndim, pipeline_mode=pl.Buffered(1))
    wspecs = [const(a) for a in (mu, w0, a0, k_k, k_a, wwa, g2p)]
    row = lambda w: pl.BlockSpec((tm, w), lambda i: (i, 0))
    if per_seq:
        tiles = seq_rows // tm
        shift_spec = pl.BlockSpec((None, 1, RC_PAD), lambda i: (i // tiles, 0, 0))
        last_spec = pl.BlockSpec((None, 1, RC_PAD), lambda i: (i // tiles, 0, 0))
        last_shape = jax.ShapeDtypeStruct((R // seq_rows, 1, RC_PAD), F32)
    else:
        shift_spec = pl.BlockSpec((tm // seq_rows, RC_PAD), lambda i: (i, 0))
        last_spec = pl.BlockSpec((tm // seq_rows, RC_PAD), lambda i: (i, 0))
        last_shape = jax.ShapeDtypeStruct((R // seq_rows, RC_PAD), F32)
    outs = pl.pallas_call(
        functools.partial(_in_rwkv_kernel, seq_rows=seq_rows, sub=sub),
        grid=(R // tm,),
        in_specs=[row(D), const(nw), mod.spec(tm, 0), mod.spec(tm, 1), const(w_r), shift_spec] + wspecs,
        out_specs=[row(D)] + [row(D)] * 7 + [last_spec],
        out_shape=[jax.ShapeDtypeStruct((R, D), BF16)] + [jax.ShapeDtypeStruct((R, D), F32)] * 7
                  + [last_shape],
        scratch_shapes=[pltpu.VMEM((SUBLANES, RC_PAD), F32), pltpu.VMEM((tm, RC_PAD), F32)],
        compiler_params=_cparams(("arbitrary",)),
        name="in_rwkv",
    )(x, nw, mod.arr, mod.arr, w_r, shift, mu, w0, a0, k_k, k_a, wwa, g2p)
    return outs[0], outs[1:8], outs[8]


def _pair_masks():
    lane = _iota2((1, LANES), 1)
    return (lane < RWKV_N).astype(F32), (lane >= RWKV_N).astype(F32)


def _stack(x, mA, mB):
    return jnp.concatenate([x * mA, x * mB], axis=0)


def _rwkv_scores(cs, lw, cs_end, r, k, kap, bet, mA, mB):
    e_neg = jnp.exp(-cs)
    e_rem = jnp.exp(cs_end - cs)
    rg = _stack(r * jnp.exp(cs), mA, mB)
    kg = _stack(kap * jnp.exp(cs - lw), mA, mB)
    kd = _stack(k * e_neg, mA, mB)
    bd = _stack(bet * e_neg, mA, mB)
    kh = _stack(k * e_rem, mA, mB)
    bh = _stack(bet * e_rem, mA, mB)
    return rg, kg, kd, bd, kh, bh


def _rwkv_epilogue(o, r, k, v, g, lnw, lnb, rk):
    avg = jnp.where(_iota2((LANES, LANES), 0) // RWKV_N == _iota2((LANES, LANES), 1) // RWKV_N,
                    1.0 / RWKV_N, 0.0).astype(F32)
    mean = _dot_exact_rhs(o, avg)
    d = o - mean
    var = _dot_exact_rhs(d * d, avg)
    on = d * lax.rsqrt(var + LNX_EPS) * lnw + lnb
    bonus = _dot_exact_rhs(r * k * rk, avg) * float(RWKV_N) * v
    return (on + bonus) * g


def _rwkv_chunks(r_ref, lw_ref, k_ref, v_ref, kap_ref, bet_ref, g_ref, lnw_ref, lnb_ref, rk_ref,
                 o_ref, st_ref, offs, hp):
    C, c = CHUNK, RWKV_SUB
    C2 = 2 * C
    mA, mB = _pair_masks()
    rr, cc = _iota2((C2, C2), 0), _iota2((C2, C2), 1)
    eye = (rr == cc).astype(F32)
    strict = rr > cc
    incl = rr >= cc
    blk = (rr // c) == (cc // c)
    tri = (_iota2((C, C), 0) >= _iota2((C, C), 1)).astype(F32)
    avg = jnp.where(_iota2((LANES, LANES), 0) // RWKV_N == _iota2((LANES, LANES), 1) // RWKV_N,
                    1.0 / RWKV_N, 0.0).astype(F32)
    chains = [(r0, p) for r0 in offs for p in range(hp)]
    J = range(len(chains))
    pcols = [slice(p * LANES, (p + 1) * LANES) for _, p in chains]

    ld = lambda ref: [ref[pl.ds(r0, C), pcols[j]] for j, (r0, _) in zip(J, chains)]
    r, lw, k, v, kap, bet = ld(r_ref), ld(lw_ref), ld(k_ref), ld(v_ref), ld(kap_ref), ld(bet_ref)
    yield
    cs = [_dot_exact_lhs(tri, lw[j]) for j in J]
    cs_end = [cs[j][C - 1:C, :] for j in J]
    yield
    sc = [_rwkv_scores(cs[j], lw[j], cs_end[j], r[j], k[j], kap[j], bet[j], mA, mB) for j in J]
    rg, kg, kd, bd, kh, bh = ([s[i] for s in sc] for i in range(6))
    vs = [_stack(v[j], mA, mB) for j in J]
    kbd = [jnp.concatenate([kd[j], bd[j]], axis=0) for j in J]
    yield
    sck = [_dot(kg[j], kbd[j], _NT) for j in J]
    yield
    scr = [_dot(rg[j], kbd[j], _NT) for j in J]
    Ak = [jnp.where(strict, sck[j][:, :C2], 0.0) for j in J]
    Ab = [jnp.where(strict, sck[j][:, C2:], 0.0) for j in J]
    yield
    Bkb = [jnp.concatenate([jnp.where(incl, scr[j][:, :C2], 0.0),
                            jnp.where(incl, -scr[j][:, C2:], 0.0)], axis=1) for j in J]
    LD = [jnp.where(blk, Ab[j], 0.0) for j in J]
    X = LD
    Dinv = [eye - LD[j] for j in J]
    for _ in range(int(math.log2(c)) - 1):
        X = [_dot(X[j], X[j]) for j in J]
        yield
        Dinv = [Dinv[j] + _dot(Dinv[j], X[j]) for j in J]
        yield
    Q = [_dot(Dinv[j], Ab[j] - LD[j]) for j in J]
    yield
    T0 = [Dinv[j] - _dot(Q[j], Dinv[j]) for j in J]
    yield
    for _ in range(int(math.log2(C // c)) - 1):
        Q = [_dot(Q[j], Q[j]) for j in J]
        yield
        T0 = [T0[j] + _dot(Q[j], T0[j]) for j in J]
        yield
    E = [(eye - T0[j]) - _dot_hp(Ab[j], T0[j]) for j in J]
    yield
    Tm = [T0[j] + _dot(T0[j], E[j]) for j in J]
    yield
    khb = [jnp.concatenate([kh[j], -bh[j]], axis=0) for j in J]
    AkV = [_dot(Ak[j], vs[j]) for j in J]
    dec = [jnp.exp(cs_end[j]) for j in J]
    yield
    bonus = [_dot_exact_rhs(r[j] * k[j] * rk_ref[:, pcols[j]], avg, parts=2) * float(RWKV_N) * v[j]
             for j in J]
    yield

    for u in range(len(offs)):
        K = range(u * hp, (u + 1) * hp)
        S = {j: st_ref[chains[j][1]] for j in K}
        rhs = {j: _dot(kg[j], S[j], _NT) + AkV[j] for j in K}
        yield
        U = {j: _dot_hp(Tm[j], rhs[j]) for j in K}
        yield
        vu = {j: jnp.concatenate([vs[j], U[j]], axis=0) for j in K}
        for j in K:
            st_ref[chains[j][1]] = S[j] * dec[j] + _dot(vu[j], khb[j], _TN)
        yield
        O = {j: _dot(rg[j], S[j], _NT) + _dot(Bkb[j], vu[j]) for j in K}
        o = {j: O[j][:C, :] + O[j][C:, :] for j in K}
        yield
        mean = {j: _dot_exact_rhs(o[j], avg, parts=2) for j in K}
        d = {j: o[j] - mean[j] for j in K}
        yield
        var = {j: _dot_exact_rhs(d[j] * d[j], avg, parts=2) for j in K}
        yield
        for j in K:
            r0 = chains[j][0]
            on = d[j] * lax.rsqrt(var[j] + LNX_EPS) * lnw_ref[:, pcols[j]] + lnb_ref[:, pcols[j]]
            o_ref[pl.ds(r0, C), pcols[j]] = (on + bonus[j]) * g_ref[pl.ds(r0, C), pcols[j]]
        yield


def _interleave(gens):
    live = list(gens)
    while live:
        for gen in list(live):
            try:
                next(gen)
            except StopIteration:
                live.remove(gen)


def _scan_prompt_kernel(q_ref, f_ref, i_ref, og_ref, lb_ref, nw_ref,
                        r_ref, lw_ref, k_ref, v_ref, kap_ref, bet_ref, g_ref, lnw_ref, lnb_ref, rk_ref,
                        oa_ref, hs_ref, ob_ref, rs_ref, hst_ref, rst_ref, *, nchunks, hh, hp):
    tt = pl.program_id(1)

    @pl.when(tt == 0)
    def _():
        hst_ref[...] = jnp.zeros_like(hst_ref)
        rst_ref[...] = jnp.zeros_like(rst_ref)

    C = CHUNK
    unroll = 2 if nchunks % 2 == 0 else 1

    def body(ci, carry):
        base = ci * (unroll * C)
        offs = [pl.multiple_of(base + u * C, C) for u in range(unroll)]
        _interleave([
            _rwkv_chunks(r_ref, lw_ref, k_ref, v_ref, kap_ref, bet_ref, g_ref, lnw_ref, lnb_ref, rk_ref,
                         ob_ref, rst_ref, offs, hp),
            _hgrn_chunks(q_ref, f_ref, i_ref, og_ref, lb_ref, nw_ref, oa_ref, hst_ref, offs, hh),
        ])
        return carry

    lax.fori_loop(0, nchunks // unroll, body, 0)

    @pl.when(tt == pl.num_programs(1) - 1)
    def _():
        for j in range(hh):
            hs_ref[j] = hst_ref[j].T
        for p in range(hp):
            S = rst_ref[p]
            rs_ref[2 * p] = S[:RWKV_N, :RWKV_N]
            rs_ref[2 * p + 1] = S[RWKV_N:, RWKV_N:]


def _scan_prompt(P_h, lb_logits, nw, arrs, g, lnw, lnb, rk, B, T, tt_rows=256):
    R, D = arrs[0].shape
    hh = P_h.shape[1] // (4 * HGRN_DK)
    hp = D // LANES
    wh = hh * HGRN_DK
    tt_rows = min(tt_rows, T)
    nt = T // tt_rows
    hcol = lambda off: pl.BlockSpec((tt_rows, wh), lambda b, t: (b * nt + t, off))
    col = pl.BlockSpec((tt_rows, D), lambda b, t: (b * nt + t, 0))
    const = lambda a: pl.BlockSpec(a.shape, lambda b, t: (0,) * a.ndim)
    nw = nw.reshape(1, HGRN_DK)
    lnw, lnb, rk = lnw.reshape(1, D), lnb.reshape(1, D), rk.reshape(1, D)
    oa, hs, ob, rs = pl.pallas_call(
        functools.partial(_scan_prompt_kernel, nchunks=tt_rows // CHUNK, hh=hh, hp=hp),
        grid=(B, nt),
        in_specs=[hcol(0), hcol(1), hcol(2), hcol(3), const(lb_logits), const(nw)]
                 + [col] * 7 + [const(lnw), const(lnb), const(rk)],
        out_specs=[pl.BlockSpec((tt_rows, wh), lambda b, t: (b * nt + t, 0)),
                   pl.BlockSpec((None, hh, HGRN_DK, HGRN_DK), lambda b, t: (b, 0, 0, 0)),
                   col,
                   pl.BlockSpec((None, 2 * hp, RWKV_N, RWKV_N), lambda b, t: (b, 0, 0, 0))],
        out_shape=[jax.ShapeDtypeStruct((R, wh), F32),
                   jax.ShapeDtypeStruct((B, hh, HGRN_DK, HGRN_DK), F32),
                   jax.ShapeDtypeStruct((R, D), F32),
                   jax.ShapeDtypeStruct((B, 2 * hp, RWKV_N, RWKV_N), F32)],
        scratch_shapes=[pltpu.VMEM((hh, HGRN_DK, HGRN_DK), F32), pltpu.VMEM((hp, LANES, LANES), F32)],
        compiler_params=_cparams(("parallel", "arbitrary")),
        name="scan_prompt",
    )(P_h, P_h, P_h, P_h, lb_logits, nw, *arrs, g, lnw, lnb, rk)
    return oa, hs, ob, rs


def _rwkv_sample_kernel(r_ref, lw_ref, k_ref, v_ref, kap_ref, bet_ref, g_ref, lnw_ref, lnb_ref, rk_ref,
                        s0_ref, o_ref, s_ref, sbd_ref, *, nseq, T, hp):
    C = nseq * T
    C2 = 2 * C
    N = RWKV_N
    mA, mB = _pair_masks()
    bdm = (_iota2((LANES, LANES), 0) // N == _iota2((LANES, LANES), 1) // N).astype(F32)
    rr, cc = _iota2((C2, C2), 0), _iota2((C2, C2), 1)
    eye = (rr == cc).astype(F32)
    same = (rr // T) == (cc // T)
    strict = same & (rr > cc)
    incl = same & (rr >= cc)
    r1, c1 = _iota2((C, C), 0), _iota2((C, C), 1)
    seg_incl = ((r1 // T == c1 // T) & (r1 >= c1)).astype(F32)
    seg_all = (r1 // T == c1 // T).astype(F32)
    zero = jnp.zeros((N, N), F32)

    J = range(hp)
    cols = [slice(j * LANES, (j + 1) * LANES) for j in J]
    ld = lambda ref: [ref[:, cols[j]] for j in J]
    r, lw, k, v, kap, bet = ld(r_ref), ld(lw_ref), ld(k_ref), ld(v_ref), ld(kap_ref), ld(bet_ref)
    cs = [_dot_exact_lhs(seg_incl, lw[j]) for j in J]
    cs_end = [_dot_exact_lhs(seg_all, lw[j]) for j in J]
    sc = [_rwkv_scores(cs[j], lw[j], cs_end[j], r[j], k[j], kap[j], bet[j], mA, mB) for j in J]
    rg, kg, kd, bd, kh, bh = ([s[i] for s in sc] for i in range(6))
    vs = [_stack(v[j], mA, mB) for j in J]
    kbd = [jnp.concatenate([kd[j], bd[j]], axis=0) for j in J]
    sck = [_dot(kg[j], kbd[j], _NT) for j in J]
    scr = [_dot(rg[j], kbd[j], _NT) for j in J]
    Ak = [jnp.where(strict, sck[j][:, :C2], 0.0) for j in J]
    Ab = [jnp.where(strict, sck[j][:, C2:], 0.0) for j in J]
    Bkb = [jnp.concatenate([jnp.where(incl, scr[j][:, :C2], 0.0),
                            jnp.where(incl, -scr[j][:, C2:], 0.0)], axis=1) for j in J]
    Tm = [_neumann_inv(Ab[j], eye, int(math.log2(T)) - 1) for j in J]
    kgp = [kg[j][:C, :] + kg[j][C:, :] for j in J]
    rgp = [rg[j][:C, :] + rg[j][C:, :] for j in J]
    ks = [[] for _ in J]
    rs = [[] for _ in J]
    for n in range(nseq):
        sl = slice(n * T, (n + 1) * T)
        for j in J:
            Sbd = jnp.concatenate(
                [jnp.concatenate([s0_ref[n, 2 * j], zero], axis=1),
                 jnp.concatenate([zero, s0_ref[n, 2 * j + 1]], axis=1)], axis=0)
            sbd_ref[n, j] = Sbd
            res = _dot(jnp.concatenate([kgp[j][sl], rgp[j][sl]], axis=0), Sbd, _NT)
            ks[j].append(res[:T])
            rs[j].append(res[T:])
    KS = [_stack(jnp.concatenate(ks[j], axis=0), mA, mB) for j in J]
    RS = [_stack(jnp.concatenate(rs[j], axis=0), mA, mB) for j in J]
    rhs = [KS[j] + _dot(Ak[j], vs[j]) for j in J]
    U = [_dot_hp(Tm[j], rhs[j]) for j in J]
    vu = [jnp.concatenate([vs[j], U[j]], axis=0) for j in J]
    O = [RS[j] + _dot(Bkb[j], vu[j]) for j in J]
    Up = [U[j][:C, :] + U[j][C:, :] for j in J]
    khp = [kh[j][:C, :] + kh[j][C:, :] for j in J]
    nbhp = [-(bh[j][:C, :] + bh[j][C:, :]) for j in J]
    dec = [jnp.exp(cs_end[j]) for j in J]
    for n in range(nseq):
        sl = slice(n * T, (n + 1) * T)
        for j in J:
            upd = _dot(jnp.concatenate([v[j][sl], Up[j][sl]], axis=0),
                       jnp.concatenate([khp[j][sl], nbhp[j][sl]], axis=0), _TN)
            Sn = sbd_ref[n, j] * dec[j][n * T:n * T + 1, :] + upd * bdm
            s_ref[n, 2 * j] = Sn[:N, :N]
            s_ref[n, 2 * j + 1] = Sn[N:, N:]
    for j in J:
        o = O[j][:C, :] + O[j][C:, :]
        o_ref[:, cols[j]] = _rwkv_epilogue(o, r[j], k[j], v[j], g_ref[:, cols[j]],
                                           lnw_ref[:, cols[j]], lnb_ref[:, cols[j]], rk_ref[:, cols[j]])


def _rwkv_sample(arrs, g, lnw, lnb, rk, s0, T, nseq=16, hp=4):
    R, D = arrs[0].shape
    NP = D // LANES
    B = R // T
    rows = nseq * T
    wd = hp * LANES
    col = pl.BlockSpec((rows, wd), lambda p, gi: (gi, p))
    par = pl.BlockSpec((1, wd), lambda p, gi: (0, p))
    sspec = pl.BlockSpec((nseq, 2 * hp, RWKV_N, RWKV_N), lambda p, gi: (gi, p, 0, 0))
    o, s = pl.pallas_call(
        functools.partial(_rwkv_sample_kernel, nseq=nseq, T=T, hp=hp),
        grid=(NP // hp, B // nseq),
        in_specs=[col] * 7 + [par] * 3 + [sspec],
        out_specs=[col, sspec],
        out_shape=[jax.ShapeDtypeStruct((R, D), F32), jax.ShapeDtypeStruct(s0.shape, F32)],
        scratch_shapes=[pltpu.VMEM((nseq, hp, LANES, LANES), F32)],
        compiler_params=_cparams(("parallel", "parallel")),
        name="rwkv_sample",
    )(*arrs, g, lnw.reshape(1, D), lnb.reshape(1, D), rk.reshape(1, D), s0)
    return o, s


def _merge_kernel(x_ref, h_ref, oa_ref, ob_ref, wg_ref, w_ref, gt_ref, o_ref, *, rep):
    D = x_ref.shape[1]
    pg = lax.dot_general(h_ref[...], wg_ref[...], _NT, preferred_element_type=F32)
    m = _sigmoid(pg[:, :D]) * oa_ref[...] + _sigmoid(pg[:, D:]) * ob_ref[...]
    o_ref[...] = x_ref[...] + _rep_rows(gt_ref[...], rep) * _dot(m, w_ref[...])


def _merge_out(x, h, o_a, o_b, w_g, w_out, mod, tm):
    R, D = x.shape
    row = pl.BlockSpec((tm, D), lambda i: (i, 0))
    const = lambda a: pl.BlockSpec(a.shape, lambda i: (0,) * a.ndim, pipeline_mode=pl.Buffered(1))
    return pl.pallas_call(
        functools.partial(_merge_kernel, rep=mod.rep),
        grid=(R // tm,),
        in_specs=[row, row, row, row, const(w_g), const(w_out), mod.spec(tm, 2)],
        out_specs=row,
        out_shape=jax.ShapeDtypeStruct((R, D), F32),
        compiler_params=_cparams(("parallel",)),
        name="merge_out",
    )(x, h, o_a, o_b, w_g, w_out, mod.arr)


def _mlp_kernel(x_ref, nw_ref, sh_ref, sc_ref, gt_ref, wu_ref, wd_ref, fw_ref, o_ref, *, tf, rep):
    x = x_ref[...]
    h = (_rms(x, nw_ref[...]) * (1.0 + _rep_rows(sc_ref[...], rep)) + _rep_rows(sh_ref[...], rep)).astype(BF16)
    acc = None
    for c in range(wu_ref.shape[1] // tf):
        u = jnp.dot(h, wu_ref[:, c * tf:(c + 1) * tf], preferred_element_type=F32)
        u = jnp.square(jnp.maximum(u, 0.0))
        part = _dot(u, wd_ref[c * tf:(c + 1) * tf, :])
        acc = part if acc is None else acc + part
    o_ref[...] = _rms(x + _rep_rows(gt_ref[...], rep) * acc, fw_ref[...])


def _mlp(x, nw, w_up, w_down, fw, mod, tm, tf=1024):
    R, D = x.shape
    const = lambda a: pl.BlockSpec(a.shape, lambda i: (0,) * a.ndim, pipeline_mode=pl.Buffered(1))
    row = pl.BlockSpec((tm, D), lambda i: (i, 0))
    nw, fw = nw.reshape(1, D), fw.reshape(1, D)
    return pl.pallas_call(
        functools.partial(_mlp_kernel, tf=tf, rep=mod.rep),
        grid=(R // tm,),
        in_specs=[row, const(nw), mod.spec(tm, 3), mod.spec(tm, 4), mod.spec(tm, 5),
                  const(w_up), const(w_down), const(fw)],
        out_specs=row,
        out_shape=jax.ShapeDtypeStruct((R, D), F32),
        compiler_params=_cparams(("parallel",)),
        name="mlp",
    )(x, nw, mod.arr, mod.arr, mod.arr, w_up, w_down, fw)


def _run(x3, mod_rows, s_hgrn, s_rwkv, s_shift, W, prompt):
    B, T, D = x3.shape
    R = B * T
    x = x3.reshape(R, D)
    if prompt:
        mod = _Mod(mod_rows.reshape(B, 1, -1), True, T, D)
        tm = min(1024, T)
    else:
        mod = _Mod(mod_rows, False, T, D)
        tm = min(512, R)

    rw = (W["mu"], W["w0"], W["a0"], W["k_k"], W["k_a"], W["wwa"], W["g2p"])
    nw1 = W["norm1_w"].reshape(1, D)
    if prompt:
        shift_in = jnp.zeros((B, 1, RC_PAD), F32)
        h, arrs, last = _in_rwkv(x, nw1, mod, W["w_r"], shift_in, rw, True, T)
        shift_out = last[:, 0, :RC_COLS]
        P_h = _matmul(h, W["w_in_t"], 4 * D, tm, 2048, "proj_h")
        o_a, hg, o_b, rs = _scan_prompt(P_h, W["lb_logits"], W["hgrn_norm_w"], arrs[:6], arrs[6],
                                        W["lnx_w"], W["lnx_b"], W["r_k"], B, T)
    else:
        shift_in = jnp.pad(s_shift, ((0, 0), (0, RC_PAD - RC_COLS)))
        h, arrs, last = _in_rwkv(x, nw1, mod, W["w_r"], shift_in, rw, False, T)
        shift_out = last[:, :RC_COLS]
        P_h = _matmul(h, W["w_in_t"], 4 * D, tm, 2048, "proj_h")
        o_a, hg = _hgrn_sample(P_h, W["lb_logits"], W["hgrn_norm_w"], s_hgrn, T)
        o_b, rs = _rwkv_sample(arrs[:6], arrs[6], W["lnx_w"], W["lnx_b"], W["r_k"], s_rwkv, T)

    x1 = _merge_out(x, h, o_a, o_b, W["w_g"], W["w_out"], mod, min(tm, 512))
    y = _mlp(x1, W["norm2_w"], W["w_up"], W["w_down"], W["final_norm_w"], mod, min(tm, 512))
    return y.reshape(B, T, D), hg, rs, shift_out


def kernel(x_prompt, x_sample, state_hgrn, state_rwkv, state_shift, c_prompt, c_sample, norm1_w, norm2_w, ada_w, ada_b, w_in, lb_logits, hgrn_norm_w, rwkv_mu, rwkv_w0, rwkv_w2, rwkv_a0, rwkv_a2, rwkv_g2, rwkv_k_k, rwkv_k_a, rwkv_r_k, rwkv_lnx_w, rwkv_lnx_b, w_out, w_up, w_down, final_norm_w):
    depth = norm1_w.shape[0]
    assert depth == 1, "single-layer step"
    D = x_prompt.shape[-1]
    HW = 4 * D
    w_in_t = w_in[0].T.astype(BF16)
    row = lambda a: a.reshape(1, -1)
    wwa = jnp.zeros((LANES, 2 * D), F32)
    wwa = wwa.at[:W_LORA, :D].set(rwkv_w2[0]).at[W_LORA:W_LORA + A_LORA, D:].set(rwkv_a2[0])
    W = {
        "norm1_w": norm1_w[0], "norm2_w": norm2_w[0], "final_norm_w": final_norm_w,
        "w_in_t": w_in_t,
        "w_r": jnp.pad(w_in[0][:, HW:HW + RC_COLS], ((0, 0), (0, RC_PAD - RC_COLS))).astype(BF16),
        "w_g": w_in_t[HW + RC_COLS:],
        "lb_logits": lb_logits, "hgrn_norm_w": hgrn_norm_w[0],
        "mu": jnp.pad(row(rwkv_mu[0]), ((0, 0), (0, RC_PAD - RC_COLS))),
        "w0": row(rwkv_w0[0]), "a0": row(rwkv_a0[0]), "k_k": row(rwkv_k_k[0]), "k_a": row(rwkv_k_a[0]),
        "wwa": wwa.astype(BF16),
        "g2p": jnp.pad(rwkv_g2[0], ((0, G_LORA_PAD - G_LORA), (0, 0))).astype(BF16),
        "lnx_w": rwkv_lnx_w[0], "lnx_b": rwkv_lnx_b[0], "r_k": rwkv_r_k[0].reshape(-1),
        "w_out": w_out[0].astype(BF16), "w_up": w_up[0].astype(BF16), "w_down": w_down[0].astype(BF16),
    }
    nb = x_prompt.shape[0]
    mod = _adaln(jnp.concatenate([c_prompt, c_sample], axis=0), ada_w[0], ada_b[0])
    y_p, hg_p, rw_p, sh_p = _run(x_prompt, mod[:nb], None, None, None, W, True)
    y_s, hg_s, rw_s, sh_s = _run(x_sample, mod[nb:], state_hgrn[0], state_rwkv[0], state_shift[0], W, False)
    ex = lambda a: a[None]
    return (y_p, y_s, ex(hg_p), ex(rw_p), ex(sh_p), ex(hg_s), ex(rw_s), ex(sh_s))
```

```python
import functools
import math

import jax
import jax.numpy as jnp
from jax import lax
from jax.experimental import pallas as pl
from jax.experimental.pallas import tpu as pltpu

F32 = jnp.float32
BF16 = jnp.bfloat16

LANES = 128
SUBLANES = 8
VMEM_LIMIT = 48 * 1024 * 1024

HGRN_DK = 128
RWKV_N = 64
W_LORA = 64
A_LORA = 64
G_LORA = 160
G_LORA_PAD = 256
RC_MAIN = 3 * 1024
RC_COLS = RC_MAIN + W_LORA + A_LORA + G_LORA
RC_PAD = RC_MAIN + LANES + G_LORA_PAD
RMS_EPS = 1e-6
LNX_EPS = 64e-5
CHUNK = 64
SUB = 8
RWKV_SUB = 8


def _cparams(sem, vmem=VMEM_LIMIT):
    return pltpu.CompilerParams(dimension_semantics=sem, vmem_limit_bytes=vmem)


_NN = (((1,), (0,)), ((), ()))
_NT = (((1,), (1,)), ((), ()))
_TN = (((0,), (0,)), ((), ()))


def _dot(a, b, dims=_NN):
    return lax.dot_general(a.astype(BF16), b.astype(BF16), dims, preferred_element_type=F32)


def _split2(x):
    hi = x.astype(BF16)
    lo = (x - hi.astype(F32)).astype(BF16)
    return hi, lo


def _split3(x):
    hi = x.astype(BF16)
    r1 = x - hi.astype(F32)
    mid = r1.astype(BF16)
    lo = (r1 - mid.astype(F32)).astype(BF16)
    return hi, mid, lo


def _dot_exact_lhs(a, x, dims=_NN):
    a = a.astype(BF16)
    hi, mid, lo = _split3(x)
    d = lambda p: lax.dot_general(a, p, dims, preferred_element_type=F32)
    return d(hi) + d(mid) + d(lo)


def _dot_exact_rhs(x, b, dims=_NN, parts=3):
    b = b.astype(BF16)
    d = lambda p: lax.dot_general(p, b, dims, preferred_element_type=F32)
    if parts == 2:
        hi, lo = _split2(x)
        return d(hi) + d(lo)
    hi, mid, lo = _split3(x)
    return d(hi) + d(mid) + d(lo)


def _dot_hp(a, b, dims=_NN):
    ah, al = _split2(a)
    bh, bl = _split2(b)
    d = lambda p, q: lax.dot_general(p, q, dims, preferred_element_type=F32)
    return d(ah, bh) + d(ah, bl) + d(al, bh)


def _sigmoid(x):
    return 1.0 / (1.0 + jnp.exp(-x))


def _iota2(shape, dim):
    return lax.broadcasted_iota(jnp.int32, shape, dim)


def _neumann_inv(L, eye, nsq):
    X = L
    P = eye - L
    for _ in range(nsq):
        X = _dot_hp(X, X)
        P = P + _dot_hp(P, X)
    return P


def _adaln_kernel(c_ref, w_ref, b_ref, o_ref):
    c = c_ref[...]
    s = c * _sigmoid(c)
    o_ref[...] = _dot(s, w_ref[...]) + b_ref[...]


def _adaln(c_all, ada_w, ada_b, tn=1536):
    R, D = c_all.shape
    N = ada_w.shape[1]
    return pl.pallas_call(
        _adaln_kernel,
        grid=(N // tn,),
        in_specs=[pl.BlockSpec((R, D), lambda j: (0, 0)),
                  pl.BlockSpec((D, tn), lambda j: (0, j)),
                  pl.BlockSpec((1, tn), lambda j: (0, j))],
        out_specs=pl.BlockSpec((R, tn), lambda j: (0, j)),
        out_shape=jax.ShapeDtypeStruct((R, N), F32),
        compiler_params=_cparams(("parallel",)),
        name="adaln",
    )(c_all, ada_w, ada_b.reshape(1, N))


class _Mod:
    def __init__(self, arr, per_seq, seq_rows, D):
        self.arr, self.per_seq, self.seq_rows, self.D = arr, per_seq, seq_rows, D
        self.rep = 1 if per_seq else seq_rows

    def spec(self, tm, comp):
        D = self.D
        if self.per_seq:
            tiles = self.seq_rows // tm
            return pl.BlockSpec((None, 1, D), lambda i, *_: (i // tiles, 0, comp))
        return pl.BlockSpec((tm // self.seq_rows, D), lambda i, *_: (i, comp))


def _rep_rows(v, rep):
    if rep == 1:
        return v
    n, w = v.shape
    return jnp.broadcast_to(v[:, None, :], (n, rep, w)).reshape(n * rep, w)


def _rms(x, w):
    return x * lax.rsqrt(jnp.mean(x * x, axis=-1, keepdims=True) + RMS_EPS) * w


def _mm_kernel(x_ref, wt_ref, o_ref):
    o_ref[...] = lax.dot_general(x_ref[...], wt_ref[...], _NT, preferred_element_type=F32)


def _matmul(x, wt, n_out, tm, tn, name):
    R, K = x.shape
    N = n_out
    return pl.pallas_call(
        _mm_kernel,
        grid=(R // tm, N // tn),
        in_specs=[pl.BlockSpec((tm, K), lambda i, j: (i, 0)),
                  pl.BlockSpec((tn, K), lambda i, j: (j, 0))],
        out_specs=pl.BlockSpec((tm, tn), lambda i, j: (i, j)),
        out_shape=jax.ShapeDtypeStruct((R, N), F32),
        compiler_params=_cparams(("parallel", "arbitrary")),
        name=name,
    )(x, wt)


def _hgrn_gates(fp, lb_logits):
    m = jnp.max(lb_logits, axis=0, keepdims=True)
    e = jnp.exp(lb_logits - m)
    lb = e[0:1, :] / jnp.sum(e, axis=0, keepdims=True)
    f = lb + (1.0 - lb) * _sigmoid(fp)
    return jnp.log(f), 1.0 - f


def _hgrn_diag(q, kin, b, c):
    R = q.shape[0]
    nb = R // c
    q3 = q.reshape(nb, c, LANES)
    k3 = kin.reshape(nb, c, LANES)
    b3 = (b * math.log2(math.e)).reshape(nb, c, LANES)
    lane_mod = lax.broadcasted_iota(jnp.int32, (nb, c, R), 2) % c
    acc = jnp.zeros((nb, c, R), F32)
    for s in range(c):
        d = jnp.minimum(b3 - b3[:, s:s + 1, :], 0.0)
        col = jnp.sum(q3 * k3[:, s:s + 1, :] * jnp.exp2(d), axis=-1, keepdims=True)
        acc = jnp.where(lane_mod == s, col, acc)
    rr, cc = _iota2((R, R), 0), _iota2((R, R), 1)
    keep = ((rr // c) == (cc // c)) & (rr >= cc)
    return jnp.where(keep, acc.reshape(R, R), 0.0)


def _hgrn_cross(q, kin, b, m):
    R = q.shape[0]
    h = m // 2
    nb = R // m
    ref = jnp.broadcast_to(b.reshape(nb, m, LANES)[:, h - 1:h, :], (nb, m, LANES)).reshape(R, LANES)
    upper = (_iota2((R, 1), 0) % m) >= h
    qs = q * jnp.exp(jnp.where(upper, b - ref, -1e30))
    ks = kin * jnp.exp(jnp.where(upper, -1e30, ref - b))
    sc = _dot(qs, ks, _NT)
    if nb == 1:
        return sc
    same_blk = (_iota2((R, R), 0) // m) == (_iota2((R, R), 1) // m)
    return jnp.where(same_blk, sc, 0.0)


def _hgrn_out(o, og, nw):
    o = o * lax.rsqrt(jnp.mean(o * o, axis=-1, keepdims=True) + RMS_EPS) * nw
    return o * (og * _sigmoid(og))


def _hgrn_chunks(q_ref, f_ref, i_ref, g_ref, lb_ref, nw_ref, o_ref, st_ref, offs, hp):
    C, c = CHUNK, SUB
    nb = C // c
    tri = (_iota2((C, C), 0) >= _iota2((C, C), 1)).astype(F32)
    nw = nw_ref[...]
    H = range(hp)
    cols = [slice(j * LANES, (j + 1) * LANES) for j in H]
    lane_mod = lax.broadcasted_iota(jnp.int32, (nb, c, C), 2) % c
    rr, cc = _iota2((C, C), 0), _iota2((C, C), 1)
    keep = ((rr // c) == (cc // c)) & (rr >= cc)
    for r0 in offs:
        q = [q_ref[pl.ds(r0, C), cols[j]] for j in H]
        v = [i_ref[pl.ds(r0, C), cols[j]] for j in H]
        gates = [_hgrn_gates(f_ref[pl.ds(r0, C), cols[j]], lb_ref[:, cols[j]]) for j in H]
        lf = [gt[0] for gt in gates]
        kin = [gt[1] for gt in gates]
        yield
        b = [_dot_exact_lhs(tri, lf[j]) for j in H]
        b_last = [b[j][C - 1:C, :] for j in H]
        yield
        ST = [st_ref[j] for j in H]
        o = [_dot(q[j] * jnp.exp(b[j]), ST[j], _NT) for j in H]
        yield
        for j in H:
            st_ref[j] = ST[j] * jnp.exp(b_last[j]) + _dot(v[j], kin[j] * jnp.exp(b_last[j] - b[j]), _TN)
        yield
        q3 = [q[j].reshape(nb, c, LANES) for j in H]
        k3 = [kin[j].reshape(nb, c, LANES) for j in H]
        b3 = [(b[j] * math.log2(math.e)).reshape(nb, c, LANES) for j in H]
        acc = [jnp.zeros((nb, c, C), F32) for _ in H]
        for s in range(c):
            for j in H:
                d = jnp.minimum(b3[j] - b3[j][:, s:s + 1, :], 0.0)
                col = jnp.sum(q3[j] * k3[j][:, s:s + 1, :] * jnp.exp2(d), axis=-1, keepdims=True)
                acc[j] = jnp.where(lane_mod == s, col, acc[j])
            yield
        A = [jnp.where(keep, acc[j].reshape(C, C), 0.0) for j in H]
        m = 2 * c
        while m <= C:
            A = [A[j] + _hgrn_cross(q[j], kin[j], b[j], m) for j in H]
            m *= 2
            yield
        o = [o[j] + _dot(A[j], v[j]) for j in H]
        yield
        for j in H:
            o_ref[pl.ds(r0, C), cols[j]] = _hgrn_out(o[j], g_ref[pl.ds(r0, C), cols[j]], nw)
        yield


def _hgrn_sample_kernel(q_ref, f_ref, i_ref, g_ref, lb_ref, nw_ref, s0_ref, o_ref, s_ref, *, nseq, T, hp):
    R = nseq * T
    rr, cc = _iota2((R, R), 0), _iota2((R, R), 1)
    same = (rr // T) == (cc // T)
    seg_incl = (same & (rr >= cc)).astype(F32)
    seg_all = same.astype(F32)
    nw = nw_ref[...]
    H = range(hp)
    cols = [slice(j * LANES, (j + 1) * LANES) for j in H]
    q = [q_ref[:, cols[j]] for j in H]
    v = [i_ref[:, cols[j]] for j in H]
    gates = [_hgrn_gates(f_ref[:, cols[j]], lb_ref[:, cols[j]]) for j in H]
    lf = [gt[0] for gt in gates]
    kin = [gt[1] for gt in gates]
    b = [_dot_exact_lhs(seg_incl, lf[j]) for j in H]
    b_tot = [_dot_exact_lhs(seg_all, lf[j]) for j in H]
    A = [_hgrn_diag(q[j], kin[j], b[j], T) for j in H]
    o = [_dot(A[j], v[j]) for j in H]
    qg = [q[j] * jnp.exp(b[j]) for j in H]
    kh = [kin[j] * jnp.exp(b_tot[j] - b[j]) for j in H]
    ecol = [jnp.transpose(jnp.exp(b_tot[j]).reshape(nseq, T, LANES)[:, 0, :]) for j in H]
    inter = [[] for _ in H]
    for n in range(nseq):
        sl = slice(n * T, (n + 1) * T)
        for j in H:
            S = s0_ref[n, j]
            inter[j].append(_dot(qg[j][sl], S))
            s_ref[n, j] = S * ecol[j][:, n:n + 1] + _dot(kh[j][sl], v[j][sl], _TN)
    for j in H:
        oj = o[j] + jnp.concatenate(inter[j], axis=0)
        o_ref[:, cols[j]] = _hgrn_out(oj, g_ref[:, cols[j]], nw)


def _hgrn_sample(P_h, lb_logits, nw, s0, T, nseq=16, hp=4):
    R, W = P_h.shape
    H = W // (4 * HGRN_DK)
    HG = H // hp
    wd = hp * HGRN_DK
    B = R // T
    rows = nseq * T
    col = lambda off: pl.BlockSpec((rows, wd), lambda h, g: (g, off * HG + h))
    sspec = pl.BlockSpec((nseq, hp, HGRN_DK, HGRN_DK), lambda h, g: (g, h, 0, 0))
    o, s = pl.pallas_call(
        functools.partial(_hgrn_sample_kernel, nseq=nseq, T=T, hp=hp),
        grid=(HG, B // nseq),
        in_specs=[col(0), col(1), col(2), col(3),
                  pl.BlockSpec((lb_logits.shape[0], wd), lambda h, g: (0, h)),
                  pl.BlockSpec((1, HGRN_DK), lambda h, g: (0, 0)),
                  sspec],
        out_specs=[pl.BlockSpec((rows, wd), lambda h, g: (g, h)), sspec],
        out_shape=[jax.ShapeDtypeStruct((R, H * HGRN_DK), F32),
                   jax.ShapeDtypeStruct(s0.shape, F32)],
        compiler_params=_cparams(("parallel", "parallel")),
        name="hgrn_sample",
    )(P_h, P_h, P_h, P_h, lb_logits, nw.reshape(1, HGRN_DK), s0)
    return o, s


def _rwkv_prep_body(rc, prev, mu_ref, w0_ref, a0_ref, kk_ref, ka_ref, wwa_ref, g2_ref, outs, rows):
    r_o, lw_o, k_o, v_o, kap_o, bet_o, g_o = outs
    D = RC_MAIN // 3
    xs = rc + mu_ref[...] * (prev - rc)
    r = xs[:, 0:D]
    kb = xs[:, D:2 * D]
    v = xs[:, 2 * D:3 * D]
    xwa = xs[:, RC_MAIN:RC_MAIN + LANES]
    xg = xs[:, RC_MAIN + LANES:]
    lane = _iota2((1, LANES), 1)
    wa = _dot(jnp.where(lane < W_LORA, jnp.tanh(xwa), xwa), wwa_ref[...])
    w = w0_ref[...] + wa[:, :D]
    a = _sigmoid(a0_ref[...] + wa[:, D:])
    lw = -math.exp(-0.5) * _sigmoid(w)
    g = _dot(_sigmoid(xg), g2_ref[...])
    kk = kb * kk_ref[...]
    seg = (_iota2((LANES, LANES), 0) // RWKV_N == _iota2((LANES, LANES), 1) // RWKV_N).astype(F32)
    k_a = ka_ref[...]
    for p in range(D // LANES):
        sl = slice(p * LANES, (p + 1) * LANES)
        kkp = kk[:, sl]
        kap = kkp * lax.rsqrt(jnp.maximum(_dot_exact_rhs(kkp * kkp, seg, parts=2), 1e-24))
        ap = a[:, sl]
        kap_o[rows, sl] = kap
        bet_o[rows, sl] = ap * kap
        k_o[rows, sl] = kb[:, sl] * (1.0 + (ap - 1.0) * k_a[:, sl])
    r_o[rows, :] = r
    lw_o[rows, :] = lw
    v_o[rows, :] = v
    g_o[rows, :] = g


def _in_rwkv_kernel(x_ref, nw_ref, sh_ref, sc_ref, wr_ref, shift_ref, mu_ref, w0_ref, a0_ref, kk_ref,
                    ka_ref, wwa_ref, g2_ref, h_ref, r_o, lw_o, k_o, v_o, kap_o, bet_o, g_o, last_o,
                    carry_ref, rc_ref, *, seq_rows, sub):
    i = pl.program_id(0)
    tm = x_ref.shape[0]
    per_seq = seq_rows >= tm

    @pl.when(i == 0)
    def _():
        carry_ref[...] = jnp.zeros_like(carry_ref)

    rep = 1 if per_seq else seq_rows
    h = (_rms(x_ref[...], nw_ref[...]) * (1.0 + _rep_rows(sc_ref[...], rep))
         + _rep_rows(sh_ref[...], rep)).astype(BF16)
    h_ref[...] = h
    outs = (r_o, lw_o, k_o, v_o, kap_o, bet_o, g_o)
    rid = _iota2((sub, 1), 0)
    rc_ref[...] = jnp.dot(h, wr_ref[...], preferred_element_type=F32)
    prev_last = None
    for s in range(tm // sub):
        rows = slice(s * sub, (s + 1) * sub)
        rc = rc_ref[rows, :]
        rolled = pltpu.roll(rc, 1, axis=0)
        if per_seq:
            if s == 0:
                start = (i % (seq_rows // tm)) == 0
                first = jnp.where(start, shift_ref[...], carry_ref[SUBLANES - 1:SUBLANES, :])
            else:
                first = prev_last
            prev = jnp.where(rid == 0, first, rolled)
        else:
            seqs = slice(s * sub // seq_rows, (s + 1) * sub // seq_rows)
            prev = jnp.where(rid % seq_rows == 0, _rep_rows(shift_ref[seqs, :], seq_rows), rolled)
            last_o[seqs, :] = rc.reshape(sub // seq_rows, seq_rows, RC_PAD)[:, seq_rows - 1, :]
        prev_last = rc[sub - 1:sub, :]
        if s == tm // sub - 1 and per_seq:
            carry_ref[...] = rc[sub - SUBLANES:, :]
            last_o[...] = prev_last
        _rwkv_prep_body(rc, prev, mu_ref, w0_ref, a0_ref, kk_ref, ka_ref, wwa_ref, g2_ref, outs, rows)


def _in_rwkv(x, nw, mod, w_r, shift, params, per_seq, seq_rows, tm=256, sub=128):
    R, D = x.shape
    tm = min(tm, seq_rows) if per_seq else min(tm, R)
    sub = min(sub, tm)
    mu, w0, a0, k_k, k_a, wwa, g2p = params
    const = lambda a: pl.BlockSpec(a.shape, lambda i: (0,) * a.ndim, pipeline_mode=pl.Buffered(1))
    wspecs = [const(a) for a in (mu, w0, a0, k_k, k_a, wwa, g2p)]
    row = lambda w: pl.BlockSpec((tm, w), lambda i: (i, 0))
    if per_seq:
        tiles = seq_rows // tm
        shift_spec = pl.BlockSpec((None, 1, RC_PAD), lambda i: (i // tiles, 0, 0))
        last_spec = pl.BlockSpec((None, 1, RC_PAD), lambda i: (i // tiles, 0, 0))
        last_shape = jax.ShapeDtypeStruct((R // seq_rows, 1, RC_PAD), F32)
    else:
        shift_spec = pl.BlockSpec((tm // seq_rows, RC_PAD), lambda i: (i, 0))
        last_spec = pl.BlockSpec((tm // seq_rows, RC_PAD), lambda i: (i, 0))
        last_shape = jax.ShapeDtypeStruct((R // seq_rows, RC_PAD), F32)
    outs = pl.pallas_call(
        functools.partial(_in_rwkv_kernel, seq_rows=seq_rows, sub=sub),
        grid=(R // tm,),
        in_specs=[row(D), const(nw), mod.spec(tm, 0), mod.spec(tm, 1), const(w_r), shift_spec] + wspecs,
        out_specs=[row(D)] + [row(D)] * 7 + [last_spec],
        out_shape=[jax.ShapeDtypeStruct((R, D), BF16)] + [jax.ShapeDtypeStruct((R, D), F32)] * 7
                  + [last_shape],
        scratch_shapes=[pltpu.VMEM((SUBLANES, RC_PAD), F32), pltpu.VMEM((tm, RC_PAD), F32)],
        compiler_params=_cparams(("arbitrary",)),
        name="in_rwkv",
    )(x, nw, mod.arr, mod.arr, w_r, shift, mu, w0, a0, k_k, k_a, wwa, g2p)
    return outs[0], outs[1:8], outs[8]


def _pair_masks():
    lane = _iota2((1, LANES), 1)
    return (lane < RWKV_N).astype(F32), (lane >= RWKV_N).astype(F32)


def _stack(x, mA, mB, dtype=F32):
    x = x.astype(dtype)
    return jnp.concatenate([x * mA.astype(dtype), x * mB.astype(dtype)], axis=0)


def _rwkv_scores(cs, lw, cs_end, r, k, kap, bet, mA, mB, dtype=F32):
    e_neg = jnp.exp(-cs)
    e_rem = jnp.exp(cs_end - cs)
    rg = _stack(r * jnp.exp(cs), mA, mB, dtype)
    kg = _stack(kap * jnp.exp(cs - lw), mA, mB, dtype)
    kd = _stack(k * e_neg, mA, mB, dtype)
    bd = _stack(bet * e_neg, mA, mB, dtype)
    kh = _stack(k * e_rem, mA, mB, dtype)
    bh = _stack(bet * e_rem, mA, mB, dtype)
    return rg, kg, kd, bd, kh, bh


def _rwkv_epilogue(o, r, k, v, g, lnw, lnb, rk):
    avg = jnp.where(_iota2((LANES, LANES), 0) // RWKV_N == _iota2((LANES, LANES), 1) // RWKV_N,
                    1.0 / RWKV_N, 0.0).astype(F32)
    mean = _dot_exact_rhs(o, avg)
    d = o - mean
    var = _dot_exact_rhs(d * d, avg)
    on = d * lax.rsqrt(var + LNX_EPS) * lnw + lnb
    bonus = _dot_exact_rhs(r * k * rk, avg) * float(RWKV_N) * v
    return (on + bonus) * g


def _rwkv_chunks(r_ref, lw_ref, k_ref, v_ref, kap_ref, bet_ref, g_ref, lnw_ref, lnb_ref, rk_ref,
                 o_ref, st_ref, offs, hp):
    C, c = CHUNK, RWKV_SUB
    C2 = 2 * C
    mA, mB = _pair_masks()
    rr, cc = _iota2((C2, C2), 0), _iota2((C2, C2), 1)
    eye = (rr == cc).astype(F32)
    strict = rr > cc
    incl = rr >= cc
    blk = (rr // c) == (cc // c)
    tri = (_iota2((C, C), 0) >= _iota2((C, C), 1)).astype(F32)
    avg = jnp.where(_iota2((LANES, LANES), 0) // RWKV_N == _iota2((LANES, LANES), 1) // RWKV_N,
                    1.0 / RWKV_N, 0.0).astype(F32)
    chains = [(r0, p) for r0 in offs for p in range(hp)]
    J = range(len(chains))
    pcols = [slice(p * LANES, (p + 1) * LANES) for _, p in chains]

    ld = lambda ref: [ref[pl.ds(r0, C), pcols[j]] for j, (r0, _) in zip(J, chains)]
    r, lw, k, v, kap, bet = ld(r_ref), ld(lw_ref), ld(k_ref), ld(v_ref), ld(kap_ref), ld(bet_ref)
    yield
    cs = [_dot_exact_lhs(tri, lw[j]) for j in J]
    cs_end = [cs[j][C - 1:C, :] for j in J]
    yield
    sc = [_rwkv_scores(cs[j], lw[j], cs_end[j], r[j], k[j], kap[j], bet[j], mA, mB, BF16) for j in J]
    rg, kg, kd, bd, kh, bh = ([s[i] for s in sc] for i in range(6))
    vs = [_stack(v[j], mA, mB, BF16) for j in J]
    kbd = [jnp.concatenate([kd[j], bd[j]], axis=0) for j in J]
    yield
    sck = [_dot(kg[j], kbd[j], _NT) for j in J]
    yield
    scr = [_dot(rg[j], kbd[j], _NT) for j in J]
    Ak = [jnp.where(strict, sck[j][:, :C2], 0.0) for j in J]
    Ab = [jnp.where(strict, sck[j][:, C2:], 0.0) for j in J]
    AkV = [_dot(Ak[j], vs[j]) for j in J]
    yield
    Bkb = [jnp.concatenate([jnp.where(incl, scr[j][:, :C2], 0.0),
                            jnp.where(incl, -scr[j][:, C2:], 0.0)], axis=1).astype(BF16) for j in J]
    LD = [jnp.where(blk, Ab[j], 0.0) for j in J]
    X = LD
    Dinv = [eye - LD[j] for j in J]
    for _ in range(int(math.log2(c)) - 1):
        X = [_dot(X[j], X[j]) for j in J]
        yield
        Dinv = [Dinv[j] + _dot(Dinv[j], X[j]) for j in J]
        yield
    Q = [_dot(Dinv[j], Ab[j] - LD[j]) for j in J]
    yield
    T0 = [Dinv[j] - _dot(Q[j], Dinv[j]) for j in J]
    yield
    for _ in range(int(math.log2(C // c)) - 1):
        Q = [_dot(Q[j], Q[j]) for j in J]
        yield
        T0 = [T0[j] + _dot(Q[j], T0[j]) for j in J]
        yield
    E = [(eye - T0[j]) - _dot_hp(Ab[j], T0[j]) for j in J]
    yield
    Tm = [T0[j] + _dot(T0[j], E[j]) for j in J]
    yield
    khb = [jnp.concatenate([kh[j], -bh[j]], axis=0) for j in J]
    dec = [jnp.exp(cs_end[j]) for j in J]
    yield

    for u in range(len(offs)):
        K = range(u * hp, (u + 1) * hp)
        S = {j: st_ref[chains[j][1]] for j in K}
        rhs = {j: _dot(kg[j], S[j], _NT) + AkV[j] for j in K}
        yield
        U = {j: _dot_hp(Tm[j], rhs[j]) for j in K}
        yield
        vu = {j: jnp.concatenate([vs[j], U[j].astype(BF16)], axis=0) for j in K}
        for j in K:
            st_ref[chains[j][1]] = S[j] * dec[j] + _dot(vu[j], khb[j], _TN)
        yield
        O = {j: _dot(rg[j], S[j], _NT) + _dot(Bkb[j], vu[j]) for j in K}
        o = {j: O[j][:C, :] + O[j][C:, :] for j in K}
        yield
        mean = {j: _dot_exact_rhs(o[j], avg, parts=2) for j in K}
        d = {j: o[j] - mean[j] for j in K}
        yield
        var = {j: _dot_exact_rhs(d[j] * d[j], avg, parts=2) for j in K}
        rkk = {j: r_ref[pl.ds(chains[j][0], C), pcols[j]] * k_ref[pl.ds(chains[j][0], C), pcols[j]]
               * rk_ref[:, pcols[j]] for j in K}
        yield
        bsum = {j: _dot_exact_rhs(rkk[j], avg, parts=2) for j in K}
        yield
        for j in K:
            rows = pl.ds(chains[j][0], C)
            on = d[j] * lax.rsqrt(var[j] + LNX_EPS) * lnw_ref[:, pcols[j]] + lnb_ref[:, pcols[j]]
            bonus = bsum[j] * float(RWKV_N) * v_ref[rows, pcols[j]]
            o_ref[rows, pcols[j]] = (on + bonus) * g_ref[rows, pcols[j]]
        yield


def _interleave(gens):
    live = list(gens)
    while live:
        for gen in list(live):
            try:
                next(gen)
            except StopIteration:
                live.remove(gen)


def _scan_prompt_kernel(q_ref, f_ref, i_ref, og_ref, lb_ref, nw_ref,
                        r_ref, lw_ref, k_ref, v_ref, kap_ref, bet_ref, g_ref, lnw_ref, lnb_ref, rk_ref,
                        oa_ref, hs_ref, ob_ref, rs_ref, hst_ref, rst_ref, *, nchunks, hh, hp):
    tt = pl.program_id(1)

    @pl.when(tt == 0)
    def _():
        hst_ref[...] = jnp.zeros_like(hst_ref)
        rst_ref[...] = jnp.zeros_like(rst_ref)

    C = CHUNK
    unroll = 2 if nchunks % 2 == 0 else 1

    def body(ci, carry):
        base = ci * (unroll * C)
        offs = [pl.multiple_of(base + u * C, C) for u in range(unroll)]
        _interleave([
            _rwkv_chunks(r_ref, lw_ref, k_ref, v_ref, kap_ref, bet_ref, g_ref, lnw_ref, lnb_ref, rk_ref,
                         ob_ref, rst_ref, offs, hp),
            _hgrn_chunks(q_ref, f_ref, i_ref, og_ref, lb_ref, nw_ref, oa_ref, hst_ref, offs, hh),
        ])
        return carry

    lax.fori_loop(0, nchunks // unroll, body, 0)

    @pl.when(tt == pl.num_programs(1) - 1)
    def _():
        for j in range(hh):
            hs_ref[j] = hst_ref[j].T
        for p in range(hp):
            S = rst_ref[p]
            rs_ref[2 * p] = S[:RWKV_N, :RWKV_N]
            rs_ref[2 * p + 1] = S[RWKV_N:, RWKV_N:]


def _scan_prompt(P_h, lb_logits, nw, arrs, g, lnw, lnb, rk, B, T, tt_rows=256):
    R, D = arrs[0].shape
    hh = P_h.shape[1] // (4 * HGRN_DK)
    hp = D // LANES
    wh = hh * HGRN_DK
    tt_rows = min(tt_rows, T)
    nt = T // tt_rows
    hcol = lambda off: pl.BlockSpec((tt_rows, wh), lambda b, t: (b * nt + t, off))
    col = pl.BlockSpec((tt_rows, D), lambda b, t: (b * nt + t, 0))
    const = lambda a: pl.BlockSpec(a.shape, lambda b, t: (0,) * a.ndim)
    nw = nw.reshape(1, HGRN_DK)
    lnw, lnb, rk = lnw.reshape(1, D), lnb.reshape(1, D), rk.reshape(1, D)
    oa, hs, ob, rs = pl.pallas_call(
        functools.partial(_scan_prompt_kernel, nchunks=tt_rows // CHUNK, hh=hh, hp=hp),
        grid=(B, nt),
        in_specs=[hcol(0), hcol(1), hcol(2), hcol(3), const(lb_logits), const(nw)]
                 + [col] * 7 + [const(lnw), const(lnb), const(rk)],
        out_specs=[pl.BlockSpec((tt_rows, wh), lambda b, t: (b * nt + t, 0)),
                   pl.BlockSpec((None, hh, HGRN_DK, HGRN_DK), lambda b, t: (b, 0, 0, 0)),
                   col,
                   pl.BlockSpec((None, 2 * hp, RWKV_N, RWKV_N), lambda b, t: (b, 0, 0, 0))],
        out_shape=[jax.ShapeDtypeStruct((R, wh), F32),
                   jax.ShapeDtypeStruct((B, hh, HGRN_DK, HGRN_DK), F32),
                   jax.ShapeDtypeStruct((R, D), F32),
                   jax.ShapeDtypeStruct((B, 2 * hp, RWKV_N, RWKV_N), F32)],
        scratch_shapes=[pltpu.VMEM((hh, HGRN_DK, HGRN_DK), F32), pltpu.VMEM((hp, LANES, LANES), F32)],
        compiler_params=_cparams(("parallel", "arbitrary")),
        name="scan_prompt",
    )(P_h, P_h, P_h, P_h, lb_logits, nw, *arrs, g, lnw, lnb, rk)
    return oa, hs, ob, rs


def _rwkv_sample_kernel(r_ref, lw_ref, k_ref, v_ref, kap_ref, bet_ref, g_ref, lnw_ref, lnb_ref, rk_ref,
                        s0_ref, o_ref, s_ref, sbd_ref, *, nseq, T, hp):
    C = nseq * T
    C2 = 2 * C
    N = RWKV_N
    mA, mB = _pair_masks()
    bdm = (_iota2((LANES, LANES), 0) // N == _iota2((LANES, LANES), 1) // N).astype(F32)
    rr, cc = _iota2((C2, C2), 0), _iota2((C2, C2), 1)
    eye = (rr == cc).astype(F32)
    same = (rr // T) == (cc // T)
    strict = same & (rr > cc)
    incl = same & (rr >= cc)
    r1, c1 = _iota2((C, C), 0), _iota2((C, C), 1)
    seg_incl = ((r1 // T == c1 // T) & (r1 >= c1)).astype(F32)
    seg_all = (r1 // T == c1 // T).astype(F32)
    zero = jnp.zeros((N, N), F32)

    J = range(hp)
    cols = [slice(j * LANES, (j + 1) * LANES) for j in J]
    ld = lambda ref: [ref[:, cols[j]] for j in J]
    r, lw, k, v, kap, bet = ld(r_ref), ld(lw_ref), ld(k_ref), ld(v_ref), ld(kap_ref), ld(bet_ref)
    cs = [_dot_exact_lhs(seg_incl, lw[j]) for j in J]
    cs_end = [_dot_exact_lhs(seg_all, lw[j]) for j in J]
    sc = [_rwkv_scores(cs[j], lw[j], cs_end[j], r[j], k[j], kap[j], bet[j], mA, mB) for j in J]
    rg, kg, kd, bd, kh, bh = ([s[i] for s in sc] for i in range(6))
    vs = [_stack(v[j], mA, mB) for j in J]
    kbd = [jnp.concatenate([kd[j], bd[j]], axis=0) for j in J]
    sck = [_dot(kg[j], kbd[j], _NT) for j in J]
    scr = [_dot(rg[j], kbd[j], _NT) for j in J]
    Ak = [jnp.where(strict, sck[j][:, :C2], 0.0) for j in J]
    Ab = [jnp.where(strict, sck[j][:, C2:], 0.0) for j in J]
    Bkb = [jnp.concatenate([jnp.where(incl, scr[j][:, :C2], 0.0),
                            jnp.where(incl, -scr[j][:, C2:], 0.0)], axis=1) for j in J]
    Tm = [_neumann_inv(Ab[j], eye, int(math.log2(T)) - 1) for j in J]
    kgp = [kg[j][:C, :] + kg[j][C:, :] for j in J]
    rgp = [rg[j][:C, :] + rg[j][C:, :] for j in J]
    ks = [[] for _ in J]
    rs = [[] for _ in J]
    for n in range(nseq):
        sl = slice(n * T, (n + 1) * T)
        for j in J:
            Sbd = jnp.concatenate(
                [jnp.concatenate([s0_ref[n, 2 * j], zero], axis=1),
                 jnp.concatenate([zero, s0_ref[n, 2 * j + 1]], axis=1)], axis=0)
            sbd_ref[n, j] = Sbd
            res = _dot(jnp.concatenate([kgp[j][sl], rgp[j][sl]], axis=0), Sbd, _NT)
            ks[j].append(res[:T])
            rs[j].append(res[T:])
    KS = [_stack(jnp.concatenate(ks[j], axis=0), mA, mB) for j in J]
    RS = [_stack(jnp.concatenate(rs[j], axis=0), mA, mB) for j in J]
    rhs = [KS[j] + _dot(Ak[j], vs[j]) for j in J]
    U = [_dot_hp(Tm[j], rhs[j]) for j in J]
    vu = [jnp.concatenate([vs[j], U[j]], axis=0) for j in J]
    O = [RS[j] + _dot(Bkb[j], vu[j]) for j in J]
    Up = [U[j][:C, :] + U[j][C:, :] for j in J]
    khp = [kh[j][:C, :] + kh[j][C:, :] for j in J]
    nbhp = [-(bh[j][:C, :] + bh[j][C:, :]) for j in J]
    dec = [jnp.exp(cs_end[j]) for j in J]
    for n in range(nseq):
        sl = slice(n * T, (n + 1) * T)
        for j in J:
            upd = _dot(jnp.concatenate([v[j][sl], Up[j][sl]], axis=0),
                       jnp.concatenate([khp[j][sl], nbhp[j][sl]], axis=0), _TN)
            Sn = sbd_ref[n, j] * dec[j][n * T:n * T + 1, :] + upd * bdm
            s_ref[n, 2 * j] = Sn[:N, :N]
            s_ref[n, 2 * j + 1] = Sn[N:, N:]
    for j in J:
        o = O[j][:C, :] + O[j][C:, :]
        o_ref[:, cols[j]] = _rwkv_epilogue(o, r[j], k[j], v[j], g_ref[:, cols[j]],
                                           lnw_ref[:, cols[j]], lnb_ref[:, cols[j]], rk_ref[:, cols[j]])


def _rwkv_sample(arrs, g, lnw, lnb, rk, s0, T, nseq=16, hp=4):
    R, D = arrs[0].shape
    NP = D // LANES
    B = R // T
    rows = nseq * T
    wd = hp * LANES
    col = pl.BlockSpec((rows, wd), lambda p, gi: (gi, p))
    par = pl.BlockSpec((1, wd), lambda p, gi: (0, p))
    sspec = pl.BlockSpec((nseq, 2 * hp, RWKV_N, RWKV_N), lambda p, gi: (gi, p, 0, 0))
    o, s = pl.pallas_call(
        functools.partial(_rwkv_sample_kernel, nseq=nseq, T=T, hp=hp),
        grid=(NP // hp, B // nseq),
        in_specs=[col] * 7 + [par] * 3 + [sspec],
        out_specs=[col, sspec],
        out_shape=[jax.ShapeDtypeStruct((R, D), F32), jax.ShapeDtypeStruct(s0.shape, F32)],
        scratch_shapes=[pltpu.VMEM((nseq, hp, LANES, LANES), F32)],
        compiler_params=_cparams(("parallel", "parallel")),
        name="rwkv_sample",
    )(*arrs, g, lnw.reshape(1, D), lnb.reshape(1, D), rk.reshape(1, D), s0)
    return o, s


def _merge_kernel(x_ref, h_ref, oa_ref, ob_ref, wg_ref, w_ref, gt_ref, o_ref, *, rep):
    D = x_ref.shape[1]
    pg = lax.dot_general(h_ref[...], wg_ref[...], _NT, preferred_element_type=F32)
    m = _sigmoid(pg[:, :D]) * oa_ref[...] + _sigmoid(pg[:, D:]) * ob_ref[...]
    o_ref[...] = x_ref[...] + _rep_rows(gt_ref[...], rep) * _dot(m, w_ref[...])


def _merge_out(x, h, o_a, o_b, w_g, w_out, mod, tm):
    R, D = x.shape
    row = pl.BlockSpec((tm, D), lambda i: (i, 0))
    const = lambda a: pl.BlockSpec(a.shape, lambda i: (0,) * a.ndim, pipeline_mode=pl.Buffered(1))
    return pl.pallas_call(
        functools.partial(_merge_kernel, rep=mod.rep),
        grid=(R // tm,),
        in_specs=[row, row, row, row, const(w_g), const(w_out), mod.spec(tm, 2)],
        out_specs=row,
        out_shape=jax.ShapeDtypeStruct((R, D), F32),
        compiler_params=_cparams(("parallel",)),
        name="merge_out",
    )(x, h, o_a, o_b, w_g, w_out, mod.arr)


def _mlp_kernel(x_ref, nw_ref, sh_ref, sc_ref, gt_ref, wu_ref, wd_ref, fw_ref, o_ref, *, tf, rep):
    x = x_ref[...]
    h = (_rms(x, nw_ref[...]) * (1.0 + _rep_rows(sc_ref[...], rep)) + _rep_rows(sh_ref[...], rep)).astype(BF16)
    acc = None
    for c in range(wu_ref.shape[1] // tf):
        u = jnp.dot(h, wu_ref[:, c * tf:(c + 1) * tf], preferred_element_type=F32)
        u = jnp.square(jnp.maximum(u, 0.0))
        part = _dot(u, wd_ref[c * tf:(c + 1) * tf, :])
        acc = part if acc is None else acc + part
    o_ref[...] = _rms(x + _rep_rows(gt_ref[...], rep) * acc, fw_ref[...])


def _mlp(x, nw, w_up, w_down, fw, mod, tm, tf=1024):
    R, D = x.shape
    const = lambda a: pl.BlockSpec(a.shape, lambda i: (0,) * a.ndim, pipeline_mode=pl.Buffered(1))
    row = pl.BlockSpec((tm, D), lambda i: (i, 0))
    nw, fw = nw.reshape(1, D), fw.reshape(1, D)
    return pl.pallas_call(
        functools.partial(_mlp_kernel, tf=tf, rep=mod.rep),
        grid=(R // tm,),
        in_specs=[row, const(nw), mod.spec(tm, 3), mod.spec(tm, 4), mod.spec(tm, 5),
                  const(w_up), const(w_down), const(fw)],
        out_specs=row,
        out_shape=jax.ShapeDtypeStruct((R, D), F32),
        compiler_params=_cparams(("parallel",)),
        name="mlp",
    )(x, nw, mod.arr, mod.arr, mod.arr, w_up, w_down, fw)


def _run(x3, mod_rows, s_hgrn, s_rwkv, s_shift, W, prompt):
    B, T, D = x3.shape
    R = B * T
    x = x3.reshape(R, D)
    if prompt:
        mod = _Mod(mod_rows.reshape(B, 1, -1), True, T, D)
        tm = min(1024, T)
    else:
        mod = _Mod(mod_rows, False, T, D)
        tm = min(512, R)

    rw = (W["mu"], W["w0"], W["a0"], W["k_k"], W["k_a"], W["wwa"], W["g2p"])
    nw1 = W["norm1_w"].reshape(1, D)
    if prompt:
        shift_in = jnp.zeros((B, 1, RC_PAD), F32)
        h, arrs, last = _in_rwkv(x, nw1, mod, W["w_r"], shift_in, rw, True, T)
        shift_out = last[:, 0, :RC_COLS]
        P_h = _matmul(h, W["w_in_t"], 4 * D, tm, 2048, "proj_h")
        o_a, hg, o_b, rs = _scan_prompt(P_h, W["lb_logits"], W["hgrn_norm_w"], arrs[:6], arrs[6],
                                        W["lnx_w"], W["lnx_b"], W["r_k"], B, T)
    else:
        shift_in = jnp.pad(s_shift, ((0, 0), (0, RC_PAD - RC_COLS)))
        h, arrs, last = _in_rwkv(x, nw1, mod, W["w_r"], shift_in, rw, False, T)
        shift_out = last[:, :RC_COLS]
        P_h = _matmul(h, W["w_in_t"], 4 * D, tm, 2048, "proj_h")
        o_a, hg = _hgrn_sample(P_h, W["lb_logits"], W["hgrn_norm_w"], s_hgrn, T)
        o_b, rs = _rwkv_sample(arrs[:6], arrs[6], W["lnx_w"], W["lnx_b"], W["r_k"], s_rwkv, T)

    x1 = _merge_out(x, h, o_a, o_b, W["w_g"], W["w_out"], mod, min(tm, 512))
    y = _mlp(x1, W["norm2_w"], W["w_up"], W["w_down"], W["final_norm_w"], mod, min(tm, 512))
    return y.reshape(B, T, D), hg, rs, shift_out


def kernel(x_prompt, x_sample, state_hgrn, state_rwkv, state_shift, c_prompt, c_sample, norm1_w, norm2_w, ada_w, ada_b, w_in, lb_logits, hgrn_norm_w, rwkv_mu, rwkv_w0, rwkv_w2, rwkv_a0, rwkv_a2, rwkv_g2, rwkv_k_k, rwkv_k_a, rwkv_r_k, rwkv_lnx_w, rwkv_lnx_b, w_out, w_up, w_down, final_norm_w):
    depth = norm1_w.shape[0]
    assert depth == 1, "single-layer step"
    D = x_prompt.shape[-1]
    HW = 4 * D
    w_in_t = w_in[0].T.astype(BF16)
    row = lambda a: a.reshape(1, -1)
    wwa = jnp.zeros((LANES, 2 * D), F32)
    wwa = wwa.at[:W_LORA, :D].set(rwkv_w2[0]).at[W_LORA:W_LORA + A_LORA, D:].set(rwkv_a2[0])
    W = {
        "norm1_w": norm1_w[0], "norm2_w": norm2_w[0], "final_norm_w": final_norm_w,
        "w_in_t": w_in_t,
        "w_r": jnp.pad(w_in[0][:, HW:HW + RC_COLS], ((0, 0), (0, RC_PAD - RC_COLS))).astype(BF16),
        "w_g": w_in_t[HW + RC_COLS:],
        "lb_logits": lb_logits, "hgrn_norm_w": hgrn_norm_w[0],
        "mu": jnp.pad(row(rwkv_mu[0]), ((0, 0), (0, RC_PAD - RC_COLS))),
        "w0": row(rwkv_w0[0]), "a0": row(rwkv_a0[0]), "k_k": row(rwkv_k_k[0]), "k_a": row(rwkv_k_a[0]),
        "wwa": wwa.astype(BF16),
        "g2p": jnp.pad(rwkv_g2[0], ((0, G_LORA_PAD - G_LORA), (0, 0))).astype(BF16),
        "lnx_w": rwkv_lnx_w[0], "lnx_b": rwkv_lnx_b[0], "r_k": rwkv_r_k[0].reshape(-1),
        "w_out": w_out[0].astype(BF16), "w_up": w_up[0].astype(BF16), "w_down": w_down[0].astype(BF16),
    }
    nb = x_prompt.shape[0]
    mod = _adaln(jnp.concatenate([c_prompt, c_sample], axis=0), ada_w[0], ada_b[0])
    y_p, hg_p, rw_p, sh_p = _run(x_prompt, mod[:nb], None, None, None, W, True)
    y_s, hg_s, rw_s, sh_s = _run(x_sample, mod[nb:], state_hgrn[0], state_rwkv[0], state_shift[0], W, False)
    ex = lambda a: a[None]
    return (y_p, y_s, ex(hg_p), ex(rw_p), ex(sh_p), ex(hg_s), ex(rw_s), ex(sh_s))
```

```python
import functools
import math

import jax
import jax.numpy as jnp
from jax import lax
from jax.experimental import pallas as pl
from jax.experimental.pallas import tpu as pltpu

F32 = jnp.float32
BF16 = jnp.bfloat16

LANES = 128
SUBLANES = 8
VMEM_LIMIT = 48 * 1024 * 1024

HGRN_DK = 128
RWKV_N = 64
W_LORA = 64
A_LORA = 64
G_LORA = 160
G_LORA_PAD = 256
RC_MAIN = 3 * 1024
RC_COLS = RC_MAIN + W_LORA + A_LORA + G_LORA
RC_PAD = RC_MAIN + LANES + G_LORA_PAD
RMS_EPS = 1e-6
LNX_EPS = 64e-5
CHUNK = 64
SUB = 8
RWKV_SUB = 8


def _cparams(sem, vmem=VMEM_LIMIT):
    return pltpu.CompilerParams(dimension_semantics=sem, vmem_limit_bytes=vmem)


_NN = (((1,), (0,)), ((), ()))
_NT = (((1,), (1,)), ((), ()))
_TN = (((0,), (0,)), ((), ()))


def _dot(a, b, dims=_NN):
    return lax.dot_general(a.astype(BF16), b.astype(BF16), dims, preferred_element_type=F32)


def _split2(x):
    hi = x.astype(BF16)
    lo = (x - hi.astype(F32)).astype(BF16)
    return hi, lo


def _split3(x):
    hi = x.astype(BF16)
    r1 = x - hi.astype(F32)
    mid = r1.astype(BF16)
    lo = (r1 - mid.astype(F32)).astype(BF16)
    return hi, mid, lo


def _dot_exact_lhs(a, x, dims=_NN):
    a = a.astype(BF16)
    hi, mid, lo = _split3(x)
    d = lambda p: lax.dot_general(a, p, dims, preferred_element_type=F32)
    return d(hi) + d(mid) + d(lo)


def _dot_exact_rhs(x, b, dims=_NN, parts=3):
    b = b.astype(BF16)
    d = lambda p: lax.dot_general(p, b, dims, preferred_element_type=F32)
    if parts == 2:
        hi, lo = _split2(x)
        return d(hi) + d(lo)
    hi, mid, lo = _split3(x)
    return d(hi) + d(mid) + d(lo)


def _dot_hp(a, b, dims=_NN):
    ah, al = _split2(a)
    bh, bl = _split2(b)
    d = lambda p, q: lax.dot_general(p, q, dims, preferred_element_type=F32)
    return d(ah, bh) + d(ah, bl) + d(al, bh)


def _sigmoid(x):
    return 1.0 / (1.0 + jnp.exp(-x))


def _iota2(shape, dim):
    return lax.broadcasted_iota(jnp.int32, shape, dim)


def _neumann_inv(L, eye, nsq):
    X = L
    P = eye - L
    for _ in range(nsq):
        X = _dot_hp(X, X)
        P = P + _dot_hp(P, X)
    return P


def _adaln_kernel(c_ref, w_ref, b_ref, o_ref):
    c = c_ref[...]
    s = c * _sigmoid(c)
    o_ref[...] = _dot(s, w_ref[...]) + b_ref[...]


def _adaln(c_all, ada_w, ada_b, tn=1536):
    R, D = c_all.shape
    N = ada_w.shape[1]
    return pl.pallas_call(
        _adaln_kernel,
        grid=(N // tn,),
        in_specs=[pl.BlockSpec((R, D), lambda j: (0, 0)),
                  pl.BlockSpec((D, tn), lambda j: (0, j)),
                  pl.BlockSpec((1, tn), lambda j: (0, j))],
        out_specs=pl.BlockSpec((R, tn), lambda j: (0, j)),
        out_shape=jax.ShapeDtypeStruct((R, N), F32),
        compiler_params=_cparams(("parallel",)),
        name="adaln",
    )(c_all, ada_w, ada_b.reshape(1, N))


class _Mod:
    def __init__(self, arr, per_seq, seq_rows, D):
        self.arr, self.per_seq, self.seq_rows, self.D = arr, per_seq, seq_rows, D
        self.rep = 1 if per_seq else seq_rows

    def spec(self, tm, comp):
        D = self.D
        if self.per_seq:
            tiles = self.seq_rows // tm
            return pl.BlockSpec((None, 1, D), lambda i, *_: (i // tiles, 0, comp))
        return pl.BlockSpec((tm // self.seq_rows, D), lambda i, *_: (i, comp))


def _rep_rows(v, rep):
    if rep == 1:
        return v
    n, w = v.shape
    return jnp.broadcast_to(v[:, None, :], (n, rep, w)).reshape(n * rep, w)


def _rms(x, w):
    return x * lax.rsqrt(jnp.mean(x * x, axis=-1, keepdims=True) + RMS_EPS) * w


def _mm_kernel(x_ref, wt_ref, o_ref):
    o_ref[...] = lax.dot_general(x_ref[...], wt_ref[...], _NT, preferred_element_type=F32)


def _matmul(x, wt, n_out, tm, tn, name):
    R, K = x.shape
    N = n_out
    return pl.pallas_call(
        _mm_kernel,
        grid=(R // tm, N // tn),
        in_specs=[pl.BlockSpec((tm, K), lambda i, j: (i, 0)),
                  pl.BlockSpec((tn, K), lambda i, j: (j, 0))],
        out_specs=pl.BlockSpec((tm, tn), lambda i, j: (i, j)),
        out_shape=jax.ShapeDtypeStruct((R, N), F32),
        compiler_params=_cparams(("parallel", "arbitrary")),
        name=name,
    )(x, wt)


def _hgrn_gates(fp, lb_logits):
    m = jnp.max(lb_logits, axis=0, keepdims=True)
    e = jnp.exp(lb_logits - m)
    lb = e[0:1, :] / jnp.sum(e, axis=0, keepdims=True)
    f = lb + (1.0 - lb) * _sigmoid(fp)
    return jnp.log(f), 1.0 - f


def _hgrn_diag(q, kin, b, c):
    R = q.shape[0]
    nb = R // c
    q3 = q.reshape(nb, c, LANES)
    k3 = kin.reshape(nb, c, LANES)
    b3 = (b * math.log2(math.e)).reshape(nb, c, LANES)
    lane_mod = lax.broadcasted_iota(jnp.int32, (nb, c, R), 2) % c
    acc = jnp.zeros((nb, c, R), F32)
    for s in range(c):
        d = jnp.minimum(b3 - b3[:, s:s + 1, :], 0.0)
        col = jnp.sum(q3 * k3[:, s:s + 1, :] * jnp.exp2(d), axis=-1, keepdims=True)
        acc = jnp.where(lane_mod == s, col, acc)
    rr, cc = _iota2((R, R), 0), _iota2((R, R), 1)
    keep = ((rr // c) == (cc // c)) & (rr >= cc)
    return jnp.where(keep, acc.reshape(R, R), 0.0)


def _hgrn_cross(q, kin, b, m):
    R = q.shape[0]
    h = m // 2
    nb = R // m
    ref = jnp.broadcast_to(b.reshape(nb, m, LANES)[:, h - 1:h, :], (nb, m, LANES)).reshape(R, LANES)
    upper = (_iota2((R, 1), 0) % m) >= h
    qs = q * jnp.exp(jnp.where(upper, b - ref, -1e30))
    ks = kin * jnp.exp(jnp.where(upper, -1e30, ref - b))
    sc = _dot(qs, ks, _NT)
    if nb == 1:
        return sc
    same_blk = (_iota2((R, R), 0) // m) == (_iota2((R, R), 1) // m)
    return jnp.where(same_blk, sc, 0.0)


def _hgrn_out(o, og, nw):
    o = o * lax.rsqrt(jnp.mean(o * o, axis=-1, keepdims=True) + RMS_EPS) * nw
    return o * (og * _sigmoid(og))


def _hgrn_chunks(p_ref, lb_ref, nw_ref, o_ref, st_ref, offs, hp):
    C, c = CHUNK, SUB
    nb = C // c
    W = hp * LANES
    tri = (_iota2((C, C), 0) >= _iota2((C, C), 1)).astype(F32)
    nw = nw_ref[...]
    H = range(hp)
    cols = [slice(j * LANES, (j + 1) * LANES) for j in H]
    pcol = lambda g, j: slice(g * W + j * LANES, g * W + (j + 1) * LANES)
    lane_mod = lax.broadcasted_iota(jnp.int32, (nb, c, C), 2) % c
    rr, cc = _iota2((C, C), 0), _iota2((C, C), 1)
    keep = ((rr // c) == (cc // c)) & (rr >= cc)
    for r0 in offs:
        q = [p_ref[pl.ds(r0, C), pcol(0, j)] for j in H]
        v = [p_ref[pl.ds(r0, C), pcol(2, j)] for j in H]
        gates = [_hgrn_gates(p_ref[pl.ds(r0, C), pcol(1, j)], lb_ref[:, cols[j]]) for j in H]
        lf = [gt[0] for gt in gates]
        kin = [gt[1] for gt in gates]
        yield
        b = [_dot_exact_lhs(tri, lf[j]) for j in H]
        b_last = [b[j][C - 1:C, :] for j in H]
        yield
        ST = [st_ref[j] for j in H]
        o = [_dot(q[j] * jnp.exp(b[j]), ST[j], _NT) for j in H]
        yield
        for j in H:
            st_ref[j] = ST[j] * jnp.exp(b_last[j]) + _dot(v[j], kin[j] * jnp.exp(b_last[j] - b[j]), _TN)
        yield
        q3 = [q[j].reshape(nb, c, LANES) for j in H]
        k3 = [kin[j].reshape(nb, c, LANES) for j in H]
        b3 = [(b[j] * math.log2(math.e)).reshape(nb, c, LANES) for j in H]
        acc = [jnp.zeros((nb, c, C), F32) for _ in H]
        for s in range(c):
            for j in H:
                d = jnp.minimum(b3[j] - b3[j][:, s:s + 1, :], 0.0)
                col = jnp.sum(q3[j] * k3[j][:, s:s + 1, :] * jnp.exp2(d), axis=-1, keepdims=True)
                acc[j] = jnp.where(lane_mod == s, col, acc[j])
            yield
        A = [jnp.where(keep, acc[j].reshape(C, C), 0.0) for j in H]
        m = 2 * c
        while m <= C:
            A = [A[j] + _hgrn_cross(q[j], kin[j], b[j], m) for j in H]
            m *= 2
            yield
        o = [o[j] + _dot(A[j], v[j]) for j in H]
        yield
        for j in H:
            o_ref[pl.ds(r0, C), cols[j]] = _hgrn_out(o[j], p_ref[pl.ds(r0, C), pcol(3, j)], nw)
        yield


def _hgrn_sample_kernel(q_ref, f_ref, i_ref, g_ref, lb_ref, nw_ref, s0_ref, o_ref, s_ref, *, nseq, T, hp):
    R = nseq * T
    rr, cc = _iota2((R, R), 0), _iota2((R, R), 1)
    same = (rr // T) == (cc // T)
    seg_incl = (same & (rr >= cc)).astype(F32)
    seg_all = same.astype(F32)
    nw = nw_ref[...]
    H = range(hp)
    cols = [slice(j * LANES, (j + 1) * LANES) for j in H]
    q = [q_ref[:, cols[j]] for j in H]
    v = [i_ref[:, cols[j]] for j in H]
    gates = [_hgrn_gates(f_ref[:, cols[j]], lb_ref[:, cols[j]]) for j in H]
    lf = [gt[0] for gt in gates]
    kin = [gt[1] for gt in gates]
    b = [_dot_exact_lhs(seg_incl, lf[j]) for j in H]
    b_tot = [_dot_exact_lhs(seg_all, lf[j]) for j in H]
    A = [_hgrn_diag(q[j], kin[j], b[j], T) for j in H]
    o = [_dot(A[j], v[j]) for j in H]
    qg = [q[j] * jnp.exp(b[j]) for j in H]
    kh = [kin[j] * jnp.exp(b_tot[j] - b[j]) for j in H]
    ecol = [jnp.transpose(jnp.exp(b_tot[j]).reshape(nseq, T, LANES)[:, 0, :]) for j in H]
    inter = [[] for _ in H]
    for n in range(nseq):
        sl = slice(n * T, (n + 1) * T)
        for j in H:
            S = s0_ref[n, j]
            inter[j].append(_dot(qg[j][sl], S))
            s_ref[n, j] = S * ecol[j][:, n:n + 1] + _dot(kh[j][sl], v[j][sl], _TN)
    for j in H:
        oj = o[j] + jnp.concatenate(inter[j], axis=0)
        o_ref[:, cols[j]] = _hgrn_out(oj, g_ref[:, cols[j]], nw)


def _hgrn_sample(P_h, lb_logits, nw, s0, T, nseq=16, hp=8):
    R, W = P_h.shape
    H = W // (4 * HGRN_DK)
    HG = H // hp
    wd = hp * HGRN_DK
    B = R // T
    rows = nseq * T
    col = lambda off: pl.BlockSpec((rows, wd), lambda h, g: (g, off * HG + h))
    sspec = pl.BlockSpec((nseq, hp, HGRN_DK, HGRN_DK), lambda h, g: (g, h, 0, 0))
    o, s = pl.pallas_call(
        functools.partial(_hgrn_sample_kernel, nseq=nseq, T=T, hp=hp),
        grid=(HG, B // nseq),
        in_specs=[col(0), col(1), col(2), col(3),
                  pl.BlockSpec((lb_logits.shape[0], wd), lambda h, g: (0, h)),
                  pl.BlockSpec((1, HGRN_DK), lambda h, g: (0, 0)),
                  sspec],
        out_specs=[pl.BlockSpec((rows, wd), lambda h, g: (g, h)), sspec],
        out_shape=[jax.ShapeDtypeStruct((R, H * HGRN_DK), F32),
                   jax.ShapeDtypeStruct(s0.shape, F32)],
        compiler_params=_cparams(("parallel", "parallel")),
        name="hgrn_sample",
    )(P_h, P_h, P_h, P_h, lb_logits, nw.reshape(1, HGRN_DK), s0)
    return o, s


def _rwkv_prep_body(rc, prev, mu_ref, w0_ref, a0_ref, kk_ref, ka_ref, wwa_ref, g2_ref, outs, rows):
    r_o, lw_o, k_o, v_o, kap_o, bet_o, g_o = outs
    D = RC_MAIN // 3
    xs = rc + mu_ref[...] * (prev - rc)
    r = xs[:, 0:D]
    kb = xs[:, D:2 * D]
    v = xs[:, 2 * D:3 * D]
    xwa = xs[:, RC_MAIN:RC_MAIN + LANES]
    xg = xs[:, RC_MAIN + LANES:]
    lane = _iota2((1, LANES), 1)
    wa = _dot(jnp.where(lane < W_LORA, jnp.tanh(xwa), xwa), wwa_ref[...])
    w = w0_ref[...] + wa[:, :D]
    a = _sigmoid(a0_ref[...] + wa[:, D:])
    lw = -math.exp(-0.5) * _sigmoid(w)
    g = _dot(_sigmoid(xg), g2_ref[...])
    kk = kb * kk_ref[...]
    seg = (_iota2((LANES, LANES), 0) // RWKV_N == _iota2((LANES, LANES), 1) // RWKV_N).astype(F32)
    k_a = ka_ref[...]
    for p in range(D // LANES):
        sl = slice(p * LANES, (p + 1) * LANES)
        kkp = kk[:, sl]
        kap = kkp * lax.rsqrt(jnp.maximum(_dot_exact_rhs(kkp * kkp, seg, parts=2), 1e-24))
        ap = a[:, sl]
        kap_o[rows, sl] = kap
        bet_o[rows, sl] = ap * kap
        k_o[rows, sl] = kb[:, sl] * (1.0 + (ap - 1.0) * k_a[:, sl])
    r_o[rows, :] = r
    lw_o[rows, :] = lw
    v_o[rows, :] = v
    g_o[rows, :] = g


def _in_rwkv_kernel(x_ref, nw_ref, sh_ref, sc_ref, wr_ref, shift_ref, mu_ref, w0_ref, a0_ref, kk_ref,
                    ka_ref, wwa_ref, g2_ref, h_ref, r_o, lw_o, k_o, v_o, kap_o, bet_o, g_o, last_o,
                    carry_ref, rc_ref, *, seq_rows, sub):
    i = pl.program_id(0)
    tm = x_ref.shape[0]
    per_seq = seq_rows >= tm

    @pl.when(i == 0)
    def _():
        carry_ref[...] = jnp.zeros_like(carry_ref)

    rep = 1 if per_seq else seq_rows
    h = (_rms(x_ref[...], nw_ref[...]) * (1.0 + _rep_rows(sc_ref[...], rep))
         + _rep_rows(sh_ref[...], rep)).astype(BF16)
    h_ref[...] = h
    outs = (r_o, lw_o, k_o, v_o, kap_o, bet_o, g_o)
    rid = _iota2((sub, 1), 0)
    rc_ref[...] = jnp.dot(h, wr_ref[...], preferred_element_type=F32)
    prev_last = None
    for s in range(tm // sub):
        rows = slice(s * sub, (s + 1) * sub)
        rc = rc_ref[rows, :]
        rolled = pltpu.roll(rc, 1, axis=0)
        if per_seq:
            if s == 0:
                start = (i % (seq_rows // tm)) == 0
                first = jnp.where(start, shift_ref[...], carry_ref[SUBLANES - 1:SUBLANES, :])
            else:
                first = prev_last
            prev = jnp.where(rid == 0, first, rolled)
        else:
            seqs = slice(s * sub // seq_rows, (s + 1) * sub // seq_rows)
            prev = jnp.where(rid % seq_rows == 0, _rep_rows(shift_ref[seqs, :], seq_rows), rolled)
            last_o[seqs, :] = rc.reshape(sub // seq_rows, seq_rows, RC_PAD)[:, seq_rows - 1, :]
        prev_last = rc[sub - 1:sub, :]
        if s == tm // sub - 1 and per_seq:
            carry_ref[...] = rc[sub - SUBLANES:, :]
            last_o[...] = prev_last
        _rwkv_prep_body(rc, prev, mu_ref, w0_ref, a0_ref, kk_ref, ka_ref, wwa_ref, g2_ref, outs, rows)


def _in_rwkv(x, nw, mod, w_r, shift, params, per_seq, seq_rows, tm=256, sub=128):
    R, D = x.shape
    tm = min(tm, seq_rows) if per_seq else min(tm, R)
    sub = min(sub, tm)
    mu, w0, a0, k_k, k_a, wwa, g2p = params
    const = lambda a: pl.BlockSpec(a.shape, lambda i: (0,) * a.ndim, pipeline_mode=pl.Buffered(1))
    wspecs = [const(a) for a in (mu, w0, a0, k_k, k_a, wwa, g2p)]
    row = lambda w: pl.BlockSpec((tm, w), lambda i: (i, 0))
    if per_seq:
        tiles = seq_rows // tm
        shift_spec = pl.BlockSpec((None, 1, RC_PAD), lambda i: (i // tiles, 0, 0))
        last_spec = pl.BlockSpec((None, 1, RC_PAD), lambda i: (i // tiles, 0, 0))
        last_shape = jax.ShapeDtypeStruct((R // seq_rows, 1, RC_PAD), F32)
    else:
        shift_spec = pl.BlockSpec((tm // seq_rows, RC_PAD), lambda i: (i, 0))
        last_spec = pl.BlockSpec((tm // seq_rows, RC_PAD), lambda i: (i, 0))
        last_shape = jax.ShapeDtypeStruct((R // seq_rows, RC_PAD), F32)
    outs = pl.pallas_call(
        functools.partial(_in_rwkv_kernel, seq_rows=seq_rows, sub=sub),
        grid=(R // tm,),
        in_specs=[row(D), const(nw), mod.spec(tm, 0), mod.spec(tm, 1), const(w_r), shift_spec] + wspecs,
        out_specs=[row(D)] + [row(D)] * 7 + [last_spec],
        out_shape=[jax.ShapeDtypeStruct((R, D), BF16)] + [jax.ShapeDtypeStruct((R, D), F32)] * 7
                  + [last_shape],
        scratch_shapes=[pltpu.VMEM((SUBLANES, RC_PAD), F32), pltpu.VMEM((tm, RC_PAD), F32)],
        compiler_params=_cparams(("arbitrary",)),
        name="in_rwkv",
    )(x, nw, mod.arr, mod.arr, w_r, shift, mu, w0, a0, k_k, k_a, wwa, g2p)
    return outs[0], outs[1:8], outs[8]


def _pair_masks():
    lane = _iota2((1, LANES), 1)
    return (lane < RWKV_N).astype(F32), (lane >= RWKV_N).astype(F32)


def _stack(x, mA, mB, dtype=F32):
    x = x.astype(dtype)
    return jnp.concatenate([x * mA.astype(dtype), x * mB.astype(dtype)], axis=0)


def _rwkv_scores(cs, lw, cs_end, r, k, kap, bet, mA, mB, dtype=F32):
    e_neg = jnp.exp(-cs)
    e_rem = jnp.exp(cs_end - cs)
    rg = _stack(r * jnp.exp(cs), mA, mB, dtype)
    kg = _stack(kap * jnp.exp(cs - lw), mA, mB, dtype)
    kd = _stack(k * e_neg, mA, mB, dtype)
    bd = _stack(bet * e_neg, mA, mB, dtype)
    kh = _stack(k * e_rem, mA, mB, dtype)
    bh = _stack(bet * e_rem, mA, mB, dtype)
    return rg, kg, kd, bd, kh, bh


def _rwkv_epilogue(o, r, k, v, g, lnw, lnb, rk):
    avg = jnp.where(_iota2((LANES, LANES), 0) // RWKV_N == _iota2((LANES, LANES), 1) // RWKV_N,
                    1.0 / RWKV_N, 0.0).astype(F32)
    mean = _dot_exact_rhs(o, avg)
    d = o - mean
    var = _dot_exact_rhs(d * d, avg)
    on = d * lax.rsqrt(var + LNX_EPS) * lnw + lnb
    bonus = _dot_exact_rhs(r * k * rk, avg) * float(RWKV_N) * v
    return (on + bonus) * g


def _rwkv_chunks(r_ref, lw_ref, k_ref, v_ref, kap_ref, bet_ref, g_ref, lnw_ref, lnb_ref, rk_ref,
                 o_ref, st_ref, offs, hp):
    C, c = CHUNK, RWKV_SUB
    C2 = 2 * C
    mA, mB = _pair_masks()
    rr, cc = _iota2((C2, C2), 0), _iota2((C2, C2), 1)
    eye = (rr == cc).astype(F32)
    strict = rr > cc
    incl = rr >= cc
    blk = (rr // c) == (cc // c)
    tri = (_iota2((C, C), 0) >= _iota2((C, C), 1)).astype(F32)
    avg = jnp.where(_iota2((LANES, LANES), 0) // RWKV_N == _iota2((LANES, LANES), 1) // RWKV_N,
                    1.0 / RWKV_N, 0.0).astype(F32)
    chains = [(r0, p) for r0 in offs for p in range(hp)]
    J = range(len(chains))
    pcols = [slice(p * LANES, (p + 1) * LANES) for _, p in chains]

    ld = lambda ref: [ref[pl.ds(r0, C), pcols[j]] for j, (r0, _) in zip(J, chains)]
    r, lw, k, v, kap, bet = ld(r_ref), ld(lw_ref), ld(k_ref), ld(v_ref), ld(kap_ref), ld(bet_ref)
    yield
    cs = [_dot_exact_lhs(tri, lw[j]) for j in J]
    cs_end = [cs[j][C - 1:C, :] for j in J]
    yield
    sc = [_rwkv_scores(cs[j], lw[j], cs_end[j], r[j], k[j], kap[j], bet[j], mA, mB, BF16) for j in J]
    rg, kg, kd, bd, kh, bh = ([s[i] for s in sc] for i in range(6))
    vs = [_stack(v[j], mA, mB, BF16) for j in J]
    kbd = [jnp.concatenate([kd[j], bd[j]], axis=0) for j in J]
    yield
    sck = [_dot(kg[j], kbd[j], _NT) for j in J]
    yield
    scr = [_dot(rg[j], kbd[j], _NT) for j in J]
    Ak = [jnp.where(strict, sck[j][:, :C2], 0.0) for j in J]
    Ab = [jnp.where(strict, sck[j][:, C2:], 0.0) for j in J]
    AkV = [_dot(Ak[j], vs[j]) for j in J]
    yield
    Bkb = [jnp.concatenate([jnp.where(incl, scr[j][:, :C2], 0.0),
                            jnp.where(incl, -scr[j][:, C2:], 0.0)], axis=1).astype(BF16) for j in J]
    LD = [jnp.where(blk, Ab[j], 0.0) for j in J]
    X = LD
    Dinv = [eye - LD[j] for j in J]
    for _ in range(int(math.log2(c)) - 1):
        X = [_dot(X[j], X[j]) for j in J]
        yield
        Dinv = [Dinv[j] + _dot(Dinv[j], X[j]) for j in J]
        yield
    Q = [_dot(Dinv[j], Ab[j] - LD[j]) for j in J]
    yield
    T0 = [Dinv[j] - _dot(Q[j], Dinv[j]) for j in J]
    yield
    for _ in range(int(math.log2(C // c)) - 1):
        Q = [_dot(Q[j], Q[j]) for j in J]
        yield
        T0 = [T0[j] + _dot(Q[j], T0[j]) for j in J]
        yield
    E = [(eye - T0[j]) - _dot_hp(Ab[j], T0[j]) for j in J]
    yield
    Tm = [T0[j] + _dot(T0[j], E[j]) for j in J]
    yield
    khb = [jnp.concatenate([kh[j], -bh[j]], axis=0) for j in J]
    dec = [jnp.exp(cs_end[j]) for j in J]
    yield

    for u in range(len(offs)):
        K = range(u * hp, (u + 1) * hp)
        S = {j: st_ref[chains[j][1]] for j in K}
        rhs = {j: _dot(kg[j], S[j], _NT) + AkV[j] for j in K}
        yield
        U = {j: _dot_hp(Tm[j], rhs[j]) for j in K}
        yield
        vu = {j: jnp.concatenate([vs[j], U[j].astype(BF16)], axis=0) for j in K}
        for j in K:
            st_ref[chains[j][1]] = S[j] * dec[j] + _dot(vu[j], khb[j], _TN)
        yield
        O = {j: _dot(rg[j], S[j], _NT) + _dot(Bkb[j], vu[j]) for j in K}
        o = {j: O[j][:C, :] + O[j][C:, :] for j in K}
        yield
        mean = {j: _dot_exact_rhs(o[j], avg, parts=2) for j in K}
        d = {j: o[j] - mean[j] for j in K}
        yield
        var = {j: _dot_exact_rhs(d[j] * d[j], avg, parts=2) for j in K}
        rkk = {j: r_ref[pl.ds(chains[j][0], C), pcols[j]] * k_ref[pl.ds(chains[j][0], C), pcols[j]]
               * rk_ref[:, pcols[j]] for j in K}
        yield
        bsum = {j: _dot_exact_rhs(rkk[j], avg, parts=2) for j in K}
        yield
        for j in K:
            rows = pl.ds(chains[j][0], C)
            on = d[j] * lax.rsqrt(var[j] + LNX_EPS) * lnw_ref[:, pcols[j]] + lnb_ref[:, pcols[j]]
            bonus = bsum[j] * float(RWKV_N) * v_ref[rows, pcols[j]]
            o_ref[rows, pcols[j]] = (on + bonus) * g_ref[rows, pcols[j]]
        yield


def _interleave(gens):
    live = list(gens)
    while live:
        for gen in list(live):
            try:
                next(gen)
            except StopIteration:
                live.remove(gen)


def _scan_prompt_kernel(p_ref, lb_ref, nw_ref,
                        r_ref, lw_ref, k_ref, v_ref, kap_ref, bet_ref, g_ref, lnw_ref, lnb_ref, rk_ref,
                        oa_ref, hs_ref, ob_ref, rs_ref, hst_ref, rst_ref, *, nchunks, hh, hp):
    tt = pl.program_id(1)

    @pl.when(tt == 0)
    def _():
        hst_ref[...] = jnp.zeros_like(hst_ref)
        rst_ref[...] = jnp.zeros_like(rst_ref)

    C = CHUNK
    unroll = 2 if nchunks % 2 == 0 else 1

    def body(ci, carry):
        base = ci * (unroll * C)
        offs = [pl.multiple_of(base + u * C, C) for u in range(unroll)]
        _interleave([
            _rwkv_chunks(r_ref, lw_ref, k_ref, v_ref, kap_ref, bet_ref, g_ref, lnw_ref, lnb_ref, rk_ref,
                         ob_ref, rst_ref, offs, hp),
            _hgrn_chunks(p_ref, lb_ref, nw_ref, oa_ref, hst_ref, offs, hh),
        ])
        return carry

    lax.fori_loop(0, nchunks // unroll, body, 0)

    @pl.when(tt == pl.num_programs(1) - 1)
    def _():
        for j in range(hh):
            hs_ref[j] = hst_ref[j].T
        for p in range(hp):
            S = rst_ref[p]
            rs_ref[2 * p] = S[:RWKV_N, :RWKV_N]
            rs_ref[2 * p + 1] = S[RWKV_N:, RWKV_N:]


def _scan_prompt(P_h, lb_logits, nw, arrs, g, lnw, lnb, rk, B, T, tt_rows=256):
    R, D = arrs[0].shape
    hh = P_h.shape[1] // (4 * HGRN_DK)
    hp = D // LANES
    wh = hh * HGRN_DK
    tt_rows = min(tt_rows, T)
    nt = T // tt_rows
    col = pl.BlockSpec((tt_rows, D), lambda b, t: (b * nt + t, 0))
    const = lambda a: pl.BlockSpec(a.shape, lambda b, t: (0,) * a.ndim)
    nw = nw.reshape(1, HGRN_DK)
    lnw, lnb, rk = lnw.reshape(1, D), lnb.reshape(1, D), rk.reshape(1, D)
    oa, hs, ob, rs = pl.pallas_call(
        functools.partial(_scan_prompt_kernel, nchunks=tt_rows // CHUNK, hh=hh, hp=hp),
        grid=(B, nt),
        in_specs=[pl.BlockSpec((tt_rows, 4 * wh), lambda b, t: (b * nt + t, 0)),
                  const(lb_logits), const(nw)]
                 + [col] * 7 + [const(lnw), const(lnb), const(rk)],
        out_specs=[pl.BlockSpec((tt_rows, wh), lambda b, t: (b * nt + t, 0)),
                   pl.BlockSpec((None, hh, HGRN_DK, HGRN_DK), lambda b, t: (b, 0, 0, 0)),
                   col,
                   pl.BlockSpec((None, 2 * hp, RWKV_N, RWKV_N), lambda b, t: (b, 0, 0, 0))],
        out_shape=[jax.ShapeDtypeStruct((R, wh), F32),
                   jax.ShapeDtypeStruct((B, hh, HGRN_DK, HGRN_DK), F32),
                   jax.ShapeDtypeStruct((R, D), F32),
                   jax.ShapeDtypeStruct((B, 2 * hp, RWKV_N, RWKV_N), F32)],
        scratch_shapes=[pltpu.VMEM((hh, HGRN_DK, HGRN_DK), F32), pltpu.VMEM((hp, LANES, LANES), F32)],
        compiler_params=_cparams(("parallel", "arbitrary")),
        name="scan_prompt",
    )(P_h, lb_logits, nw, *arrs, g, lnw, lnb, rk)
    return oa, hs, ob, rs


def _rwkv_sample_kernel(r_ref, lw_ref, k_ref, v_ref, kap_ref, bet_ref, g_ref, lnw_ref, lnb_ref, rk_ref,
                        s0_ref, o_ref, s_ref, sbd_ref, *, nseq, T, hp):
    C = nseq * T
    C2 = 2 * C
    N = RWKV_N
    mA, mB = _pair_masks()
    bdm = (_iota2((LANES, LANES), 0) // N == _iota2((LANES, LANES), 1) // N).astype(F32)
    rr, cc = _iota2((C2, C2), 0), _iota2((C2, C2), 1)
    eye = (rr == cc).astype(F32)
    same = (rr // T) == (cc // T)
    strict = same & (rr > cc)
    incl = same & (rr >= cc)
    r1, c1 = _iota2((C, C), 0), _iota2((C, C), 1)
    seg_incl = ((r1 // T == c1 // T) & (r1 >= c1)).astype(F32)
    seg_all = (r1 // T == c1 // T).astype(F32)
    zero = jnp.zeros((N, N), F32)

    J = range(hp)
    cols = [slice(j * LANES, (j + 1) * LANES) for j in J]
    ld = lambda ref: [ref[:, cols[j]] for j in J]
    r, lw, k, v, kap, bet = ld(r_ref), ld(lw_ref), ld(k_ref), ld(v_ref), ld(kap_ref), ld(bet_ref)
    cs = [_dot_exact_lhs(seg_incl, lw[j]) for j in J]
    cs_end = [_dot_exact_lhs(seg_all, lw[j]) for j in J]
    sc = [_rwkv_scores(cs[j], lw[j], cs_end[j], r[j], k[j], kap[j], bet[j], mA, mB) for j in J]
    rg, kg, kd, bd, kh, bh = ([s[i] for s in sc] for i in range(6))
    vs = [_stack(v[j], mA, mB) for j in J]
    kbd = [jnp.concatenate([kd[j], bd[j]], axis=0) for j in J]
    sck = [_dot(kg[j], kbd[j], _NT) for j in J]
    scr = [_dot(rg[j], kbd[j], _NT) for j in J]
    Ak = [jnp.where(strict, sck[j][:, :C2], 0.0) for j in J]
    Ab = [jnp.where(strict, sck[j][:, C2:], 0.0) for j in J]
    Bkb = [jnp.concatenate([jnp.where(incl, scr[j][:, :C2], 0.0),
                            jnp.where(incl, -scr[j][:, C2:], 0.0)], axis=1) for j in J]
    Tm = [_neumann_inv(Ab[j], eye, int(math.log2(T)) - 1) for j in J]
    kgp = [kg[j][:C, :] + kg[j][C:, :] for j in J]
    rgp = [rg[j][:C, :] + rg[j][C:, :] for j in J]
    ks = [[] for _ in J]
    rs = [[] for _ in J]
    for n in range(nseq):
        sl = slice(n * T, (n + 1) * T)
        for j in J:
            Sbd = jnp.concatenate(
                [jnp.concatenate([s0_ref[n, 2 * j], zero], axis=1),
                 jnp.concatenate([zero, s0_ref[n, 2 * j + 1]], axis=1)], axis=0)
            sbd_ref[n, j] = Sbd
            res = _dot(jnp.concatenate([kgp[j][sl], rgp[j][sl]], axis=0), Sbd, _NT)
            ks[j].append(res[:T])
            rs[j].append(res[T:])
    KS = [_stack(jnp.concatenate(ks[j], axis=0), mA, mB) for j in J]
    RS = [_stack(jnp.concatenate(rs[j], axis=0), mA, mB) for j in J]
    rhs = [KS[j] + _dot(Ak[j], vs[j]) for j in J]
    U = [_dot_hp(Tm[j], rhs[j]) for j in J]
    vu = [jnp.concatenate([vs[j], U[j]], axis=0) for j in J]
    O = [RS[j] + _dot(Bkb[j], vu[j]) for j in J]
    Up = [U[j][:C, :] + U[j][C:, :] for j in J]
    khp = [kh[j][:C, :] + kh[j][C:, :] for j in J]
    nbhp = [-(bh[j][:C, :] + bh[j][C:, :]) for j in J]
    dec = [jnp.exp(cs_end[j]) for j in J]
    for n in range(nseq):
        sl = slice(n * T, (n + 1) * T)
        for j in J:
            upd = _dot(jnp.concatenate([v[j][sl], Up[j][sl]], axis=0),
                       jnp.concatenate([khp[j][sl], nbhp[j][sl]], axis=0), _TN)
            Sn = sbd_ref[n, j] * dec[j][n * T:n * T + 1, :] + upd * bdm
            s_ref[n, 2 * j] = Sn[:N, :N]
            s_ref[n, 2 * j + 1] = Sn[N:, N:]
    for j in J:
        o = O[j][:C, :] + O[j][C:, :]
        o_ref[:, cols[j]] = _rwkv_epilogue(o, r[j], k[j], v[j], g_ref[:, cols[j]],
                                           lnw_ref[:, cols[j]], lnb_ref[:, cols[j]], rk_ref[:, cols[j]])


def _rwkv_sample(arrs, g, lnw, lnb, rk, s0, T, nseq=16, hp=8):
    R, D = arrs[0].shape
    NP = D // LANES
    B = R // T
    rows = nseq * T
    wd = hp * LANES
    col = pl.BlockSpec((rows, wd), lambda p, gi: (gi, p))
    par = pl.BlockSpec((1, wd), lambda p, gi: (0, p))
    sspec = pl.BlockSpec((nseq, 2 * hp, RWKV_N, RWKV_N), lambda p, gi: (gi, p, 0, 0))
    o, s = pl.pallas_call(
        functools.partial(_rwkv_sample_kernel, nseq=nseq, T=T, hp=hp),
        grid=(NP // hp, B // nseq),
        in_specs=[col] * 7 + [par] * 3 + [sspec],
        out_specs=[col, sspec],
        out_shape=[jax.ShapeDtypeStruct((R, D), F32), jax.ShapeDtypeStruct(s0.shape, F32)],
        scratch_shapes=[pltpu.VMEM((nseq, hp, LANES, LANES), F32)],
        compiler_params=_cparams(("parallel", "parallel")),
        name="rwkv_sample",
    )(*arrs, g, lnw.reshape(1, D), lnb.reshape(1, D), rk.reshape(1, D), s0)
    return o, s


def _merge_kernel(x_ref, h_ref, oa_ref, ob_ref, wg_ref, w_ref, gt_ref, o_ref, *, rep):
    D = x_ref.shape[1]
    pg = lax.dot_general(h_ref[...], wg_ref[...], _NT, preferred_element_type=F32)
    m = _sigmoid(pg[:, :D]) * oa_ref[...] + _sigmoid(pg[:, D:]) * ob_ref[...]
    o_ref[...] = x_ref[...] + _rep_rows(gt_ref[...], rep) * _dot(m, w_ref[...])


def _merge_out(x, h, o_a, o_b, w_g, w_out, mod, tm):
    R, D = x.shape
    row = pl.BlockSpec((tm, D), lambda i: (i, 0))
    const = lambda a: pl.BlockSpec(a.shape, lambda i: (0,) * a.ndim, pipeline_mode=pl.Buffered(1))
    return pl.pallas_call(
        functools.partial(_merge_kernel, rep=mod.rep),
        grid=(R // tm,),
        in_specs=[row, row, row, row, const(w_g), const(w_out), mod.spec(tm, 2)],
        out_specs=row,
        out_shape=jax.ShapeDtypeStruct((R, D), F32),
        compiler_params=_cparams(("parallel",)),
        name="merge_out",
    )(x, h, o_a, o_b, w_g, w_out, mod.arr)


def _mlp_kernel(x_ref, nw_ref, sh_ref, sc_ref, gt_ref, wu_ref, wd_ref, fw_ref, o_ref, *, tf, rep):
    x = x_ref[...]
    h = (_rms(x, nw_ref[...]) * (1.0 + _rep_rows(sc_ref[...], rep)) + _rep_rows(sh_ref[...], rep)).astype(BF16)
    acc = None
    for c in range(wu_ref.shape[1] // tf):
        u = jnp.dot(h, wu_ref[:, c * tf:(c + 1) * tf], preferred_element_type=F32)
        u = jnp.square(jnp.maximum(u, 0.0))
        part = _dot(u, wd_ref[c * tf:(c + 1) * tf, :])
        acc = part if acc is None else acc + part
    o_ref[...] = _rms(x + _rep_rows(gt_ref[...], rep) * acc, fw_ref[...])


def _mlp(x, nw, w_up, w_down, fw, mod, tm, tf=1024):
    R, D = x.shape
    const = lambda a: pl.BlockSpec(a.shape, lambda i: (0,) * a.ndim, pipeline_mode=pl.Buffered(1))
    row = pl.BlockSpec((tm, D), lambda i: (i, 0))
    nw, fw = nw.reshape(1, D), fw.reshape(1, D)
    return pl.pallas_call(
        functools.partial(_mlp_kernel, tf=tf, rep=mod.rep),
        grid=(R // tm,),
        in_specs=[row, const(nw), mod.spec(tm, 3), mod.spec(tm, 4), mod.spec(tm, 5),
                  const(w_up), const(w_down), const(fw)],
        out_specs=row,
        out_shape=jax.ShapeDtypeStruct((R, D), F32),
        compiler_params=_cparams(("parallel",)),
        name="mlp",
    )(x, nw, mod.arr, mod.arr, mod.arr, w_up, w_down, fw)


def _run(x3, mod_rows, s_hgrn, s_rwkv, s_shift, W, prompt):
    B, T, D = x3.shape
    R = B * T
    x = x3.reshape(R, D)
    if prompt:
        mod = _Mod(mod_rows.reshape(B, 1, -1), True, T, D)
        tm = min(1024, T)
    else:
        mod = _Mod(mod_rows, False, T, D)
        tm = min(512, R)

    rw = (W["mu"], W["w0"], W["a0"], W["k_k"], W["k_a"], W["wwa"], W["g2p"])
    nw1 = W["norm1_w"].reshape(1, D)
    if prompt:
        shift_in = jnp.zeros((B, 1, RC_PAD), F32)
        h, arrs, last = _in_rwkv(x, nw1, mod, W["w_r"], shift_in, rw, True, T)
        shift_out = last[:, 0, :RC_COLS]
        P_h = _matmul(h, W["w_in_t"], 4 * D, tm, 2048, "proj_h")
        o_a, hg, o_b, rs = _scan_prompt(P_h, W["lb_logits"], W["hgrn_norm_w"], arrs[:6], arrs[6],
                                        W["lnx_w"], W["lnx_b"], W["r_k"], B, T)
    else:
        shift_in = jnp.pad(s_shift, ((0, 0), (0, RC_PAD - RC_COLS)))
        h, arrs, last = _in_rwkv(x, nw1, mod, W["w_r"], shift_in, rw, False, T)
        shift_out = last[:, :RC_COLS]
        P_h = _matmul(h, W["w_in_t"], 4 * D, tm, 2048, "proj_h")
        o_a, hg = _hgrn_sample(P_h, W["lb_logits"], W["hgrn_norm_w"], s_hgrn, T)
        o_b, rs = _rwkv_sample(arrs[:6], arrs[6], W["lnx_w"], W["lnx_b"], W["r_k"], s_rwkv, T)

    x1 = _merge_out(x, h, o_a, o_b, W["w_g"], W["w_out"], mod, min(tm, 512))
    y = _mlp(x1, W["norm2_w"], W["w_up"], W["w_down"], W["final_norm_w"], mod, min(tm, 512))
    return y.reshape(B, T, D), hg, rs, shift_out


def kernel(x_prompt, x_sample, state_hgrn, state_rwkv, state_shift, c_prompt, c_sample, norm1_w, norm2_w, ada_w, ada_b, w_in, lb_logits, hgrn_norm_w, rwkv_mu, rwkv_w0, rwkv_w2, rwkv_a0, rwkv_a2, rwkv_g2, rwkv_k_k, rwkv_k_a, rwkv_r_k, rwkv_lnx_w, rwkv_lnx_b, w_out, w_up, w_down, final_norm_w):
    depth = norm1_w.shape[0]
    assert depth == 1, "single-layer step"
    D = x_prompt.shape[-1]
    HW = 4 * D
    w_in_t = w_in[0].T.astype(BF16)
    row = lambda a: a.reshape(1, -1)
    wwa = jnp.zeros((LANES, 2 * D), F32)
    wwa = wwa.at[:W_LORA, :D].set(rwkv_w2[0]).at[W_LORA:W_LORA + A_LORA, D:].set(rwkv_a2[0])
    W = {
        "norm1_w": norm1_w[0], "norm2_w": norm2_w[0], "final_norm_w": final_norm_w,
        "w_in_t": w_in_t,
        "w_r": jnp.pad(w_in[0][:, HW:HW + RC_COLS], ((0, 0), (0, RC_PAD - RC_COLS))).astype(BF16),
        "w_g": w_in_t[HW + RC_COLS:],
        "lb_logits": lb_logits, "hgrn_norm_w": hgrn_norm_w[0],
        "mu": jnp.pad(row(rwkv_mu[0]), ((0, 0), (0, RC_PAD - RC_COLS))),
        "w0": row(rwkv_w0[0]), "a0": row(rwkv_a0[0]), "k_k": row(rwkv_k_k[0]), "k_a": row(rwkv_k_a[0]),
        "wwa": wwa.astype(BF16),
        "g2p": jnp.pad(rwkv_g2[0], ((0, G_LORA_PAD - G_LORA), (0, 0))).astype(BF16),
        "lnx_w": rwkv_lnx_w[0], "lnx_b": rwkv_lnx_b[0], "r_k": rwkv_r_k[0].reshape(-1),
        "w_out": w_out[0].astype(BF16), "w_up": w_up[0].astype(BF16), "w_down": w_down[0].astype(BF16),
    }
    nb = x_prompt.shape[0]
    mod = _adaln(jnp.concatenate([c_prompt, c_sample], axis=0), ada_w[0], ada_b[0])
    y_p, hg_p, rw_p, sh_p = _run(x_prompt, mod[:nb], None, None, None, W, True)
    y_s, hg_s, rw_s, sh_s = _run(x_sample, mod[nb:], state_hgrn[0], state_rwkv[0], state_shift[0], W, False)
    ex = lambda a: a[None]
    return (y_p, y_s, ex(hg_p), ex(rw_p), ex(sh_p), ex(hg_s), ex(rw_s), ex(sh_s))
```

```python
import functools
import math

import jax
import jax.numpy as jnp
from jax import lax
from jax.experimental import pallas as pl
from jax.experimental.pallas import tpu as pltpu

F32 = jnp.float32
BF16 = jnp.bfloat16

LANES = 128
SUBLANES = 8
VMEM_LIMIT = 48 * 1024 * 1024

HGRN_DK = 128
RWKV_N = 64
W_LORA = 64
A_LORA = 64
G_LORA = 160
G_LORA_PAD = 256
RC_MAIN = 3 * 1024
RC_COLS = RC_MAIN + W_LORA + A_LORA + G_LORA
RC_PAD = RC_MAIN + LANES + G_LORA_PAD
RMS_EPS = 1e-6
LNX_EPS = 64e-5
CHUNK = 64
SUB = 8
RWKV_SUB = 8


def _cparams(sem, vmem=VMEM_LIMIT):
    return pltpu.CompilerParams(dimension_semantics=sem, vmem_limit_bytes=vmem)


_NN = (((1,), (0,)), ((), ()))
_NT = (((1,), (1,)), ((), ()))
_TN = (((0,), (0,)), ((), ()))


def _dot(a, b, dims=_NN):
    return lax.dot_general(a.astype(BF16), b.astype(BF16), dims, preferred_element_type=F32)


def _split2(x):
    hi = x.astype(BF16)
    lo = (x - hi.astype(F32)).astype(BF16)
    return hi, lo


def _split3(x):
    hi = x.astype(BF16)
    r1 = x - hi.astype(F32)
    mid = r1.astype(BF16)
    lo = (r1 - mid.astype(F32)).astype(BF16)
    return hi, mid, lo


def _dot_exact_lhs(a, x, dims=_NN):
    a = a.astype(BF16)
    hi, mid, lo = _split3(x)
    d = lambda p: lax.dot_general(a, p, dims, preferred_element_type=F32)
    return d(hi) + d(mid) + d(lo)


def _dot_exact_rhs(x, b, dims=_NN, parts=3):
    b = b.astype(BF16)
    d = lambda p: lax.dot_general(p, b, dims, preferred_element_type=F32)
    if parts == 2:
        hi, lo = _split2(x)
        return d(hi) + d(lo)
    hi, mid, lo = _split3(x)
    return d(hi) + d(mid) + d(lo)


def _dot_hp(a, b, dims=_NN):
    ah, al = _split2(a)
    bh, bl = _split2(b)
    d = lambda p, q: lax.dot_general(p, q, dims, preferred_element_type=F32)
    return d(ah, bh) + d(ah, bl) + d(al, bh)


def _sigmoid(x):
    return 1.0 / (1.0 + jnp.exp(-x))


def _iota2(shape, dim):
    return lax.broadcasted_iota(jnp.int32, shape, dim)


def _neumann_inv(L, eye, nsq):
    X = L
    P = eye - L
    for _ in range(nsq):
        X = _dot_hp(X, X)
        P = P + _dot_hp(P, X)
    return P


def _adaln_kernel(c_ref, w_ref, b_ref, o_ref):
    c = c_ref[...]
    s = c * _sigmoid(c)
    o_ref[...] = _dot(s, w_ref[...]) + b_ref[...]


def _adaln(c_all, ada_w, ada_b, tn=1536):
    R, D = c_all.shape
    N = ada_w.shape[1]
    return pl.pallas_call(
        _adaln_kernel,
        grid=(N // tn,),
        in_specs=[pl.BlockSpec((R, D), lambda j: (0, 0)),
                  pl.BlockSpec((D, tn), lambda j: (0, j)),
                  pl.BlockSpec((1, tn), lambda j: (0, j))],
        out_specs=pl.BlockSpec((R, tn), lambda j: (0, j)),
        out_shape=jax.ShapeDtypeStruct((R, N), F32),
        compiler_params=_cparams(("parallel",)),
        name="adaln",
    )(c_all, ada_w, ada_b.reshape(1, N))


class _Mod:
    def __init__(self, arr, per_seq, seq_rows, D):
        self.arr, self.per_seq, self.seq_rows, self.D = arr, per_seq, seq_rows, D
        self.rep = 1 if per_seq else seq_rows

    def spec(self, tm, comp):
        D = self.D
        if self.per_seq:
            tiles = self.seq_rows // tm
            return pl.BlockSpec((None, 1, D), lambda i, *_: (i // tiles, 0, comp))
        return pl.BlockSpec((tm // self.seq_rows, D), lambda i, *_: (i, comp))


def _rep_rows(v, rep):
    if rep == 1:
        return v
    n, w = v.shape
    return jnp.broadcast_to(v[:, None, :], (n, rep, w)).reshape(n * rep, w)


def _rms(x, w):
    return x * lax.rsqrt(jnp.mean(x * x, axis=-1, keepdims=True) + RMS_EPS) * w


def _mm_kernel(x_ref, wt_ref, o_ref):
    o_ref[...] = lax.dot_general(x_ref[...], wt_ref[...], _NT, preferred_element_type=F32)


def _matmul(x, wt, n_out, tm, tn, name):
    R, K = x.shape
    N = n_out
    return pl.pallas_call(
        _mm_kernel,
        grid=(R // tm, N // tn),
        in_specs=[pl.BlockSpec((tm, K), lambda i, j: (i, 0)),
                  pl.BlockSpec((tn, K), lambda i, j: (j, 0))],
        out_specs=pl.BlockSpec((tm, tn), lambda i, j: (i, j)),
        out_shape=jax.ShapeDtypeStruct((R, N), F32),
        compiler_params=_cparams(("parallel", "arbitrary")),
        name=name,
    )(x, wt)


def _hgrn_gates(fp, lb_logits, base2=False):
    m = jnp.max(lb_logits, axis=0, keepdims=True)
    e = jnp.exp(lb_logits - m)
    lb = e[0:1, :] / jnp.sum(e, axis=0, keepdims=True)
    f = lb + (1.0 - lb) * _sigmoid(fp)
    return (jnp.log2(f) if base2 else jnp.log(f)), 1.0 - f


def _hgrn_diag(q, kin, b, c):
    R = q.shape[0]
    nb = R // c
    q3 = q.reshape(nb, c, LANES)
    k3 = kin.reshape(nb, c, LANES)
    b3 = (b * math.log2(math.e)).reshape(nb, c, LANES)
    lane_mod = lax.broadcasted_iota(jnp.int32, (nb, c, R), 2) % c
    acc = jnp.zeros((nb, c, R), F32)
    for s in range(c):
        d = jnp.minimum(b3 - b3[:, s:s + 1, :], 0.0)
        col = jnp.sum(q3 * k3[:, s:s + 1, :] * jnp.exp2(d), axis=-1, keepdims=True)
        acc = jnp.where(lane_mod == s, col, acc)
    rr, cc = _iota2((R, R), 0), _iota2((R, R), 1)
    keep = ((rr // c) == (cc // c)) & (rr >= cc)
    return jnp.where(keep, acc.reshape(R, R), 0.0)


def _hgrn_cross(q, kin, b2, m):
    R = q.shape[0]
    h = m // 2
    nb = R // m
    ref = jnp.broadcast_to(b2.reshape(nb, m, LANES)[:, h - 1:h, :], (nb, m, LANES)).reshape(R, LANES)
    upper = (_iota2((R, 1), 0) % m) >= h
    e = jnp.exp2(-jnp.abs(b2 - ref))
    qs = jnp.where(upper, q * e, 0.0)
    ks = jnp.where(upper, 0.0, kin * e)
    sc = _dot(qs, ks, _NT)
    if nb == 1:
        return sc
    same_blk = (_iota2((R, R), 0) // m) == (_iota2((R, R), 1) // m)
    return jnp.where(same_blk, sc, 0.0)


def _hgrn_out(o, og, nw):
    o = o * lax.rsqrt(jnp.mean(o * o, axis=-1, keepdims=True) + RMS_EPS) * nw
    return o * (og * _sigmoid(og))


def _hgrn_chunks(p_ref, lb_ref, nw_ref, o_ref, st_ref, offs, hp):
    C, c = CHUNK, SUB
    nb = C // c
    W = hp * LANES
    tri = (_iota2((C, C), 0) >= _iota2((C, C), 1)).astype(F32)
    nw = nw_ref[...]
    H = range(hp)
    cols = [slice(j * LANES, (j + 1) * LANES) for j in H]
    pcol = lambda g, j: slice(g * W + j * LANES, g * W + (j + 1) * LANES)
    lane_mod = lax.broadcasted_iota(jnp.int32, (nb, c, C), 2) % c
    rr, cc = _iota2((C, C), 0), _iota2((C, C), 1)
    keep = ((rr // c) == (cc // c)) & (rr >= cc)
    for r0 in offs:
        q = [p_ref[pl.ds(r0, C), pcol(0, j)] for j in H]
        v = [p_ref[pl.ds(r0, C), pcol(2, j)] for j in H]
        gates = [_hgrn_gates(p_ref[pl.ds(r0, C), pcol(1, j)], lb_ref[:, cols[j]], base2=True) for j in H]
        lf = [gt[0] for gt in gates]
        kin = [gt[1] for gt in gates]
        yield
        b = [_dot_exact_lhs(tri, lf[j]) for j in H]
        b_last = [b[j][C - 1:C, :] for j in H]
        yield
        ST = [st_ref[j] for j in H]
        o = [_dot(q[j] * jnp.exp2(b[j]), ST[j], _NT) for j in H]
        yield
        for j in H:
            st_ref[j] = ST[j] * jnp.exp2(b_last[j]) + _dot(v[j], kin[j] * jnp.exp2(b_last[j] - b[j]), _TN)
        yield
        q3 = [q[j].reshape(nb, c, LANES) for j in H]
        k3 = [kin[j].reshape(nb, c, LANES) for j in H]
        b3 = [b[j].reshape(nb, c, LANES) for j in H]
        acc = [jnp.zeros((nb, c, C), F32) for _ in H]
        for s in range(c):
            for j in H:
                d = jnp.minimum(b3[j] - b3[j][:, s:s + 1, :], 0.0)
                col = jnp.sum(q3[j] * k3[j][:, s:s + 1, :] * jnp.exp2(d), axis=-1, keepdims=True)
                acc[j] = jnp.where(lane_mod == s, col, acc[j])
            yield
        A = [jnp.where(keep, acc[j].reshape(C, C), 0.0) for j in H]
        m = 2 * c
        while m <= C:
            A = [A[j] + _hgrn_cross(q[j], kin[j], b[j], m) for j in H]
            m *= 2
            yield
        o = [o[j] + _dot(A[j], v[j]) for j in H]
        yield
        for j in H:
            o_ref[pl.ds(r0, C), cols[j]] = _hgrn_out(o[j], p_ref[pl.ds(r0, C), pcol(3, j)], nw)
        yield


def _hgrn_sample_kernel(q_ref, f_ref, i_ref, g_ref, lb_ref, nw_ref, s0_ref, o_ref, s_ref, *, nseq, T, hp):
    R = nseq * T
    rr, cc = _iota2((R, R), 0), _iota2((R, R), 1)
    same = (rr // T) == (cc // T)
    seg_incl = (same & (rr >= cc)).astype(F32)
    seg_all = same.astype(F32)
    nw = nw_ref[...]
    H = range(hp)
    cols = [slice(j * LANES, (j + 1) * LANES) for j in H]
    q = [q_ref[:, cols[j]] for j in H]
    v = [i_ref[:, cols[j]] for j in H]
    gates = [_hgrn_gates(f_ref[:, cols[j]], lb_ref[:, cols[j]]) for j in H]
    lf = [gt[0] for gt in gates]
    kin = [gt[1] for gt in gates]
    b = [_dot_exact_lhs(seg_incl, lf[j]) for j in H]
    b_tot = [_dot_exact_lhs(seg_all, lf[j]) for j in H]
    A = [_hgrn_diag(q[j], kin[j], b[j], T) for j in H]
    o = [_dot(A[j], v[j]) for j in H]
    qg = [q[j] * jnp.exp(b[j]) for j in H]
    kh = [kin[j] * jnp.exp(b_tot[j] - b[j]) for j in H]
    ecol = [jnp.transpose(jnp.exp(b_tot[j]).reshape(nseq, T, LANES)[:, 0, :]) for j in H]
    inter = [[] for _ in H]
    for n in range(nseq):
        sl = slice(n * T, (n + 1) * T)
        for j in H:
            S = s0_ref[n, j]
            inter[j].append(_dot(qg[j][sl], S))
            s_ref[n, j] = S * ecol[j][:, n:n + 1] + _dot(kh[j][sl], v[j][sl], _TN)
    for j in H:
        oj = o[j] + jnp.concatenate(inter[j], axis=0)
        o_ref[:, cols[j]] = _hgrn_out(oj, g_ref[:, cols[j]], nw)


def _hgrn_sample(P_h, lb_logits, nw, s0, T, nseq=16, hp=8):
    R, W = P_h.shape
    H = W // (4 * HGRN_DK)
    HG = H // hp
    wd = hp * HGRN_DK
    B = R // T
    rows = nseq * T
    col = lambda off: pl.BlockSpec((rows, wd), lambda h, g: (g, off * HG + h))
    sspec = pl.BlockSpec((nseq, hp, HGRN_DK, HGRN_DK), lambda h, g: (g, h, 0, 0))
    o, s = pl.pallas_call(
        functools.partial(_hgrn_sample_kernel, nseq=nseq, T=T, hp=hp),
        grid=(HG, B // nseq),
        in_specs=[col(0), col(1), col(2), col(3),
                  pl.BlockSpec((lb_logits.shape[0], wd), lambda h, g: (0, h)),
                  pl.BlockSpec((1, HGRN_DK), lambda h, g: (0, 0)),
                  sspec],
        out_specs=[pl.BlockSpec((rows, wd), lambda h, g: (g, h)), sspec],
        out_shape=[jax.ShapeDtypeStruct((R, H * HGRN_DK), F32),
                   jax.ShapeDtypeStruct(s0.shape, F32)],
        compiler_params=_cparams(("parallel", "parallel")),
        name="hgrn_sample",
    )(P_h, P_h, P_h, P_h, lb_logits, nw.reshape(1, HGRN_DK), s0)
    return o, s


def _rwkv_prep_body(rc, prev, mu_ref, w0_ref, a0_ref, kk_ref, ka_ref, wwa_ref, g2_ref, outs, rows):
    r_o, lw_o, k_o, v_o, kap_o, bet_o, g_o = outs
    D = RC_MAIN // 3
    xs = rc + mu_ref[...] * (prev - rc)
    r = xs[:, 0:D]
    kb = xs[:, D:2 * D]
    v = xs[:, 2 * D:3 * D]
    xwa = xs[:, RC_MAIN:RC_MAIN + LANES]
    xg = xs[:, RC_MAIN + LANES:]
    lane = _iota2((1, LANES), 1)
    wa = _dot(jnp.where(lane < W_LORA, jnp.tanh(xwa), xwa), wwa_ref[...])
    w = w0_ref[...] + wa[:, :D]
    a = _sigmoid(a0_ref[...] + wa[:, D:])
    lw = -(math.exp(-0.5) * math.log2(math.e)) * _sigmoid(w)
    g = _dot(_sigmoid(xg), g2_ref[...])
    kk = kb * kk_ref[...]
    seg = (_iota2((LANES, LANES), 0) // RWKV_N == _iota2((LANES, LANES), 1) // RWKV_N).astype(F32)
    k_a = ka_ref[...]
    for p in range(D // LANES):
        sl = slice(p * LANES, (p + 1) * LANES)
        kkp = kk[:, sl]
        kap = kkp * lax.rsqrt(jnp.maximum(_dot_exact_rhs(kkp * kkp, seg, parts=2), 1e-24))
        ap = a[:, sl]
        kap_o[rows, sl] = kap
        bet_o[rows, sl] = ap * kap
        k_o[rows, sl] = kb[:, sl] * (1.0 + (ap - 1.0) * k_a[:, sl])
    r_o[rows, :] = r
    lw_o[rows, :] = lw
    v_o[rows, :] = v
    g_o[rows, :] = g


def _in_rwkv_kernel(x_ref, nw_ref, sh_ref, sc_ref, wr_ref, shift_ref, mu_ref, w0_ref, a0_ref, kk_ref,
                    ka_ref, wwa_ref, g2_ref, h_ref, r_o, lw_o, k_o, v_o, kap_o, bet_o, g_o, last_o,
                    carry_ref, rc_ref, *, seq_rows, sub):
    i = pl.program_id(0)
    tm = x_ref.shape[0]
    per_seq = seq_rows >= tm

    @pl.when(i == 0)
    def _():
        carry_ref[...] = jnp.zeros_like(carry_ref)

    rep = 1 if per_seq else seq_rows
    h = (_rms(x_ref[...], nw_ref[...]) * (1.0 + _rep_rows(sc_ref[...], rep))
         + _rep_rows(sh_ref[...], rep)).astype(BF16)
    h_ref[...] = h
    outs = (r_o, lw_o, k_o, v_o, kap_o, bet_o, g_o)
    rid = _iota2((sub, 1), 0)
    rc_ref[...] = jnp.dot(h, wr_ref[...], preferred_element_type=F32)
    prev_last = None
    for s in range(tm // sub):
        rows = slice(s * sub, (s + 1) * sub)
        rc = rc_ref[rows, :]
        rolled = pltpu.roll(rc, 1, axis=0)
        if per_seq:
            if s == 0:
                start = (i % (seq_rows // tm)) == 0
                first = jnp.where(start, shift_ref[...], carry_ref[SUBLANES - 1:SUBLANES, :])
            else:
                first = prev_last
            prev = jnp.where(rid == 0, first, rolled)
        else:
            seqs = slice(s * sub // seq_rows, (s + 1) * sub // seq_rows)
            prev = jnp.where(rid % seq_rows == 0, _rep_rows(shift_ref[seqs, :], seq_rows), rolled)
            last_o[seqs, :] = rc.reshape(sub // seq_rows, seq_rows, RC_PAD)[:, seq_rows - 1, :]
        prev_last = rc[sub - 1:sub, :]
        if s == tm // sub - 1 and per_seq:
            carry_ref[...] = rc[sub - SUBLANES:, :]
            last_o[...] = prev_last
        _rwkv_prep_body(rc, prev, mu_ref, w0_ref, a0_ref, kk_ref, ka_ref, wwa_ref, g2_ref, outs, rows)


def _in_rwkv(x, nw, mod, w_r, shift, params, per_seq, seq_rows, tm=256, sub=128):
    R, D = x.shape
    tm = min(tm, seq_rows) if per_seq else min(tm, R)
    sub = min(sub, tm)
    mu, w0, a0, k_k, k_a, wwa, g2p = params
    const = lambda a: pl.BlockSpec(a.shape, lambda i: (0,) * a.ndim, pipeline_mode=pl.Buffered(1))
    wspecs = [const(a) for a in (mu, w0, a0, k_k, k_a, wwa, g2p)]
    row = lambda w: pl.BlockSpec((tm, w), lambda i: (i, 0))
    if per_seq:
        tiles = seq_rows // tm
        shift_spec = pl.BlockSpec((None, 1, RC_PAD), lambda i: (i // tiles, 0, 0))
        last_spec = pl.BlockSpec((None, 1, RC_PAD), lambda i: (i // tiles, 0, 0))
        last_shape = jax.ShapeDtypeStruct((R // seq_rows, 1, RC_PAD), F32)
    else:
        shift_spec = pl.BlockSpec((tm // seq_rows, RC_PAD), lambda i: (i, 0))
        last_spec = pl.BlockSpec((tm // seq_rows, RC_PAD), lambda i: (i, 0))
        last_shape = jax.ShapeDtypeStruct((R // seq_rows, RC_PAD), F32)
    outs = pl.pallas_call(
        functools.partial(_in_rwkv_kernel, seq_rows=seq_rows, sub=sub),
        grid=(R // tm,),
        in_specs=[row(D), const(nw), mod.spec(tm, 0), mod.spec(tm, 1), const(w_r), shift_spec] + wspecs,
        out_specs=[row(D)] + [row(D)] * 7 + [last_spec],
        out_shape=[jax.ShapeDtypeStruct((R, D), BF16)] + [jax.ShapeDtypeStruct((R, D), F32)] * 7
                  + [last_shape],
        scratch_shapes=[pltpu.VMEM((SUBLANES, RC_PAD), F32), pltpu.VMEM((tm, RC_PAD), F32)],
        compiler_params=_cparams(("arbitrary",)),
        name="in_rwkv",
    )(x, nw, mod.arr, mod.arr, w_r, shift, mu, w0, a0, k_k, k_a, wwa, g2p)
    return outs[0], outs[1:8], outs[8]


def _pair_masks():
    lane = _iota2((1, LANES), 1)
    return (lane < RWKV_N).astype(F32), (lane >= RWKV_N).astype(F32)


def _stack(x, mA, mB, dtype=F32):
    x = x.astype(dtype)
    return jnp.concatenate([x * mA.astype(dtype), x * mB.astype(dtype)], axis=0)


def _rwkv_scores(cs, lw, cs_end, r, k, kap, bet, mA, mB, dtype=F32):
    e_neg = jnp.exp2(-cs)
    e_rem = jnp.exp2(cs_end - cs)
    rg = _stack(r * jnp.exp2(cs), mA, mB, dtype)
    kg = _stack(kap * jnp.exp2(cs - lw), mA, mB, dtype)
    kd = _stack(k * e_neg, mA, mB, dtype)
    bd = _stack(bet * e_neg, mA, mB, dtype)
    kh = _stack(k * e_rem, mA, mB, dtype)
    bh = _stack(bet * e_rem, mA, mB, dtype)
    return rg, kg, kd, bd, kh, bh


def _rwkv_epilogue(o, r, k, v, g, lnw, lnb, rk):
    avg = jnp.where(_iota2((LANES, LANES), 0) // RWKV_N == _iota2((LANES, LANES), 1) // RWKV_N,
                    1.0 / RWKV_N, 0.0).astype(F32)
    mean = _dot_exact_rhs(o, avg)
    d = o - mean
    var = _dot_exact_rhs(d * d, avg)
    on = d * lax.rsqrt(var + LNX_EPS) * lnw + lnb
    bonus = _dot_exact_rhs(r * k * rk, avg) * float(RWKV_N) * v
    return (on + bonus) * g


def _rwkv_chunks(r_ref, lw_ref, k_ref, v_ref, kap_ref, bet_ref, g_ref, lnw_ref, lnb_ref, rk_ref,
                 o_ref, st_ref, offs, hp):
    C, c = CHUNK, RWKV_SUB
    C2 = 2 * C
    mA, mB = _pair_masks()
    rr, cc = _iota2((C2, C2), 0), _iota2((C2, C2), 1)
    eye = (rr == cc).astype(F32)
    strict = rr > cc
    incl = rr >= cc
    blk = (rr // c) == (cc // c)
    tri = (_iota2((C, C), 0) >= _iota2((C, C), 1)).astype(F32)
    avg = jnp.where(_iota2((LANES, LANES), 0) // RWKV_N == _iota2((LANES, LANES), 1) // RWKV_N,
                    1.0 / RWKV_N, 0.0).astype(F32)
    chains = [(r0, p) for r0 in offs for p in range(hp)]
    J = range(len(chains))
    pcols = [slice(p * LANES, (p + 1) * LANES) for _, p in chains]

    ld = lambda ref: [ref[pl.ds(r0, C), pcols[j]] for j, (r0, _) in zip(J, chains)]
    r, lw, k, v, kap, bet = ld(r_ref), ld(lw_ref), ld(k_ref), ld(v_ref), ld(kap_ref), ld(bet_ref)
    yield
    cs = [_dot_exact_lhs(tri, lw[j]) for j in J]
    cs_end = [cs[j][C - 1:C, :] for j in J]
    yield
    sc = [_rwkv_scores(cs[j], lw[j], cs_end[j], r[j], k[j], kap[j], bet[j], mA, mB, BF16) for j in J]
    rg, kg, kd, bd, kh, bh = ([s[i] for s in sc] for i in range(6))
    vs = [_stack(v[j], mA, mB, BF16) for j in J]
    kbd = [jnp.concatenate([kd[j], bd[j]], axis=0) for j in J]
    yield
    sck = [_dot(kg[j], kbd[j], _NT) for j in J]
    yield
    scr = [_dot(rg[j], kbd[j], _NT) for j in J]
    Ak = [jnp.where(strict, sck[j][:, :C2], 0.0) for j in J]
    Ab = [jnp.where(strict, sck[j][:, C2:], 0.0) for j in J]
    AkV = [_dot(Ak[j], vs[j]) for j in J]
    yield
    Bkb = [jnp.concatenate([jnp.where(incl, scr[j][:, :C2], 0.0),
                            jnp.where(incl, -scr[j][:, C2:], 0.0)], axis=1).astype(BF16) for j in J]
    LD = [jnp.where(blk, Ab[j], 0.0) for j in J]
    X = LD
    Dinv = [eye - LD[j] for j in J]
    for _ in range(int(math.log2(c)) - 1):
        X = [_dot(X[j], X[j]) for j in J]
        yield
        Dinv = [Dinv[j] + _dot(Dinv[j], X[j]) for j in J]
        yield
    Q = [_dot(Dinv[j], Ab[j] - LD[j]) for j in J]
    yield
    T0 = [Dinv[j] - _dot(Q[j], Dinv[j]) for j in J]
    yield
    for _ in range(int(math.log2(C // c)) - 1):
        Q = [_dot(Q[j], Q[j]) for j in J]
        yield
        T0 = [T0[j] + _dot(Q[j], T0[j]) for j in J]
        yield
    E = [(eye - T0[j]) - _dot_hp(Ab[j], T0[j]) for j in J]
    yield
    Tm = [T0[j] + _dot(T0[j], E[j]) for j in J]
    yield
    khb = [jnp.concatenate([kh[j], -bh[j]], axis=0) for j in J]
    dec = [jnp.exp2(cs_end[j]) for j in J]
    yield

    for u in range(len(offs)):
        K = range(u * hp, (u + 1) * hp)
        S = {j: st_ref[chains[j][1]] for j in K}
        rhs = {j: _dot(kg[j], S[j], _NT) + AkV[j] for j in K}
        yield
        U = {j: _dot_hp(Tm[j], rhs[j]) for j in K}
        yield
        vu = {j: jnp.concatenate([vs[j], U[j].astype(BF16)], axis=0) for j in K}
        for j in K:
            st_ref[chains[j][1]] = S[j] * dec[j] + _dot(vu[j], khb[j], _TN)
        yield
        O = {j: _dot(rg[j], S[j], _NT) + _dot(Bkb[j], vu[j]) for j in K}
        o = {j: O[j][:C, :] + O[j][C:, :] for j in K}
        yield
        mean = {j: _dot_exact_rhs(o[j], avg, parts=2) for j in K}
        d = {j: o[j] - mean[j] for j in K}
        yield
        var = {j: _dot_exact_rhs(d[j] * d[j], avg, parts=2) for j in K}
        rkk = {j: r_ref[pl.ds(chains[j][0], C), pcols[j]] * k_ref[pl.ds(chains[j][0], C), pcols[j]]
               * rk_ref[:, pcols[j]] for j in K}
        yield
        bsum = {j: _dot_exact_rhs(rkk[j], avg, parts=2) for j in K}
        yield
        for j in K:
            rows = pl.ds(chains[j][0], C)
            on = d[j] * lax.rsqrt(var[j] + LNX_EPS) * lnw_ref[:, pcols[j]] + lnb_ref[:, pcols[j]]
            bonus = bsum[j] * float(RWKV_N) * v_ref[rows, pcols[j]]
            o_ref[rows, pcols[j]] = (on + bonus) * g_ref[rows, pcols[j]]
        yield


def _interleave(gens):
    live = list(gens)
    while live:
        for gen in list(live):
            try:
                next(gen)
            except StopIteration:
                live.remove(gen)


def _scan_prompt_kernel(p_ref, lb_ref, nw_ref,
                        r_ref, lw_ref, k_ref, v_ref, kap_ref, bet_ref, g_ref, lnw_ref, lnb_ref, rk_ref,
                        oa_ref, hs_ref, ob_ref, rs_ref, hst_ref, rst_ref, *, nchunks, hh, hp):
    tt = pl.program_id(1)

    @pl.when(tt == 0)
    def _():
        hst_ref[...] = jnp.zeros_like(hst_ref)
        rst_ref[...] = jnp.zeros_like(rst_ref)

    C = CHUNK
    unroll = 2 if nchunks % 2 == 0 else 1

    def body(ci, carry):
        base = ci * (unroll * C)
        offs = [pl.multiple_of(base + u * C, C) for u in range(unroll)]
        _interleave([
            _rwkv_chunks(r_ref, lw_ref, k_ref, v_ref, kap_ref, bet_ref, g_ref, lnw_ref, lnb_ref, rk_ref,
                         ob_ref, rst_ref, offs, hp),
            _hgrn_chunks(p_ref, lb_ref, nw_ref, oa_ref, hst_ref, offs, hh),
        ])
        return carry

    lax.fori_loop(0, nchunks // unroll, body, 0)

    @pl.when(tt == pl.num_programs(1) - 1)
    def _():
        for j in range(hh):
            hs_ref[j] = hst_ref[j].T
        for p in range(hp):
            S = rst_ref[p]
            rs_ref[2 * p] = S[:RWKV_N, :RWKV_N]
            rs_ref[2 * p + 1] = S[RWKV_N:, RWKV_N:]


def _scan_prompt(P_h, lb_logits, nw, arrs, g, lnw, lnb, rk, B, T, tt_rows=256):
    R, D = arrs[0].shape
    hh = P_h.shape[1] // (4 * HGRN_DK)
    hp = D // LANES
    wh = hh * HGRN_DK
    tt_rows = min(tt_rows, T)
    nt = T // tt_rows
    col = pl.BlockSpec((tt_rows, D), lambda b, t: (b * nt + t, 0))
    const = lambda a: pl.BlockSpec(a.shape, lambda b, t: (0,) * a.ndim)
    nw = nw.reshape(1, HGRN_DK)
    lnw, lnb, rk = lnw.reshape(1, D), lnb.reshape(1, D), rk.reshape(1, D)
    oa, hs, ob, rs = pl.pallas_call(
        functools.partial(_scan_prompt_kernel, nchunks=tt_rows // CHUNK, hh=hh, hp=hp),
        grid=(B, nt),
        in_specs=[pl.BlockSpec((tt_rows, 4 * wh), lambda b, t: (b * nt + t, 0)),
                  const(lb_logits), const(nw)]
                 + [col] * 7 + [const(lnw), const(lnb), const(rk)],
        out_specs=[pl.BlockSpec((tt_rows, wh), lambda b, t: (b * nt + t, 0)),
                   pl.BlockSpec((None, hh, HGRN_DK, HGRN_DK), lambda b, t: (b, 0, 0, 0)),
                   col,
                   pl.BlockSpec((None, 2 * hp, RWKV_N, RWKV_N), lambda b, t: (b, 0, 0, 0))],
        out_shape=[jax.ShapeDtypeStruct((R, wh), F32),
                   jax.ShapeDtypeStruct((B, hh, HGRN_DK, HGRN_DK), F32),
                   jax.ShapeDtypeStruct((R, D), F32),
                   jax.ShapeDtypeStruct((B, 2 * hp, RWKV_N, RWKV_N), F32)],
        scratch_shapes=[pltpu.VMEM((hh, HGRN_DK, HGRN_DK), F32), pltpu.VMEM((hp, LANES, LANES), F32)],
        compiler_params=_cparams(("parallel", "arbitrary")),
        name="scan_prompt",
    )(P_h, lb_logits, nw, *arrs, g, lnw, lnb, rk)
    return oa, hs, ob, rs


def _rwkv_sample_kernel(r_ref, lw_ref, k_ref, v_ref, kap_ref, bet_ref, g_ref, lnw_ref, lnb_ref, rk_ref,
                        s0_ref, o_ref, s_ref, sbd_ref, *, nseq, T, hp):
    C = nseq * T
    C2 = 2 * C
    N = RWKV_N
    mA, mB = _pair_masks()
    bdm = (_iota2((LANES, LANES), 0) // N == _iota2((LANES, LANES), 1) // N).astype(F32)
    rr, cc = _iota2((C2, C2), 0), _iota2((C2, C2), 1)
    eye = (rr == cc).astype(F32)
    same = (rr // T) == (cc // T)
    strict = same & (rr > cc)
    incl = same & (rr >= cc)
    r1, c1 = _iota2((C, C), 0), _iota2((C, C), 1)
    seg_incl = ((r1 // T == c1 // T) & (r1 >= c1)).astype(F32)
    seg_all = (r1 // T == c1 // T).astype(F32)
    zero = jnp.zeros((N, N), F32)

    J = range(hp)
    cols = [slice(j * LANES, (j + 1) * LANES) for j in J]
    ld = lambda ref: [ref[:, cols[j]] for j in J]
    r, lw, k, v, kap, bet = ld(r_ref), ld(lw_ref), ld(k_ref), ld(v_ref), ld(kap_ref), ld(bet_ref)
    cs = [_dot_exact_lhs(seg_incl, lw[j]) for j in J]
    cs_end = [_dot_exact_lhs(seg_all, lw[j]) for j in J]
    sc = [_rwkv_scores(cs[j], lw[j], cs_end[j], r[j], k[j], kap[j], bet[j], mA, mB) for j in J]
    rg, kg, kd, bd, kh, bh = ([s[i] for s in sc] for i in range(6))
    vs = [_stack(v[j], mA, mB) for j in J]
    kbd = [jnp.concatenate([kd[j], bd[j]], axis=0) for j in J]
    sck = [_dot(kg[j], kbd[j], _NT) for j in J]
    scr = [_dot(rg[j], kbd[j], _NT) for j in J]
    Ak = [jnp.where(strict, sck[j][:, :C2], 0.0) for j in J]
    Ab = [jnp.where(strict, sck[j][:, C2:], 0.0) for j in J]
    Bkb = [jnp.concatenate([jnp.where(incl, scr[j][:, :C2], 0.0),
                            jnp.where(incl, -scr[j][:, C2:], 0.0)], axis=1) for j in J]
    Tm = [_neumann_inv(Ab[j], eye, int(math.log2(T)) - 1) for j in J]
    kgp = [kg[j][:C, :] + kg[j][C:, :] for j in J]
    rgp = [rg[j][:C, :] + rg[j][C:, :] for j in J]
    ks = [[] for _ in J]
    rs = [[] for _ in J]
    for n in range(nseq):
        sl = slice(n * T, (n + 1) * T)
        for j in J:
            Sbd = jnp.concatenate(
                [jnp.concatenate([s0_ref[n, 2 * j], zero], axis=1),
                 jnp.concatenate([zero, s0_ref[n, 2 * j + 1]], axis=1)], axis=0)
            sbd_ref[n, j] = Sbd
            res = _dot(jnp.concatenate([kgp[j][sl], rgp[j][sl]], axis=0), Sbd, _NT)
            ks[j].append(res[:T])
            rs[j].append(res[T:])
    KS = [_stack(jnp.concatenate(ks[j], axis=0), mA, mB) for j in J]
    RS = [_stack(jnp.concatenate(rs[j], axis=0), mA, mB) for j in J]
    rhs = [KS[j] + _dot(Ak[j], vs[j]) for j in J]
    U = [_dot_hp(Tm[j], rhs[j]) for j in J]
    vu = [jnp.concatenate([vs[j], U[j]], axis=0) for j in J]
    O = [RS[j] + _dot(Bkb[j], vu[j]) for j in J]
    Up = [U[j][:C, :] + U[j][C:, :] for j in J]
    khp = [kh[j][:C, :] + kh[j][C:, :] for j in J]
    nbhp = [-(bh[j][:C, :] + bh[j][C:, :]) for j in J]
    dec = [jnp.exp2(cs_end[j]) for j in J]
    for n in range(nseq):
        sl = slice(n * T, (n + 1) * T)
        for j in J:
            upd = _dot(jnp.concatenate([v[j][sl], Up[j][sl]], axis=0),
                       jnp.concatenate([khp[j][sl], nbhp[j][sl]], axis=0), _TN)
            Sn = sbd_ref[n, j] * dec[j][n * T:n * T + 1, :] + upd * bdm
            s_ref[n, 2 * j] = Sn[:N, :N]
            s_ref[n, 2 * j + 1] = Sn[N:, N:]
    for j in J:
        o = O[j][:C, :] + O[j][C:, :]
        o_ref[:, cols[j]] = _rwkv_epilogue(o, r[j], k[j], v[j], g_ref[:, cols[j]],
                                           lnw_ref[:, cols[j]], lnb_ref[:, cols[j]], rk_ref[:, cols[j]])


def _rwkv_sample(arrs, g, lnw, lnb, rk, s0, T, nseq=16, hp=8):
    R, D = arrs[0].shape
    NP = D // LANES
    B = R // T
    rows = nseq * T
    wd = hp * LANES
    col = pl.BlockSpec((rows, wd), lambda p, gi: (gi, p))
    par = pl.BlockSpec((1, wd), lambda p, gi: (0, p))
    sspec = pl.BlockSpec((nseq, 2 * hp, RWKV_N, RWKV_N), lambda p, gi: (gi, p, 0, 0))
    o, s = pl.pallas_call(
        functools.partial(_rwkv_sample_kernel, nseq=nseq, T=T, hp=hp),
        grid=(NP // hp, B // nseq),
        in_specs=[col] * 7 + [par] * 3 + [sspec],
        out_specs=[col, sspec],
        out_shape=[jax.ShapeDtypeStruct((R, D), F32), jax.ShapeDtypeStruct(s0.shape, F32)],
        scratch_shapes=[pltpu.VMEM((nseq, hp, LANES, LANES), F32)],
        compiler_params=_cparams(("parallel", "parallel")),
        name="rwkv_sample",
    )(*arrs, g, lnw.reshape(1, D), lnb.reshape(1, D), rk.reshape(1, D), s0)
    return o, s


def _merge_kernel(x_ref, h_ref, oa_ref, ob_ref, wg_ref, w_ref, gt_ref, o_ref, *, rep):
    D = x_ref.shape[1]
    pg = lax.dot_general(h_ref[...], wg_ref[...], _NT, preferred_element_type=F32)
    m = _sigmoid(pg[:, :D]) * oa_ref[...] + _sigmoid(pg[:, D:]) * ob_ref[...]
    o_ref[...] = x_ref[...] + _rep_rows(gt_ref[...], rep) * _dot(m, w_ref[...])


def _merge_out(x, h, o_a, o_b, w_g, w_out, mod, tm):
    R, D = x.shape
    row = pl.BlockSpec((tm, D), lambda i: (i, 0))
    const = lambda a: pl.BlockSpec(a.shape, lambda i: (0,) * a.ndim, pipeline_mode=pl.Buffered(1))
    return pl.pallas_call(
        functools.partial(_merge_kernel, rep=mod.rep),
        grid=(R // tm,),
        in_specs=[row, row, row, row, const(w_g), const(w_out), mod.spec(tm, 2)],
        out_specs=row,
        out_shape=jax.ShapeDtypeStruct((R, D), F32),
        compiler_params=_cparams(("parallel",)),
        name="merge_out",
    )(x, h, o_a, o_b, w_g, w_out, mod.arr)


def _mlp_kernel(x_ref, nw_ref, sh_ref, sc_ref, gt_ref, wu_ref, wd_ref, fw_ref, o_ref, *, tf, rep):
    x = x_ref[...]
    h = (_rms(x, nw_ref[...]) * (1.0 + _rep_rows(sc_ref[...], rep)) + _rep_rows(sh_ref[...], rep)).astype(BF16)
    acc = None
    for c in range(wu_ref.shape[1] // tf):
        u = jnp.dot(h, wu_ref[:, c * tf:(c + 1) * tf], preferred_element_type=F32)
        u = jnp.square(jnp.maximum(u, 0.0))
        part = _dot(u, wd_ref[c * tf:(c + 1) * tf, :])
        acc = part if acc is None else acc + part
    o_ref[...] = _rms(x + _rep_rows(gt_ref[...], rep) * acc, fw_ref[...])


def _mlp(x, nw, w_up, w_down, fw, mod, tm, tf=1024):
    R, D = x.shape
    const = lambda a: pl.BlockSpec(a.shape, lambda i: (0,) * a.ndim, pipeline_mode=pl.Buffered(1))
    row = pl.BlockSpec((tm, D), lambda i: (i, 0))
    nw, fw = nw.reshape(1, D), fw.reshape(1, D)
    return pl.pallas_call(
        functools.partial(_mlp_kernel, tf=tf, rep=mod.rep),
        grid=(R // tm,),
        in_specs=[row, const(nw), mod.spec(tm, 3), mod.spec(tm, 4), mod.spec(tm, 5),
                  const(w_up), const(w_down), const(fw)],
        out_specs=row,
        out_shape=jax.ShapeDtypeStruct((R, D), F32),
        compiler_params=_cparams(("parallel",)),
        name="mlp",
    )(x, nw, mod.arr, mod.arr, mod.arr, w_up, w_down, fw)


def _run(x3, mod_rows, s_hgrn, s_rwkv, s_shift, W, prompt):
    B, T, D = x3.shape
    R = B * T
    x = x3.reshape(R, D)
    if prompt:
        mod = _Mod(mod_rows.reshape(B, 1, -1), True, T, D)
        tm = min(1024, T)
    else:
        mod = _Mod(mod_rows, False, T, D)
        tm = min(512, R)

    rw = (W["mu"], W["w0"], W["a0"], W["k_k"], W["k_a"], W["wwa"], W["g2p"])
    nw1 = W["norm1_w"].reshape(1, D)
    if prompt:
        shift_in = jnp.zeros((B, 1, RC_PAD), F32)
        h, arrs, last = _in_rwkv(x, nw1, mod, W["w_r"], shift_in, rw, True, T)
        shift_out = last[:, 0, :RC_COLS]
        P_h = _matmul(h, W["w_in_t"], 4 * D, tm, 2048, "proj_h")
        o_a, hg, o_b, rs = _scan_prompt(P_h, W["lb_logits"], W["hgrn_norm_w"], arrs[:6], arrs[6],
                                        W["lnx_w"], W["lnx_b"], W["r_k"], B, T)
    else:
        shift_in = jnp.pad(s_shift, ((0, 0), (0, RC_PAD - RC_COLS)))
        h, arrs, last = _in_rwkv(x, nw1, mod, W["w_r"], shift_in, rw, False, T)
        shift_out = last[:, :RC_COLS]
        P_h = _matmul(h, W["w_in_t"], 4 * D, tm, 2048, "proj_h")
        o_a, hg = _hgrn_sample(P_h, W["lb_logits"], W["hgrn_norm_w"], s_hgrn, T)
        o_b, rs = _rwkv_sample(arrs[:6], arrs[6], W["lnx_w"], W["lnx_b"], W["r_k"], s_rwkv, T)

    x1 = _merge_out(x, h, o_a, o_b, W["w_g"], W["w_out"], mod, min(tm, 512))
    y = _mlp(x1, W["norm2_w"], W["w_up"], W["w_down"], W["final_norm_w"], mod, min(tm, 512))
    return y.reshape(B, T, D), hg, rs, shift_out


def kernel(x_prompt, x_sample, state_hgrn, state_rwkv, state_shift, c_prompt, c_sample, norm1_w, norm2_w, ada_w, ada_b, w_in, lb_logits, hgrn_norm_w, rwkv_mu, rwkv_w0, rwkv_w2, rwkv_a0, rwkv_a2, rwkv_g2, rwkv_k_k, rwkv_k_a, rwkv_r_k, rwkv_lnx_w, rwkv_lnx_b, w_out, w_up, w_down, final_norm_w):
    depth = norm1_w.shape[0]
    assert depth == 1, "single-layer step"
    D = x_prompt.shape[-1]
    HW = 4 * D
    w_in_t = w_in[0].T.astype(BF16)
    row = lambda a: a.reshape(1, -1)
    wwa = jnp.zeros((LANES, 2 * D), F32)
    wwa = wwa.at[:W_LORA, :D].set(rwkv_w2[0]).at[W_LORA:W_LORA + A_LORA, D:].set(rwkv_a2[0])
    W = {
        "norm1_w": norm1_w[0], "norm2_w": norm2_w[0], "final_norm_w": final_norm_w,
        "w_in_t": w_in_t,
        "w_r": jnp.pad(w_in[0][:, HW:HW + RC_COLS], ((0, 0), (0, RC_PAD - RC_COLS))).astype(BF16),
        "w_g": w_in_t[HW + RC_COLS:],
        "lb_logits": lb_logits, "hgrn_norm_w": hgrn_norm_w[0],
        "mu": jnp.pad(row(rwkv_mu[0]), ((0, 0), (0, RC_PAD - RC_COLS))),
        "w0": row(rwkv_w0[0]), "a0": row(rwkv_a0[0]), "k_k": row(rwkv_k_k[0]), "k_a": row(rwkv_k_a[0]),
        "wwa": wwa.astype(BF16),
        "g2p": jnp.pad(rwkv_g2[0], ((0, G_LORA_PAD - G_LORA), (0, 0))).astype(BF16),
        "lnx_w": rwkv_lnx_w[0], "lnx_b": rwkv_lnx_b[0], "r_k": rwkv_r_k[0].reshape(-1),
        "w_out": w_out[0].astype(BF16), "w_up": w_up[0].astype(BF16), "w_down": w_down[0].astype(BF16),
    }
    nb = x_prompt.shape[0]
    mod = _adaln(jnp.concatenate([c_prompt, c_sample], axis=0), ada_w[0], ada_b[0])
    y_p, hg_p, rw_p, sh_p = _run(x_prompt, mod[:nb], None, None, None, W, True)
    y_s, hg_s, rw_s, sh_s = _run(x_sample, mod[nb:], state_hgrn[0], state_rwkv[0], state_shift[0], W, False)
    ex = lambda a: a[None]
    return (y_p, y_s, ex(hg_p), ex(rw_p), ex(sh_p), ex(hg_s), ex(rw_s), ex(sh_s))
```

```python
import functools
import math

import jax
import jax.numpy as jnp
from jax import lax
from jax.experimental import pallas as pl
from jax.experimental.pallas import tpu as pltpu

F32 = jnp.float32
BF16 = jnp.bfloat16

LANES = 128
SUBLANES = 8
VMEM_LIMIT = 48 * 1024 * 1024
VMEM_LIMIT_IN = 56 * 1024 * 1024

HGRN_DK = 128
RWKV_N = 64
W_LORA = 64
A_LORA = 64
G_LORA = 160
G_LORA_PAD = 256
RC_MAIN = 3 * 1024
RC_COLS = RC_MAIN + W_LORA + A_LORA + G_LORA
RC_PAD = RC_MAIN + LANES + G_LORA_PAD
RMS_EPS = 1e-6
LNX_EPS = 64e-5
CHUNK = 64
SUB = 8
RWKV_SUB = 8


def _cparams(sem, vmem=VMEM_LIMIT):
    return pltpu.CompilerParams(dimension_semantics=sem, vmem_limit_bytes=vmem)


_NN = (((1,), (0,)), ((), ()))
_NT = (((1,), (1,)), ((), ()))
_TN = (((0,), (0,)), ((), ()))


def _dot(a, b, dims=_NN):
    return lax.dot_general(a.astype(BF16), b.astype(BF16), dims, preferred_element_type=F32)


def _split2(x):
    hi = x.astype(BF16)
    lo = (x - hi.astype(F32)).astype(BF16)
    return hi, lo


def _split3(x):
    hi = x.astype(BF16)
    r1 = x - hi.astype(F32)
    mid = r1.astype(BF16)
    lo = (r1 - mid.astype(F32)).astype(BF16)
    return hi, mid, lo


def _dot_exact_lhs(a, x, dims=_NN):
    a = a.astype(BF16)
    hi, mid, lo = _split3(x)
    d = lambda p: lax.dot_general(a, p, dims, preferred_element_type=F32)
    return d(hi) + d(mid) + d(lo)


def _dot_exact_rhs(x, b, dims=_NN, parts=3):
    b = b.astype(BF16)
    d = lambda p: lax.dot_general(p, b, dims, preferred_element_type=F32)
    if parts == 2:
        hi, lo = _split2(x)
        return d(hi) + d(lo)
    hi, mid, lo = _split3(x)
    return d(hi) + d(mid) + d(lo)


def _dot_hp(a, b, dims=_NN):
    ah, al = _split2(a)
    bh, bl = _split2(b)
    d = lambda p, q: lax.dot_general(p, q, dims, preferred_element_type=F32)
    return d(ah, bh) + d(ah, bl) + d(al, bh)


def _sigmoid(x):
    return 1.0 / (1.0 + jnp.exp(-x))


def _iota2(shape, dim):
    return lax.broadcasted_iota(jnp.int32, shape, dim)


def _neumann_inv(L, eye, nsq):
    X = L
    P = eye - L
    for _ in range(nsq):
        X = _dot_hp(X, X)
        P = P + _dot_hp(P, X)
    return P


def _adaln_kernel(c_ref, w_ref, b_ref, o_ref):
    c = c_ref[...]
    s = c * _sigmoid(c)
    o_ref[...] = _dot(s, w_ref[...]) + b_ref[...]


def _adaln(c_all, ada_w, ada_b, tn=1536):
    R, D = c_all.shape
    N = ada_w.shape[1]
    return pl.pallas_call(
        _adaln_kernel,
        grid=(N // tn,),
        in_specs=[pl.BlockSpec((R, D), lambda j: (0, 0)),
                  pl.BlockSpec((D, tn), lambda j: (0, j)),
                  pl.BlockSpec((1, tn), lambda j: (0, j))],
        out_specs=pl.BlockSpec((R, tn), lambda j: (0, j)),
        out_shape=jax.ShapeDtypeStruct((R, N), F32),
        compiler_params=_cparams(("parallel",)),
        name="adaln",
    )(c_all, ada_w, ada_b.reshape(1, N))


class _Mod:
    def __init__(self, arr, per_seq, seq_rows, D):
        self.arr, self.per_seq, self.seq_rows, self.D = arr, per_seq, seq_rows, D
        self.rep = 1 if per_seq else seq_rows

    def spec(self, tm, comp):
        D = self.D
        if self.per_seq:
            tiles = self.seq_rows // tm
            return pl.BlockSpec((None, 1, D), lambda i, *_: (i // tiles, 0, comp))
        return pl.BlockSpec((tm // self.seq_rows, D), lambda i, *_: (i, comp))


def _rep_rows(v, rep):
    if rep == 1:
        return v
    n, w = v.shape
    return jnp.broadcast_to(v[:, None, :], (n, rep, w)).reshape(n * rep, w)


def _rms(x, w):
    return x * lax.rsqrt(jnp.mean(x * x, axis=-1, keepdims=True) + RMS_EPS) * w


def _mm_kernel(x_ref, wt_ref, o_ref):
    o_ref[...] = lax.dot_general(x_ref[...], wt_ref[...], _NT, preferred_element_type=F32)


def _matmul(x, wt, n_out, tm, tn, name):
    R, K = x.shape
    N = n_out
    return pl.pallas_call(
        _mm_kernel,
        grid=(R // tm, N // tn),
        in_specs=[pl.BlockSpec((tm, K), lambda i, j: (i, 0)),
                  pl.BlockSpec((tn, K), lambda i, j: (j, 0))],
        out_specs=pl.BlockSpec((tm, tn), lambda i, j: (i, j)),
        out_shape=jax.ShapeDtypeStruct((R, N), F32),
        compiler_params=_cparams(("parallel", "arbitrary")),
        name=name,
    )(x, wt)


def _hgrn_gates(fp, lb_logits, base2=False):
    m = jnp.max(lb_logits, axis=0, keepdims=True)
    e = jnp.exp(lb_logits - m)
    lb = e[0:1, :] / jnp.sum(e, axis=0, keepdims=True)
    f = lb + (1.0 - lb) * _sigmoid(fp)
    return (jnp.log2(f) if base2 else jnp.log(f)), 1.0 - f


def _hgrn_diag(q, kin, b, c):
    R = q.shape[0]
    nb = R // c
    q3 = q.reshape(nb, c, LANES)
    k3 = kin.reshape(nb, c, LANES)
    b3 = (b * math.log2(math.e)).reshape(nb, c, LANES)
    lane_mod = lax.broadcasted_iota(jnp.int32, (nb, c, R), 2) % c
    acc = jnp.zeros((nb, c, R), F32)
    for s in range(c):
        d = jnp.minimum(b3 - b3[:, s:s + 1, :], 0.0)
        col = jnp.sum(q3 * k3[:, s:s + 1, :] * jnp.exp2(d), axis=-1, keepdims=True)
        acc = jnp.where(lane_mod == s, col, acc)
    rr, cc = _iota2((R, R), 0), _iota2((R, R), 1)
    keep = ((rr // c) == (cc // c)) & (rr >= cc)
    return jnp.where(keep, acc.reshape(R, R), 0.0)


def _hgrn_cross(q, kin, b2, m):
    R = q.shape[0]
    h = m // 2
    nb = R // m
    ref = jnp.broadcast_to(b2.reshape(nb, m, LANES)[:, h - 1:h, :], (nb, m, LANES)).reshape(R, LANES)
    upper = (_iota2((R, 1), 0) % m) >= h
    e = jnp.exp2(-jnp.abs(b2 - ref))
    qs = jnp.where(upper, q * e, 0.0)
    ks = jnp.where(upper, 0.0, kin * e)
    sc = _dot(qs, ks, _NT)
    if nb == 1:
        return sc
    same_blk = (_iota2((R, R), 0) // m) == (_iota2((R, R), 1) // m)
    return jnp.where(same_blk, sc, 0.0)


def _hgrn_out(o, og, nw):
    o = o * lax.rsqrt(jnp.mean(o * o, axis=-1, keepdims=True) + RMS_EPS) * nw
    return o * (og * _sigmoid(og))


def _hgrn_chunks(p_ref, lb_ref, nw_ref, o_ref, st_ref, offs, hp):
    C, c = CHUNK, SUB
    nb = C // c
    W = hp * LANES
    tri = (_iota2((C, C), 0) >= _iota2((C, C), 1)).astype(F32)
    nw = nw_ref[...]
    H = range(hp)
    cols = [slice(j * LANES, (j + 1) * LANES) for j in H]
    pcol = lambda g, j: slice(g * W + j * LANES, g * W + (j + 1) * LANES)
    lane_mod = lax.broadcasted_iota(jnp.int32, (nb, c, C), 2) % c
    rr, cc = _iota2((C, C), 0), _iota2((C, C), 1)
    keep = ((rr // c) == (cc // c)) & (rr >= cc)
    for r0 in offs:
        q = [p_ref[pl.ds(r0, C), pcol(0, j)] for j in H]
        v = [p_ref[pl.ds(r0, C), pcol(2, j)] for j in H]
        gates = [_hgrn_gates(p_ref[pl.ds(r0, C), pcol(1, j)], lb_ref[:, cols[j]], base2=True) for j in H]
        lf = [gt[0] for gt in gates]
        kin = [gt[1] for gt in gates]
        yield
        b = [_dot_exact_lhs(tri, lf[j]) for j in H]
        b_last = [b[j][C - 1:C, :] for j in H]
        yield
        ST = [st_ref[j] for j in H]
        o = [_dot(q[j] * jnp.exp2(b[j]), ST[j], _NT) for j in H]
        yield
        for j in H:
            st_ref[j] = ST[j] * jnp.exp2(b_last[j]) + _dot(v[j], kin[j] * jnp.exp2(b_last[j] - b[j]), _TN)
        yield
        q3 = [q[j].reshape(nb, c, LANES) for j in H]
        k3 = [kin[j].reshape(nb, c, LANES) for j in H]
        b3 = [b[j].reshape(nb, c, LANES) for j in H]
        acc = [jnp.zeros((nb, c, C), F32) for _ in H]
        for s in range(c):
            for j in H:
                d = jnp.minimum(b3[j] - b3[j][:, s:s + 1, :], 0.0)
                col = jnp.sum(q3[j] * k3[j][:, s:s + 1, :] * jnp.exp2(d), axis=-1, keepdims=True)
                acc[j] = jnp.where(lane_mod == s, col, acc[j])
            yield
        A = [jnp.where(keep, acc[j].reshape(C, C), 0.0) for j in H]
        m = 2 * c
        while m <= C:
            A = [A[j] + _hgrn_cross(q[j], kin[j], b[j], m) for j in H]
            m *= 2
            yield
        o = [o[j] + _dot(A[j], v[j]) for j in H]
        yield
        for j in H:
            o_ref[pl.ds(r0, C), cols[j]] = _hgrn_out(o[j], p_ref[pl.ds(r0, C), pcol(3, j)], nw)
        yield


def _hgrn_sample_kernel(q_ref, f_ref, i_ref, g_ref, lb_ref, nw_ref, s0_ref, o_ref, s_ref, *, nseq, T, hp):
    R = nseq * T
    rr, cc = _iota2((R, R), 0), _iota2((R, R), 1)
    same = (rr // T) == (cc // T)
    seg_incl = (same & (rr >= cc)).astype(F32)
    seg_all = same.astype(F32)
    nw = nw_ref[...]
    H = range(hp)
    cols = [slice(j * LANES, (j + 1) * LANES) for j in H]
    q = [q_ref[:, cols[j]] for j in H]
    v = [i_ref[:, cols[j]] for j in H]
    gates = [_hgrn_gates(f_ref[:, cols[j]], lb_ref[:, cols[j]]) for j in H]
    lf = [gt[0] for gt in gates]
    kin = [gt[1] for gt in gates]
    b = [_dot_exact_lhs(seg_incl, lf[j]) for j in H]
    b_tot = [_dot_exact_lhs(seg_all, lf[j]) for j in H]
    A = [_hgrn_diag(q[j], kin[j], b[j], T) for j in H]
    o = [_dot(A[j], v[j]) for j in H]
    qg = [q[j] * jnp.exp(b[j]) for j in H]
    kh = [kin[j] * jnp.exp(b_tot[j] - b[j]) for j in H]
    ecol = [jnp.transpose(jnp.exp(b_tot[j]).reshape(nseq, T, LANES)[:, 0, :]) for j in H]
    inter = [[] for _ in H]
    for n in range(nseq):
        sl = slice(n * T, (n + 1) * T)
        for j in H:
            S = s0_ref[n, j]
            inter[j].append(_dot(qg[j][sl], S))
            s_ref[n, j] = S * ecol[j][:, n:n + 1] + _dot(kh[j][sl], v[j][sl], _TN)
    for j in H:
        oj = o[j] + jnp.concatenate(inter[j], axis=0)
        o_ref[:, cols[j]] = _hgrn_out(oj, g_ref[:, cols[j]], nw)


def _hgrn_sample(P_h, lb_logits, nw, s0, T, nseq=16, hp=8):
    R, W = P_h.shape
    H = W // (4 * HGRN_DK)
    HG = H // hp
    wd = hp * HGRN_DK
    B = R // T
    rows = nseq * T
    col = lambda off: pl.BlockSpec((rows, wd), lambda h, g: (g, off * HG + h))
    sspec = pl.BlockSpec((nseq, hp, HGRN_DK, HGRN_DK), lambda h, g: (g, h, 0, 0))
    o, s = pl.pallas_call(
        functools.partial(_hgrn_sample_kernel, nseq=nseq, T=T, hp=hp),
        grid=(HG, B // nseq),
        in_specs=[col(0), col(1), col(2), col(3),
                  pl.BlockSpec((lb_logits.shape[0], wd), lambda h, g: (0, h)),
                  pl.BlockSpec((1, HGRN_DK), lambda h, g: (0, 0)),
                  sspec],
        out_specs=[pl.BlockSpec((rows, wd), lambda h, g: (g, h)), sspec],
        out_shape=[jax.ShapeDtypeStruct((R, H * HGRN_DK), F32),
                   jax.ShapeDtypeStruct(s0.shape, F32)],
        compiler_params=_cparams(("parallel", "parallel")),
        name="hgrn_sample",
    )(P_h, P_h, P_h, P_h, lb_logits, nw.reshape(1, HGRN_DK), s0)
    return o, s


def _rwkv_prep_body(rc, prev, mu_ref, w0_ref, a0_ref, kk_ref, ka_ref, wwa_ref, g2_ref, outs, rows):
    r_o, lw_o, k_o, v_o, kap_o, bet_o, g_o = outs
    D = RC_MAIN // 3
    xs = rc + mu_ref[...] * (prev - rc)
    r = xs[:, 0:D]
    kb = xs[:, D:2 * D]
    v = xs[:, 2 * D:3 * D]
    xwa = xs[:, RC_MAIN:RC_MAIN + LANES]
    xg = xs[:, RC_MAIN + LANES:]
    lane = _iota2((1, LANES), 1)
    wa = _dot(jnp.where(lane < W_LORA, jnp.tanh(xwa), xwa), wwa_ref[...])
    w = w0_ref[...] + wa[:, :D]
    a = _sigmoid(a0_ref[...] + wa[:, D:])
    lw = -(math.exp(-0.5) * math.log2(math.e)) * _sigmoid(w)
    g = _dot(_sigmoid(xg), g2_ref[...])
    kk = kb * kk_ref[...]
    seg = (_iota2((LANES, LANES), 0) // RWKV_N == _iota2((LANES, LANES), 1) // RWKV_N).astype(F32)
    k_a = ka_ref[...]
    for p in range(D // LANES):
        sl = slice(p * LANES, (p + 1) * LANES)
        kkp = kk[:, sl]
        kap = kkp * lax.rsqrt(jnp.maximum(_dot_exact_rhs(kkp * kkp, seg, parts=2), 1e-24))
        ap = a[:, sl]
        kap_o[rows, sl] = kap
        bet_o[rows, sl] = ap * kap
        k_o[rows, sl] = kb[:, sl] * (1.0 + (ap - 1.0) * k_a[:, sl])
    r_o[rows, :] = r
    lw_o[rows, :] = lw
    v_o[rows, :] = v
    g_o[rows, :] = g


def _in_rwkv_kernel(x_ref, nw_ref, sh_ref, sc_ref, wr_ref, shift_ref, mu_ref, w0_ref, a0_ref, kk_ref,
                    ka_ref, wwa_ref, g2_ref, h_ref, r_o, lw_o, k_o, v_o, kap_o, bet_o, g_o, last_o,
                    carry_ref, rc_ref, *, seq_rows, sub):
    i = pl.program_id(0)
    tm = x_ref.shape[0]
    per_seq = seq_rows >= tm

    @pl.when(i == 0)
    def _():
        carry_ref[...] = jnp.zeros_like(carry_ref)

    rep = 1 if per_seq else seq_rows
    h = (_rms(x_ref[...], nw_ref[...]) * (1.0 + _rep_rows(sc_ref[...], rep))
         + _rep_rows(sh_ref[...], rep)).astype(BF16)
    h_ref[...] = h
    outs = (r_o, lw_o, k_o, v_o, kap_o, bet_o, g_o)
    rid = _iota2((sub, 1), 0)
    rc_ref[...] = jnp.dot(h, wr_ref[...], preferred_element_type=F32)
    prev_last = None
    for s in range(tm // sub):
        rows = slice(s * sub, (s + 1) * sub)
        rc = rc_ref[rows, :]
        rolled = pltpu.roll(rc, 1, axis=0)
        if per_seq:
            if s == 0:
                start = (i % (seq_rows // tm)) == 0
                first = jnp.where(start, shift_ref[...], carry_ref[SUBLANES - 1:SUBLANES, :])
            else:
                first = prev_last
            prev = jnp.where(rid == 0, first, rolled)
        else:
            seqs = slice(s * sub // seq_rows, (s + 1) * sub // seq_rows)
            prev = jnp.where(rid % seq_rows == 0, _rep_rows(shift_ref[seqs, :], seq_rows), rolled)
            last_o[seqs, :] = rc.reshape(sub // seq_rows, seq_rows, RC_PAD)[:, seq_rows - 1, :]
        prev_last = rc[sub - 1:sub, :]
        if s == tm // sub - 1 and per_seq:
            carry_ref[...] = rc[sub - SUBLANES:, :]
            last_o[...] = prev_last
        _rwkv_prep_body(rc, prev, mu_ref, w0_ref, a0_ref, kk_ref, ka_ref, wwa_ref, g2_ref, outs, rows)


def _in_rwkv(x, nw, mod, w_r, shift, params, per_seq, seq_rows, tm=512, sub=128):
    R, D = x.shape
    tm = min(tm, seq_rows) if per_seq else min(tm, R)
    sub = min(sub, tm)
    mu, w0, a0, k_k, k_a, wwa, g2p = params
    const = lambda a: pl.BlockSpec(a.shape, lambda i: (0,) * a.ndim, pipeline_mode=pl.Buffered(1))
    wspecs = [const(a) for a in (mu, w0, a0, k_k, k_a, wwa, g2p)]
    row = lambda w: pl.BlockSpec((tm, w), lambda i: (i, 0))
    if per_seq:
        tiles = seq_rows // tm
        shift_spec = pl.BlockSpec((None, 1, RC_PAD), lambda i: (i // tiles, 0, 0))
        last_spec = pl.BlockSpec((None, 1, RC_PAD), lambda i: (i // tiles, 0, 0))
        last_shape = jax.ShapeDtypeStruct((R // seq_rows, 1, RC_PAD), F32)
    else:
        shift_spec = pl.BlockSpec((tm // seq_rows, RC_PAD), lambda i: (i, 0))
        last_spec = pl.BlockSpec((tm // seq_rows, RC_PAD), lambda i: (i, 0))
        last_shape = jax.ShapeDtypeStruct((R // seq_rows, RC_PAD), F32)
    outs = pl.pallas_call(
        functools.partial(_in_rwkv_kernel, seq_rows=seq_rows, sub=sub),
        grid=(R // tm,),
        in_specs=[row(D), const(nw), mod.spec(tm, 0), mod.spec(tm, 1), const(w_r), shift_spec] + wspecs,
        out_specs=[row(D)] + [row(D)] * 7 + [last_spec],
        out_shape=[jax.ShapeDtypeStruct((R, D), BF16)] + [jax.ShapeDtypeStruct((R, D), F32)] * 7
                  + [last_shape],
        scratch_shapes=[pltpu.VMEM((SUBLANES, RC_PAD), F32), pltpu.VMEM((tm, RC_PAD), F32)],
        compiler_params=_cparams(("arbitrary",), VMEM_LIMIT_IN),
        name="in_rwkv",
    )(x, nw, mod.arr, mod.arr, w_r, shift, mu, w0, a0, k_k, k_a, wwa, g2p)
    return outs[0], outs[1:8], outs[8]


def _pair_masks():
    lane = _iota2((1, LANES), 1)
    return (lane < RWKV_N).astype(F32), (lane >= RWKV_N).astype(F32)


def _stack(x, mA, mB, dtype=F32):
    x = x.astype(dtype)
    return jnp.concatenate([x * mA.astype(dtype), x * mB.astype(dtype)], axis=0)


def _rwkv_scores(cs, lw, cs_end, r, k, kap, bet, mA, mB, dtype=F32):
    e_neg = jnp.exp2(-cs)
    e_rem = jnp.exp2(cs_end - cs)
    rg = _stack(r * jnp.exp2(cs), mA, mB, dtype)
    kg = _stack(kap * jnp.exp2(cs - lw), mA, mB, dtype)
    kd = _stack(k * e_neg, mA, mB, dtype)
    bd = _stack(bet * e_neg, mA, mB, dtype)
    kh = _stack(k * e_rem, mA, mB, dtype)
    bh = _stack(bet * e_rem, mA, mB, dtype)
    return rg, kg, kd, bd, kh, bh


def _rwkv_epilogue(o, r, k, v, g, lnw, lnb, rk):
    avg = jnp.where(_iota2((LANES, LANES), 0) // RWKV_N == _iota2((LANES, LANES), 1) // RWKV_N,
                    1.0 / RWKV_N, 0.0).astype(F32)
    mean = _dot_exact_rhs(o, avg)
    d = o - mean
    var = _dot_exact_rhs(d * d, avg)
    on = d * lax.rsqrt(var + LNX_EPS) * lnw + lnb
    bonus = _dot_exact_rhs(r * k * rk, avg) * float(RWKV_N) * v
    return (on + bonus) * g


def _rwkv_chunks(r_ref, lw_ref, k_ref, v_ref, kap_ref, bet_ref, g_ref, lnw_ref, lnb_ref, rk_ref,
                 o_ref, st_ref, offs, hp):
    C, c = CHUNK, RWKV_SUB
    C2 = 2 * C
    mA, mB = _pair_masks()
    rr, cc = _iota2((C2, C2), 0), _iota2((C2, C2), 1)
    eye = (rr == cc).astype(F32)
    strict = rr > cc
    incl = rr >= cc
    blk = (rr // c) == (cc // c)
    tri = (_iota2((C, C), 0) >= _iota2((C, C), 1)).astype(F32)
    avg = jnp.where(_iota2((LANES, LANES), 0) // RWKV_N == _iota2((LANES, LANES), 1) // RWKV_N,
                    1.0 / RWKV_N, 0.0).astype(F32)
    chains = [(r0, p) for r0 in offs for p in range(hp)]
    J = range(len(chains))
    pcols = [slice(p * LANES, (p + 1) * LANES) for _, p in chains]

    ld = lambda ref: [ref[pl.ds(r0, C), pcols[j]] for j, (r0, _) in zip(J, chains)]
    r, lw, k, v, kap, bet = ld(r_ref), ld(lw_ref), ld(k_ref), ld(v_ref), ld(kap_ref), ld(bet_ref)
    yield
    cs = [_dot_exact_lhs(tri, lw[j]) for j in J]
    cs_end = [cs[j][C - 1:C, :] for j in J]
    yield
    sc = [_rwkv_scores(cs[j], lw[j], cs_end[j], r[j], k[j], kap[j], bet[j], mA, mB, BF16) for j in J]
    rg, kg, kd, bd, kh, bh = ([s[i] for s in sc] for i in range(6))
    vs = [_stack(v[j], mA, mB, BF16) for j in J]
    kbd = [jnp.concatenate([kd[j], bd[j]], axis=0) for j in J]
    yield
    sck = [_dot(kg[j], kbd[j], _NT) for j in J]
    yield
    scr = [_dot(rg[j], kbd[j], _NT) for j in J]
    Ak = [jnp.where(strict, sck[j][:, :C2], 0.0) for j in J]
    Ab = [jnp.where(strict, sck[j][:, C2:], 0.0) for j in J]
    AkV = [_dot(Ak[j], vs[j]) for j in J]
    yield
    Bkb = [jnp.concatenate([jnp.where(incl, scr[j][:, :C2], 0.0),
                            jnp.where(incl, -scr[j][:, C2:], 0.0)], axis=1).astype(BF16) for j in J]
    LD = [jnp.where(blk, Ab[j], 0.0) for j in J]
    X = LD
    Dinv = [eye - LD[j] for j in J]
    for _ in range(int(math.log2(c)) - 1):
        X = [_dot(X[j], X[j]) for j in J]
        yield
        Dinv = [Dinv[j] + _dot(Dinv[j], X[j]) for j in J]
        yield
    Q = [_dot(Dinv[j], Ab[j] - LD[j]) for j in J]
    yield
    T0 = [Dinv[j] - _dot(Q[j], Dinv[j]) for j in J]
    yield
    for _ in range(int(math.log2(C // c)) - 1):
        Q = [_dot(Q[j], Q[j]) for j in J]
        yield
        T0 = [T0[j] + _dot(Q[j], T0[j]) for j in J]
        yield
    E = [(eye - T0[j]) - _dot_hp(Ab[j], T0[j]) for j in J]
    yield
    Tm = [T0[j] + _dot(T0[j], E[j]) for j in J]
    yield
    khb = [jnp.concatenate([kh[j], -bh[j]], axis=0) for j in J]
    dec = [jnp.exp2(cs_end[j]) for j in J]
    yield

    for u in range(len(offs)):
        K = range(u * hp, (u + 1) * hp)
        S = {j: st_ref[chains[j][1]] for j in K}
        rhs = {j: _dot(kg[j], S[j], _NT) + AkV[j] for j in K}
        yield
        U = {j: _dot_hp(Tm[j], rhs[j]) for j in K}
        yield
        vu = {j: jnp.concatenate([vs[j], U[j].astype(BF16)], axis=0) for j in K}
        for j in K:
            st_ref[chains[j][1]] = S[j] * dec[j] + _dot(vu[j], khb[j], _TN)
        yield
        O = {j: _dot(rg[j], S[j], _NT) + _dot(Bkb[j], vu[j]) for j in K}
        o = {j: O[j][:C, :] + O[j][C:, :] for j in K}
        yield
        mean = {j: _dot_exact_rhs(o[j], avg, parts=2) for j in K}
        d = {j: o[j] - mean[j] for j in K}
        yield
        var = {j: _dot_exact_rhs(d[j] * d[j], avg, parts=2) for j in K}
        rkk = {j: r_ref[pl.ds(chains[j][0], C), pcols[j]] * k_ref[pl.ds(chains[j][0], C), pcols[j]]
               * rk_ref[:, pcols[j]] for j in K}
        yield
        bsum = {j: _dot_exact_rhs(rkk[j], avg, parts=2) for j in K}
        yield
        for j in K:
            rows = pl.ds(chains[j][0], C)
            on = d[j] * lax.rsqrt(var[j] + LNX_EPS) * lnw_ref[:, pcols[j]] + lnb_ref[:, pcols[j]]
            bonus = bsum[j] * float(RWKV_N) * v_ref[rows, pcols[j]]
            o_ref[rows, pcols[j]] = (on + bonus) * g_ref[rows, pcols[j]]
        yield


def _interleave(gens):
    live = list(gens)
    while live:
        for gen in list(live):
            try:
                next(gen)
            except StopIteration:
                live.remove(gen)


def _scan_prompt_kernel(p_ref, lb_ref, nw_ref,
                        r_ref, lw_ref, k_ref, v_ref, kap_ref, bet_ref, g_ref, lnw_ref, lnb_ref, rk_ref,
                        oa_ref, hs_ref, ob_ref, rs_ref, hst_ref, rst_ref, *, nchunks, hh, hp):
    tt = pl.program_id(1)

    @pl.when(tt == 0)
    def _():
        hst_ref[...] = jnp.zeros_like(hst_ref)
        rst_ref[...] = jnp.zeros_like(rst_ref)

    C = CHUNK
    unroll = 2 if nchunks % 2 == 0 else 1

    def body(ci, carry):
        base = ci * (unroll * C)
        offs = [pl.multiple_of(base + u * C, C) for u in range(unroll)]
        _interleave([
            _rwkv_chunks(r_ref, lw_ref, k_ref, v_ref, kap_ref, bet_ref, g_ref, lnw_ref, lnb_ref, rk_ref,
                         ob_ref, rst_ref, offs, hp),
            _hgrn_chunks(p_ref, lb_ref, nw_ref, oa_ref, hst_ref, offs, hh),
        ])
        return carry

    lax.fori_loop(0, nchunks // unroll, body, 0)

    @pl.when(tt == pl.num_programs(1) - 1)
    def _():
        for j in range(hh):
            hs_ref[j] = hst_ref[j].T
        for p in range(hp):
            S = rst_ref[p]
            rs_ref[2 * p] = S[:RWKV_N, :RWKV_N]
            rs_ref[2 * p + 1] = S[RWKV_N:, RWKV_N:]


def _scan_prompt(P_h, lb_logits, nw, arrs, g, lnw, lnb, rk, B, T, tt_rows=256):
    R, D = arrs[0].shape
    hh = P_h.shape[1] // (4 * HGRN_DK)
    hp = D // LANES
    wh = hh * HGRN_DK
    tt_rows = min(tt_rows, T)
    nt = T // tt_rows
    col = pl.BlockSpec((tt_rows, D), lambda b, t: (b * nt + t, 0))
    const = lambda a: pl.BlockSpec(a.shape, lambda b, t: (0,) * a.ndim)
    nw = nw.reshape(1, HGRN_DK)
    lnw, lnb, rk = lnw.reshape(1, D), lnb.reshape(1, D), rk.reshape(1, D)
    oa, hs, ob, rs = pl.pallas_call(
        functools.partial(_scan_prompt_kernel, nchunks=tt_rows // CHUNK, hh=hh, hp=hp),
        grid=(B, nt),
        in_specs=[pl.BlockSpec((tt_rows, 4 * wh), lambda b, t: (b * nt + t, 0)),
                  const(lb_logits), const(nw)]
                 + [col] * 7 + [const(lnw), const(lnb), const(rk)],
        out_specs=[pl.BlockSpec((tt_rows, wh), lambda b, t: (b * nt + t, 0)),
                   pl.BlockSpec((None, hh, HGRN_DK, HGRN_DK), lambda b, t: (b, 0, 0, 0)),
                   col,
                   pl.BlockSpec((None, 2 * hp, RWKV_N, RWKV_N), lambda b, t: (b, 0, 0, 0))],
        out_shape=[jax.ShapeDtypeStruct((R, wh), F32),
                   jax.ShapeDtypeStruct((B, hh, HGRN_DK, HGRN_DK), F32),
                   jax.ShapeDtypeStruct((R, D), F32),
                   jax.ShapeDtypeStruct((B, 2 * hp, RWKV_N, RWKV_N), F32)],
        scratch_shapes=[pltpu.VMEM((hh, HGRN_DK, HGRN_DK), F32), pltpu.VMEM((hp, LANES, LANES), F32)],
        compiler_params=_cparams(("parallel", "arbitrary")),
        name="scan_prompt",
    )(P_h, lb_logits, nw, *arrs, g, lnw, lnb, rk)
    return oa, hs, ob, rs


def _rwkv_sample_kernel(r_ref, lw_ref, k_ref, v_ref, kap_ref, bet_ref, g_ref, lnw_ref, lnb_ref, rk_ref,
                        s0_ref, o_ref, s_ref, sbd_ref, *, nseq, T, hp):
    C = nseq * T
    C2 = 2 * C
    N = RWKV_N
    mA, mB = _pair_masks()
    bdm = (_iota2((LANES, LANES), 0) // N == _iota2((LANES, LANES), 1) // N).astype(F32)
    rr, cc = _iota2((C2, C2), 0), _iota2((C2, C2), 1)
    eye = (rr == cc).astype(F32)
    same = (rr // T) == (cc // T)
    strict = same & (rr > cc)
    incl = same & (rr >= cc)
    r1, c1 = _iota2((C, C), 0), _iota2((C, C), 1)
    seg_incl = ((r1 // T == c1 // T) & (r1 >= c1)).astype(F32)
    seg_all = (r1 // T == c1 // T).astype(F32)
    zero = jnp.zeros((N, N), F32)

    J = range(hp)
    cols = [slice(j * LANES, (j + 1) * LANES) for j in J]
    ld = lambda ref: [ref[:, cols[j]] for j in J]
    r, lw, k, v, kap, bet = ld(r_ref), ld(lw_ref), ld(k_ref), ld(v_ref), ld(kap_ref), ld(bet_ref)
    cs = [_dot_exact_lhs(seg_incl, lw[j]) for j in J]
    cs_end = [_dot_exact_lhs(seg_all, lw[j]) for j in J]
    sc = [_rwkv_scores(cs[j], lw[j], cs_end[j], r[j], k[j], kap[j], bet[j], mA, mB) for j in J]
    rg, kg, kd, bd, kh, bh = ([s[i] for s in sc] for i in range(6))
    vs = [_stack(v[j], mA, mB) for j in J]
    kbd = [jnp.concatenate([kd[j], bd[j]], axis=0) for j in J]
    sck = [_dot(kg[j], kbd[j], _NT) for j in J]
    scr = [_dot(rg[j], kbd[j], _NT) for j in J]
    Ak = [jnp.where(strict, sck[j][:, :C2], 0.0) for j in J]
    Ab = [jnp.where(strict, sck[j][:, C2:], 0.0) for j in J]
    Bkb = [jnp.concatenate([jnp.where(incl, scr[j][:, :C2], 0.0),
                            jnp.where(incl, -scr[j][:, C2:], 0.0)], axis=1) for j in J]
    Tm = [_neumann_inv(Ab[j], eye, int(math.log2(T)) - 1) for j in J]
    kgp = [kg[j][:C, :] + kg[j][C:, :] for j in J]
    rgp = [rg[j][:C, :] + rg[j][C:, :] for j in J]
    ks = [[] for _ in J]
    rs = [[] for _ in J]
    for n in range(nseq):
        sl = slice(n * T, (n + 1) * T)
        for j in J:
            Sbd = jnp.concatenate(
                [jnp.concatenate([s0_ref[n, 2 * j], zero], axis=1),
                 jnp.concatenate([zero, s0_ref[n, 2 * j + 1]], axis=1)], axis=0)
            sbd_ref[n, j] = Sbd
            res = _dot(jnp.concatenate([kgp[j][sl], rgp[j][sl]], axis=0), Sbd, _NT)
            ks[j].append(res[:T])
            rs[j].append(res[T:])
    KS = [_stack(jnp.concatenate(ks[j], axis=0), mA, mB) for j in J]
    RS = [_stack(jnp.concatenate(rs[j], axis=0), mA, mB) for j in J]
    rhs = [KS[j] + _dot(Ak[j], vs[j]) for j in J]
    U = [_dot_hp(Tm[j], rhs[j]) for j in J]
    vu = [jnp.concatenate([vs[j], U[j]], axis=0) for j in J]
    O = [RS[j] + _dot(Bkb[j], vu[j]) for j in J]
    Up = [U[j][:C, :] + U[j][C:, :] for j in J]
    khp = [kh[j][:C, :] + kh[j][C:, :] for j in J]
    nbhp = [-(bh[j][:C, :] + bh[j][C:, :]) for j in J]
    dec = [jnp.exp2(cs_end[j]) for j in J]
    for n in range(nseq):
        sl = slice(n * T, (n + 1) * T)
        for j in J:
            upd = _dot(jnp.concatenate([v[j][sl], Up[j][sl]], axis=0),
                       jnp.concatenate([khp[j][sl], nbhp[j][sl]], axis=0), _TN)
            Sn = sbd_ref[n, j] * dec[j][n * T:n * T + 1, :] + upd * bdm
            s_ref[n, 2 * j] = Sn[:N, :N]
            s_ref[n, 2 * j + 1] = Sn[N:, N:]
    for j in J:
        o = O[j][:C, :] + O[j][C:, :]
        o_ref[:, cols[j]] = _rwkv_epilogue(o, r[j], k[j], v[j], g_ref[:, cols[j]],
                                           lnw_ref[:, cols[j]], lnb_ref[:, cols[j]], rk_ref[:, cols[j]])


def _rwkv_sample(arrs, g, lnw, lnb, rk, s0, T, nseq=16, hp=8):
    R, D = arrs[0].shape
    NP = D // LANES
    B = R // T
    rows = nseq * T
    wd = hp * LANES
    col = pl.BlockSpec((rows, wd), lambda p, gi: (gi, p))
    par = pl.BlockSpec((1, wd), lambda p, gi: (0, p))
    sspec = pl.BlockSpec((nseq, 2 * hp, RWKV_N, RWKV_N), lambda p, gi: (gi, p, 0, 0))
    o, s = pl.pallas_call(
        functools.partial(_rwkv_sample_kernel, nseq=nseq, T=T, hp=hp),
        grid=(NP // hp, B // nseq),
        in_specs=[col] * 7 + [par] * 3 + [sspec],
        out_specs=[col, sspec],
        out_shape=[jax.ShapeDtypeStruct((R, D), F32), jax.ShapeDtypeStruct(s0.shape, F32)],
        scratch_shapes=[pltpu.VMEM((nseq, hp, LANES, LANES), F32)],
        compiler_params=_cparams(("parallel", "parallel")),
        name="rwkv_sample",
    )(*arrs, g, lnw.reshape(1, D), lnb.reshape(1, D), rk.reshape(1, D), s0)
    return o, s


def _merge_kernel(x_ref, h_ref, oa_ref, ob_ref, wg_ref, w_ref, gt_ref, o_ref, *, rep):
    D = x_ref.shape[1]
    pg = lax.dot_general(h_ref[...], wg_ref[...], _NT, preferred_element_type=F32)
    m = _sigmoid(pg[:, :D]) * oa_ref[...] + _sigmoid(pg[:, D:]) * ob_ref[...]
    o_ref[...] = x_ref[...] + _rep_rows(gt_ref[...], rep) * _dot(m, w_ref[...])


def _merge_out(x, h, o_a, o_b, w_g, w_out, mod, tm):
    R, D = x.shape
    row = pl.BlockSpec((tm, D), lambda i: (i, 0))
    const = lambda a: pl.BlockSpec(a.shape, lambda i: (0,) * a.ndim, pipeline_mode=pl.Buffered(1))
    return pl.pallas_call(
        functools.partial(_merge_kernel, rep=mod.rep),
        grid=(R // tm,),
        in_specs=[row, row, row, row, const(w_g), const(w_out), mod.spec(tm, 2)],
        out_specs=row,
        out_shape=jax.ShapeDtypeStruct((R, D), F32),
        compiler_params=_cparams(("parallel",)),
        name="merge_out",
    )(x, h, o_a, o_b, w_g, w_out, mod.arr)


def _mlp_kernel(x_ref, nw_ref, sh_ref, sc_ref, gt_ref, wu_ref, wd_ref, fw_ref, o_ref, *, tf, rep):
    x = x_ref[...]
    h = (_rms(x, nw_ref[...]) * (1.0 + _rep_rows(sc_ref[...], rep)) + _rep_rows(sh_ref[...], rep)).astype(BF16)
    acc = None
    for c in range(wu_ref.shape[1] // tf):
        u = jnp.dot(h, wu_ref[:, c * tf:(c + 1) * tf], preferred_element_type=F32)
        u = jnp.square(jnp.maximum(u, 0.0))
        part = _dot(u, wd_ref[c * tf:(c + 1) * tf, :])
        acc = part if acc is None else acc + part
    o_ref[...] = _rms(x + _rep_rows(gt_ref[...], rep) * acc, fw_ref[...])


def _mlp(x, nw, w_up, w_down, fw, mod, tm, tf=1024):
    R, D = x.shape
    const = lambda a: pl.BlockSpec(a.shape, lambda i: (0,) * a.ndim, pipeline_mode=pl.Buffered(1))
    row = pl.BlockSpec((tm, D), lambda i: (i, 0))
    nw, fw = nw.reshape(1, D), fw.reshape(1, D)
    return pl.pallas_call(
        functools.partial(_mlp_kernel, tf=tf, rep=mod.rep),
        grid=(R // tm,),
        in_specs=[row, const(nw), mod.spec(tm, 3), mod.spec(tm, 4), mod.spec(tm, 5),
                  const(w_up), const(w_down), const(fw)],
        out_specs=row,
        out_shape=jax.ShapeDtypeStruct((R, D), F32),
        compiler_params=_cparams(("parallel",)),
        name="mlp",
    )(x, nw, mod.arr, mod.arr, mod.arr, w_up, w_down, fw)


def _run(x3, mod_rows, s_hgrn, s_rwkv, s_shift, W, prompt):
    B, T, D = x3.shape
    R = B * T
    x = x3.reshape(R, D)
    if prompt:
        mod = _Mod(mod_rows.reshape(B, 1, -1), True, T, D)
        tm = min(1024, T)
    else:
        mod = _Mod(mod_rows, False, T, D)
        tm = min(512, R)

    rw = (W["mu"], W["w0"], W["a0"], W["k_k"], W["k_a"], W["wwa"], W["g2p"])
    nw1 = W["norm1_w"].reshape(1, D)
    if prompt:
        shift_in = jnp.zeros((B, 1, RC_PAD), F32)
        h, arrs, last = _in_rwkv(x, nw1, mod, W["w_r"], shift_in, rw, True, T)
        shift_out = last[:, 0, :RC_COLS]
        P_h = _matmul(h, W["w_in_t"], 4 * D, tm, 2048, "proj_h")
        o_a, hg, o_b, rs = _scan_prompt(P_h, W["lb_logits"], W["hgrn_norm_w"], arrs[:6], arrs[6],
                                        W["lnx_w"], W["lnx_b"], W["r_k"], B, T)
    else:
        shift_in = jnp.pad(s_shift, ((0, 0), (0, RC_PAD - RC_COLS)))
        h, arrs, last = _in_rwkv(x, nw1, mod, W["w_r"], shift_in, rw, False, T)
        shift_out = last[:, :RC_COLS]
        P_h = _matmul(h, W["w_in_t"], 4 * D, tm, 2048, "proj_h")
        o_a, hg = _hgrn_sample(P_h, W["lb_logits"], W["hgrn_norm_w"], s_hgrn, T)
        o_b, rs = _rwkv_sample(arrs[:6], arrs[6], W["lnx_w"], W["lnx_b"], W["r_k"], s_rwkv, T)

    x1 = _merge_out(x, h, o_a, o_b, W["w_g"], W["w_out"], mod, min(tm, 512))
    y = _mlp(x1, W["norm2_w"], W["w_up"], W["w_down"], W["final_norm_w"], mod, min(tm, 512))
    return y.reshape(B, T, D), hg, rs, shift_out


def kernel(x_prompt, x_sample, state_hgrn, state_rwkv, state_shift, c_prompt, c_sample, norm1_w, norm2_w, ada_w, ada_b, w_in, lb_logits, hgrn_norm_w, rwkv_mu, rwkv_w0, rwkv_w2, rwkv_a0, rwkv_a2, rwkv_g2, rwkv_k_k, rwkv_k_a, rwkv_r_k, rwkv_lnx_w, rwkv_lnx_b, w_out, w_up, w_down, final_norm_w):
    depth = norm1_w.shape[0]
    assert depth == 1, "single-layer step"
    D = x_prompt.shape[-1]
    HW = 4 * D
    w_in_t = w_in[0].T.astype(BF16)
    row = lambda a: a.reshape(1, -1)
    wwa = jnp.zeros((LANES, 2 * D), F32)
    wwa = wwa.at[:W_LORA, :D].set(rwkv_w2[0]).at[W_LORA:W_LORA + A_LORA, D:].set(rwkv_a2[0])
    W = {
        "norm1_w": norm1_w[0], "norm2_w": norm2_w[0], "final_norm_w": final_norm_w,
        "w_in_t": w_in_t,
        "w_r": jnp.pad(w_in[0][:, HW:HW + RC_COLS], ((0, 0), (0, RC_PAD - RC_COLS))).astype(BF16),
        "w_g": w_in_t[HW + RC_COLS:],
        "lb_logits": lb_logits, "hgrn_norm_w": hgrn_norm_w[0],
        "mu": jnp.pad(row(rwkv_mu[0]), ((0, 0), (0, RC_PAD - RC_COLS))),
        "w0": row(rwkv_w0[0]), "a0": row(rwkv_a0[0]), "k_k": row(rwkv_k_k[0]), "k_a": row(rwkv_k_a[0]),
        "wwa": wwa.astype(BF16),
        "g2p": jnp.pad(rwkv_g2[0], ((0, G_LORA_PAD - G_LORA), (0, 0))).astype(BF16),
        "lnx_w": rwkv_lnx_w[0], "lnx_b": rwkv_lnx_b[0], "r_k": rwkv_r_k[0].reshape(-1),
        "w_out": w_out[0].astype(BF16), "w_up": w_up[0].astype(BF16), "w_down": w_down[0].astype(BF16),
    }
    nb = x_prompt.shape[0]
    mod = _adaln(jnp.concatenate([c_prompt, c_sample], axis=0), ada_w[0], ada_b[0])
    y_p, hg_p, rw_p, sh_p = _run(x_prompt, mod[:nb], None, None, None, W, True)
    y_s, hg_s, rw_s, sh_s = _run(x_sample, mod[nb:], state_hgrn[0], state_rwkv[0], state_shift[0], W, False)
    ex = lambda a: a[None]
    return (y_p, y_s, ex(hg_p), ex(rw_p), ex(sh_p), ex(hg_s), ex(rw_s), ex(sh_s))
```

```python
import functools
import math

import jax
import jax.numpy as jnp
from jax import lax
from jax.experimental import pallas as pl
from jax.experimental.pallas import tpu as pltpu

F32 = jnp.float32
BF16 = jnp.bfloat16

LANES = 128
SUBLANES = 8
VMEM_LIMIT = 48 * 1024 * 1024
VMEM_LIMIT_IN = 56 * 1024 * 1024

HGRN_DK = 128
RWKV_N = 64
W_LORA = 64
A_LORA = 64
G_LORA = 160
G_LORA_PAD = 256
RC_MAIN = 3 * 1024
RC_COLS = RC_MAIN + W_LORA + A_LORA + G_LORA
RC_PAD = RC_MAIN + LANES + G_LORA_PAD
RMS_EPS = 1e-6
LNX_EPS = 64e-5
CHUNK = 64
SUB = 8
RWKV_SUB = 8


def _cparams(sem, vmem=VMEM_LIMIT):
    return pltpu.CompilerParams(dimension_semantics=sem, vmem_limit_bytes=vmem)


_NN = (((1,), (0,)), ((), ()))
_NT = (((1,), (1,)), ((), ()))
_TN = (((0,), (0,)), ((), ()))


def _dot(a, b, dims=_NN):
    return lax.dot_general(a.astype(BF16), b.astype(BF16), dims, preferred_element_type=F32)


def _split2(x):
    hi = x.astype(BF16)
    lo = (x - hi.astype(F32)).astype(BF16)
    return hi, lo


def _split3(x):
    hi = x.astype(BF16)
    r1 = x - hi.astype(F32)
    mid = r1.astype(BF16)
    lo = (r1 - mid.astype(F32)).astype(BF16)
    return hi, mid, lo


def _dot_exact_lhs(a, x, dims=_NN):
    a = a.astype(BF16)
    hi, mid, lo = _split3(x)
    d = lambda p: lax.dot_general(a, p, dims, preferred_element_type=F32)
    return d(hi) + d(mid) + d(lo)


def _dot_exact_rhs(x, b, dims=_NN, parts=3):
    b = b.astype(BF16)
    d = lambda p: lax.dot_general(p, b, dims, preferred_element_type=F32)
    if parts == 2:
        hi, lo = _split2(x)
        return d(hi) + d(lo)
    hi, mid, lo = _split3(x)
    return d(hi) + d(mid) + d(lo)


def _dot_hp(a, b, dims=_NN):
    ah, al = _split2(a)
    bh, bl = _split2(b)
    d = lambda p, q: lax.dot_general(p, q, dims, preferred_element_type=F32)
    return d(ah, bh) + d(ah, bl) + d(al, bh)


def _sigmoid(x):
    return 1.0 / (1.0 + jnp.exp(-x))


def _iota2(shape, dim):
    return lax.broadcasted_iota(jnp.int32, shape, dim)


def _neumann_inv(L, eye, nsq):
    X = L
    P = eye - L
    for _ in range(nsq):
        X = _dot_hp(X, X)
        P = P + _dot_hp(P, X)
    return P


def _adaln_kernel(c_ref, w_ref, b_ref, o_ref):
    c = c_ref[...]
    s = c * _sigmoid(c)
    o_ref[...] = _dot(s, w_ref[...]) + b_ref[...]


def _adaln(c_all, ada_w, ada_b, tn=1536):
    R, D = c_all.shape
    N = ada_w.shape[1]
    return pl.pallas_call(
        _adaln_kernel,
        grid=(N // tn,),
        in_specs=[pl.BlockSpec((R, D), lambda j: (0, 0)),
                  pl.BlockSpec((D, tn), lambda j: (0, j)),
                  pl.BlockSpec((1, tn), lambda j: (0, j))],
        out_specs=pl.BlockSpec((R, tn), lambda j: (0, j)),
        out_shape=jax.ShapeDtypeStruct((R, N), F32),
        compiler_params=_cparams(("parallel",)),
        name="adaln",
    )(c_all, ada_w, ada_b.reshape(1, N))


class _Mod:
    def __init__(self, arr, per_seq, seq_rows, D):
        self.arr, self.per_seq, self.seq_rows, self.D = arr, per_seq, seq_rows, D
        self.rep = 1 if per_seq else seq_rows

    def spec(self, tm, comp):
        D = self.D
        if self.per_seq:
            tiles = self.seq_rows // tm
            return pl.BlockSpec((None, 1, D), lambda i, *_: (i // tiles, 0, comp))
        return pl.BlockSpec((tm // self.seq_rows, D), lambda i, *_: (i, comp))


def _rep_rows(v, rep):
    if rep == 1:
        return v
    n, w = v.shape
    return jnp.broadcast_to(v[:, None, :], (n, rep, w)).reshape(n * rep, w)


def _rms(x, w):
    return x * lax.rsqrt(jnp.mean(x * x, axis=-1, keepdims=True) + RMS_EPS) * w


def _mm_kernel(x_ref, wt_ref, o_ref):
    o_ref[...] = lax.dot_general(x_ref[...], wt_ref[...], _NT, preferred_element_type=F32)


def _matmul(x, wt, n_out, tm, tn, name):
    R, K = x.shape
    N = n_out
    return pl.pallas_call(
        _mm_kernel,
        grid=(R // tm, N // tn),
        in_specs=[pl.BlockSpec((tm, K), lambda i, j: (i, 0)),
                  pl.BlockSpec((tn, K), lambda i, j: (j, 0))],
        out_specs=pl.BlockSpec((tm, tn), lambda i, j: (i, j)),
        out_shape=jax.ShapeDtypeStruct((R, N), F32),
        compiler_params=_cparams(("parallel", "arbitrary")),
        name=name,
    )(x, wt)


def _hgrn_gates(fp, lb_logits, base2=False):
    m = jnp.max(lb_logits, axis=0, keepdims=True)
    e = jnp.exp(lb_logits - m)
    lb = e[0:1, :] / jnp.sum(e, axis=0, keepdims=True)
    f = lb + (1.0 - lb) * _sigmoid(fp)
    return (jnp.log2(f) if base2 else jnp.log(f)), 1.0 - f


def _hgrn_diag(q, kin, b, c):
    R = q.shape[0]
    nb = R // c
    q3 = q.reshape(nb, c, LANES)
    k3 = kin.reshape(nb, c, LANES)
    b3 = (b * math.log2(math.e)).reshape(nb, c, LANES)
    lane_mod = lax.broadcasted_iota(jnp.int32, (nb, c, R), 2) % c
    acc = jnp.zeros((nb, c, R), F32)
    for s in range(c):
        d = jnp.minimum(b3 - b3[:, s:s + 1, :], 0.0)
        col = jnp.sum(q3 * k3[:, s:s + 1, :] * jnp.exp2(d), axis=-1, keepdims=True)
        acc = jnp.where(lane_mod == s, col, acc)
    rr, cc = _iota2((R, R), 0), _iota2((R, R), 1)
    keep = ((rr // c) == (cc // c)) & (rr >= cc)
    return jnp.where(keep, acc.reshape(R, R), 0.0)


def _hgrn_cross(q, kin, b2, m):
    R = q.shape[0]
    h = m // 2
    nb = R // m
    ref = jnp.broadcast_to(b2.reshape(nb, m, LANES)[:, h - 1:h, :], (nb, m, LANES)).reshape(R, LANES)
    upper = (_iota2((R, 1), 0) % m) >= h
    e = jnp.exp2(-jnp.abs(b2 - ref))
    qs = jnp.where(upper, q * e, 0.0)
    ks = jnp.where(upper, 0.0, kin * e)
    sc = _dot(qs, ks, _NT)
    if nb == 1:
        return sc
    same_blk = (_iota2((R, R), 0) // m) == (_iota2((R, R), 1) // m)
    return jnp.where(same_blk, sc, 0.0)


def _hgrn_out(o, og, nw):
    o = o * lax.rsqrt(jnp.mean(o * o, axis=-1, keepdims=True) + RMS_EPS) * nw
    return o * (og * _sigmoid(og))


def _hgrn_chunks(p_ref, lb_ref, nw_ref, o_ref, st_ref, offs, hp):
    C, c = CHUNK, SUB
    nb = C // c
    W = hp * LANES
    tri = (_iota2((C, C), 0) >= _iota2((C, C), 1)).astype(F32)
    nw = nw_ref[...]
    H = range(hp)
    cols = [slice(j * LANES, (j + 1) * LANES) for j in H]
    pcol = lambda g, j: slice(g * W + j * LANES, g * W + (j + 1) * LANES)
    lane_mod = lax.broadcasted_iota(jnp.int32, (nb, c, C), 2) % c
    rr, cc = _iota2((C, C), 0), _iota2((C, C), 1)
    keep = ((rr // c) == (cc // c)) & (rr >= cc)
    for r0 in offs:
        q = [p_ref[pl.ds(r0, C), pcol(0, j)] for j in H]
        v = [p_ref[pl.ds(r0, C), pcol(2, j)] for j in H]
        gates = [_hgrn_gates(p_ref[pl.ds(r0, C), pcol(1, j)], lb_ref[:, cols[j]], base2=True) for j in H]
        lf = [gt[0] for gt in gates]
        kin = [gt[1] for gt in gates]
        yield
        b = [_dot_exact_lhs(tri, lf[j]) for j in H]
        b_last = [b[j][C - 1:C, :] for j in H]
        yield
        ST = [st_ref[j] for j in H]
        o = [_dot(q[j] * jnp.exp2(b[j]), ST[j], _NT) for j in H]
        yield
        for j in H:
            st_ref[j] = ST[j] * jnp.exp2(b_last[j]) + _dot(v[j], kin[j] * jnp.exp2(b_last[j] - b[j]), _TN)
        yield
        q3 = [q[j].reshape(nb, c, LANES) for j in H]
        k3 = [kin[j].reshape(nb, c, LANES) for j in H]
        b3 = [b[j].reshape(nb, c, LANES) for j in H]
        acc = [jnp.zeros((nb, c, C), F32) for _ in H]
        for s in range(c):
            for j in H:
                d = jnp.minimum(b3[j] - b3[j][:, s:s + 1, :], 0.0)
                col = jnp.sum(q3[j] * k3[j][:, s:s + 1, :] * jnp.exp2(d), axis=-1, keepdims=True)
                acc[j] = jnp.where(lane_mod == s, col, acc[j])
            yield
        A = [jnp.where(keep, acc[j].reshape(C, C), 0.0) for j in H]
        m = 2 * c
        while m <= C:
            A = [A[j] + _hgrn_cross(q[j], kin[j], b[j], m) for j in H]
            m *= 2
            yield
        o = [o[j] + _dot(A[j], v[j]) for j in H]
        yield
        for j in H:
            o_ref[pl.ds(r0, C), cols[j]] = _hgrn_out(o[j], p_ref[pl.ds(r0, C), pcol(3, j)], nw)
        yield


def _hgrn_sample_kernel(q_ref, f_ref, i_ref, g_ref, lb_ref, nw_ref, s0_ref, o_ref, s_ref, *, nseq, T, hp):
    R = nseq * T
    rr, cc = _iota2((R, R), 0), _iota2((R, R), 1)
    same = (rr // T) == (cc // T)
    seg_incl = (same & (rr >= cc)).astype(F32)
    seg_all = same.astype(F32)
    nw = nw_ref[...]
    H = range(hp)
    cols = [slice(j * LANES, (j + 1) * LANES) for j in H]
    q = [q_ref[:, cols[j]] for j in H]
    v = [i_ref[:, cols[j]] for j in H]
    gates = [_hgrn_gates(f_ref[:, cols[j]], lb_ref[:, cols[j]]) for j in H]
    lf = [gt[0] for gt in gates]
    kin = [gt[1] for gt in gates]
    b = [_dot_exact_lhs(seg_incl, lf[j]) for j in H]
    b_tot = [_dot_exact_lhs(seg_all, lf[j]) for j in H]
    A = [_hgrn_diag(q[j], kin[j], b[j], T) for j in H]
    o = [_dot(A[j], v[j]) for j in H]
    qg = [q[j] * jnp.exp(b[j]) for j in H]
    kh = [kin[j] * jnp.exp(b_tot[j] - b[j]) for j in H]
    ecol = [jnp.transpose(jnp.exp(b_tot[j]).reshape(nseq, T, LANES)[:, 0, :]) for j in H]
    inter = [[] for _ in H]
    for n in range(nseq):
        sl = slice(n * T, (n + 1) * T)
        for j in H:
            S = s0_ref[n, j]
            inter[j].append(_dot(qg[j][sl], S))
            s_ref[n, j] = S * ecol[j][:, n:n + 1] + _dot(kh[j][sl], v[j][sl], _TN)
    for j in H:
        oj = o[j] + jnp.concatenate(inter[j], axis=0)
        o_ref[:, cols[j]] = _hgrn_out(oj, g_ref[:, cols[j]], nw)


def _hgrn_sample(P_h, lb_logits, nw, s0, T, nseq=16, hp=8):
    R, W = P_h.shape
    H = W // (4 * HGRN_DK)
    HG = H // hp
    wd = hp * HGRN_DK
    B = R // T
    rows = nseq * T
    col = lambda off: pl.BlockSpec((rows, wd), lambda h, g: (g, off * HG + h))
    sspec = pl.BlockSpec((nseq, hp, HGRN_DK, HGRN_DK), lambda h, g: (g, h, 0, 0))
    o, s = pl.pallas_call(
        functools.partial(_hgrn_sample_kernel, nseq=nseq, T=T, hp=hp),
        grid=(HG, B // nseq),
        in_specs=[col(0), col(1), col(2), col(3),
                  pl.BlockSpec((lb_logits.shape[0], wd), lambda h, g: (0, h)),
                  pl.BlockSpec((1, HGRN_DK), lambda h, g: (0, 0)),
                  sspec],
        out_specs=[pl.BlockSpec((rows, wd), lambda h, g: (g, h)), sspec],
        out_shape=[jax.ShapeDtypeStruct((R, H * HGRN_DK), F32),
                   jax.ShapeDtypeStruct(s0.shape, F32)],
        compiler_params=_cparams(("parallel", "parallel")),
        name="hgrn_sample",
    )(P_h, P_h, P_h, P_h, lb_logits, nw.reshape(1, HGRN_DK), s0)
    return o, s


def _rwkv_prep_body(rc, prev, mu_ref, w0_ref, a0_ref, kk_ref, ka_ref, wwa_ref, g2_ref, outs, rows):
    r_o, lw_o, k_o, v_o, kap_o, bet_o, g_o = outs
    D = RC_MAIN // 3
    xs = rc + mu_ref[...] * (prev - rc)
    r = xs[:, 0:D]
    kb = xs[:, D:2 * D]
    v = xs[:, 2 * D:3 * D]
    xwa = xs[:, RC_MAIN:RC_MAIN + LANES]
    xg = xs[:, RC_MAIN + LANES:]
    lane = _iota2((1, LANES), 1)
    wa = _dot(jnp.where(lane < W_LORA, jnp.tanh(xwa), xwa), wwa_ref[...])
    w = w0_ref[...] + wa[:, :D]
    a = _sigmoid(a0_ref[...] + wa[:, D:])
    lw = -(math.exp(-0.5) * math.log2(math.e)) * _sigmoid(w)
    g = _dot(_sigmoid(xg), g2_ref[...])
    kk = kb * kk_ref[...]
    mA, mB = _pair_masks()
    k_a = ka_ref[...]
    for p in range(D // LANES):
        sl = slice(p * LANES, (p + 1) * LANES)
        kkp = kk[:, sl]
        kap = kkp * lax.rsqrt(jnp.maximum(_head_sum(kkp * kkp, mA, mB), 1e-24))
        ap = a[:, sl]
        kap_o[rows, sl] = kap
        bet_o[rows, sl] = ap * kap
        k_o[rows, sl] = kb[:, sl] * (1.0 + (ap - 1.0) * k_a[:, sl])
    r_o[rows, :] = r
    lw_o[rows, :] = lw
    v_o[rows, :] = v
    g_o[rows, :] = g


def _in_rwkv_kernel(x_ref, nw_ref, sh_ref, sc_ref, wr_ref, shift_ref, mu_ref, w0_ref, a0_ref, kk_ref,
                    ka_ref, wwa_ref, g2_ref, h_ref, r_o, lw_o, k_o, v_o, kap_o, bet_o, g_o, last_o,
                    carry_ref, rc_ref, *, seq_rows, sub):
    i = pl.program_id(0)
    tm = x_ref.shape[0]
    per_seq = seq_rows >= tm

    @pl.when(i == 0)
    def _():
        carry_ref[...] = jnp.zeros_like(carry_ref)

    rep = 1 if per_seq else seq_rows
    h = (_rms(x_ref[...], nw_ref[...]) * (1.0 + _rep_rows(sc_ref[...], rep))
         + _rep_rows(sh_ref[...], rep)).astype(BF16)
    h_ref[...] = h
    outs = (r_o, lw_o, k_o, v_o, kap_o, bet_o, g_o)
    rid = _iota2((sub, 1), 0)
    rc_ref[...] = jnp.dot(h, wr_ref[...], preferred_element_type=F32)
    prev_last = None
    for s in range(tm // sub):
        rows = slice(s * sub, (s + 1) * sub)
        rc = rc_ref[rows, :]
        rolled = pltpu.roll(rc, 1, axis=0)
        if per_seq:
            if s == 0:
                start = (i % (seq_rows // tm)) == 0
                first = jnp.where(start, shift_ref[...], carry_ref[SUBLANES - 1:SUBLANES, :])
            else:
                first = prev_last
            prev = jnp.where(rid == 0, first, rolled)
        else:
            seqs = slice(s * sub // seq_rows, (s + 1) * sub // seq_rows)
            prev = jnp.where(rid % seq_rows == 0, _rep_rows(shift_ref[seqs, :], seq_rows), rolled)
            last_o[seqs, :] = rc.reshape(sub // seq_rows, seq_rows, RC_PAD)[:, seq_rows - 1, :]
        prev_last = rc[sub - 1:sub, :]
        if s == tm // sub - 1 and per_seq:
            carry_ref[...] = rc[sub - SUBLANES:, :]
            last_o[...] = prev_last
        _rwkv_prep_body(rc, prev, mu_ref, w0_ref, a0_ref, kk_ref, ka_ref, wwa_ref, g2_ref, outs, rows)


def _in_rwkv(x, nw, mod, w_r, shift, params, per_seq, seq_rows, tm=512, sub=128):
    R, D = x.shape
    tm = min(tm, seq_rows) if per_seq else min(tm, R)
    sub = min(sub, tm)
    mu, w0, a0, k_k, k_a, wwa, g2p = params
    const = lambda a: pl.BlockSpec(a.shape, lambda i: (0,) * a.ndim, pipeline_mode=pl.Buffered(1))
    wspecs = [const(a) for a in (mu, w0, a0, k_k, k_a, wwa, g2p)]
    row = lambda w: pl.BlockSpec((tm, w), lambda i: (i, 0))
    if per_seq:
        tiles = seq_rows // tm
        shift_spec = pl.BlockSpec((None, 1, RC_PAD), lambda i: (i // tiles, 0, 0))
        last_spec = pl.BlockSpec((None, 1, RC_PAD), lambda i: (i // tiles, 0, 0))
        last_shape = jax.ShapeDtypeStruct((R // seq_rows, 1, RC_PAD), F32)
    else:
        shift_spec = pl.BlockSpec((tm // seq_rows, RC_PAD), lambda i: (i, 0))
        last_spec = pl.BlockSpec((tm // seq_rows, RC_PAD), lambda i: (i, 0))
        last_shape = jax.ShapeDtypeStruct((R // seq_rows, RC_PAD), F32)
    outs = pl.pallas_call(
        functools.partial(_in_rwkv_kernel, seq_rows=seq_rows, sub=sub),
        grid=(R // tm,),
        in_specs=[row(D), const(nw), mod.spec(tm, 0), mod.spec(tm, 1), const(w_r), shift_spec] + wspecs,
        out_specs=[row(D)] + [row(D)] * 7 + [last_spec],
        out_shape=[jax.ShapeDtypeStruct((R, D), BF16)] + [jax.ShapeDtypeStruct((R, D), F32)] * 7
                  + [last_shape],
        scratch_shapes=[pltpu.VMEM((SUBLANES, RC_PAD), F32), pltpu.VMEM((tm, RC_PAD), F32)],
        compiler_params=_cparams(("arbitrary",), VMEM_LIMIT_IN),
        name="in_rwkv",
    )(x, nw, mod.arr, mod.arr, w_r, shift, mu, w0, a0, k_k, k_a, wwa, g2p)
    return outs[0], outs[1:8], outs[8]


def _pair_masks():
    lane = _iota2((1, LANES), 1)
    return (lane < RWKV_N).astype(F32), (lane >= RWKV_N).astype(F32)


def _head_sum(x, mA, mB):
    sa = jnp.sum(x * mA, axis=-1, keepdims=True)
    sb = jnp.sum(x * mB, axis=-1, keepdims=True)
    return sa * mA + sb * mB


def _stack(x, mA, mB, dtype=F32):
    x = x.astype(dtype)
    return jnp.concatenate([x * mA.astype(dtype), x * mB.astype(dtype)], axis=0)


def _rwkv_scores(cs, lw, cs_end, r, k, kap, bet, mA, mB, dtype=F32):
    e_neg = jnp.exp2(-cs)
    e_rem = jnp.exp2(cs_end - cs)
    rg = _stack(r * jnp.exp2(cs), mA, mB, dtype)
    kg = _stack(kap * jnp.exp2(cs - lw), mA, mB, dtype)
    kd = _stack(k * e_neg, mA, mB, dtype)
    bd = _stack(bet * e_neg, mA, mB, dtype)
    kh = _stack(k * e_rem, mA, mB, dtype)
    bh = _stack(bet * e_rem, mA, mB, dtype)
    return rg, kg, kd, bd, kh, bh


def _rwkv_epilogue(o, r, k, v, g, lnw, lnb, rk):
    avg = jnp.where(_iota2((LANES, LANES), 0) // RWKV_N == _iota2((LANES, LANES), 1) // RWKV_N,
                    1.0 / RWKV_N, 0.0).astype(F32)
    mean = _dot_exact_rhs(o, avg)
    d = o - mean
    var = _dot_exact_rhs(d * d, avg)
    on = d * lax.rsqrt(var + LNX_EPS) * lnw + lnb
    bonus = _dot_exact_rhs(r * k * rk, avg) * float(RWKV_N) * v
    return (on + bonus) * g


def _rwkv_chunks(r_ref, lw_ref, k_ref, v_ref, kap_ref, bet_ref, g_ref, lnw_ref, lnb_ref, rk_ref,
                 o_ref, st_ref, offs, hp):
    C, c = CHUNK, RWKV_SUB
    C2 = 2 * C
    mA, mB = _pair_masks()
    rr, cc = _iota2((C2, C2), 0), _iota2((C2, C2), 1)
    eye = (rr == cc).astype(F32)
    strict = rr > cc
    incl = rr >= cc
    blk = (rr // c) == (cc // c)
    tri = (_iota2((C, C), 0) >= _iota2((C, C), 1)).astype(F32)
    chains = [(r0, p) for r0 in offs for p in range(hp)]
    J = range(len(chains))
    pcols = [slice(p * LANES, (p + 1) * LANES) for _, p in chains]

    ld = lambda ref: [ref[pl.ds(r0, C), pcols[j]] for j, (r0, _) in zip(J, chains)]
    r, lw, k, v, kap, bet = ld(r_ref), ld(lw_ref), ld(k_ref), ld(v_ref), ld(kap_ref), ld(bet_ref)
    yield
    cs = [_dot_exact_lhs(tri, lw[j]) for j in J]
    cs_end = [cs[j][C - 1:C, :] for j in J]
    yield
    sc = [_rwkv_scores(cs[j], lw[j], cs_end[j], r[j], k[j], kap[j], bet[j], mA, mB, BF16) for j in J]
    rg, kg, kd, bd, kh, bh = ([s[i] for s in sc] for i in range(6))
    vs = [_stack(v[j], mA, mB, BF16) for j in J]
    kbd = [jnp.concatenate([kd[j], bd[j]], axis=0) for j in J]
    yield
    sck = [_dot(kg[j], kbd[j], _NT) for j in J]
    yield
    scr = [_dot(rg[j], kbd[j], _NT) for j in J]
    Ak = [jnp.where(strict, sck[j][:, :C2], 0.0) for j in J]
    Ab = [jnp.where(strict, sck[j][:, C2:], 0.0) for j in J]
    AkV = [_dot(Ak[j], vs[j]) for j in J]
    yield
    Bkb = [jnp.concatenate([jnp.where(incl, scr[j][:, :C2], 0.0),
                            jnp.where(incl, -scr[j][:, C2:], 0.0)], axis=1).astype(BF16) for j in J]
    LD = [jnp.where(blk, Ab[j], 0.0) for j in J]
    X = LD
    Dinv = [eye - LD[j] for j in J]
    for _ in range(int(math.log2(c)) - 1):
        X = [_dot(X[j], X[j]) for j in J]
        yield
        Dinv = [Dinv[j] + _dot(Dinv[j], X[j]) for j in J]
        yield
    Q = [_dot(Dinv[j], Ab[j] - LD[j]) for j in J]
    yield
    T0 = [Dinv[j] - _dot(Q[j], Dinv[j]) for j in J]
    yield
    for _ in range(int(math.log2(C // c)) - 1):
        Q = [_dot(Q[j], Q[j]) for j in J]
        yield
        T0 = [T0[j] + _dot(Q[j], T0[j]) for j in J]
        yield
    E = [(eye - T0[j]) - _dot_hp(Ab[j], T0[j]) for j in J]
    yield
    Tm = [T0[j] + _dot(T0[j], E[j]) for j in J]
    yield
    khb = [jnp.concatenate([kh[j], -bh[j]], axis=0) for j in J]
    dec = [jnp.exp2(cs_end[j]) for j in J]
    yield

    for u in range(len(offs)):
        K = range(u * hp, (u + 1) * hp)
        S = {j: st_ref[chains[j][1]] for j in K}
        rhs = {j: _dot(kg[j], S[j], _NT) + AkV[j] for j in K}
        yield
        U = {j: _dot_hp(Tm[j], rhs[j]) for j in K}
        yield
        vu = {j: jnp.concatenate([vs[j], U[j].astype(BF16)], axis=0) for j in K}
        for j in K:
            st_ref[chains[j][1]] = S[j] * dec[j] + _dot(vu[j], khb[j], _TN)
        yield
        O = {j: _dot(rg[j], S[j], _NT) + _dot(Bkb[j], vu[j]) for j in K}
        o = {j: O[j][:C, :] + O[j][C:, :] for j in K}
        yield
        mean = {j: _head_sum(o[j], mA, mB) * (1.0 / RWKV_N) for j in K}
        d = {j: o[j] - mean[j] for j in K}
        yield
        var = {j: _head_sum(d[j] * d[j], mA, mB) * (1.0 / RWKV_N) for j in K}
        rkk = {j: r_ref[pl.ds(chains[j][0], C), pcols[j]] * k_ref[pl.ds(chains[j][0], C), pcols[j]]
               * rk_ref[:, pcols[j]] for j in K}
        yield
        bsum = {j: _head_sum(rkk[j], mA, mB) for j in K}
        yield
        for j in K:
            rows = pl.ds(chains[j][0], C)
            on = d[j] * lax.rsqrt(var[j] + LNX_EPS) * lnw_ref[:, pcols[j]] + lnb_ref[:, pcols[j]]
            bonus = bsum[j] * v_ref[rows, pcols[j]]
            o_ref[rows, pcols[j]] = (on + bonus) * g_ref[rows, pcols[j]]
        yield


def _interleave(gens):
    live = list(gens)
    while live:
        for gen in list(live):
            try:
                next(gen)
            except StopIteration:
                live.remove(gen)


def _scan_prompt_kernel(p_ref, lb_ref, nw_ref,
                        r_ref, lw_ref, k_ref, v_ref, kap_ref, bet_ref, g_ref, lnw_ref, lnb_ref, rk_ref,
                        oa_ref, hs_ref, ob_ref, rs_ref, hst_ref, rst_ref, *, nchunks, hh, hp):
    tt = pl.program_id(1)

    @pl.when(tt == 0)
    def _():
        hst_ref[...] = jnp.zeros_like(hst_ref)
        rst_ref[...] = jnp.zeros_like(rst_ref)

    C = CHUNK
    unroll = 2 if nchunks % 2 == 0 else 1

    def body(ci, carry):
        base = ci * (unroll * C)
        offs = [pl.multiple_of(base + u * C, C) for u in range(unroll)]
        _interleave([
            _rwkv_chunks(r_ref, lw_ref, k_ref, v_ref, kap_ref, bet_ref, g_ref, lnw_ref, lnb_ref, rk_ref,
                         ob_ref, rst_ref, offs, hp),
            _hgrn_chunks(p_ref, lb_ref, nw_ref, oa_ref, hst_ref, offs, hh),
        ])
        return carry

    lax.fori_loop(0, nchunks // unroll, body, 0)

    @pl.when(tt == pl.num_programs(1) - 1)
    def _():
        for j in range(hh):
            hs_ref[j] = hst_ref[j].T
        for p in range(hp):
            S = rst_ref[p]
            rs_ref[2 * p] = S[:RWKV_N, :RWKV_N]
            rs_ref[2 * p + 1] = S[RWKV_N:, RWKV_N:]


def _scan_prompt(P_h, lb_logits, nw, arrs, g, lnw, lnb, rk, B, T, tt_rows=256):
    R, D = arrs[0].shape
    hh = P_h.shape[1] // (4 * HGRN_DK)
    hp = D // LANES
    wh = hh * HGRN_DK
    tt_rows = min(tt_rows, T)
    nt = T // tt_rows
    col = pl.BlockSpec((tt_rows, D), lambda b, t: (b * nt + t, 0))
    const = lambda a: pl.BlockSpec(a.shape, lambda b, t: (0,) * a.ndim)
    nw = nw.reshape(1, HGRN_DK)
    lnw, lnb, rk = lnw.reshape(1, D), lnb.reshape(1, D), rk.reshape(1, D)
    oa, hs, ob, rs = pl.pallas_call(
        functools.partial(_scan_prompt_kernel, nchunks=tt_rows // CHUNK, hh=hh, hp=hp),
        grid=(B, nt),
        in_specs=[pl.BlockSpec((tt_rows, 4 * wh), lambda b, t: (b * nt + t, 0)),
                  const(lb_logits), const(nw)]
                 + [col] * 7 + [const(lnw), const(lnb), const(rk)],
        out_specs=[pl.BlockSpec((tt_rows, wh), lambda b, t: (b * nt + t, 0)),
                   pl.BlockSpec((None, hh, HGRN_DK, HGRN_DK), lambda b, t: (b, 0, 0, 0)),
                   col,
                   pl.BlockSpec((None, 2 * hp, RWKV_N, RWKV_N), lambda b, t: (b, 0, 0, 0))],
        out_shape=[jax.ShapeDtypeStruct((R, wh), F32),
                   jax.ShapeDtypeStruct((B, hh, HGRN_DK, HGRN_DK), F32),
                   jax.ShapeDtypeStruct((R, D), F32),
                   jax.ShapeDtypeStruct((B, 2 * hp, RWKV_N, RWKV_N), F32)],
        scratch_shapes=[pltpu.VMEM((hh, HGRN_DK, HGRN_DK), F32), pltpu.VMEM((hp, LANES, LANES), F32)],
        compiler_params=_cparams(("parallel", "arbitrary")),
        name="scan_prompt",
    )(P_h, lb_logits, nw, *arrs, g, lnw, lnb, rk)
    return oa, hs, ob, rs


def _rwkv_sample_kernel(r_ref, lw_ref, k_ref, v_ref, kap_ref, bet_ref, g_ref, lnw_ref, lnb_ref, rk_ref,
                        s0_ref, o_ref, s_ref, sbd_ref, *, nseq, T, hp):
    C = nseq * T
    C2 = 2 * C
    N = RWKV_N
    mA, mB = _pair_masks()
    bdm = (_iota2((LANES, LANES), 0) // N == _iota2((LANES, LANES), 1) // N).astype(F32)
    rr, cc = _iota2((C2, C2), 0), _iota2((C2, C2), 1)
    eye = (rr == cc).astype(F32)
    same = (rr // T) == (cc // T)
    strict = same & (rr > cc)
    incl = same & (rr >= cc)
    r1, c1 = _iota2((C, C), 0), _iota2((C, C), 1)
    seg_incl = ((r1 // T == c1 // T) & (r1 >= c1)).astype(F32)
    seg_all = (r1 // T == c1 // T).astype(F32)
    zero = jnp.zeros((N, N), F32)

    J = range(hp)
    cols = [slice(j * LANES, (j + 1) * LANES) for j in J]
    ld = lambda ref: [ref[:, cols[j]] for j in J]
    r, lw, k, v, kap, bet = ld(r_ref), ld(lw_ref), ld(k_ref), ld(v_ref), ld(kap_ref), ld(bet_ref)
    cs = [_dot_exact_lhs(seg_incl, lw[j]) for j in J]
    cs_end = [_dot_exact_lhs(seg_all, lw[j]) for j in J]
    sc = [_rwkv_scores(cs[j], lw[j], cs_end[j], r[j], k[j], kap[j], bet[j], mA, mB) for j in J]
    rg, kg, kd, bd, kh, bh = ([s[i] for s in sc] for i in range(6))
    vs = [_stack(v[j], mA, mB) for j in J]
    kbd = [jnp.concatenate([kd[j], bd[j]], axis=0) for j in J]
    sck = [_dot(kg[j], kbd[j], _NT) for j in J]
    scr = [_dot(rg[j], kbd[j], _NT) for j in J]
    Ak = [jnp.where(strict, sck[j][:, :C2], 0.0) for j in J]
    Ab = [jnp.where(strict, sck[j][:, C2:], 0.0) for j in J]
    Bkb = [jnp.concatenate([jnp.where(incl, scr[j][:, :C2], 0.0),
                            jnp.where(incl, -scr[j][:, C2:], 0.0)], axis=1) for j in J]
    Tm = [_neumann_inv(Ab[j], eye, int(math.log2(T)) - 1) for j in J]
    kgp = [kg[j][:C, :] + kg[j][C:, :] for j in J]
    rgp = [rg[j][:C, :] + rg[j][C:, :] for j in J]
    ks = [[] for _ in J]
    rs = [[] for _ in J]
    for n in range(nseq):
        sl = slice(n * T, (n + 1) * T)
        for j in J:
            Sbd = jnp.concatenate(
                [jnp.concatenate([s0_ref[n, 2 * j], zero], axis=1),
                 jnp.concatenate([zero, s0_ref[n, 2 * j + 1]], axis=1)], axis=0)
            sbd_ref[n, j] = Sbd
            res = _dot(jnp.concatenate([kgp[j][sl], rgp[j][sl]], axis=0), Sbd, _NT)
            ks[j].append(res[:T])
            rs[j].append(res[T:])
    KS = [_stack(jnp.concatenate(ks[j], axis=0), mA, mB) for j in J]
    RS = [_stack(jnp.concatenate(rs[j], axis=0), mA, mB) for j in J]
    rhs = [KS[j] + _dot(Ak[j], vs[j]) for j in J]
    U = [_dot_hp(Tm[j], rhs[j]) for j in J]
    vu = [jnp.concatenate([vs[j], U[j]], axis=0) for j in J]
    O = [RS[j] + _dot(Bkb[j], vu[j]) for j in J]
    Up = [U[j][:C, :] + U[j][C:, :] for j in J]
    khp = [kh[j][:C, :] + kh[j][C:, :] for j in J]
    nbhp = [-(bh[j][:C, :] + bh[j][C:, :]) for j in J]
    dec = [jnp.exp2(cs_end[j]) for j in J]
    for n in range(nseq):
        sl = slice(n * T, (n + 1) * T)
        for j in J:
            upd = _dot(jnp.concatenate([v[j][sl], Up[j][sl]], axis=0),
                       jnp.concatenate([khp[j][sl], nbhp[j][sl]], axis=0), _TN)
            Sn = sbd_ref[n, j] * dec[j][n * T:n * T + 1, :] + upd * bdm
            s_ref[n, 2 * j] = Sn[:N, :N]
            s_ref[n, 2 * j + 1] = Sn[N:, N:]
    for j in J:
        o = O[j][:C, :] + O[j][C:, :]
        o_ref[:, cols[j]] = _rwkv_epilogue(o, r[j], k[j], v[j], g_ref[:, cols[j]],
                                           lnw_ref[:, cols[j]], lnb_ref[:, cols[j]], rk_ref[:, cols[j]])


def _rwkv_sample(arrs, g, lnw, lnb, rk, s0, T, nseq=16, hp=8):
    R, D = arrs[0].shape
    NP = D // LANES
    B = R // T
    rows = nseq * T
    wd = hp * LANES
    col = pl.BlockSpec((rows, wd), lambda p, gi: (gi, p))
    par = pl.BlockSpec((1, wd), lambda p, gi: (0, p))
    sspec = pl.BlockSpec((nseq, 2 * hp, RWKV_N, RWKV_N), lambda p, gi: (gi, p, 0, 0))
    o, s = pl.pallas_call(
        functools.partial(_rwkv_sample_kernel, nseq=nseq, T=T, hp=hp),
        grid=(NP // hp, B // nseq),
        in_specs=[col] * 7 + [par] * 3 + [sspec],
        out_specs=[col, sspec],
        out_shape=[jax.ShapeDtypeStruct((R, D), F32), jax.ShapeDtypeStruct(s0.shape, F32)],
        scratch_shapes=[pltpu.VMEM((nseq, hp, LANES, LANES), F32)],
        compiler_params=_cparams(("parallel", "parallel")),
        name="rwkv_sample",
    )(*arrs, g, lnw.reshape(1, D), lnb.reshape(1, D), rk.reshape(1, D), s0)
    return o, s


def _merge_kernel(x_ref, h_ref, oa_ref, ob_ref, wg_ref, w_ref, gt_ref, o_ref, *, rep):
    D = x_ref.shape[1]
    pg = lax.dot_general(h_ref[...], wg_ref[...], _NT, preferred_element_type=F32)
    m = _sigmoid(pg[:, :D]) * oa_ref[...] + _sigmoid(pg[:, D:]) * ob_ref[...]
    o_ref[...] = x_ref[...] + _rep_rows(gt_ref[...], rep) * _dot(m, w_ref[...])


def _merge_out(x, h, o_a, o_b, w_g, w_out, mod, tm):
    R, D = x.shape
    row = pl.BlockSpec((tm, D), lambda i: (i, 0))
    const = lambda a: pl.BlockSpec(a.shape, lambda i: (0,) * a.ndim, pipeline_mode=pl.Buffered(1))
    return pl.pallas_call(
        functools.partial(_merge_kernel, rep=mod.rep),
        grid=(R // tm,),
        in_specs=[row, row, row, row, const(w_g), const(w_out), mod.spec(tm, 2)],
        out_specs=row,
        out_shape=jax.ShapeDtypeStruct((R, D), F32),
        compiler_params=_cparams(("parallel",)),
        name="merge_out",
    )(x, h, o_a, o_b, w_g, w_out, mod.arr)


def _mlp_kernel(x_ref, nw_ref, sh_ref, sc_ref, gt_ref, wu_ref, wd_ref, fw_ref, o_ref, *, tf, rep):
    x = x_ref[...]
    h = (_rms(x, nw_ref[...]) * (1.0 + _rep_rows(sc_ref[...], rep)) + _rep_rows(sh_ref[...], rep)).astype(BF16)
    acc = None
    for c in range(wu_ref.shape[1] // tf):
        u = jnp.dot(h, wu_ref[:, c * tf:(c + 1) * tf], preferred_element_type=F32)
        u = jnp.square(jnp.maximum(u, 0.0))
        part = _dot(u, wd_ref[c * tf:(c + 1) * tf, :])
        acc = part if acc is None else acc + part
    o_ref[...] = _rms(x + _rep_rows(gt_ref[...], rep) * acc, fw_ref[...])


def _mlp(x, nw, w_up, w_down, fw, mod, tm, tf=1024):
    R, D = x.shape
    const = lambda a: pl.BlockSpec(a.shape, lambda i: (0,) * a.ndim, pipeline_mode=pl.Buffered(1))
    row = pl.BlockSpec((tm, D), lambda i: (i, 0))
    nw, fw = nw.reshape(1, D), fw.reshape(1, D)
    return pl.pallas_call(
        functools.partial(_mlp_kernel, tf=tf, rep=mod.rep),
        grid=(R // tm,),
        in_specs=[row, const(nw), mod.spec(tm, 3), mod.spec(tm, 4), mod.spec(tm, 5),
                  const(w_up), const(w_down), const(fw)],
        out_specs=row,
        out_shape=jax.ShapeDtypeStruct((R, D), F32),
        compiler_params=_cparams(("parallel",)),
        name="mlp",
    )(x, nw, mod.arr, mod.arr, mod.arr, w_up, w_down, fw)


def _run(x3, mod_rows, s_hgrn, s_rwkv, s_shift, W, prompt):
    B, T, D = x3.shape
    R = B * T
    x = x3.reshape(R, D)
    if prompt:
        mod = _Mod(mod_rows.reshape(B, 1, -1), True, T, D)
        tm = min(1024, T)
    else:
        mod = _Mod(mod_rows, False, T, D)
        tm = min(512, R)

    rw = (W["mu"], W["w0"], W["a0"], W["k_k"], W["k_a"], W["wwa"], W["g2p"])
    nw1 = W["norm1_w"].reshape(1, D)
    if prompt:
        shift_in = jnp.zeros((B, 1, RC_PAD), F32)
        h, arrs, last = _in_rwkv(x, nw1, mod, W["w_r"], shift_in, rw, True, T)
        shift_out = last[:, 0, :RC_COLS]
        P_h = _matmul(h, W["w_in_t"], 4 * D, tm, 2048, "proj_h")
        o_a, hg, o_b, rs = _scan_prompt(P_h, W["lb_logits"], W["hgrn_norm_w"], arrs[:6], arrs[6],
                                        W["lnx_w"], W["lnx_b"], W["r_k"], B, T)
    else:
        shift_in = jnp.pad(s_shift, ((0, 0), (0, RC_PAD - RC_COLS)))
        h, arrs, last = _in_rwkv(x, nw1, mod, W["w_r"], shift_in, rw, False, T)
        shift_out = last[:, :RC_COLS]
        P_h = _matmul(h, W["w_in_t"], 4 * D, tm, 2048, "proj_h")
        o_a, hg = _hgrn_sample(P_h, W["lb_logits"], W["hgrn_norm_w"], s_hgrn, T)
        o_b, rs = _rwkv_sample(arrs[:6], arrs[6], W["lnx_w"], W["lnx_b"], W["r_k"], s_rwkv, T)

    x1 = _merge_out(x, h, o_a, o_b, W["w_g"], W["w_out"], mod, min(tm, 512))
    y = _mlp(x1, W["norm2_w"], W["w_up"], W["w_down"], W["final_norm_w"], mod, min(tm, 512))
    return y.reshape(B, T, D), hg, rs, shift_out


def kernel(x_prompt, x_sample, state_hgrn, state_rwkv, state_shift, c_prompt, c_sample, norm1_w, norm2_w, ada_w, ada_b, w_in, lb_logits, hgrn_norm_w, rwkv_mu, rwkv_w0, rwkv_w2, rwkv_a0, rwkv_a2, rwkv_g2, rwkv_k_k, rwkv_k_a, rwkv_r_k, rwkv_lnx_w, rwkv_lnx_b, w_out, w_up, w_down, final_norm_w):
    depth = norm1_w.shape[0]
    assert depth == 1, "single-layer step"
    D = x_prompt.shape[-1]
    HW = 4 * D
    w_in_t = w_in[0].T.astype(BF16)
    row = lambda a: a.reshape(1, -1)
    wwa = jnp.zeros((LANES, 2 * D), F32)
    wwa = wwa.at[:W_LORA, :D].set(rwkv_w2[0]).at[W_LORA:W_LORA + A_LORA, D:].set(rwkv_a2[0])
    W = {
        "norm1_w": norm1_w[0], "norm2_w": norm2_w[0], "final_norm_w": final_norm_w,
        "w_in_t": w_in_t,
        "w_r": jnp.pad(w_in[0][:, HW:HW + RC_COLS], ((0, 0), (0, RC_PAD - RC_COLS))).astype(BF16),
        "w_g": w_in_t[HW + RC_COLS:],
        "lb_logits": lb_logits, "hgrn_norm_w": hgrn_norm_w[0],
        "mu": jnp.pad(row(rwkv_mu[0]), ((0, 0), (0, RC_PAD - RC_COLS))),
        "w0": row(rwkv_w0[0]), "a0": row(rwkv_a0[0]), "k_k": row(rwkv_k_k[0]), "k_a": row(rwkv_k_a[0]),
        "wwa": wwa.astype(BF16),
        "g2p": jnp.pad(rwkv_g2[0], ((0, G_LORA_PAD - G_LORA), (0, 0))).astype(BF16),
        "lnx_w": rwkv_lnx_w[0], "lnx_b": rwkv_lnx_b[0], "r_k": rwkv_r_k[0].reshape(-1),
        "w_out": w_out[0].astype(BF16), "w_up": w_up[0].astype(BF16), "w_down": w_down[0].astype(BF16),
    }
    nb = x_prompt.shape[0]
    mod = _adaln(jnp.concatenate([c_prompt, c_sample], axis=0), ada_w[0], ada_b[0])
    y_p, hg_p, rw_p, sh_p = _run(x_prompt, mod[:nb], None, None, None, W, True)
    y_s, hg_s, rw_s, sh_s = _run(x_sample, mod[nb:], state_hgrn[0], state_rwkv[0], state_shift[0], W, False)
    ex = lambda a: a[None]
    return (y_p, y_s, ex(hg_p), ex(rw_p), ex(sh_p), ex(hg_s), ex(rw_s), ex(sh_s))
```

```python
import functools
import math

import jax
import jax.numpy as jnp
from jax import lax
from jax.experimental import pallas as pl
from jax.experimental.pallas import tpu as pltpu

F32 = jnp.float32
BF16 = jnp.bfloat16

LANES = 128
SUBLANES = 8
VMEM_LIMIT = 48 * 1024 * 1024
VMEM_LIMIT_IN = 56 * 1024 * 1024

HGRN_DK = 128
RWKV_N = 64
W_LORA = 64
A_LORA = 64
G_LORA = 160
G_LORA_PAD = 256
RC_MAIN = 3 * 1024
RC_COLS = RC_MAIN + W_LORA + A_LORA + G_LORA
RC_PAD = RC_MAIN + LANES + G_LORA_PAD
RMS_EPS = 1e-6
LNX_EPS = 64e-5
CHUNK = 64
SUB = 8
RWKV_SUB = 8


def _cparams(sem, vmem=VMEM_LIMIT):
    return pltpu.CompilerParams(dimension_semantics=sem, vmem_limit_bytes=vmem)


_NN = (((1,), (0,)), ((), ()))
_NT = (((1,), (1,)), ((), ()))
_TN = (((0,), (0,)), ((), ()))


def _dot(a, b, dims=_NN):
    return lax.dot_general(a.astype(BF16), b.astype(BF16), dims, preferred_element_type=F32)


def _split2(x):
    hi = x.astype(BF16)
    lo = (x - hi.astype(F32)).astype(BF16)
    return hi, lo


def _split3(x):
    hi = x.astype(BF16)
    r1 = x - hi.astype(F32)
    mid = r1.astype(BF16)
    lo = (r1 - mid.astype(F32)).astype(BF16)
    return hi, mid, lo


def _dot_exact_lhs(a, x, dims=_NN):
    a = a.astype(BF16)
    hi, mid, lo = _split3(x)
    d = lambda p: lax.dot_general(a, p, dims, preferred_element_type=F32)
    return d(hi) + d(mid) + d(lo)


def _dot_hp(a, b, dims=_NN):
    ah, al = _split2(a)
    bh, bl = _split2(b)
    d = lambda p, q: lax.dot_general(p, q, dims, preferred_element_type=F32)
    return d(ah, bh) + d(ah, bl) + d(al, bh)


def _sigmoid(x):
    return 1.0 / (1.0 + jnp.exp(-x))


def _iota2(shape, dim):
    return lax.broadcasted_iota(jnp.int32, shape, dim)


def _neumann_inv(L, eye, nsq):
    X = L
    P = eye - L
    for _ in range(nsq):
        X = _dot_hp(X, X)
        P = P + _dot_hp(P, X)
    return P


def _adaln_kernel(c_ref, w_ref, b_ref, o_ref):
    c = c_ref[...]
    s = c * _sigmoid(c)
    o_ref[...] = _dot(s, w_ref[...]) + b_ref[...]


def _adaln(c_all, ada_w, ada_b, tn=1536):
    R, D = c_all.shape
    N = ada_w.shape[1]
    return pl.pallas_call(
        _adaln_kernel,
        grid=(N // tn,),
        in_specs=[pl.BlockSpec((R, D), lambda j: (0, 0)),
                  pl.BlockSpec((D, tn), lambda j: (0, j)),
                  pl.BlockSpec((1, tn), lambda j: (0, j))],
        out_specs=pl.BlockSpec((R, tn), lambda j: (0, j)),
        out_shape=jax.ShapeDtypeStruct((R, N), F32),
        compiler_params=_cparams(("parallel",)),
        name="adaln",
    )(c_all, ada_w, ada_b.reshape(1, N))


class _Mod:
    def __init__(self, arr, per_seq, seq_rows, D):
        self.arr, self.per_seq, self.seq_rows, self.D = arr, per_seq, seq_rows, D
        self.rep = 1 if per_seq else seq_rows

    def spec(self, tm, comp):
        D = self.D
        if self.per_seq:
            tiles = self.seq_rows // tm
            return pl.BlockSpec((None, 1, D), lambda i, *_: (i // tiles, 0, comp))
        return pl.BlockSpec((tm // self.seq_rows, D), lambda i, *_: (i, comp))


def _rep_rows(v, rep):
    if rep == 1:
        return v
    n, w = v.shape
    return jnp.broadcast_to(v[:, None, :], (n, rep, w)).reshape(n * rep, w)


def _rms(x, w):
    return x * lax.rsqrt(jnp.mean(x * x, axis=-1, keepdims=True) + RMS_EPS) * w


def _mm_kernel(x_ref, wt_ref, o_ref):
    o_ref[...] = lax.dot_general(x_ref[...], wt_ref[...], _NT, preferred_element_type=F32)


def _matmul(x, wt, n_out, tm, tn, name):
    R, K = x.shape
    N = n_out
    return pl.pallas_call(
        _mm_kernel,
        grid=(R // tm, N // tn),
        in_specs=[pl.BlockSpec((tm, K), lambda i, j: (i, 0)),
                  pl.BlockSpec((tn, K), lambda i, j: (j, 0))],
        out_specs=pl.BlockSpec((tm, tn), lambda i, j: (i, j)),
        out_shape=jax.ShapeDtypeStruct((R, N), F32),
        compiler_params=_cparams(("parallel", "arbitrary")),
        name=name,
    )(x, wt)


def _hgrn_gates(fp, lb_logits, base2=False):
    m = jnp.max(lb_logits, axis=0, keepdims=True)
    e = jnp.exp(lb_logits - m)
    lb = e[0:1, :] / jnp.sum(e, axis=0, keepdims=True)
    f = lb + (1.0 - lb) * _sigmoid(fp)
    return (jnp.log2(f) if base2 else jnp.log(f)), 1.0 - f


def _hgrn_diag(q, kin, b, c):
    R = q.shape[0]
    nb = R // c
    q3 = q.reshape(nb, c, LANES)
    k3 = kin.reshape(nb, c, LANES)
    b3 = (b * math.log2(math.e)).reshape(nb, c, LANES)
    lane_mod = lax.broadcasted_iota(jnp.int32, (nb, c, R), 2) % c
    acc = jnp.zeros((nb, c, R), F32)
    for s in range(c):
        d = jnp.minimum(b3 - b3[:, s:s + 1, :], 0.0)
        col = jnp.sum(q3 * k3[:, s:s + 1, :] * jnp.exp2(d), axis=-1, keepdims=True)
        acc = jnp.where(lane_mod == s, col, acc)
    rr, cc = _iota2((R, R), 0), _iota2((R, R), 1)
    keep = ((rr // c) == (cc // c)) & (rr >= cc)
    return jnp.where(keep, acc.reshape(R, R), 0.0)


def _hgrn_cross(q, kin, b2, m):
    R = q.shape[0]
    h = m // 2
    nb = R // m
    ref = jnp.broadcast_to(b2.reshape(nb, m, LANES)[:, h - 1:h, :], (nb, m, LANES)).reshape(R, LANES)
    upper = (_iota2((R, 1), 0) % m) >= h
    e = jnp.exp2(-jnp.abs(b2 - ref))
    qs = jnp.where(upper, q * e, 0.0)
    ks = jnp.where(upper, 0.0, kin * e)
    sc = _dot(qs, ks, _NT)
    if nb == 1:
        return sc
    same_blk = (_iota2((R, R), 0) // m) == (_iota2((R, R), 1) // m)
    return jnp.where(same_blk, sc, 0.0)


def _hgrn_out(o, og, nw):
    o = o * lax.rsqrt(jnp.mean(o * o, axis=-1, keepdims=True) + RMS_EPS) * nw
    return o * (og * _sigmoid(og))


def _hgrn_chunks(p_ref, lb_ref, nw_ref, o_ref, st_ref, offs, hp):
    C, c = CHUNK, SUB
    nb = C // c
    W = hp * LANES
    tri = (_iota2((C, C), 0) >= _iota2((C, C), 1)).astype(F32)
    nw = nw_ref[...]
    H = range(hp)
    cols = [slice(j * LANES, (j + 1) * LANES) for j in H]
    pcol = lambda g, j: slice(g * W + j * LANES, g * W + (j + 1) * LANES)
    lane_mod = lax.broadcasted_iota(jnp.int32, (nb, c, C), 2) % c
    rr, cc = _iota2((C, C), 0), _iota2((C, C), 1)
    keep = ((rr // c) == (cc // c)) & (rr >= cc)
    for r0 in offs:
        q = [p_ref[pl.ds(r0, C), pcol(0, j)] for j in H]
        v = [p_ref[pl.ds(r0, C), pcol(2, j)] for j in H]
        gates = [_hgrn_gates(p_ref[pl.ds(r0, C), pcol(1, j)], lb_ref[:, cols[j]], base2=True) for j in H]
        lf = [gt[0] for gt in gates]
        kin = [gt[1] for gt in gates]
        yield
        b = [_dot_exact_lhs(tri, lf[j]) for j in H]
        b_last = [b[j][C - 1:C, :] for j in H]
        yield
        ST = [st_ref[j] for j in H]
        o = [_dot(q[j] * jnp.exp2(b[j]), ST[j], _NT) for j in H]
        yield
        for j in H:
            st_ref[j] = ST[j] * jnp.exp2(b_last[j]) + _dot(v[j], kin[j] * jnp.exp2(b_last[j] - b[j]), _TN)
        yield
        q3 = [q[j].reshape(nb, c, LANES) for j in H]
        k3 = [kin[j].reshape(nb, c, LANES) for j in H]
        b3 = [b[j].reshape(nb, c, LANES) for j in H]
        acc = [jnp.zeros((nb, c, C), F32) for _ in H]
        for s in range(c):
            for j in H:
                d = jnp.minimum(b3[j] - b3[j][:, s:s + 1, :], 0.0)
                col = jnp.sum(q3[j] * k3[j][:, s:s + 1, :] * jnp.exp2(d), axis=-1, keepdims=True)
                acc[j] = jnp.where(lane_mod == s, col, acc[j])
            yield
        A = [jnp.where(keep, acc[j].reshape(C, C), 0.0) for j in H]
        m = 2 * c
        while m <= C:
            A = [A[j] + _hgrn_cross(q[j], kin[j], b[j], m) for j in H]
            m *= 2
            yield
        o = [o[j] + _dot(A[j], v[j]) for j in H]
        yield
        for j in H:
            o_ref[pl.ds(r0, C), cols[j]] = _hgrn_out(o[j], p_ref[pl.ds(r0, C), pcol(3, j)], nw)
        yield


def _hgrn_sample_kernel(q_ref, f_ref, i_ref, g_ref, lb_ref, nw_ref, s0_ref, o_ref, s_ref, *, nseq, T, hp):
    R = nseq * T
    rr, cc = _iota2((R, R), 0), _iota2((R, R), 1)
    same = (rr // T) == (cc // T)
    seg_incl = (same & (rr >= cc)).astype(F32)
    seg_all = same.astype(F32)
    nw = nw_ref[...]
    H = range(hp)
    cols = [slice(j * LANES, (j + 1) * LANES) for j in H]
    q = [q_ref[:, cols[j]] for j in H]
    v = [i_ref[:, cols[j]] for j in H]
    gates = [_hgrn_gates(f_ref[:, cols[j]], lb_ref[:, cols[j]]) for j in H]
    lf = [gt[0] for gt in gates]
    kin = [gt[1] for gt in gates]
    b = [_dot_exact_lhs(seg_incl, lf[j]) for j in H]
    b_tot = [_dot_exact_lhs(seg_all, lf[j]) for j in H]
    A = [_hgrn_diag(q[j], kin[j], b[j], T) for j in H]
    o = [_dot(A[j], v[j]) for j in H]
    qg = [q[j] * jnp.exp(b[j]) for j in H]
    kh = [kin[j] * jnp.exp(b_tot[j] - b[j]) for j in H]
    ecol = [jnp.transpose(jnp.exp(b_tot[j]).reshape(nseq, T, LANES)[:, 0, :]) for j in H]
    inter = [[] for _ in H]
    for n in range(nseq):
        sl = slice(n * T, (n + 1) * T)
        for j in H:
            S = s0_ref[n, j]
            inter[j].append(_dot(qg[j][sl], S))
            s_ref[n, j] = S * ecol[j][:, n:n + 1] + _dot(kh[j][sl], v[j][sl], _TN)
    for j in H:
        oj = o[j] + jnp.concatenate(inter[j], axis=0)
        o_ref[:, cols[j]] = _hgrn_out(oj, g_ref[:, cols[j]], nw)


def _hgrn_sample(P_h, lb_logits, nw, s0, T, nseq=16, hp=8):
    R, W = P_h.shape
    H = W // (4 * HGRN_DK)
    HG = H // hp
    wd = hp * HGRN_DK
    B = R // T
    rows = nseq * T
    col = lambda off: pl.BlockSpec((rows, wd), lambda h, g: (g, off * HG + h))
    sspec = pl.BlockSpec((nseq, hp, HGRN_DK, HGRN_DK), lambda h, g: (g, h, 0, 0))
    o, s = pl.pallas_call(
        functools.partial(_hgrn_sample_kernel, nseq=nseq, T=T, hp=hp),
        grid=(HG, B // nseq),
        in_specs=[col(0), col(1), col(2), col(3),
                  pl.BlockSpec((lb_logits.shape[0], wd), lambda h, g: (0, h)),
                  pl.BlockSpec((1, HGRN_DK), lambda h, g: (0, 0)),
                  sspec],
        out_specs=[pl.BlockSpec((rows, wd), lambda h, g: (g, h)), sspec],
        out_shape=[jax.ShapeDtypeStruct((R, H * HGRN_DK), F32),
                   jax.ShapeDtypeStruct(s0.shape, F32)],
        compiler_params=_cparams(("parallel", "parallel")),
        name="hgrn_sample",
    )(P_h, P_h, P_h, P_h, lb_logits, nw.reshape(1, HGRN_DK), s0)
    return o, s


def _rwkv_prep_body(rc, prev, mu_ref, w0_ref, a0_ref, kk_ref, ka_ref, wwa_ref, g2_ref, outs, rows):
    r_o, lw_o, k_o, v_o, kap_o, bet_o, g_o = outs
    D = RC_MAIN // 3
    xs = rc + mu_ref[...] * (prev - rc)
    r = xs[:, 0:D]
    kb = xs[:, D:2 * D]
    v = xs[:, 2 * D:3 * D]
    xwa = xs[:, RC_MAIN:RC_MAIN + LANES]
    xg = xs[:, RC_MAIN + LANES:]
    lane = _iota2((1, LANES), 1)
    wa = _dot(jnp.where(lane < W_LORA, jnp.tanh(xwa), xwa), wwa_ref[...])
    w = w0_ref[...] + wa[:, :D]
    a = _sigmoid(a0_ref[...] + wa[:, D:])
    lw = -(math.exp(-0.5) * math.log2(math.e)) * _sigmoid(w)
    g = _dot(_sigmoid(xg), g2_ref[...])
    kk = kb * kk_ref[...]
    mA, mB = _pair_masks()
    k_a = ka_ref[...]
    for p in range(D // LANES):
        sl = slice(p * LANES, (p + 1) * LANES)
        kkp = kk[:, sl]
        kap = kkp * lax.rsqrt(jnp.maximum(_head_sum(kkp * kkp, mA, mB), 1e-24))
        ap = a[:, sl]
        kap_o[rows, sl] = kap
        bet_o[rows, sl] = ap * kap
        k_o[rows, sl] = kb[:, sl] * (1.0 + (ap - 1.0) * k_a[:, sl])
    r_o[rows, :] = r
    lw_o[rows, :] = lw
    v_o[rows, :] = v
    g_o[rows, :] = g


def _in_rwkv_kernel(x_ref, nw_ref, sh_ref, sc_ref, wr_ref, shift_ref, mu_ref, w0_ref, a0_ref, kk_ref,
                    ka_ref, wwa_ref, g2_ref, h_ref, r_o, lw_o, k_o, v_o, kap_o, bet_o, g_o, last_o,
                    carry_ref, rc_ref, *, seq_rows, sub):
    i = pl.program_id(0)
    tm = x_ref.shape[0]
    per_seq = seq_rows >= tm

    @pl.when(i == 0)
    def _():
        carry_ref[...] = jnp.zeros_like(carry_ref)

    rep = 1 if per_seq else seq_rows
    h = (_rms(x_ref[...], nw_ref[...]) * (1.0 + _rep_rows(sc_ref[...], rep))
         + _rep_rows(sh_ref[...], rep)).astype(BF16)
    h_ref[...] = h
    outs = (r_o, lw_o, k_o, v_o, kap_o, bet_o, g_o)
    rid = _iota2((sub, 1), 0)
    rc_ref[...] = jnp.dot(h, wr_ref[...], preferred_element_type=F32)
    prev_last = None
    for s in range(tm // sub):
        rows = slice(s * sub, (s + 1) * sub)
        rc = rc_ref[rows, :]
        rolled = pltpu.roll(rc, 1, axis=0)
        if per_seq:
            if s == 0:
                start = (i % (seq_rows // tm)) == 0
                first = jnp.where(start, shift_ref[...], carry_ref[SUBLANES - 1:SUBLANES, :])
            else:
                first = prev_last
            prev = jnp.where(rid == 0, first, rolled)
        else:
            seqs = slice(s * sub // seq_rows, (s + 1) * sub // seq_rows)
            prev = jnp.where(rid % seq_rows == 0, _rep_rows(shift_ref[seqs, :], seq_rows), rolled)
            last_o[seqs, :] = rc.reshape(sub // seq_rows, seq_rows, RC_PAD)[:, seq_rows - 1, :]
        prev_last = rc[sub - 1:sub, :]
        if s == tm // sub - 1 and per_seq:
            carry_ref[...] = rc[sub - SUBLANES:, :]
            last_o[...] = prev_last
        _rwkv_prep_body(rc, prev, mu_ref, w0_ref, a0_ref, kk_ref, ka_ref, wwa_ref, g2_ref, outs, rows)


def _in_rwkv(x, nw, mod, w_r, shift, params, per_seq, seq_rows, tm=512, sub=128):
    R, D = x.shape
    tm = min(tm, seq_rows) if per_seq else min(tm, R)
    sub = min(sub, tm)
    mu, w0, a0, k_k, k_a, wwa, g2p = params
    const = lambda a: pl.BlockSpec(a.shape, lambda i: (0,) * a.ndim, pipeline_mode=pl.Buffered(1))
    wspecs = [const(a) for a in (mu, w0, a0, k_k, k_a, wwa, g2p)]
    row = lambda w: pl.BlockSpec((tm, w), lambda i: (i, 0))
    if per_seq:
        tiles = seq_rows // tm
        shift_spec = pl.BlockSpec((None, 1, RC_PAD), lambda i: (i // tiles, 0, 0))
        last_spec = pl.BlockSpec((None, 1, RC_PAD), lambda i: (i // tiles, 0, 0))
        last_shape = jax.ShapeDtypeStruct((R // seq_rows, 1, RC_PAD), F32)
    else:
        shift_spec = pl.BlockSpec((tm // seq_rows, RC_PAD), lambda i: (i, 0))
        last_spec = pl.BlockSpec((tm // seq_rows, RC_PAD), lambda i: (i, 0))
        last_shape = jax.ShapeDtypeStruct((R // seq_rows, RC_PAD), F32)
    outs = pl.pallas_call(
        functools.partial(_in_rwkv_kernel, seq_rows=seq_rows, sub=sub),
        grid=(R // tm,),
        in_specs=[row(D), const(nw), mod.spec(tm, 0), mod.spec(tm, 1), const(w_r), shift_spec] + wspecs,
        out_specs=[row(D)] + [row(D)] * 7 + [last_spec],
        out_shape=[jax.ShapeDtypeStruct((R, D), BF16)] + [jax.ShapeDtypeStruct((R, D), F32)] * 7
                  + [last_shape],
        scratch_shapes=[pltpu.VMEM((SUBLANES, RC_PAD), F32), pltpu.VMEM((tm, RC_PAD), F32)],
        compiler_params=_cparams(("arbitrary",), VMEM_LIMIT_IN),
        name="in_rwkv",
    )(x, nw, mod.arr, mod.arr, w_r, shift, mu, w0, a0, k_k, k_a, wwa, g2p)
    return outs[0], outs[1:8], outs[8]


def _pair_masks():
    lane = _iota2((1, LANES), 1)
    return (lane < RWKV_N).astype(F32), (lane >= RWKV_N).astype(F32)


def _head_sum(x, mA, mB):
    sa = jnp.sum(x * mA, axis=-1, keepdims=True)
    sb = jnp.sum(x * mB, axis=-1, keepdims=True)
    return sa * mA + sb * mB


def _stack(x, mA, mB, dtype=F32):
    x = x.astype(dtype)
    return jnp.concatenate([x * mA.astype(dtype), x * mB.astype(dtype)], axis=0)


def _rwkv_scores(cs, lw, cs_end, r, k, kap, bet, mA, mB, dtype=F32):
    e_neg = jnp.exp2(-cs)
    e_rem = jnp.exp2(cs_end - cs)
    rg = _stack(r * jnp.exp2(cs), mA, mB, dtype)
    kg = _stack(kap * jnp.exp2(cs - lw), mA, mB, dtype)
    kd = _stack(k * e_neg, mA, mB, dtype)
    bd = _stack(bet * e_neg, mA, mB, dtype)
    kh = _stack(k * e_rem, mA, mB, dtype)
    bh = _stack(bet * e_rem, mA, mB, dtype)
    return rg, kg, kd, bd, kh, bh


def _rwkv_epilogue(o, r, k, v, g, lnw, lnb, rk, mA, mB):
    mean = _head_sum(o, mA, mB) * (1.0 / RWKV_N)
    d = o - mean
    var = _head_sum(d * d, mA, mB) * (1.0 / RWKV_N)
    on = d * lax.rsqrt(var + LNX_EPS) * lnw + lnb
    bonus = _head_sum(r * k * rk, mA, mB) * v
    return (on + bonus) * g


def _rwkv_chunks(r_ref, lw_ref, k_ref, v_ref, kap_ref, bet_ref, g_ref, lnw_ref, lnb_ref, rk_ref,
                 o_ref, st_ref, offs, hp):
    C, c = CHUNK, RWKV_SUB
    C2 = 2 * C
    mA, mB = _pair_masks()
    rr, cc = _iota2((C2, C2), 0), _iota2((C2, C2), 1)
    eye = (rr == cc).astype(F32)
    strict = rr > cc
    incl = rr >= cc
    blk = (rr // c) == (cc // c)
    tri = (_iota2((C, C), 0) >= _iota2((C, C), 1)).astype(F32)
    chains = [(r0, p) for r0 in offs for p in range(hp)]
    J = range(len(chains))
    pcols = [slice(p * LANES, (p + 1) * LANES) for _, p in chains]

    ld = lambda ref: [ref[pl.ds(r0, C), pcols[j]] for j, (r0, _) in zip(J, chains)]
    r, lw, k, v, kap, bet = ld(r_ref), ld(lw_ref), ld(k_ref), ld(v_ref), ld(kap_ref), ld(bet_ref)
    yield
    cs = [_dot_exact_lhs(tri, lw[j]) for j in J]
    cs_end = [cs[j][C - 1:C, :] for j in J]
    yield
    sc = [_rwkv_scores(cs[j], lw[j], cs_end[j], r[j], k[j], kap[j], bet[j], mA, mB, BF16) for j in J]
    rg, kg, kd, bd, kh, bh = ([s[i] for s in sc] for i in range(6))
    vs = [_stack(v[j], mA, mB, BF16) for j in J]
    kbd = [jnp.concatenate([kd[j], bd[j]], axis=0) for j in J]
    yield
    sck = [_dot(kg[j], kbd[j], _NT) for j in J]
    yield
    scr = [_dot(rg[j], kbd[j], _NT) for j in J]
    Ak = [jnp.where(strict, sck[j][:, :C2], 0.0) for j in J]
    Ab = [jnp.where(strict, sck[j][:, C2:], 0.0) for j in J]
    AkV = [_dot(Ak[j], vs[j]) for j in J]
    yield
    Bkb = [jnp.concatenate([jnp.where(incl, scr[j][:, :C2], 0.0),
                            jnp.where(incl, -scr[j][:, C2:], 0.0)], axis=1).astype(BF16) for j in J]
    LD = [jnp.where(blk, Ab[j], 0.0) for j in J]
    X = LD
    Dinv = [eye - LD[j] for j in J]
    for _ in range(int(math.log2(c)) - 1):
        X = [_dot(X[j], X[j]) for j in J]
        yield
        Dinv = [Dinv[j] + _dot(Dinv[j], X[j]) for j in J]
        yield
    Q = [_dot(Dinv[j], jnp.where(blk, 0.0, Ab[j])) for j in J]
    yield
    T0 = [Dinv[j] - _dot(Q[j], Dinv[j]) for j in J]
    yield
    for _ in range(int(math.log2(C // c)) - 1):
        Q = [_dot(Q[j], Q[j]) for j in J]
        yield
        T0 = [T0[j] + _dot(Q[j], T0[j]) for j in J]
        yield
    E = [(eye - T0[j]) - _dot_hp(Ab[j], T0[j]) for j in J]
    yield
    Tm = [T0[j] + _dot(T0[j], E[j]) for j in J]
    yield
    khb = [jnp.concatenate([kh[j], -bh[j]], axis=0) for j in J]
    dec = [jnp.exp2(cs_end[j]) for j in J]
    yield

    for u in range(len(offs)):
        K = range(u * hp, (u + 1) * hp)
        S = {j: st_ref[chains[j][1]] for j in K}
        rhs = {j: _dot(kg[j], S[j], _NT) + AkV[j] for j in K}
        yield
        U = {j: _dot_hp(Tm[j], rhs[j]) for j in K}
        yield
        vu = {j: jnp.concatenate([vs[j], U[j].astype(BF16)], axis=0) for j in K}
        for j in K:
            st_ref[chains[j][1]] = S[j] * dec[j] + _dot(vu[j], khb[j], _TN)
        yield
        O = {j: _dot(rg[j], S[j], _NT) + _dot(Bkb[j], vu[j]) for j in K}
        o = {j: O[j][:C, :] + O[j][C:, :] for j in K}
        yield
        mean = {j: _head_sum(o[j], mA, mB) * (1.0 / RWKV_N) for j in K}
        d = {j: o[j] - mean[j] for j in K}
        yield
        var = {j: _head_sum(d[j] * d[j], mA, mB) * (1.0 / RWKV_N) for j in K}
        rkk = {j: r_ref[pl.ds(chains[j][0], C), pcols[j]] * k_ref[pl.ds(chains[j][0], C), pcols[j]]
               * rk_ref[:, pcols[j]] for j in K}
        yield
        bsum = {j: _head_sum(rkk[j], mA, mB) for j in K}
        yield
        for j in K:
            rows = pl.ds(chains[j][0], C)
            on = d[j] * lax.rsqrt(var[j] + LNX_EPS) * lnw_ref[:, pcols[j]] + lnb_ref[:, pcols[j]]
            bonus = bsum[j] * v_ref[rows, pcols[j]]
            o_ref[rows, pcols[j]] = (on + bonus) * g_ref[rows, pcols[j]]
        yield


def _interleave(gens):
    live = list(gens)
    while live:
        for gen in list(live):
            try:
                next(gen)
            except StopIteration:
                live.remove(gen)


def _scan_prompt_kernel(p_ref, lb_ref, nw_ref,
                        r_ref, lw_ref, k_ref, v_ref, kap_ref, bet_ref, g_ref, lnw_ref, lnb_ref, rk_ref,
                        oa_ref, hs_ref, ob_ref, rs_ref, hst_ref, rst_ref, *, nchunks, hh, hp):
    tt = pl.program_id(1)

    @pl.when(tt == 0)
    def _():
        hst_ref[...] = jnp.zeros_like(hst_ref)
        rst_ref[...] = jnp.zeros_like(rst_ref)

    C = CHUNK
    unroll = 2 if nchunks % 2 == 0 else 1

    def body(ci, carry):
        base = ci * (unroll * C)
        offs = [pl.multiple_of(base + u * C, C) for u in range(unroll)]
        _interleave([
            _rwkv_chunks(r_ref, lw_ref, k_ref, v_ref, kap_ref, bet_ref, g_ref, lnw_ref, lnb_ref, rk_ref,
                         ob_ref, rst_ref, offs, hp),
            _hgrn_chunks(p_ref, lb_ref, nw_ref, oa_ref, hst_ref, offs, hh),
        ])
        return carry

    lax.fori_loop(0, nchunks // unroll, body, 0)

    @pl.when(tt == pl.num_programs(1) - 1)
    def _():
        for j in range(hh):
            hs_ref[j] = hst_ref[j].T
        for p in range(hp):
            S = rst_ref[p]
            rs_ref[2 * p] = S[:RWKV_N, :RWKV_N]
            rs_ref[2 * p + 1] = S[RWKV_N:, RWKV_N:]


def _scan_prompt(P_h, lb_logits, nw, arrs, g, lnw, lnb, rk, B, T, tt_rows=256):
    R, D = arrs[0].shape
    hh = P_h.shape[1] // (4 * HGRN_DK)
    hp = D // LANES
    wh = hh * HGRN_DK
    tt_rows = min(tt_rows, T)
    nt = T // tt_rows
    col = pl.BlockSpec((tt_rows, D), lambda b, t: (b * nt + t, 0))
    const = lambda a: pl.BlockSpec(a.shape, lambda b, t: (0,) * a.ndim)
    nw = nw.reshape(1, HGRN_DK)
    lnw, lnb, rk = lnw.reshape(1, D), lnb.reshape(1, D), rk.reshape(1, D)
    oa, hs, ob, rs = pl.pallas_call(
        functools.partial(_scan_prompt_kernel, nchunks=tt_rows // CHUNK, hh=hh, hp=hp),
        grid=(B, nt),
        in_specs=[pl.BlockSpec((tt_rows, 4 * wh), lambda b, t: (b * nt + t, 0)),
                  const(lb_logits), const(nw)]
                 + [col] * 7 + [const(lnw), const(lnb), const(rk)],
        out_specs=[pl.BlockSpec((tt_rows, wh), lambda b, t: (b * nt + t, 0)),
                   pl.BlockSpec((None, hh, HGRN_DK, HGRN_DK), lambda b, t: (b, 0, 0, 0)),
                   col,
                   pl.BlockSpec((None, 2 * hp, RWKV_N, RWKV_N), lambda b, t: (b, 0, 0, 0))],
        out_shape=[jax.ShapeDtypeStruct((R, wh), F32),
                   jax.ShapeDtypeStruct((B, hh, HGRN_DK, HGRN_DK), F32),
                   jax.ShapeDtypeStruct((R, D), F32),
                   jax.ShapeDtypeStruct((B, 2 * hp, RWKV_N, RWKV_N), F32)],
        scratch_shapes=[pltpu.VMEM((hh, HGRN_DK, HGRN_DK), F32), pltpu.VMEM((hp, LANES, LANES), F32)],
        compiler_params=_cparams(("parallel", "arbitrary")),
        name="scan_prompt",
    )(P_h, lb_logits, nw, *arrs, g, lnw, lnb, rk)
    return oa, hs, ob, rs


def _rwkv_sample_kernel(r_ref, lw_ref, k_ref, v_ref, kap_ref, bet_ref, g_ref, lnw_ref, lnb_ref, rk_ref,
                        s0_ref, o_ref, s_ref, sbd_ref, *, nseq, T, hp):
    C = nseq * T
    C2 = 2 * C
    N = RWKV_N
    mA, mB = _pair_masks()
    bdm = (_iota2((LANES, LANES), 0) // N == _iota2((LANES, LANES), 1) // N).astype(F32)
    rr, cc = _iota2((C2, C2), 0), _iota2((C2, C2), 1)
    eye = (rr == cc).astype(F32)
    same = (rr // T) == (cc // T)
    strict = same & (rr > cc)
    incl = same & (rr >= cc)
    r1, c1 = _iota2((C, C), 0), _iota2((C, C), 1)
    seg_incl = ((r1 // T == c1 // T) & (r1 >= c1)).astype(F32)
    seg_all = (r1 // T == c1 // T).astype(F32)
    zero = jnp.zeros((N, N), F32)

    J = range(hp)
    cols = [slice(j * LANES, (j + 1) * LANES) for j in J]
    ld = lambda ref: [ref[:, cols[j]] for j in J]
    r, lw, k, v, kap, bet = ld(r_ref), ld(lw_ref), ld(k_ref), ld(v_ref), ld(kap_ref), ld(bet_ref)
    cs = [_dot_exact_lhs(seg_incl, lw[j]) for j in J]
    cs_end = [_dot_exact_lhs(seg_all, lw[j]) for j in J]
    sc = [_rwkv_scores(cs[j], lw[j], cs_end[j], r[j], k[j], kap[j], bet[j], mA, mB) for j in J]
    rg, kg, kd, bd, kh, bh = ([s[i] for s in sc] for i in range(6))
    vs = [_stack(v[j], mA, mB) for j in J]
    kbd = [jnp.concatenate([kd[j], bd[j]], axis=0) for j in J]
    sck = [_dot(kg[j], kbd[j], _NT) for j in J]
    scr = [_dot(rg[j], kbd[j], _NT) for j in J]
    Ak = [jnp.where(strict, sck[j][:, :C2], 0.0) for j in J]
    Ab = [jnp.where(strict, sck[j][:, C2:], 0.0) for j in J]
    Bkb = [jnp.concatenate([jnp.where(incl, scr[j][:, :C2], 0.0),
                            jnp.where(incl, -scr[j][:, C2:], 0.0)], axis=1) for j in J]
    Tm = [_neumann_inv(Ab[j], eye, int(math.log2(T)) - 1) for j in J]
    kgp = [kg[j][:C, :] + kg[j][C:, :] for j in J]
    rgp = [rg[j][:C, :] + rg[j][C:, :] for j in J]
    ks = [[] for _ in J]
    rs = [[] for _ in J]
    for n in range(nseq):
        sl = slice(n * T, (n + 1) * T)
        for j in J:
            Sbd = jnp.concatenate(
                [jnp.concatenate([s0_ref[n, 2 * j], zero], axis=1),
                 jnp.concatenate([zero, s0_ref[n, 2 * j + 1]], axis=1)], axis=0)
            sbd_ref[n, j] = Sbd
            res = _dot(jnp.concatenate([kgp[j][sl], rgp[j][sl]], axis=0), Sbd, _NT)
            ks[j].append(res[:T])
            rs[j].append(res[T:])
    KS = [_stack(jnp.concatenate(ks[j], axis=0), mA, mB) for j in J]
    RS = [_stack(jnp.concatenate(rs[j], axis=0), mA, mB) for j in J]
    rhs = [KS[j] + _dot(Ak[j], vs[j]) for j in J]
    U = [_dot_hp(Tm[j], rhs[j]) for j in J]
    vu = [jnp.concatenate([vs[j], U[j]], axis=0) for j in J]
    O = [RS[j] + _dot(Bkb[j], vu[j]) for j in J]
    Up = [U[j][:C, :] + U[j][C:, :] for j in J]
    khp = [kh[j][:C, :] + kh[j][C:, :] for j in J]
    nbhp = [-(bh[j][:C, :] + bh[j][C:, :]) for j in J]
    dec = [jnp.exp2(cs_end[j]) for j in J]
    for n in range(nseq):
        sl = slice(n * T, (n + 1) * T)
        for j in J:
            upd = _dot(jnp.concatenate([v[j][sl], Up[j][sl]], axis=0),
                       jnp.concatenate([khp[j][sl], nbhp[j][sl]], axis=0), _TN)
            Sn = sbd_ref[n, j] * dec[j][n * T:n * T + 1, :] + upd * bdm
            s_ref[n, 2 * j] = Sn[:N, :N]
            s_ref[n, 2 * j + 1] = Sn[N:, N:]
    for j in J:
        o = O[j][:C, :] + O[j][C:, :]
        o_ref[:, cols[j]] = _rwkv_epilogue(o, r[j], k[j], v[j], g_ref[:, cols[j]],
                                           lnw_ref[:, cols[j]], lnb_ref[:, cols[j]], rk_ref[:, cols[j]],
                                           mA, mB)


def _rwkv_sample(arrs, g, lnw, lnb, rk, s0, T, nseq=16, hp=8):
    R, D = arrs[0].shape
    NP = D // LANES
    B = R // T
    rows = nseq * T
    wd = hp * LANES
    col = pl.BlockSpec((rows, wd), lambda p, gi: (gi, p))
    par = pl.BlockSpec((1, wd), lambda p, gi: (0, p))
    sspec = pl.BlockSpec((nseq, 2 * hp, RWKV_N, RWKV_N), lambda p, gi: (gi, p, 0, 0))
    o, s = pl.pallas_call(
        functools.partial(_rwkv_sample_kernel, nseq=nseq, T=T, hp=hp),
        grid=(NP // hp, B // nseq),
        in_specs=[col] * 7 + [par] * 3 + [sspec],
        out_specs=[col, sspec],
        out_shape=[jax.ShapeDtypeStruct((R, D), F32), jax.ShapeDtypeStruct(s0.shape, F32)],
        scratch_shapes=[pltpu.VMEM((nseq, hp, LANES, LANES), F32)],
        compiler_params=_cparams(("parallel", "parallel")),
        name="rwkv_sample",
    )(*arrs, g, lnw.reshape(1, D), lnb.reshape(1, D), rk.reshape(1, D), s0)
    return o, s


def _merge_kernel(x_ref, h_ref, oa_ref, ob_ref, wg_ref, w_ref, gt_ref, o_ref, *, rep):
    D = x_ref.shape[1]
    pg = lax.dot_general(h_ref[...], wg_ref[...], _NT, preferred_element_type=F32)
    m = _sigmoid(pg[:, :D]) * oa_ref[...] + _sigmoid(pg[:, D:]) * ob_ref[...]
    o_ref[...] = x_ref[...] + _rep_rows(gt_ref[...], rep) * _dot(m, w_ref[...])


def _merge_out(x, h, o_a, o_b, w_g, w_out, mod, tm):
    R, D = x.shape
    row = pl.BlockSpec((tm, D), lambda i: (i, 0))
    const = lambda a: pl.BlockSpec(a.shape, lambda i: (0,) * a.ndim, pipeline_mode=pl.Buffered(1))
    return pl.pallas_call(
        functools.partial(_merge_kernel, rep=mod.rep),
        grid=(R // tm,),
        in_specs=[row, row, row, row, const(w_g), const(w_out), mod.spec(tm, 2)],
        out_specs=row,
        out_shape=jax.ShapeDtypeStruct((R, D), F32),
        compiler_params=_cparams(("parallel",)),
        name="merge_out",
    )(x, h, o_a, o_b, w_g, w_out, mod.arr)


def _mlp_kernel(x_ref, nw_ref, sh_ref, sc_ref, gt_ref, wu_ref, wd_ref, fw_ref, o_ref, *, tf, rep):
    x = x_ref[...]
    h = (_rms(x, nw_ref[...]) * (1.0 + _rep_rows(sc_ref[...], rep)) + _rep_rows(sh_ref[...], rep)).astype(BF16)
    acc = None
    for c in range(wu_ref.shape[1] // tf):
        u = jnp.dot(h, wu_ref[:, c * tf:(c + 1) * tf], preferred_element_type=F32)
        u = jnp.square(jnp.maximum(u, 0.0))
        part = _dot(u, wd_ref[c * tf:(c + 1) * tf, :])
        acc = part if acc is None else acc + part
    o_ref[...] = _rms(x + _rep_rows(gt_ref[...], rep) * acc, fw_ref[...])


def _mlp(x, nw, w_up, w_down, fw, mod, tm, tf=1024):
    R, D = x.shape
    const = lambda a: pl.BlockSpec(a.shape, lambda i: (0,) * a.ndim, pipeline_mode=pl.Buffered(1))
    row = pl.BlockSpec((tm, D), lambda i: (i, 0))
    nw, fw = nw.reshape(1, D), fw.reshape(1, D)
    return pl.pallas_call(
        functools.partial(_mlp_kernel, tf=tf, rep=mod.rep),
        grid=(R // tm,),
        in_specs=[row, const(nw), mod.spec(tm, 3), mod.spec(tm, 4), mod.spec(tm, 5),
                  const(w_up), const(w_down), const(fw)],
        out_specs=row,
        out_shape=jax.ShapeDtypeStruct((R, D), F32),
        compiler_params=_cparams(("parallel",)),
        name="mlp",
    )(x, nw, mod.arr, mod.arr, mod.arr, w_up, w_down, fw)


def _run(x3, mod_rows, s_hgrn, s_rwkv, s_shift, W, prompt):
    B, T, D = x3.shape
    R = B * T
    x = x3.reshape(R, D)
    if prompt:
        mod = _Mod(mod_rows.reshape(B, 1, -1), True, T, D)
        tm = min(1024, T)
    else:
        mod = _Mod(mod_rows, False, T, D)
        tm = min(512, R)

    rw = (W["mu"], W["w0"], W["a0"], W["k_k"], W["k_a"], W["wwa"], W["g2p"])
    nw1 = W["norm1_w"].reshape(1, D)
    if prompt:
        shift_in = jnp.zeros((B, 1, RC_PAD), F32)
        h, arrs, last = _in_rwkv(x, nw1, mod, W["w_r"], shift_in, rw, True, T)
        shift_out = last[:, 0, :RC_COLS]
        P_h = _matmul(h, W["w_in_t"], 4 * D, tm, 2048, "proj_h")
        o_a, hg, o_b, rs = _scan_prompt(P_h, W["lb_logits"], W["hgrn_norm_w"], arrs[:6], arrs[6],
                                        W["lnx_w"], W["lnx_b"], W["r_k"], B, T)
    else:
        shift_in = jnp.pad(s_shift, ((0, 0), (0, RC_PAD - RC_COLS)))
        h, arrs, last = _in_rwkv(x, nw1, mod, W["w_r"], shift_in, rw, False, T)
        shift_out = last[:, :RC_COLS]
        P_h = _matmul(h, W["w_in_t"], 4 * D, tm, 2048, "proj_h")
        o_a, hg = _hgrn_sample(P_h, W["lb_logits"], W["hgrn_norm_w"], s_hgrn, T)
        o_b, rs = _rwkv_sample(arrs[:6], arrs[6], W["lnx_w"], W["lnx_b"], W["r_k"], s_rwkv, T)

    x1 = _merge_out(x, h, o_a, o_b, W["w_g"], W["w_out"], mod, min(tm, 512))
    y = _mlp(x1, W["norm2_w"], W["w_up"], W["w_down"], W["final_norm_w"], mod, min(tm, 512))
    return y.reshape(B, T, D), hg, rs, shift_out


def kernel(x_prompt, x_sample, state_hgrn, state_rwkv, state_shift, c_prompt, c_sample, norm1_w, norm2_w, ada_w, ada_b, w_in, lb_logits, hgrn_norm_w, rwkv_mu, rwkv_w0, rwkv_w2, rwkv_a0, rwkv_a2, rwkv_g2, rwkv_k_k, rwkv_k_a, rwkv_r_k, rwkv_lnx_w, rwkv_lnx_b, w_out, w_up, w_down, final_norm_w):
    depth = norm1_w.shape[0]
    assert depth == 1, "single-layer step"
    D = x_prompt.shape[-1]
    HW = 4 * D
    w_in_t = w_in[0].T.astype(BF16)
    row = lambda a: a.reshape(1, -1)
    wwa = jnp.zeros((LANES, 2 * D), F32)
    wwa = wwa.at[:W_LORA, :D].set(rwkv_w2[0]).at[W_LORA:W_LORA + A_LORA, D:].set(rwkv_a2[0])
    W = {
        "norm1_w": norm1_w[0], "norm2_w": norm2_w[0], "final_norm_w": final_norm_w,
        "w_in_t": w_in_t,
        "w_r": jnp.pad(w_in[0][:, HW:HW + RC_COLS], ((0, 0), (0, RC_PAD - RC_COLS))).astype(BF16),
        "w_g": w_in_t[HW + RC_COLS:],
        "lb_logits": lb_logits, "hgrn_norm_w": hgrn_norm_w[0],
        "mu": jnp.pad(row(rwkv_mu[0]), ((0, 0), (0, RC_PAD - RC_COLS))),
        "w0": row(rwkv_w0[0]), "a0": row(rwkv_a0[0]), "k_k": row(rwkv_k_k[0]), "k_a": row(rwkv_k_a[0]),
        "wwa": wwa.astype(BF16),
        "g2p": jnp.pad(rwkv_g2[0], ((0, G_LORA_PAD - G_LORA), (0, 0))).astype(BF16),
        "lnx_w": rwkv_lnx_w[0], "lnx_b": rwkv_lnx_b[0], "r_k": rwkv_r_k[0].reshape(-1),
        "w_out": w_out[0].astype(BF16), "w_up": w_up[0].astype(BF16), "w_down": w_down[0].astype(BF16),
    }
    nb = x_prompt.shape[0]
    mod = _adaln(jnp.concatenate([c_prompt, c_sample], axis=0), ada_w[0], ada_b[0])
    y_p, hg_p, rw_p, sh_p = _run(x_prompt, mod[:nb], None, None, None, W, True)
    y_s, hg_s, rw_s, sh_s = _run(x_sample, mod[nb:], state_hgrn[0], state_rwkv[0], state_shift[0], W, False)
    ex = lambda a: a[None]
    return (y_p, y_s, ex(hg_p), ex(rw_p), ex(sh_p), ex(hg_s), ex(rw_s), ex(sh_s))
```

```python
import functools
import math

import jax
import jax.numpy as jnp
from jax import lax
from jax.experimental import pallas as pl
from jax.experimental.pallas import tpu as pltpu

F32 = jnp.float32
BF16 = jnp.bfloat16

LANES = 128
SUBLANES = 8
VMEM_LIMIT = 48 * 1024 * 1024
VMEM_LIMIT_IN = 56 * 1024 * 1024

HGRN_DK = 128
RWKV_N = 64
W_LORA = 64
A_LORA = 64
G_LORA = 160
G_LORA_PAD = 256
RC_MAIN = 3 * 1024
RC_COLS = RC_MAIN + W_LORA + A_LORA + G_LORA
RC_PAD = RC_MAIN + LANES + G_LORA_PAD
RMS_EPS = 1e-6
LNX_EPS = 64e-5
CHUNK = 64
SUB = 8
RWKV_SUB = 8


def _cparams(sem, vmem=VMEM_LIMIT):
    return pltpu.CompilerParams(dimension_semantics=sem, vmem_limit_bytes=vmem)


_NN = (((1,), (0,)), ((), ()))
_NT = (((1,), (1,)), ((), ()))
_TN = (((0,), (0,)), ((), ()))


def _dot(a, b, dims=_NN):
    return lax.dot_general(a.astype(BF16), b.astype(BF16), dims, preferred_element_type=F32)


def _split2(x):
    hi = x.astype(BF16)
    lo = (x - hi.astype(F32)).astype(BF16)
    return hi, lo


def _split3(x):
    hi = x.astype(BF16)
    r1 = x - hi.astype(F32)
    mid = r1.astype(BF16)
    lo = (r1 - mid.astype(F32)).astype(BF16)
    return hi, mid, lo


def _dot_exact_lhs(a, x, dims=_NN):
    a = a.astype(BF16)
    hi, mid, lo = _split3(x)
    d = lambda p: lax.dot_general(a, p, dims, preferred_element_type=F32)
    return d(hi) + d(mid) + d(lo)


def _dot_hp(a, b, dims=_NN):
    ah, al = _split2(a)
    bh, bl = _split2(b)
    d = lambda p, q: lax.dot_general(p, q, dims, preferred_element_type=F32)
    return d(ah, bh) + d(ah, bl) + d(al, bh)


def _sigmoid(x):
    return 1.0 / (1.0 + jnp.exp(-x))


def _iota2(shape, dim):
    return lax.broadcasted_iota(jnp.int32, shape, dim)


def _neumann_inv(L, eye, nsq):
    X = L
    P = eye - L
    for _ in range(nsq):
        X = _dot_hp(X, X)
        P = P + _dot_hp(P, X)
    return P


def _adaln_kernel(c_ref, w_ref, b_ref, o_ref):
    c = c_ref[...]
    s = c * _sigmoid(c)
    o_ref[...] = _dot(s, w_ref[...]) + b_ref[...]


def _adaln(c_all, ada_w, ada_b, tn=1536):
    R, D = c_all.shape
    N = ada_w.shape[1]
    return pl.pallas_call(
        _adaln_kernel,
        grid=(N // tn,),
        in_specs=[pl.BlockSpec((R, D), lambda j: (0, 0)),
                  pl.BlockSpec((D, tn), lambda j: (0, j)),
                  pl.BlockSpec((1, tn), lambda j: (0, j))],
        out_specs=pl.BlockSpec((R, tn), lambda j: (0, j)),
        out_shape=jax.ShapeDtypeStruct((R, N), F32),
        compiler_params=_cparams(("parallel",)),
        name="adaln",
    )(c_all, ada_w, ada_b.reshape(1, N))


class _Mod:
    def __init__(self, arr, per_seq, seq_rows, D):
        self.arr, self.per_seq, self.seq_rows, self.D = arr, per_seq, seq_rows, D
        self.rep = 1 if per_seq else seq_rows

    def spec(self, tm, comp):
        D = self.D
        if self.per_seq:
            tiles = self.seq_rows // tm
            return pl.BlockSpec((None, 1, D), lambda i, *_: (i // tiles, 0, comp))
        return pl.BlockSpec((tm // self.seq_rows, D), lambda i, *_: (i, comp))


def _rep_rows(v, rep):
    if rep == 1:
        return v
    n, w = v.shape
    return jnp.broadcast_to(v[:, None, :], (n, rep, w)).reshape(n * rep, w)


def _rms(x, w):
    return x * lax.rsqrt(jnp.mean(x * x, axis=-1, keepdims=True) + RMS_EPS) * w


def _mm_kernel(x_ref, wt_ref, o_ref):
    o_ref[...] = lax.dot_general(x_ref[...], wt_ref[...], _NT, preferred_element_type=F32)


def _matmul(x, wt, n_out, tm, tn, name):
    R, K = x.shape
    N = n_out
    return pl.pallas_call(
        _mm_kernel,
        grid=(R // tm, N // tn),
        in_specs=[pl.BlockSpec((tm, K), lambda i, j: (i, 0)),
                  pl.BlockSpec((tn, K), lambda i, j: (j, 0))],
        out_specs=pl.BlockSpec((tm, tn), lambda i, j: (i, j)),
        out_shape=jax.ShapeDtypeStruct((R, N), F32),
        compiler_params=_cparams(("parallel", "arbitrary")),
        name=name,
    )(x, wt)


def _hgrn_gates(fp, lb_logits, base2=False):
    m = jnp.max(lb_logits, axis=0, keepdims=True)
    e = jnp.exp(lb_logits - m)
    lb = e[0:1, :] / jnp.sum(e, axis=0, keepdims=True)
    f = lb + (1.0 - lb) * _sigmoid(fp)
    return (jnp.log2(f) if base2 else jnp.log(f)), 1.0 - f


def _hgrn_diag(q, kin, b, c):
    R = q.shape[0]
    nb = R // c
    q3 = q.reshape(nb, c, LANES)
    k3 = kin.reshape(nb, c, LANES)
    b3 = (b * math.log2(math.e)).reshape(nb, c, LANES)
    lane_mod = lax.broadcasted_iota(jnp.int32, (nb, c, R), 2) % c
    acc = jnp.zeros((nb, c, R), F32)
    for s in range(c):
        d = jnp.minimum(b3 - b3[:, s:s + 1, :], 0.0)
        col = jnp.sum(q3 * k3[:, s:s + 1, :] * jnp.exp2(d), axis=-1, keepdims=True)
        acc = jnp.where(lane_mod == s, col, acc)
    rr, cc = _iota2((R, R), 0), _iota2((R, R), 1)
    keep = ((rr // c) == (cc // c)) & (rr >= cc)
    return jnp.where(keep, acc.reshape(R, R), 0.0)


def _hgrn_cross(q, kin, b2, m):
    R = q.shape[0]
    h = m // 2
    nb = R // m
    ref = jnp.broadcast_to(b2.reshape(nb, m, LANES)[:, h - 1:h, :], (nb, m, LANES)).reshape(R, LANES)
    upper = (_iota2((R, 1), 0) % m) >= h
    e = jnp.exp2(-jnp.abs(b2 - ref))
    qs = jnp.where(upper, q * e, 0.0)
    ks = jnp.where(upper, 0.0, kin * e)
    sc = _dot(qs, ks, _NT)
    if nb == 1:
        return sc
    same_blk = (_iota2((R, R), 0) // m) == (_iota2((R, R), 1) // m)
    return jnp.where(same_blk, sc, 0.0)


def _hgrn_out(o, og, nw):
    o = o * lax.rsqrt(jnp.mean(o * o, axis=-1, keepdims=True) + RMS_EPS) * nw
    return o * (og * _sigmoid(og))


def _hgrn_chunks(p_ref, lb_ref, nw_ref, o_ref, st_ref, offs, hp):
    C, c = CHUNK, SUB
    nb = C // c
    W = hp * LANES
    tri = (_iota2((C, C), 0) >= _iota2((C, C), 1)).astype(F32)
    nw = nw_ref[...]
    H = range(hp)
    cols = [slice(j * LANES, (j + 1) * LANES) for j in H]
    pcol = lambda g, j: slice(g * W + j * LANES, g * W + (j + 1) * LANES)
    lane_mod = lax.broadcasted_iota(jnp.int32, (nb, c, C), 2) % c
    rr, cc = _iota2((C, C), 0), _iota2((C, C), 1)
    keep = ((rr // c) == (cc // c)) & (rr >= cc)
    for r0 in offs:
        q = [p_ref[pl.ds(r0, C), pcol(0, j)] for j in H]
        v = [p_ref[pl.ds(r0, C), pcol(2, j)] for j in H]
        gates = [_hgrn_gates(p_ref[pl.ds(r0, C), pcol(1, j)], lb_ref[:, cols[j]], base2=True) for j in H]
        lf = [gt[0] for gt in gates]
        kin = [gt[1] for gt in gates]
        yield
        b = [_dot_exact_lhs(tri, lf[j]) for j in H]
        b_last = [b[j][C - 1:C, :] for j in H]
        yield
        ST = [st_ref[j] for j in H]
        o = [_dot(q[j] * jnp.exp2(b[j]), ST[j], _NT) for j in H]
        yield
        for j in H:
            st_ref[j] = ST[j] * jnp.exp2(b_last[j]) + _dot(v[j], kin[j] * jnp.exp2(b_last[j] - b[j]), _TN)
        yield
        q3 = [q[j].reshape(nb, c, LANES) for j in H]
        k3 = [kin[j].reshape(nb, c, LANES) for j in H]
        b3 = [b[j].reshape(nb, c, LANES) for j in H]
        acc = [jnp.zeros((nb, c, C), F32) for _ in H]
        for s in range(c):
            for j in H:
                d = jnp.minimum(b3[j] - b3[j][:, s:s + 1, :], 0.0)
                col = jnp.sum(q3[j] * k3[j][:, s:s + 1, :] * jnp.exp2(d), axis=-1, keepdims=True)
                acc[j] = jnp.where(lane_mod == s, col, acc[j])
            yield
        A = [jnp.where(keep, acc[j].reshape(C, C), 0.0) for j in H]
        m = 2 * c
        while m <= C:
            A = [A[j] + _hgrn_cross(q[j], kin[j], b[j], m) for j in H]
            m *= 2
            yield
        o = [o[j] + _dot(A[j], v[j]) for j in H]
        yield
        for j in H:
            o_ref[pl.ds(r0, C), cols[j]] = _hgrn_out(o[j], p_ref[pl.ds(r0, C), pcol(3, j)], nw)
        yield


def _hgrn_sample_kernel(q_ref, f_ref, i_ref, g_ref, lb_ref, nw_ref, s0_ref, o_ref, s_ref, *, nseq, T, hp):
    R = nseq * T
    rr, cc = _iota2((R, R), 0), _iota2((R, R), 1)
    same = (rr // T) == (cc // T)
    seg_incl = (same & (rr >= cc)).astype(F32)
    seg_all = same.astype(F32)
    nw = nw_ref[...]
    H = range(hp)
    cols = [slice(j * LANES, (j + 1) * LANES) for j in H]
    q = [q_ref[:, cols[j]] for j in H]
    v = [i_ref[:, cols[j]] for j in H]
    gates = [_hgrn_gates(f_ref[:, cols[j]], lb_ref[:, cols[j]]) for j in H]
    lf = [gt[0] for gt in gates]
    kin = [gt[1] for gt in gates]
    b = [_dot_exact_lhs(seg_incl, lf[j]) for j in H]
    b_tot = [_dot_exact_lhs(seg_all, lf[j]) for j in H]
    A = [_hgrn_diag(q[j], kin[j], b[j], T) for j in H]
    o = [_dot(A[j], v[j]) for j in H]
    qg = [q[j] * jnp.exp(b[j]) for j in H]
    kh = [kin[j] * jnp.exp(b_tot[j] - b[j]) for j in H]
    ecol = [jnp.transpose(jnp.exp(b_tot[j]).reshape(nseq, T, LANES)[:, 0, :]) for j in H]
    inter = [[] for _ in H]
    for n in range(nseq):
        sl = slice(n * T, (n + 1) * T)
        for j in H:
            S = s0_ref[n, j]
            inter[j].append(_dot(qg[j][sl], S))
            s_ref[n, j] = S * ecol[j][:, n:n + 1] + _dot(kh[j][sl], v[j][sl], _TN)
    for j in H:
        oj = o[j] + jnp.concatenate(inter[j], axis=0)
        o_ref[:, cols[j]] = _hgrn_out(oj, g_ref[:, cols[j]], nw)


def _hgrn_sample(P_h, lb_logits, nw, s0, T, nseq=16, hp=8):
    R, W = P_h.shape
    H = W // (4 * HGRN_DK)
    HG = H // hp
    wd = hp * HGRN_DK
    B = R // T
    rows = nseq * T
    col = lambda off: pl.BlockSpec((rows, wd), lambda h, g: (g, off * HG + h))
    sspec = pl.BlockSpec((nseq, hp, HGRN_DK, HGRN_DK), lambda h, g: (g, h, 0, 0))
    o, s = pl.pallas_call(
        functools.partial(_hgrn_sample_kernel, nseq=nseq, T=T, hp=hp),
        grid=(HG, B // nseq),
        in_specs=[col(0), col(1), col(2), col(3),
                  pl.BlockSpec((lb_logits.shape[0], wd), lambda h, g: (0, h)),
                  pl.BlockSpec((1, HGRN_DK), lambda h, g: (0, 0)),
                  sspec],
        out_specs=[pl.BlockSpec((rows, wd), lambda h, g: (g, h)), sspec],
        out_shape=[jax.ShapeDtypeStruct((R, H * HGRN_DK), F32),
                   jax.ShapeDtypeStruct(s0.shape, F32)],
        compiler_params=_cparams(("parallel", "parallel")),
        name="hgrn_sample",
    )(P_h, P_h, P_h, P_h, lb_logits, nw.reshape(1, HGRN_DK), s0)
    return o, s


def _rwkv_prep_body(rc, prev, mu_ref, w0_ref, a0_ref, kk_ref, ka_ref, wwa_ref, g2_ref, outs, rows):
    r_o, lw_o, k_o, v_o, kap_o, bet_o, g_o = outs
    D = RC_MAIN // 3
    xs = rc + mu_ref[...] * (prev - rc)
    r = xs[:, 0:D]
    kb = xs[:, D:2 * D]
    v = xs[:, 2 * D:3 * D]
    xwa = xs[:, RC_MAIN:RC_MAIN + LANES]
    xg = xs[:, RC_MAIN + LANES:]
    lane = _iota2((1, LANES), 1)
    wa = _dot(jnp.where(lane < W_LORA, jnp.tanh(xwa), xwa), wwa_ref[...])
    w = w0_ref[...] + wa[:, :D]
    a = _sigmoid(a0_ref[...] + wa[:, D:])
    lw = -(math.exp(-0.5) * math.log2(math.e)) * _sigmoid(w)
    g = _dot(_sigmoid(xg), g2_ref[...])
    kk = kb * kk_ref[...]
    mA, mB = _pair_masks()
    k_a = ka_ref[...]
    for p in range(D // LANES):
        sl = slice(p * LANES, (p + 1) * LANES)
        kkp = kk[:, sl]
        kap = kkp * lax.rsqrt(jnp.maximum(_head_sum(kkp * kkp, mA, mB), 1e-24))
        ap = a[:, sl]
        kap_o[rows, sl] = kap
        bet_o[rows, sl] = ap * kap
        k_o[rows, sl] = kb[:, sl] * (1.0 + (ap - 1.0) * k_a[:, sl])
    r_o[rows, :] = r
    lw_o[rows, :] = lw
    v_o[rows, :] = v
    g_o[rows, :] = g


def _in_rwkv_kernel(x_ref, nw_ref, sh_ref, sc_ref, wr_ref, shift_ref, mu_ref, w0_ref, a0_ref, kk_ref,
                    ka_ref, wwa_ref, g2_ref, h_ref, r_o, lw_o, k_o, v_o, kap_o, bet_o, g_o, last_o,
                    carry_ref, rc_ref, *, seq_rows, sub):
    i = pl.program_id(0)
    tm = x_ref.shape[0]
    per_seq = seq_rows >= tm

    @pl.when(i == 0)
    def _():
        carry_ref[...] = jnp.zeros_like(carry_ref)

    rep = 1 if per_seq else seq_rows
    h = (_rms(x_ref[...], nw_ref[...]) * (1.0 + _rep_rows(sc_ref[...], rep))
         + _rep_rows(sh_ref[...], rep)).astype(BF16)
    h_ref[...] = h
    outs = (r_o, lw_o, k_o, v_o, kap_o, bet_o, g_o)
    rid = _iota2((sub, 1), 0)
    rc_ref[...] = jnp.dot(h, wr_ref[...], preferred_element_type=F32)
    prev_last = None
    for s in range(tm // sub):
        rows = slice(s * sub, (s + 1) * sub)
        rc = rc_ref[rows, :]
        rolled = pltpu.roll(rc, 1, axis=0)
        if per_seq:
            if s == 0:
                start = (i % (seq_rows // tm)) == 0
                first = jnp.where(start, shift_ref[...], carry_ref[SUBLANES - 1:SUBLANES, :])
            else:
                first = prev_last
            prev = jnp.where(rid == 0, first, rolled)
        else:
            seqs = slice(s * sub // seq_rows, (s + 1) * sub // seq_rows)
            prev = jnp.where(rid % seq_rows == 0, _rep_rows(shift_ref[seqs, :], seq_rows), rolled)
            last_o[seqs, :] = rc.reshape(sub // seq_rows, seq_rows, RC_PAD)[:, seq_rows - 1, :]
        prev_last = rc[sub - 1:sub, :]
        if s == tm // sub - 1 and per_seq:
            carry_ref[...] = rc[sub - SUBLANES:, :]
            last_o[...] = prev_last
        _rwkv_prep_body(rc, prev, mu_ref, w0_ref, a0_ref, kk_ref, ka_ref, wwa_ref, g2_ref, outs, rows)


def _in_rwkv(x, nw, mod, w_r, shift, params, per_seq, seq_rows, tm=512, sub=128):
    R, D = x.shape
    tm = min(tm, seq_rows) if per_seq else min(tm, R)
    sub = min(sub, tm)
    mu, w0, a0, k_k, k_a, wwa, g2p = params
    const = lambda a: pl.BlockSpec(a.shape, lambda i: (0,) * a.ndim, pipeline_mode=pl.Buffered(1))
    wspecs = [const(a) for a in (mu, w0, a0, k_k, k_a, wwa, g2p)]
    row = lambda w: pl.BlockSpec((tm, w), lambda i: (i, 0))
    if per_seq:
        tiles = seq_rows // tm
        shift_spec = pl.BlockSpec((None, 1, RC_PAD), lambda i: (i // tiles, 0, 0))
        last_spec = pl.BlockSpec((None, 1, RC_PAD), lambda i: (i // tiles, 0, 0))
        last_shape = jax.ShapeDtypeStruct((R // seq_rows, 1, RC_PAD), F32)
    else:
        shift_spec = pl.BlockSpec((tm // seq_rows, RC_PAD), lambda i: (i, 0))
        last_spec = pl.BlockSpec((tm // seq_rows, RC_PAD), lambda i: (i, 0))
        last_shape = jax.ShapeDtypeStruct((R // seq_rows, RC_PAD), F32)
    outs = pl.pallas_call(
        functools.partial(_in_rwkv_kernel, seq_rows=seq_rows, sub=sub),
        grid=(R // tm,),
        in_specs=[row(D), const(nw), mod.spec(tm, 0), mod.spec(tm, 1), const(w_r), shift_spec] + wspecs,
        out_specs=[row(D)] + [row(D)] * 7 + [last_spec],
        out_shape=[jax.ShapeDtypeStruct((R, D), BF16)] + [jax.ShapeDtypeStruct((R, D), F32)] * 7
                  + [last_shape],
        scratch_shapes=[pltpu.VMEM((SUBLANES, RC_PAD), F32), pltpu.VMEM((tm, RC_PAD), F32)],
        compiler_params=_cparams(("arbitrary",), VMEM_LIMIT_IN),
        name="in_rwkv",
    )(x, nw, mod.arr, mod.arr, w_r, shift, mu, w0, a0, k_k, k_a, wwa, g2p)
    return outs[0], outs[1:8], outs[8]


def _pair_masks():
    lane = _iota2((1, LANES), 1)
    return (lane < RWKV_N).astype(F32), (lane >= RWKV_N).astype(F32)


def _head_sum(x, mA, mB):
    sa = jnp.sum(x * mA, axis=-1, keepdims=True)
    sb = jnp.sum(x * mB, axis=-1, keepdims=True)
    return sa * mA + sb * mB


def _stack(x, mA, mB, dtype=F32):
    x = x.astype(dtype)
    return jnp.concatenate([x * mA.astype(dtype), x * mB.astype(dtype)], axis=0)


def _rwkv_scores(cs, lw, cs_end, r, k, kap, bet, mA, mB, dtype=F32):
    e_neg = jnp.exp2(-cs)
    e_rem = jnp.exp2(cs_end - cs)
    rg = _stack(r * jnp.exp2(cs), mA, mB, dtype)
    kg = _stack(kap * jnp.exp2(cs - lw), mA, mB, dtype)
    kd = _stack(k * e_neg, mA, mB, dtype)
    bd = _stack(bet * e_neg, mA, mB, dtype)
    kh = _stack(k * e_rem, mA, mB, dtype)
    bh = _stack(bet * e_rem, mA, mB, dtype)
    return rg, kg, kd, bd, kh, bh


def _rwkv_epilogue(o, r, k, v, g, lnw, lnb, rk, mA, mB):
    mean = _head_sum(o, mA, mB) * (1.0 / RWKV_N)
    d = o - mean
    var = _head_sum(d * d, mA, mB) * (1.0 / RWKV_N)
    on = d * lax.rsqrt(var + LNX_EPS) * lnw + lnb
    bonus = _head_sum(r * k * rk, mA, mB) * v
    return (on + bonus) * g


def _rwkv_chunks(r_ref, lw_ref, k_ref, v_ref, kap_ref, bet_ref, g_ref, lnw_ref, lnb_ref, rk_ref,
                 o_ref, st_ref, offs, hp):
    C, c = CHUNK, RWKV_SUB
    C2 = 2 * C
    mA, mB = _pair_masks()
    rr, cc = _iota2((C2, C2), 0), _iota2((C2, C2), 1)
    eye = (rr == cc).astype(F32)
    strict = rr > cc
    incl = rr >= cc
    blk = (rr // c) == (cc // c)
    tri = (_iota2((C, C), 0) >= _iota2((C, C), 1)).astype(F32)
    chains = [(r0, p) for r0 in offs for p in range(hp)]
    J = range(len(chains))
    pcols = [slice(p * LANES, (p + 1) * LANES) for _, p in chains]

    ld = lambda ref: [ref[pl.ds(r0, C), pcols[j]] for j, (r0, _) in zip(J, chains)]
    r, lw, k, v, kap, bet = ld(r_ref), ld(lw_ref), ld(k_ref), ld(v_ref), ld(kap_ref), ld(bet_ref)
    yield
    cs = [_dot_exact_lhs(tri, lw[j]) for j in J]
    cs_end = [cs[j][C - 1:C, :] for j in J]
    yield
    sc = [_rwkv_scores(cs[j], lw[j], cs_end[j], r[j], k[j], kap[j], bet[j], mA, mB, BF16) for j in J]
    rg, kg, kd, bd, kh, bh = ([s[i] for s in sc] for i in range(6))
    vs = [_stack(v[j], mA, mB, BF16) for j in J]
    kbd = [jnp.concatenate([kd[j], bd[j]], axis=0) for j in J]
    yield
    sck = [_dot(kg[j], kbd[j], _NT) for j in J]
    yield
    scr = [_dot(rg[j], kbd[j], _NT) for j in J]
    Ak = [jnp.where(strict, sck[j][:, :C2], 0.0) for j in J]
    Ab = [jnp.where(strict, sck[j][:, C2:], 0.0) for j in J]
    AkV = [_dot(Ak[j], vs[j]) for j in J]
    yield
    Bkb = [jnp.concatenate([jnp.where(incl, scr[j][:, :C2], 0.0),
                            jnp.where(incl, -scr[j][:, C2:], 0.0)], axis=1).astype(BF16) for j in J]
    LD = [jnp.where(blk, Ab[j], 0.0) for j in J]
    X = LD
    Dinv = [eye - LD[j] for j in J]
    for _ in range(int(math.log2(c)) - 1):
        X = [_dot(X[j], X[j]) for j in J]
        yield
        Dinv = [Dinv[j] + _dot(Dinv[j], X[j]) for j in J]
        yield
    Q = [_dot(Dinv[j], jnp.where(blk, 0.0, Ab[j])) for j in J]
    yield
    T0 = [Dinv[j] - _dot(Q[j], Dinv[j]) for j in J]
    yield
    for _ in range(int(math.log2(C // c)) - 1):
        Q = [_dot(Q[j], Q[j]) for j in J]
        yield
        T0 = [T0[j] + _dot(Q[j], T0[j]) for j in J]
        yield
    E = [(eye - T0[j]) - _dot_hp(Ab[j], T0[j]) for j in J]
    yield
    Tm = [T0[j] + _dot(T0[j], E[j]) for j in J]
    yield
    khb = [jnp.concatenate([kh[j], -bh[j]], axis=0) for j in J]
    dec = [jnp.exp2(cs_end[j]) for j in J]
    yield

    for u in range(len(offs)):
        K = range(u * hp, (u + 1) * hp)
        S = {j: st_ref[chains[j][1]] for j in K}
        rhs = {j: _dot(kg[j], S[j], _NT) + AkV[j] for j in K}
        yield
        U = {j: _dot_hp(Tm[j], rhs[j]) for j in K}
        yield
        vu = {j: jnp.concatenate([vs[j], U[j].astype(BF16)], axis=0) for j in K}
        for j in K:
            st_ref[chains[j][1]] = S[j] * dec[j] + _dot(vu[j], khb[j], _TN)
        yield
        O = {j: _dot(rg[j], S[j], _NT) + _dot(Bkb[j], vu[j]) for j in K}
        o = {j: O[j][:C, :] + O[j][C:, :] for j in K}
        yield
        mean = {j: _head_sum(o[j], mA, mB) * (1.0 / RWKV_N) for j in K}
        d = {j: o[j] - mean[j] for j in K}
        yield
        var = {j: _head_sum(d[j] * d[j], mA, mB) * (1.0 / RWKV_N) for j in K}
        rkk = {j: r_ref[pl.ds(chains[j][0], C), pcols[j]] * k_ref[pl.ds(chains[j][0], C), pcols[j]]
               * rk_ref[:, pcols[j]] for j in K}
        yield
        bsum = {j: _head_sum(rkk[j], mA, mB) for j in K}
        yield
        for j in K:
            rows = pl.ds(chains[j][0], C)
            on = d[j] * lax.rsqrt(var[j] + LNX_EPS) * lnw_ref[:, pcols[j]] + lnb_ref[:, pcols[j]]
            bonus = bsum[j] * v_ref[rows, pcols[j]]
            o_ref[rows, pcols[j]] = (on + bonus) * g_ref[rows, pcols[j]]
        yield


def _interleave(gens):
    live = list(gens)
    while live:
        for gen in list(live):
            try:
                next(gen)
            except StopIteration:
                live.remove(gen)


def _scan_prompt_kernel(p_ref, lb_ref, nw_ref,
                        r_ref, lw_ref, k_ref, v_ref, kap_ref, bet_ref, g_ref, lnw_ref, lnb_ref, rk_ref,
                        oa_ref, hs_ref, ob_ref, rs_ref, hst_ref, rst_ref, *, nchunks, hh, hp):
    tt = pl.program_id(1)

    @pl.when(tt == 0)
    def _():
        hst_ref[...] = jnp.zeros_like(hst_ref)
        rst_ref[...] = jnp.zeros_like(rst_ref)

    C = CHUNK
    unroll = 2 if nchunks % 2 == 0 else 1

    def body(ci, carry):
        base = ci * (unroll * C)
        offs = [pl.multiple_of(base + u * C, C) for u in range(unroll)]
        _interleave([
            _rwkv_chunks(r_ref, lw_ref, k_ref, v_ref, kap_ref, bet_ref, g_ref, lnw_ref, lnb_ref, rk_ref,
                         ob_ref, rst_ref, offs, hp),
            _hgrn_chunks(p_ref, lb_ref, nw_ref, oa_ref, hst_ref, offs, hh),
        ])
        return carry

    lax.fori_loop(0, nchunks // unroll, body, 0)

    @pl.when(tt == pl.num_programs(1) - 1)
    def _():
        for j in range(hh):
            hs_ref[j] = hst_ref[j].T
        for p in range(hp):
            S = rst_ref[p]
            rs_ref[2 * p] = S[:RWKV_N, :RWKV_N]
            rs_ref[2 * p + 1] = S[RWKV_N:, RWKV_N:]


def _scan_prompt(P_h, lb_logits, nw, arrs, g, lnw, lnb, rk, B, T, tt_rows=256):
    R, D = arrs[0].shape
    hh = P_h.shape[1] // (4 * HGRN_DK)
    hp = D // LANES
    wh = hh * HGRN_DK
    tt_rows = min(tt_rows, T)
    nt = T // tt_rows
    col = pl.BlockSpec((tt_rows, D), lambda b, t: (b * nt + t, 0))
    const = lambda a: pl.BlockSpec(a.shape, lambda b, t: (0,) * a.ndim)
    nw = nw.reshape(1, HGRN_DK)
    lnw, lnb, rk = lnw.reshape(1, D), lnb.reshape(1, D), rk.reshape(1, D)
    oa, hs, ob, rs = pl.pallas_call(
        functools.partial(_scan_prompt_kernel, nchunks=tt_rows // CHUNK, hh=hh, hp=hp),
        grid=(B, nt),
        in_specs=[pl.BlockSpec((tt_rows, 4 * wh), lambda b, t: (b * nt + t, 0)),
                  const(lb_logits), const(nw)]
                 + [col] * 7 + [const(lnw), const(lnb), const(rk)],
        out_specs=[pl.BlockSpec((tt_rows, wh), lambda b, t: (b * nt + t, 0)),
                   pl.BlockSpec((None, hh, HGRN_DK, HGRN_DK), lambda b, t: (b, 0, 0, 0)),
                   col,
                   pl.BlockSpec((None, 2 * hp, RWKV_N, RWKV_N), lambda b, t: (b, 0, 0, 0))],
        out_shape=[jax.ShapeDtypeStruct((R, wh), F32),
                   jax.ShapeDtypeStruct((B, hh, HGRN_DK, HGRN_DK), F32),
                   jax.ShapeDtypeStruct((R, D), F32),
                   jax.ShapeDtypeStruct((B, 2 * hp, RWKV_N, RWKV_N), F32)],
        scratch_shapes=[pltpu.VMEM((hh, HGRN_DK, HGRN_DK), F32), pltpu.VMEM((hp, LANES, LANES), F32)],
        compiler_params=_cparams(("parallel", "arbitrary")),
        name="scan_prompt",
    )(P_h, lb_logits, nw, *arrs, g, lnw, lnb, rk)
    return oa, hs, ob, rs


def _rwkv_sample_kernel(r_ref, lw_ref, k_ref, v_ref, kap_ref, bet_ref, g_ref, lnw_ref, lnb_ref, rk_ref,
                        s0_ref, o_ref, s_ref, sbd_ref, *, nseq, T, hp):
    C = nseq * T
    C2 = 2 * C
    N = RWKV_N
    mA, mB = _pair_masks()
    bdm = (_iota2((LANES, LANES), 0) // N == _iota2((LANES, LANES), 1) // N).astype(F32)
    rr, cc = _iota2((C2, C2), 0), _iota2((C2, C2), 1)
    eye = (rr == cc).astype(F32)
    same = (rr // T) == (cc // T)
    strict = same & (rr > cc)
    incl = same & (rr >= cc)
    r1, c1 = _iota2((C, C), 0), _iota2((C, C), 1)
    seg_incl = ((r1 // T == c1 // T) & (r1 >= c1)).astype(F32)
    seg_all = (r1 // T == c1 // T).astype(F32)
    zero = jnp.zeros((N, N), F32)

    J = range(hp)
    cols = [slice(j * LANES, (j + 1) * LANES) for j in J]
    ld = lambda ref: [ref[:, cols[j]] for j in J]
    r, lw, k, v, kap, bet = ld(r_ref), ld(lw_ref), ld(k_ref), ld(v_ref), ld(kap_ref), ld(bet_ref)
    cs = [_dot_exact_lhs(seg_incl, lw[j]) for j in J]
    cs_end = [_dot_exact_lhs(seg_all, lw[j]) for j in J]
    sc = [_rwkv_scores(cs[j], lw[j], cs_end[j], r[j], k[j], kap[j], bet[j], mA, mB) for j in J]
    rg, kg, kd, bd, kh, bh = ([s[i] for s in sc] for i in range(6))
    vs = [_stack(v[j], mA, mB) for j in J]
    kbd = [jnp.concatenate([kd[j], bd[j]], axis=0) for j in J]
    sck = [_dot(kg[j], kbd[j], _NT) for j in J]
    scr = [_dot(rg[j], kbd[j], _NT) for j in J]
    Ak = [jnp.where(strict, sck[j][:, :C2], 0.0) for j in J]
    Ab = [jnp.where(strict, sck[j][:, C2:], 0.0) for j in J]
    Bkb = [jnp.concatenate([jnp.where(incl, scr[j][:, :C2], 0.0),
                            jnp.where(incl, -scr[j][:, C2:], 0.0)], axis=1) for j in J]
    Tm = [_neumann_inv(Ab[j], eye, int(math.log2(T)) - 1) for j in J]
    kgp = [kg[j][:C, :] + kg[j][C:, :] for j in J]
    rgp = [rg[j][:C, :] + rg[j][C:, :] for j in J]
    ks = [[] for _ in J]
    rs = [[] for _ in J]
    for n in range(nseq):
        sl = slice(n * T, (n + 1) * T)
        for j in J:
            Sbd = jnp.concatenate(
                [jnp.concatenate([s0_ref[n, 2 * j], zero], axis=1),
                 jnp.concatenate([zero, s0_ref[n, 2 * j + 1]], axis=1)], axis=0)
            sbd_ref[n, j] = Sbd
            res = _dot(jnp.concatenate([kgp[j][sl], rgp[j][sl]], axis=0), Sbd, _NT)
            ks[j].append(res[:T])
            rs[j].append(res[T:])
    KS = [_stack(jnp.concatenate(ks[j], axis=0), mA, mB) for j in J]
    RS = [_stack(jnp.concatenate(rs[j], axis=0), mA, mB) for j in J]
    rhs = [KS[j] + _dot(Ak[j], vs[j]) for j in J]
    U = [_dot_hp(Tm[j], rhs[j]) for j in J]
    vu = [jnp.concatenate([vs[j], U[j]], axis=0) for j in J]
    O = [RS[j] + _dot(Bkb[j], vu[j]) for j in J]
    Up = [U[j][:C, :] + U[j][C:, :] for j in J]
    khp = [kh[j][:C, :] + kh[j][C:, :] for j in J]
    nbhp = [-(bh[j][:C, :] + bh[j][C:, :]) for j in J]
    dec = [jnp.exp2(cs_end[j]) for j in J]
    for n in range(nseq):
        sl = slice(n * T, (n + 1) * T)
        for j in J:
            upd = _dot(jnp.concatenate([v[j][sl], Up[j][sl]], axis=0),
                       jnp.concatenate([khp[j][sl], nbhp[j][sl]], axis=0), _TN)
            Sn = sbd_ref[n, j] * dec[j][n * T:n * T + 1, :] + upd * bdm
            s_ref[n, 2 * j] = Sn[:N, :N]
            s_ref[n, 2 * j + 1] = Sn[N:, N:]
    for j in J:
        o = O[j][:C, :] + O[j][C:, :]
        o_ref[:, cols[j]] = _rwkv_epilogue(o, r[j], k[j], v[j], g_ref[:, cols[j]],
                                           lnw_ref[:, cols[j]], lnb_ref[:, cols[j]], rk_ref[:, cols[j]],
                                           mA, mB)


def _rwkv_sample(arrs, g, lnw, lnb, rk, s0, T, nseq=16, hp=8):
    R, D = arrs[0].shape
    NP = D // LANES
    B = R // T
    rows = nseq * T
    wd = hp * LANES
    col = pl.BlockSpec((rows, wd), lambda p, gi: (gi, p))
    par = pl.BlockSpec((1, wd), lambda p, gi: (0, p))
    sspec = pl.BlockSpec((nseq, 2 * hp, RWKV_N, RWKV_N), lambda p, gi: (gi, p, 0, 0))
    o, s = pl.pallas_call(
        functools.partial(_rwkv_sample_kernel, nseq=nseq, T=T, hp=hp),
        grid=(NP // hp, B // nseq),
        in_specs=[col] * 7 + [par] * 3 + [sspec],
        out_specs=[col, sspec],
        out_shape=[jax.ShapeDtypeStruct((R, D), F32), jax.ShapeDtypeStruct(s0.shape, F32)],
        scratch_shapes=[pltpu.VMEM((nseq, hp, LANES, LANES), F32)],
        compiler_params=_cparams(("parallel", "parallel")),
        name="rwkv_sample",
    )(*arrs, g, lnw.reshape(1, D), lnb.reshape(1, D), rk.reshape(1, D), s0)
    return o, s


def _merge_kernel(x_ref, h_ref, oa_ref, ob_ref, wg_ref, w_ref, gt_ref, o_ref, *, rep):
    D = x_ref.shape[1]
    pg = lax.dot_general(h_ref[...], wg_ref[...], _NT, preferred_element_type=F32)
    m = _sigmoid(pg[:, :D]) * oa_ref[...] + _sigmoid(pg[:, D:]) * ob_ref[...]
    o_ref[...] = x_ref[...] + _rep_rows(gt_ref[...], rep) * _dot(m, w_ref[...])


def _merge_out(x, h, o_a, o_b, w_g, w_out, mod, tm):
    R, D = x.shape
    row = pl.BlockSpec((tm, D), lambda i: (i, 0))
    const = lambda a: pl.BlockSpec(a.shape, lambda i: (0,) * a.ndim, pipeline_mode=pl.Buffered(1))
    return pl.pallas_call(
        functools.partial(_merge_kernel, rep=mod.rep),
        grid=(R // tm,),
        in_specs=[row, row, row, row, const(w_g), const(w_out), mod.spec(tm, 2)],
        out_specs=row,
        out_shape=jax.ShapeDtypeStruct((R, D), F32),
        compiler_params=_cparams(("parallel",)),
        name="merge_out",
    )(x, h, o_a, o_b, w_g, w_out, mod.arr)


def _mlp_kernel(x_ref, nw_ref, sh_ref, sc_ref, gt_ref, wu_ref, wd_ref, fw_ref, o_ref, *, tf, rep):
    x = x_ref[...]
    h = (_rms(x, nw_ref[...]) * (1.0 + _rep_rows(sc_ref[...], rep)) + _rep_rows(sh_ref[...], rep)).astype(BF16)
    acc = None
    for c in range(wu_ref.shape[1] // tf):
        u = jnp.dot(h, wu_ref[:, c * tf:(c + 1) * tf], preferred_element_type=F32)
        u = jnp.square(jnp.maximum(u, 0.0))
        part = _dot(u, wd_ref[c * tf:(c + 1) * tf, :])
        acc = part if acc is None else acc + part
    o_ref[...] = _rms(x + _rep_rows(gt_ref[...], rep) * acc, fw_ref[...])


def _mlp(x, nw, w_up, w_down, fw, mod, tm, tf=1024):
    R, D = x.shape
    const = lambda a: pl.BlockSpec(a.shape, lambda i: (0,) * a.ndim, pipeline_mode=pl.Buffered(1))
    row = pl.BlockSpec((tm, D), lambda i: (i, 0))
    nw, fw = nw.reshape(1, D), fw.reshape(1, D)
    return pl.pallas_call(
        functools.partial(_mlp_kernel, tf=tf, rep=mod.rep),
        grid=(R // tm,),
        in_specs=[row, const(nw), mod.spec(tm, 3), mod.spec(tm, 4), mod.spec(tm, 5),
                  const(w_up), const(w_down), const(fw)],
        out_specs=row,
        out_shape=jax.ShapeDtypeStruct((R, D), F32),
        compiler_params=_cparams(("parallel",)),
        name="mlp",
    )(x, nw, mod.arr, mod.arr, mod.arr, w_up, w_down, fw)


def _merge_mlp_kernel(x_ref, h_ref, oa_ref, ob_ref, wg_ref, wo_ref, gt1_ref, nw_ref, sh_ref, sc_ref, gt2_ref,
                      wu_ref, wd_ref, fw_ref, o_ref, *, tf, rep):
    D = x_ref.shape[1]
    pg = lax.dot_general(h_ref[...], wg_ref[...], _NT, preferred_element_type=F32)
    m = _sigmoid(pg[:, :D]) * oa_ref[...] + _sigmoid(pg[:, D:]) * ob_ref[...]
    x = x_ref[...] + _rep_rows(gt1_ref[...], rep) * _dot(m, wo_ref[...])
    h2 = (_rms(x, nw_ref[...]) * (1.0 + _rep_rows(sc_ref[...], rep)) + _rep_rows(sh_ref[...], rep)).astype(BF16)
    acc = None
    for c in range(wu_ref.shape[1] // tf):
        u = jnp.dot(h2, wu_ref[:, c * tf:(c + 1) * tf], preferred_element_type=F32)
        u = jnp.square(jnp.maximum(u, 0.0))
        part = _dot(u, wd_ref[c * tf:(c + 1) * tf, :])
        acc = part if acc is None else acc + part
    o_ref[...] = _rms(x + _rep_rows(gt2_ref[...], rep) * acc, fw_ref[...])


def _merge_mlp(x, h, o_a, o_b, w_g, w_out, nw, w_up, w_down, fw, mod, tm, tf=1024):
    R, D = x.shape
    const = lambda a: pl.BlockSpec(a.shape, lambda i: (0,) * a.ndim, pipeline_mode=pl.Buffered(1))
    row = pl.BlockSpec((tm, D), lambda i: (i, 0))
    nw, fw = nw.reshape(1, D), fw.reshape(1, D)
    return pl.pallas_call(
        functools.partial(_merge_mlp_kernel, tf=tf, rep=mod.rep),
        grid=(R // tm,),
        in_specs=[row, row, row, row, const(w_g), const(w_out), mod.spec(tm, 2),
                  const(nw), mod.spec(tm, 3), mod.spec(tm, 4), mod.spec(tm, 5),
                  const(w_up), const(w_down), const(fw)],
        out_specs=row,
        out_shape=jax.ShapeDtypeStruct((R, D), F32),
        compiler_params=_cparams(("parallel",), VMEM_LIMIT_IN),
        name="merge_mlp",
    )(x, h, o_a, o_b, w_g, w_out, mod.arr, nw, mod.arr, mod.arr, mod.arr, w_up, w_down, fw)


def _run(x3, mod_rows, s_hgrn, s_rwkv, s_shift, W, prompt):
    B, T, D = x3.shape
    R = B * T
    x = x3.reshape(R, D)
    if prompt:
        mod = _Mod(mod_rows.reshape(B, 1, -1), True, T, D)
        tm = min(1024, T)
    else:
        mod = _Mod(mod_rows, False, T, D)
        tm = min(512, R)

    rw = (W["mu"], W["w0"], W["a0"], W["k_k"], W["k_a"], W["wwa"], W["g2p"])
    nw1 = W["norm1_w"].reshape(1, D)
    if prompt:
        shift_in = jnp.zeros((B, 1, RC_PAD), F32)
        h, arrs, last = _in_rwkv(x, nw1, mod, W["w_r"], shift_in, rw, True, T)
        shift_out = last[:, 0, :RC_COLS]
        P_h = _matmul(h, W["w_in_t"], 4 * D, tm, 2048, "proj_h")
        o_a, hg, o_b, rs = _scan_prompt(P_h, W["lb_logits"], W["hgrn_norm_w"], arrs[:6], arrs[6],
                                        W["lnx_w"], W["lnx_b"], W["r_k"], B, T)
    else:
        shift_in = jnp.pad(s_shift, ((0, 0), (0, RC_PAD - RC_COLS)))
        h, arrs, last = _in_rwkv(x, nw1, mod, W["w_r"], shift_in, rw, False, T)
        shift_out = last[:, :RC_COLS]
        P_h = _matmul(h, W["w_in_t"], 4 * D, tm, 2048, "proj_h")
        o_a, hg = _hgrn_sample(P_h, W["lb_logits"], W["hgrn_norm_w"], s_hgrn, T)
        o_b, rs = _rwkv_sample(arrs[:6], arrs[6], W["lnx_w"], W["lnx_b"], W["r_k"], s_rwkv, T)

    y = _merge_mlp(x, h, o_a, o_b, W["w_g"], W["w_out"], W["norm2_w"], W["w_up"], W["w_down"],
                   W["final_norm_w"], mod, min(tm, 512))
    return y.reshape(B, T, D), hg, rs, shift_out


def kernel(x_prompt, x_sample, state_hgrn, state_rwkv, state_shift, c_prompt, c_sample, norm1_w, norm2_w, ada_w, ada_b, w_in, lb_logits, hgrn_norm_w, rwkv_mu, rwkv_w0, rwkv_w2, rwkv_a0, rwkv_a2, rwkv_g2, rwkv_k_k, rwkv_k_a, rwkv_r_k, rwkv_lnx_w, rwkv_lnx_b, w_out, w_up, w_down, final_norm_w):
    depth = norm1_w.shape[0]
    assert depth == 1, "single-layer step"
    D = x_prompt.shape[-1]
    HW = 4 * D
    w_in_t = w_in[0].T.astype(BF16)
    row = lambda a: a.reshape(1, -1)
    wwa = jnp.zeros((LANES, 2 * D), F32)
    wwa = wwa.at[:W_LORA, :D].set(rwkv_w2[0]).at[W_LORA:W_LORA + A_LORA, D:].set(rwkv_a2[0])
    W = {
        "norm1_w": norm1_w[0], "norm2_w": norm2_w[0], "final_norm_w": final_norm_w,
        "w_in_t": w_in_t,
        "w_r": jnp.pad(w_in[0][:, HW:HW + RC_COLS], ((0, 0), (0, RC_PAD - RC_COLS))).astype(BF16),
        "w_g": w_in_t[HW + RC_COLS:],
        "lb_logits": lb_logits, "hgrn_norm_w": hgrn_norm_w[0],
        "mu": jnp.pad(row(rwkv_mu[0]), ((0, 0), (0, RC_PAD - RC_COLS))),
        "w0": row(rwkv_w0[0]), "a0": row(rwkv_a0[0]), "k_k": row(rwkv_k_k[0]), "k_a": row(rwkv_k_a[0]),
        "wwa": wwa.astype(BF16),
        "g2p": jnp.pad(rwkv_g2[0], ((0, G_LORA_PAD - G_LORA), (0, 0))).astype(BF16),
        "lnx_w": rwkv_lnx_w[0], "lnx_b": rwkv_lnx_b[0], "r_k": rwkv_r_k[0].reshape(-1),
        "w_out": w_out[0].astype(BF16), "w_up": w_up[0].astype(BF16), "w_down": w_down[0].astype(BF16),
    }
    nb = x_prompt.shape[0]
    mod = _adaln(jnp.concatenate([c_prompt, c_sample], axis=0), ada_w[0], ada_b[0])
    y_p, hg_p, rw_p, sh_p = _run(x_prompt, mod[:nb], None, None, None, W, True)
    y_s, hg_s, rw_s, sh_s = _run(x_sample, mod[nb:], state_hgrn[0], state_rwkv[0], state_shift[0], W, False)
    ex = lambda a: a[None]
    return (y_p, y_s, ex(hg_p), ex(rw_p), ex(sh_p), ex(hg_s), ex(rw_s), ex(sh_s))
```
